```python
import math
import jax
import jax.numpy as jnp
from jax import lax
import numpy as np

D_MODEL = 1024
BATCH = 16
SEQ = 256
DEPTH = 2
DEC_BATCH = 2
DEC_SEQ = 4096
PAST_LEN = 512

GRID_W = 64
EPS = 1e-6
DN_HEADS = 8
DN_DK = 128
DN_DV = 128
DN_CONV = 3
DN_CHUNK = 64
DN_QKV = DN_HEADS * (2 * DN_DK + DN_DV)
DN_PROJ = DN_QKV + DN_HEADS * DN_DV + 4 * DN_HEADS
HY_CONV = 3
HY_EMB = 33
HY_ORDER = 64
HY_DECAY_SHORT_PCT = 0.3
HY_DECAY_LONG_PCT = 1.5
HY_DECAY_TARGET = 1e-2
N_EXPERTS = 32
N_GROUPS = 4
EXPERTS_PER_GROUP = N_EXPERTS // N_GROUPS
TOP_K = 2
D_EXPERT = 512

kernel_name = 'hybrid_deltanet_hyena_moe_flow_step'

F32 = jnp.float32


def rms_norm(x, g):
    xf = x.astype(F32)
    y = xf * lax.rsqrt(jnp.mean(xf * xf, axis=-1, keepdims=True) + EPS)
    return (y * g.astype(F32)).astype(x.dtype)


def modulate(h, shift, scale):
    return h * (1.0 + scale) + shift


def l2_normalize(t):
    return t * lax.rsqrt(jnp.sum(t * t, axis=-1, keepdims=True) + EPS)


def depthwise_conv(x, w):
    k = w.shape[0]
    return lax.conv_general_dilated(
        x, w[:, None, :].astype(x.dtype), window_strides=(1,),
        padding=[(k // 2, k // 2)], dimension_numbers=('NWC', 'WIO', 'NWC'),
        feature_group_count=x.shape[-1])


def grid_pos_embed(rows):
    r = jnp.repeat(jnp.arange(rows, dtype=F32), GRID_W)
    col = jnp.tile(jnp.arange(GRID_W, dtype=F32), rows)
    quarter = D_MODEL // 4
    omega = 1.0 / (10000.0 ** (jnp.arange(quarter, dtype=F32) / quarter))
    ang_r = r[:, None] * omega
    ang_c = col[:, None] * omega
    return jnp.concatenate([jnp.sin(ang_r), jnp.cos(ang_r), jnp.sin(ang_c), jnp.cos(ang_c)], axis=-1)


def chunk_gated_delta(q, k, v, log_g, beta, s0):
    b, l, h, _ = q.shape
    dv = v.shape[-1]
    n = l // DN_CHUNK

    def chunks(t):
        t = t.reshape((b, n, DN_CHUNK, h) + t.shape[3:])
        return jnp.moveaxis(jnp.moveaxis(t, 1, 0), 3, 2)

    qc, kc, vc = chunks(q), chunks(k), chunks(v)
    gc = jnp.cumsum(chunks(log_g), axis=-1)
    bc = chunks(beta)[..., None]
    idx = jnp.arange(DN_CHUNK)
    causal = idx[:, None] >= idx[None, :]
    strict = idx[:, None] > idx[None, :]
    decay = jnp.exp(jnp.where(causal, gc[..., :, None] - gc[..., None, :], -jnp.inf))
    kb = kc * bc
    lower = jnp.where(strict, jnp.einsum('nbhcd,nbhsd->nbhcs', kb, kc) * decay, 0.0)
    a_mat = jnp.eye(DN_CHUNK, dtype=F32) + lower
    u = lax.linalg.triangular_solve(a_mat, vc * bc, left_side=True, lower=True, unit_diagonal=True)
    w = lax.linalg.triangular_solve(a_mat, kb * jnp.exp(gc)[..., None], left_side=True, lower=True, unit_diagonal=True)
    attn = jnp.einsum('nbhcd,nbhsd->nbhcs', qc, kc) * decay
    q_dec = qc * jnp.exp(gc)[..., None]
    k_dec = kc * jnp.exp(gc[..., -1:] - gc)[..., None]
    g_last = jnp.exp(gc[..., -1])[..., None, None]

    def step(s, xs):
        attn_n, q_n, k_n, u_n, w_n, gl_n = xs
        v_new = u_n - jnp.einsum('bhcd,bhde->bhce', w_n, s)
        o = jnp.einsum('bhcd,bhde->bhce', q_n, s) + jnp.einsum('bhcs,bhse->bhce', attn_n, v_new)
        s = s * gl_n + jnp.einsum('bhcd,bhce->bhde', k_n, v_new)
        return s, o

    s_final, o = lax.scan(step, s0, (attn, q_dec, k_dec, u, w, g_last))
    o = jnp.moveaxis(jnp.moveaxis(o, 2, 3), 0, 1).reshape(b, l, h, dv)
    return o, s_final


def delta_mixer(h, s0, w_in, conv_w, a_log, dt_bias, g_head, w_out):
    b, l, _ = h.shape
    proj = h @ w_in
    qkv = jax.nn.silu(depthwise_conv(proj[..., :DN_QKV], conv_w)).astype(F32)
    z = proj[..., DN_QKV:DN_QKV + DN_HEADS * DN_DV].astype(F32).reshape(b, l, DN_HEADS, DN_DV)
    ab = proj[..., DN_QKV + DN_HEADS * DN_DV:].astype(F32).reshape(b, l, 2, 2, DN_HEADS)
    log_g = -jnp.exp(a_log.astype(F32)) * jax.nn.softplus(ab[:, :, 0] + dt_bias.astype(F32))
    beta = jax.nn.sigmoid(ab[:, :, 1])
    hk = DN_HEADS * DN_DK
    q = l2_normalize(qkv[..., :hk].reshape(b, l, DN_HEADS, DN_DK)) * (DN_DK ** -0.5)
    k = l2_normalize(qkv[..., hk:2 * hk].reshape(b, l, DN_HEADS, DN_DK))
    v = qkv[..., 2 * hk:].reshape(b, l, DN_HEADS, DN_DV)
    s0 = s0.astype(F32)
    o_f, s_f = chunk_gated_delta(q, k, v, log_g[:, :, 0], beta[:, :, 0], s0[:, 0])
    rev = lambda t: jnp.flip(t, axis=1)
    o_b, s_b = chunk_gated_delta(rev(q), rev(k), rev(v), rev(log_g[:, :, 1]), rev(beta[:, :, 1]), s0[:, 1])
    o = o_f + rev(o_b)
    o = rms_norm(o, g_head) * jax.nn.silu(z)
    out = o.reshape(b, l, DN_HEADS * DN_DV).astype(h.dtype) @ w_out
    return out, jnp.stack([s_f, s_b], axis=1)


def hyena_filter(n, freq, w1, b1, w2, b2, w3, b3, w4):
    t = jnp.linspace(0.0, 1.0, n, dtype=F32)[:, None]
    bands = (HY_EMB - 1) // 2
    band_w = jnp.linspace(1e-4, bands - 1, bands, dtype=F32)[None, :]
    ang = (2.0 * math.pi / n) * jnp.arange(n, dtype=F32)[:, None] * band_w
    z = jnp.concatenate([t, jnp.cos(ang), -jnp.sin(ang)], axis=-1)
    freq = freq.astype(F32)
    hid = jnp.sin(freq[0] * (z @ w1.astype(F32) + b1.astype(F32)))
    hid = jnp.sin(freq[1] * (hid @ w2.astype(F32) + b2.astype(F32)))
    hid = jnp.sin(freq[2] * (hid @ w3.astype(F32) + b3.astype(F32)))
    filt = hid @ w4.astype(F32)
    max_decay = math.log(HY_DECAY_TARGET) / HY_DECAY_SHORT_PCT
    min_decay = math.log(HY_DECAY_TARGET) / HY_DECAY_LONG_PCT
    deltas = jnp.tile(jnp.linspace(min_decay, max_decay, D_MODEL, dtype=F32), 2)
    filt = filt * jnp.exp(-t * jnp.abs(deltas))
    filt = filt / jnp.sum(jnp.abs(filt), axis=0, keepdims=True)
    return filt.reshape(n, 2, D_MODEL)


def long_conv_bidir(v, filt):
    n = v.shape[1]
    taps = jnp.concatenate([
        filt[:1, 0] + filt[:1, 1],
        filt[1:, 0],
        jnp.zeros((1, D_MODEL), F32),
        jnp.flip(filt[1:, 1], axis=0)], axis=0)
    tf = jnp.fft.rfft(taps, axis=0)
    vf = jnp.fft.rfft(v, n=2 * n, axis=1)
    return jnp.fft.irfft(vf * tf[None], n=2 * n, axis=1)[:, :n]


def hyena_mixer(h, w_in, b_in, conv_w, conv_b, freq, fw1, fb1, fw2, fb2, fw3, fb3, fw4, skip, w_out, b_out):
    u = (depthwise_conv(h @ w_in + b_in, conv_w) + conv_b).astype(F32)
    x0, x1, v = jnp.split(u, 3, axis=-1)
    filt = hyena_filter(h.shape[1], freq, fw1, fb1, fw2, fb2, fw3, fb3, fw4)
    v = v * x1
    y = (long_conv_bidir(v, filt) + v * skip.astype(F32)) * x0
    return y.astype(h.dtype) @ w_out + b_out


def moe_ffn(h, w_router, router_bias, w1, w3, w2):
    shape = h.shape
    t = h.reshape(-1, D_MODEL)
    scores = jax.nn.sigmoid((t @ w_router).astype(F32))
    sel = scores + router_bias.astype(F32)
    grouped = sel.reshape(-1, N_GROUPS, EXPERTS_PER_GROUP)
    group_score = jnp.sum(lax.top_k(grouped, 2)[0], axis=-1)
    best = jnp.argmax(group_score, axis=-1)
    in_group = (jnp.arange(N_EXPERTS) // EXPERTS_PER_GROUP)[None, :] == best[:, None]
    _, idx = lax.top_k(jnp.where(in_group, sel, -jnp.inf), TOP_K)
    wts = jnp.take_along_axis(scores, idx, axis=-1)
    wts = wts / jnp.sum(wts, axis=-1, keepdims=True)
    gates = jnp.sum(jax.nn.one_hot(idx, N_EXPERTS, dtype=F32) * wts[..., None], axis=1)
    out = jnp.zeros(t.shape, F32)
    for e in range(N_EXPERTS):
        hid = jax.nn.silu(t @ w1[e]) * (t @ w3[e])
        out = out + gates[:, e:e + 1] * (hid @ w2[e]).astype(F32)
    return out.astype(h.dtype).reshape(shape)


def setup_inputs(seed: int = 0) -> dict:
    key = jax.random.key(seed)
    keys = iter(jax.random.split(key, 48))
    n_dn = (DEPTH + 1) // 2
    n_hy = DEPTH // 2
    d = D_MODEL

    def nrm(shape, scale):
        return scale * jax.random.normal(next(keys), shape, F32)

    def gain(shape):
        return 1.0 + nrm(shape, 0.05)

    a_log = jnp.log(jax.random.uniform(next(keys), (n_dn, 2, DN_HEADS), F32, 1.0, 16.0))
    dt = jnp.exp(jax.random.uniform(next(keys), (n_dn, 2, DN_HEADS), F32, math.log(1e-3), math.log(1e-1)))
    dt_bias = dt + jnp.log(-jnp.expm1(-dt))
    return {
        'x_prompt': nrm((BATCH, SEQ, d), 1.0),
        'x_sample': nrm((DEC_BATCH, DEC_SEQ, d), 1.0),
        'state_delta': nrm((DEC_BATCH, n_dn, 2, DN_HEADS, DN_DK, DN_DV), 0.1),
        'c': nrm((DEC_BATCH, d), 1.0),
        'c_ctx': nrm((d,), 1.0),
        'w_ada': nrm((DEPTH, d, 6 * d), 0.5 * d ** -0.5),
        'b_ada': nrm((DEPTH, 6 * d), 0.02),
        'g_norm': gain((DEPTH, 2, d)),
        'dn_w_in': nrm((n_dn, d, DN_PROJ), d ** -0.5),
        'dn_conv': nrm((n_dn, DN_CONV, DN_QKV), DN_CONV ** -0.5),
        'dn_a_log': a_log,
        'dn_dt_bias': dt_bias,
        'dn_g_head': gain((n_dn, DN_DV)),
        'dn_w_out': nrm((n_dn, DN_HEADS * DN_DV, d), (DN_HEADS * DN_DV) ** -0.5),
        'hy_w_in': nrm((n_hy, d, 3 * d), d ** -0.5),
        'hy_b_in': nrm((n_hy, 3 * d), 0.02),
        'hy_conv': nrm((n_hy, HY_CONV, 3 * d), HY_CONV ** -0.5),
        'hy_conv_b': nrm((n_hy, 3 * d), 0.02),
        'hy_freq': gain((n_hy, 3, HY_ORDER)),
        'hy_f_w1': nrm((n_hy, HY_EMB, HY_ORDER), HY_EMB ** -0.5),
        'hy_f_b1': nrm((n_hy, HY_ORDER), 0.1),
        'hy_f_w2': nrm((n_hy, HY_ORDER, HY_ORDER), HY_ORDER ** -0.5),
        'hy_f_b2': nrm((n_hy, HY_ORDER), 0.1),
        'hy_f_w3': nrm((n_hy, HY_ORDER, HY_ORDER), HY_ORDER ** -0.5),
        'hy_f_b3': nrm((n_hy, HY_ORDER), 0.1),
        'hy_f_w4': nrm((n_hy, HY_ORDER, 2 * d), HY_ORDER ** -0.5),
        'hy_skip': nrm((n_hy, d), 0.5),
        'hy_w_out': nrm((n_hy, d, d), d ** -0.5),
        'hy_b_out': nrm((n_hy, d), 0.02),
        'w_router': nrm((d, N_EXPERTS), d ** -0.5),
        'router_bias': nrm((N_EXPERTS,), 0.01),
        'moe_w1': nrm((DEPTH, N_EXPERTS, d, D_EXPERT), d ** -0.5),
        'moe_w3': nrm((DEPTH, N_EXPERTS, d, D_EXPERT), d ** -0.5),
        'moe_w2': nrm((DEPTH, N_EXPERTS, D_EXPERT, d), D_EXPERT ** -0.5),
        'g_final': gain((d,)),
    }


def reference(x_prompt, x_sample, state_delta, c, c_ctx, w_ada, b_ada, g_norm,
              dn_w_in, dn_conv, dn_a_log, dn_dt_bias, dn_g_head, dn_w_out,
              hy_w_in, hy_b_in, hy_conv, hy_conv_b, hy_freq, hy_f_w1, hy_f_b1, hy_f_w2, hy_f_b2,
              hy_f_w3, hy_f_b3, hy_f_w4, hy_skip, hy_w_out, hy_b_out,
              w_router, router_bias, moe_w1, moe_w3, moe_w2, g_final):
    b_ctx = x_prompt.shape[0]
    rows = x_sample.shape[1] // GRID_W
    x_ctx = x_prompt
    x_lat = x_sample + grid_pos_embed(rows).astype(x_sample.dtype)[None]
    cond_ctx = jax.nn.silu(c_ctx)[None, None, :]
    cond_lat = jax.nn.silu(c)[:, None, :]
    ctx_states = []
    for i in range(DEPTH):
        m_ctx = jnp.split(cond_ctx @ w_ada[i] + b_ada[i], 6, axis=-1)
        m_lat = jnp.split(cond_lat @ w_ada[i] + b_ada[i], 6, axis=-1)
        hc = modulate(rms_norm(x_ctx, g_norm[i, 0]), m_ctx[0], m_ctx[1])
        hl = modulate(rms_norm(x_lat, g_norm[i, 0]), m_lat[0], m_lat[1])
        j = i // 2
        if i % 2 == 0:
            dn = (dn_w_in[j], dn_conv[j], dn_a_log[j], dn_dt_bias[j], dn_g_head[j], dn_w_out[j])
            zero_state = jnp.zeros((b_ctx, 2, DN_HEADS, DN_DK, DN_DV), F32)
            oc, s_ctx = delta_mixer(hc, zero_state, *dn)
            ol, _ = delta_mixer(hl, state_delta[:, j], *dn)
            ctx_states.append(s_ctx.astype(x_prompt.dtype))
        else:
            hy = (hy_w_in[j], hy_b_in[j], hy_conv[j], hy_conv_b[j], hy_freq[j], hy_f_w1[j], hy_f_b1[j],
                  hy_f_w2[j], hy_f_b2[j], hy_f_w3[j], hy_f_b3[j], hy_f_w4[j], hy_skip[j], hy_w_out[j], hy_b_out[j])
            oc = hyena_mixer(hc, *hy)
            ol = hyena_mixer(hl, *hy)
        x_ctx = x_ctx + m_ctx[2] * oc
        x_lat = x_lat + m_lat[2] * ol
        hc = modulate(rms_norm(x_ctx, g_norm[i, 1]), m_ctx[3], m_ctx[4])
        hl = modulate(rms_norm(x_lat, g_norm[i, 1]), m_lat[3], m_lat[4])
        x_ctx = x_ctx + m_ctx[5] * moe_ffn(hc, w_router, router_bias, moe_w1[i], moe_w3[i], moe_w2[i])
        x_lat = x_lat + m_lat[5] * moe_ffn(hl, w_router, router_bias, moe_w1[i], moe_w3[i], moe_w2[i])
    y_prompt = rms_norm(x_ctx, g_final)
    y_sample = rms_norm(x_lat, g_final)
    new_state_delta = jnp.stack(ctx_states, axis=1)
    return (y_prompt, y_sample, new_state_delta)
```

```python
import functools
import math

import numpy as np
import jax
import jax.numpy as jnp
from jax import lax
from jax.experimental import pallas as pl
from jax.experimental.pallas import tpu as pltpu

F32 = jnp.float32
BF16 = jnp.bfloat16
HI = lax.Precision.HIGHEST

EPS = 1e-6
GRID_W = 64
DN_HEADS = 8
DN_DK = 128
DN_DV = 128
DN_CHUNK = 64
HY_EMB = 33
HY_ORDER = 64
HY_DECAY_SHORT_PCT = 0.3
HY_DECAY_LONG_PCT = 1.5
HY_DECAY_TARGET = 1e-2
N_EXPERTS = 32
N_GROUPS = 4
EXPERTS_PER_GROUP = N_EXPERTS // N_GROUPS

LANES = 128
SUBLANES = 8
TM = 256
MOE_TILE = 256
FFT_N2 = 128
VMEM_LIMIT = 48 * 1024 * 1024


def _cparams(sem):
    return pltpu.CompilerParams(dimension_semantics=sem, vmem_limit_bytes=VMEM_LIMIT)


def _sigmoid(x):
    return 1.0 / (1.0 + jnp.exp(-x))


def _silu(x):
    return x * _sigmoid(x)


def _dot(a, b, precision=None):
    return jnp.dot(a, b, preferred_element_type=F32, precision=precision)


def _const_spec(shape):
    nd = len(shape)
    return pl.BlockSpec(shape, lambda *_: (0,) * nd)


def _ada_kernel(cond_ref, w_ref, b_ref, out_ref):
    s = _silu(cond_ref[...])
    out_ref[0] = _dot(s, w_ref[0], HI) + b_ref[0]


def ada_modulation(cond, w_ada, b_ada):
    depth, d, n = w_ada.shape
    tn = 1536
    return pl.pallas_call(
        _ada_kernel,
        grid=(depth, n // tn),
        in_specs=[
            pl.BlockSpec((SUBLANES, d), lambda i, j: (0, 0)),
            pl.BlockSpec((1, d, tn), lambda i, j: (i, 0, j)),
            pl.BlockSpec((1, 1, tn), lambda i, j: (i, 0, j)),
        ],
        out_specs=pl.BlockSpec((1, SUBLANES, tn), lambda i, j: (i, 0, j)),
        out_shape=jax.ShapeDtypeStruct((depth, SUBLANES, n), F32),
        compiler_params=_cparams(("arbitrary", "arbitrary")),
        name="ada_modulation",
    )(cond, w_ada, b_ada.reshape(depth, 1, n))


def _norm_mod(x, g, mod_ref, shift_idx, scale_idx):
    ms = jnp.mean(x * x, axis=-1, keepdims=True)
    y = x * lax.rsqrt(ms + EPS) * g
    scale = mod_ref[0, scale_idx:scale_idx + 1, :]
    shift = mod_ref[0, shift_idx:shift_idx + 1, :]
    return y * (1.0 + scale) + shift


def _group_of_tile(i, tm, t_ctx, l_lat):
    r = i * tm
    return jnp.where(r < t_ctx, 0, 1 + (r - t_ctx) // l_lat)


def _nmm_kernel(x_ref, g_ref, mod_ref, *refs, n_out, has_bias, shift_idx, scale_idx):
    h = _norm_mod(x_ref[...], g_ref[...], mod_ref, shift_idx, scale_idx).astype(BF16)
    w_refs = refs[:n_out]
    b_refs = refs[n_out:n_out + (n_out if has_bias else 0)]
    o_refs = refs[n_out + len(b_refs):]
    for k in range(n_out):
        acc = _dot(h, w_refs[k][...])
        if has_bias:
            acc = acc + b_refs[k][...]
        o_refs[k][...] = acc


def norm_mod_matmul(x, g, mod, shift_idx, scale_idx, weights, biases, t_ctx, l_lat):
    t, d = x.shape
    n_out = len(weights)
    has_bias = biases is not None
    grp = functools.partial(_group_of_tile, tm=TM, t_ctx=t_ctx, l_lat=l_lat)
    in_specs = [
        pl.BlockSpec((TM, d), lambda i: (i, 0)),
        _const_spec((1, d)),
        pl.BlockSpec((1,) + mod.shape[1:], lambda i: (grp(i), 0, 0)),
    ]
    args = [x, g.reshape(1, d), mod]
    for w in weights:
        in_specs.append(_const_spec(w.shape))
        args.append(w)
    if has_bias:
        for b in biases:
            in_specs.append(_const_spec((1, b.shape[-1])))
            args.append(b.reshape(1, -1))
    out_specs = [pl.BlockSpec((TM, w.shape[1]), lambda i: (i, 0)) for w in weights]
    out_shape = [jax.ShapeDtypeStruct((t, w.shape[1]), F32) for w in weights]
    return pl.pallas_call(
        functools.partial(_nmm_kernel, n_out=n_out, has_bias=has_bias,
                          shift_idx=shift_idx, scale_idx=scale_idx),
        grid=(t // TM,),
        in_specs=in_specs,
        out_specs=out_specs,
        out_shape=out_shape,
        compiler_params=_cparams(("arbitrary",)),
        name="norm_mod_matmul",
    )(*args)


def _seq_flags(i, tm, t_ctx, l_ctx, l_lat):
    r = i * tm
    in_ctx = r < t_ctx
    pos = jnp.where(in_ctx, r % l_ctx, (r - t_ctx) % l_lat)
    length = jnp.where(in_ctx, l_ctx, l_lat)
    return pos != 0, (pos + tm) != length


def _conv3(x, prev8, next8, w_ref, has_prev, has_next):
    tm = x.shape[0]
    row = lax.broadcasted_iota(jnp.int32, (tm, 1), 0)
    halo_p = jnp.where(has_prev, prev8[SUBLANES - 1:SUBLANES, :], 0.0)
    halo_n = jnp.where(has_next, next8[0:1, :], 0.0)
    x_prev = jnp.where(row == 0, halo_p, pltpu.roll(x, 1, 0))
    x_next = jnp.where(row == tm - 1, halo_n, pltpu.roll(x, tm - 1, 0))
    return w_ref[0:1, :] * x_prev + w_ref[1:2, :] * x + w_ref[2:3, :] * x_next


def _halo_specs(c, tm, t):
    nb8 = t // SUBLANES
    per = tm // SUBLANES
    return [
        pl.BlockSpec((tm, c), lambda i: (i, 0)),
        pl.BlockSpec((SUBLANES, c), lambda i: (jnp.maximum(i * per - 1, 0), 0)),
        pl.BlockSpec((SUBLANES, c), lambda i: (jnp.minimum((i + 1) * per, nb8 - 1), 0)),
    ]


def _softplus(x):
    return jnp.maximum(x, 0.0) + jnp.log(1.0 + jnp.exp(-jnp.abs(x)))


def _dn_prep_kernel(x_ref, xp_ref, xn_ref, cw_ref, ab_ref, nega_ref, dtb_ref, tri_ref,
                    q_ref, k_ref, v_ref, gc_ref, beta_ref, gct_ref,
                    *, t_ctx, l_ctx, l_lat):
    i = pl.program_id(0)
    has_prev, has_next = _seq_flags(i, TM, t_ctx, l_ctx, l_lat)
    c = _conv3(x_ref[...], xp_ref[...], xn_ref[...], cw_ref, has_prev, has_next)
    s = _silu(c)
    hk = DN_HEADS * DN_DK
    for h in range(DN_HEADS):
        lo, hi = h * DN_DK, (h + 1) * DN_DK
        qh = s[:, lo:hi]
        kh = s[:, hk + lo:hk + hi]
        qn = lax.rsqrt(jnp.sum(qh * qh, axis=-1, keepdims=True) + EPS)
        kn = lax.rsqrt(jnp.sum(kh * kh, axis=-1, keepdims=True) + EPS)
        q_ref[:, lo:hi] = qh * (qn * (DN_DK ** -0.5))
        k_ref[:, lo:hi] = kh * kn
    v_ref[...] = s[:, 2 * hk:]
    ab = ab_ref[...]
    lg = nega_ref[...] * _softplus(ab + dtb_ref[...])
    beta_ref[...] = _sigmoid(ab)
    nch = TM // DN_CHUNK
    for d in range(2):
        gc = _dot(tri_ref[d], lg, HI)
        gc_ref[d] = gc
        gct = gc.T
        for ch in range(nch):
            gct_ref[d, ch] = gct[0:2 * DN_HEADS, ch * DN_CHUNK:(ch + 1) * DN_CHUNK]


def _chunk_tri():
    idx = np.arange(TM)
    same = (idx[:, None] // DN_CHUNK) == (idx[None, :] // DN_CHUNK)
    fwd = same & (idx[None, :] <= idx[:, None])
    bwd = same & (idx[None, :] >= idx[:, None])
    return jnp.asarray(np.stack([fwd, bwd]).astype(np.float32))


def dn_prep(qkv, ab, conv_w, a_log, dt_bias, t_ctx, l_ctx, l_lat):
    t, c = qkv.shape
    hd = DN_HEADS * DN_DK
    nega = jnp.zeros((1, LANES), F32).at[0, :2 * DN_HEADS].set(-jnp.exp(a_log.astype(F32)).reshape(-1))
    dtb = jnp.zeros((1, LANES), F32).at[0, :2 * DN_HEADS].set(dt_bias.astype(F32).reshape(-1))
    cw = jnp.zeros((SUBLANES, c), F32).at[:conv_w.shape[0]].set(conv_w)
    nch = TM // DN_CHUNK
    row_spec = lambda w: pl.BlockSpec((TM, w), lambda i: (i, 0))
    return pl.pallas_call(
        functools.partial(_dn_prep_kernel, t_ctx=t_ctx, l_ctx=l_ctx, l_lat=l_lat),
        grid=(t // TM,),
        in_specs=_halo_specs(c, TM, t) + [
            _const_spec((SUBLANES, c)),
            row_spec(LANES),
            _const_spec((1, LANES)),
            _const_spec((1, LANES)),
            _const_spec((2, TM, TM)),
        ],
        out_specs=[
            row_spec(hd), row_spec(hd), row_spec(hd),
            pl.BlockSpec((2, TM, LANES), lambda i: (0, i, 0)),
            row_spec(LANES),
            pl.BlockSpec((2, nch, 2 * DN_HEADS, DN_CHUNK), lambda i: (0, i, 0, 0)),
        ],
        out_shape=[
            jax.ShapeDtypeStruct((t, hd), F32),
            jax.ShapeDtypeStruct((t, hd), F32),
            jax.ShapeDtypeStruct((t, hd), F32),
            jax.ShapeDtypeStruct((2, t, LANES), F32),
            jax.ShapeDtypeStruct((t, LANES), F32),
            jax.ShapeDtypeStruct((2, t // DN_CHUNK, 2 * DN_HEADS, DN_CHUNK), F32),
        ],
        compiler_params=_cparams(("arbitrary",)),
        name="dn_prep",
    )(qkv, qkv, qkv, cw, ab, nega, dtb, _chunk_tri())


def _inv_unit_tri(lm, in_blk, eye):
    mm = functools.partial(_dot, precision=HI)
    dg = jnp.where(in_blk, lm, 0.0)
    off = lm - dg
    n1 = -dg
    n2 = mm(n1, n1)
    n4 = mm(n2, n2)
    n8 = mm(n4, n4)
    p = eye + n1
    p = p + mm(p, n2)
    p = p + mm(p, n4)
    p = p + mm(p, n8)
    m = mm(p, off)
    m2 = mm(m, m)
    t1 = eye - m
    t1 = t1 + mm(t1, m2)
    return mm(t1, p)


def _dn_chunk_kernel(q_ref, k_ref, v_ref, gc_ref, beta_ref, gct_ref, s0_ref,
                     o_ref, sfin_ref, s_scr, *, direction, nblk, zero_init):
    j = pl.program_id(1)
    c_sz = DN_CHUNK
    nch = TM // c_sz

    @pl.when(j == 0)
    def _():
        if zero_init:
            s_scr[...] = jnp.zeros_like(s_scr)
        else:
            s_scr[...] = s0_ref[0, 0]

    row = lax.broadcasted_iota(jnp.int32, (c_sz, c_sz), 0)
    col = lax.broadcasted_iota(jnp.int32, (c_sz, c_sz), 1)
    if direction == 0:
        incl, strict = row >= col, row > col
        last = c_sz - 1
    else:
        incl, strict = row <= col, row < col
        last = 0
    in_blk = (row // 16) == (col // 16)
    eye = (row == col).astype(F32)

    def chunk_body(ci, carry):
        ch = ci if direction == 0 else nch - 1 - ci
        r0 = pl.multiple_of(ch * c_sz, c_sz)
        rows = pl.ds(r0, c_sz)
        gct = gct_ref[0, ch]
        for h in range(DN_HEADS):
            lanes = slice(h * DN_DK, (h + 1) * DN_DK)
            gl = direction * DN_HEADS + h
            qh = q_ref[rows, lanes]
            kh = k_ref[rows, lanes]
            vh = v_ref[rows, lanes]
            gcc = gc_ref[0, rows, gl:gl + 1]
            beta = beta_ref[rows, 2 * DN_HEADS + gl:2 * DN_HEADS + gl + 1]
            gcr = gct[gl:gl + 1, :]
            dec = jnp.exp(jnp.where(incl, gcc - gcr, -jnp.inf))
            kb = kh * beta
            lhs = jnp.concatenate([qh, kb], axis=0).astype(BF16)
            prod = lax.dot_general(lhs, kh.astype(BF16), (((1,), (1,)), ((), ())),
                                   preferred_element_type=F32)
            attn = prod[:c_sz] * dec
            lm = jnp.where(strict, prod[c_sz:] * dec, 0.0)
            tinv = _inv_unit_tri(lm, in_blk, eye)
            eg = jnp.exp(gcc)
            rhs = jnp.concatenate([vh * beta, kb * eg], axis=1)
            uw = _dot(tinv, rhs, HI)
            u = uw[:, :DN_DV]
            w = uw[:, DN_DV:]
            s = s_scr[h]
            qw = jnp.concatenate([qh * eg, w], axis=0).astype(BF16)
            qs_ws = _dot(qw, s.astype(BF16))
            v_new = u - qs_ws[c_sz:]
            vb = v_new.astype(BF16)
            o = qs_ws[:c_sz] + _dot(attn.astype(BF16), vb)
            g_last = gcc[last:last + 1, :]
            k_dec = kh * jnp.exp(g_last - gcc)
            s_scr[h] = s * jnp.exp(g_last) + _dot(k_dec.T.astype(BF16), vb)
            o_ref[rows, lanes] = o
        return carry

    lax.fori_loop(0, nch, chunk_body, 0)

    @pl.when(j == nblk - 1)
    def _():
        sfin_ref[0, 0] = s_scr[...]


def dn_chunk_scan(q, k, v, gc, beta, gct, s0, *, direction, row0, n_seq, seq_len, zero_init):
    t, hd = q.shape
    nblk = seq_len // TM
    base = row0 // TM
    nch = TM // DN_CHUNK

    def blk(s, j):
        jj = j if direction == 0 else nblk - 1 - j
        return base + s * nblk + jj

    row_spec = lambda w: pl.BlockSpec((TM, w), lambda s, j: (blk(s, j), 0))
    if zero_init:
        s0 = jnp.zeros((1, 1, DN_HEADS, DN_DK, DN_DV), F32)
        s0_spec = pl.BlockSpec((1, 1, DN_HEADS, DN_DK, DN_DV), lambda s, j: (0, 0, 0, 0, 0))
    else:
        s0_spec = pl.BlockSpec((1, 1, DN_HEADS, DN_DK, DN_DV), lambda s, j: (s, direction, 0, 0, 0))
    o, sfin = pl.pallas_call(
        functools.partial(_dn_chunk_kernel, direction=direction, nblk=nblk, zero_init=zero_init),
        grid=(n_seq, nblk),
        in_specs=[
            row_spec(hd), row_spec(hd), row_spec(hd),
            pl.BlockSpec((1, TM, LANES), lambda s, j: (direction, blk(s, j), 0)),
            row_spec(LANES),
            pl.BlockSpec((1, nch, 2 * DN_HEADS, DN_CHUNK), lambda s, j: (direction, blk(s, j), 0, 0)),
            s0_spec,
        ],
        out_specs=[
            pl.BlockSpec((TM, hd), lambda s, j: (s * nblk + (j if direction == 0 else nblk - 1 - j), 0)),
            pl.BlockSpec((1, 1, DN_HEADS, DN_DK, DN_DV), lambda s, j: (s, 0, 0, 0, 0)),
        ],
        out_shape=[
            jax.ShapeDtypeStruct((n_seq * seq_len, hd), F32),
            jax.ShapeDtypeStruct((n_seq, 1, DN_HEADS, DN_DK, DN_DV), F32),
        ],
        scratch_shapes=[pltpu.VMEM((DN_HEADS, DN_DK, DN_DV), F32)],
        compiler_params=_cparams(("arbitrary", "arbitrary")),
        name="dn_chunk_scan_d%d" % direction,
    )(q, k, v, gc, beta, gct, s0)
    return o, sfin[:, 0]


def _dn_out_kernel(of_ref, ob_ref, z_ref, gh_ref, w_ref, x_ref, mod_ref, out_ref, *, gate_idx):
    o = of_ref[...] + ob_ref[...]
    z = z_ref[...]
    parts = []
    for h in range(DN_HEADS):
        lanes = slice(h * DN_DV, (h + 1) * DN_DV)
        oh = o[:, lanes]
        ms = jnp.mean(oh * oh, axis=-1, keepdims=True)
        parts.append(oh * lax.rsqrt(ms + EPS) * gh_ref[...])
    on = jnp.concatenate(parts, axis=1) * _silu(z)
    y = _dot(on.astype(BF16), w_ref[...])
    out_ref[...] = x_ref[...] + mod_ref[0, gate_idx:gate_idx + 1, :] * y


def dn_out(o_f, o_b, z, g_head, w_out, x, mod, gate_idx, t_ctx, l_lat):
    t, d = x.shape
    hd = o_f.shape[1]
    grp = functools.partial(_group_of_tile, tm=TM, t_ctx=t_ctx, l_lat=l_lat)
    row_spec = lambda w: pl.BlockSpec((TM, w), lambda i: (i, 0))
    return pl.pallas_call(
        functools.partial(_dn_out_kernel, gate_idx=gate_idx),
        grid=(t // TM,),
        in_specs=[
            row_spec(hd), row_spec(hd), row_spec(hd),
            _const_spec((1, DN_DV)),
            _const_spec(w_out.shape),
            row_spec(d),
            pl.BlockSpec((1,) + mod.shape[1:], lambda i: (grp(i), 0, 0)),
        ],
        out_specs=row_spec(d),
        out_shape=jax.ShapeDtypeStruct((t, d), F32),
        compiler_params=_cparams(("arbitrary",)),
        name="dn_out",
    )(o_f, o_b, z, g_head.reshape(1, DN_DV).astype(F32), w_out, x, mod)


def _first_argmax(vals, lane, valid):
    masked = jnp.where(valid, vals, -jnp.inf)
    m = jnp.max(masked, axis=-1, keepdims=True)
    idx = jnp.min(jnp.where(valid & (masked == m), lane, LANES), axis=-1, keepdims=True)
    return m, idx


def _router_kernel(x_ref, g_ref, mod_ref, wr_ref, rb_ref, h_ref, idx_ref, wt_ref,
                   *, shift_idx, scale_idx):
    h = _norm_mod(x_ref[...], g_ref[...], mod_ref, shift_idx, scale_idx)
    h_ref[...] = h.astype(BF16)
    scores = _sigmoid(_dot(h, wr_ref[...], HI))
    sel = scores + rb_ref[...]
    tm = scores.shape[0]
    lane = lax.broadcasted_iota(jnp.int32, (tm, LANES), 1)
    best_score = None
    for g in range(N_GROUPS):
        in_g = (lane >= g * EXPERTS_PER_GROUP) & (lane < (g + 1) * EXPERTS_PER_GROUP)
        m1, i1 = _first_argmax(sel, lane, in_g)
        m2, i2 = _first_argmax(sel, lane, in_g & (lane != i1))
        gs = m1 + m2
        if best_score is None:
            best_score, b1, b2 = gs, i1, i2
        else:
            better = gs > best_score
            best_score = jnp.where(better, gs, best_score)
            b1 = jnp.where(better, i1, b1)
            b2 = jnp.where(better, i2, b2)
    w1 = jnp.sum(jnp.where(lane == b1, scores, 0.0), axis=-1, keepdims=True)
    w2 = jnp.sum(jnp.where(lane == b2, scores, 0.0), axis=-1, keepdims=True)
    tot = w1 + w2
    idx_ref[...] = jnp.where(lane == 0, b1, jnp.where(lane == 1, b2, 0))
    wt_ref[...] = jnp.where(lane == 0, w1 / tot, jnp.where(lane == 1, w2 / tot, 0.0))


def moe_router(x, g, mod, shift_idx, scale_idx, w_router, router_bias, t_ctx, l_lat):
    t, d = x.shape
    grp = functools.partial(_group_of_tile, tm=TM, t_ctx=t_ctx, l_lat=l_lat)
    wr = jnp.zeros((d, LANES), F32).at[:, :N_EXPERTS].set(w_router.astype(F32))
    rb = jnp.zeros((1, LANES), F32).at[0, :N_EXPERTS].set(router_bias.astype(F32))
    row_spec = lambda w: pl.BlockSpec((TM, w), lambda i: (i, 0))
    return pl.pallas_call(
        functools.partial(_router_kernel, shift_idx=shift_idx, scale_idx=scale_idx),
        grid=(t // TM,),
        in_specs=[
            row_spec(d),
            _const_spec((1, d)),
            pl.BlockSpec((1,) + mod.shape[1:], lambda i: (grp(i), 0, 0)),
            _const_spec((d, LANES)),
            _const_spec((1, LANES)),
        ],
        out_specs=[row_spec(d), row_spec(LANES), row_spec(LANES)],
        out_shape=[
            jax.ShapeDtypeStruct((t, d), BF16),
            jax.ShapeDtypeStruct((t, LANES), jnp.int32),
            jax.ShapeDtypeStruct((t, LANES), F32),
        ],
        compiler_params=_cparams(("arbitrary",)),
        name="moe_router",
    )(x, g.reshape(1, d), mod, wr, rb)


def _expert_kernel(te_ref, nt_ref, xs_ref, gw_ref, w1_ref, w3_ref, w2_ref, ys_ref):
    i = pl.program_id(0)

    @pl.when(i < nt_ref[0])
    def _():
        xs = xs_ref[...]
        a = _dot(xs, w1_ref[0].astype(BF16))
        b = _dot(xs, w3_ref[0].astype(BF16))
        hid = (_silu(a) * b).astype(BF16)
        ys_ref[...] = gw_ref[...] * _dot(hid, w2_ref[0].astype(BF16))

    @pl.when(i >= nt_ref[0])
    def _():
        ys_ref[...] = jnp.zeros_like(ys_ref)


def moe_experts(xs, gw, tile_expert, n_tiles, w1, w3, w2):
    p, d = xs.shape
    de = w1.shape[-1]
    ntile = p // MOE_TILE
    grid_spec = pltpu.PrefetchScalarGridSpec(
        num_scalar_prefetch=2,
        grid=(ntile,),
        in_specs=[
            pl.BlockSpec((MOE_TILE, d), lambda i, te, nt: (i, 0)),
            pl.BlockSpec((MOE_TILE, 1), lambda i, te, nt: (i, 0)),
            pl.BlockSpec((1, d, de), lambda i, te, nt: (te[i], 0, 0)),
            pl.BlockSpec((1, d, de), lambda i, te, nt: (te[i], 0, 0)),
            pl.BlockSpec((1, de, d), lambda i, te, nt: (te[i], 0, 0)),
        ],
        out_specs=pl.BlockSpec((MOE_TILE, d), lambda i, te, nt: (i, 0)),
    )
    return pl.pallas_call(
        _expert_kernel,
        grid_spec=grid_spec,
        out_shape=jax.ShapeDtypeStruct((p, d), F32),
        compiler_params=_cparams(("arbitrary",)),
        name="moe_experts",
    )(tile_expert, n_tiles, xs, gw, w1, w3, w2)


def moe_dispatch_plan(idx, wts):
    t = idx.shape[0]
    n_assign = 2 * t
    p = n_assign + N_EXPERTS * MOE_TILE
    p = -(-p // MOE_TILE) * MOE_TILE
    e_flat = idx.reshape(-1)
    onehot = (e_flat[:, None] == jnp.arange(N_EXPERTS, dtype=jnp.int32)[None, :]).astype(jnp.int32)
    rank = jnp.cumsum(onehot, axis=0) - onehot
    counts = jnp.sum(onehot, axis=0)
    padded = ((counts + MOE_TILE - 1) // MOE_TILE) * MOE_TILE
    starts = jnp.cumsum(padded) - padded
    slot = jnp.sum(onehot * (starts[None, :] + rank), axis=1)
    token = jnp.arange(n_assign, dtype=jnp.int32) // 2
    slot_token = jnp.zeros((p,), jnp.int32).at[slot].set(token)
    slot_gate = jnp.zeros((p,), F32).at[slot].set(wts.reshape(-1))
    ends = jnp.cumsum(padded)
    tile_start = jnp.arange(p // MOE_TILE, dtype=jnp.int32) * MOE_TILE
    tile_expert = jnp.sum((tile_start[:, None] >= ends[None, :]).astype(jnp.int32), axis=1)
    n_tiles = (ends[-1] // MOE_TILE).astype(jnp.int32).reshape(1)
    last_used = jnp.maximum(n_tiles[0] - 1, 0)
    tile_expert = jnp.where(jnp.arange(p // MOE_TILE) < n_tiles[0], tile_expert,
                            tile_expert[last_used]).astype(jnp.int32)
    return slot_token, slot_gate.reshape(p, 1), tile_expert, n_tiles, slot.reshape(t, 2)


def _combine_kernel(x_ref, y0_ref, y1_ref, mod_ref, *refs, gate_idx, final_norm):
    x = x_ref[...] + mod_ref[0, gate_idx:gate_idx + 1, :] * (y0_ref[...] + y1_ref[...])
    if final_norm:
        gf_ref, out_ref = refs
        ms = jnp.mean(x * x, axis=-1, keepdims=True)
        out_ref[...] = x * lax.rsqrt(ms + EPS) * gf_ref[...]
    else:
        (out_ref,) = refs
        out_ref[...] = x


def moe_combine(x, y0, y1, mod, gate_idx, t_ctx, l_lat, g_final=None):
    t, d = x.shape
    grp = functools.partial(_group_of_tile, tm=TM, t_ctx=t_ctx, l_lat=l_lat)
    row_spec = lambda w: pl.BlockSpec((TM, w), lambda i: (i, 0))
    in_specs = [row_spec(d), row_spec(d), row_spec(d),
                pl.BlockSpec((1,) + mod.shape[1:], lambda i: (grp(i), 0, 0))]
    args = [x, y0, y1, mod]
    if g_final is not None:
        in_specs.append(_const_spec((1, d)))
        args.append(g_final.reshape(1, d).astype(F32))
    return pl.pallas_call(
        functools.partial(_combine_kernel, gate_idx=gate_idx, final_norm=g_final is not None),
        grid=(t // TM,),
        in_specs=in_specs,
        out_specs=row_spec(d),
        out_shape=jax.ShapeDtypeStruct((t, d), F32),
        compiler_params=_cparams(("arbitrary",)),
        name="moe_combine",
    )(*args)


def moe_layer(x, g, mod, w_router, router_bias, w1, w3, w2, t_ctx, l_lat, g_final=None):
    h, idx_pad, wt_pad = moe_router(x, g, mod, 3, 4, w_router, router_bias, t_ctx, l_lat)
    idx = idx_pad[:, :2]
    wts = wt_pad[:, :2]
    slot_token, slot_gate, tile_expert, n_tiles, slot = moe_dispatch_plan(idx, wts)
    xs = jnp.take(h, slot_token, axis=0)
    ys = moe_experts(xs, slot_gate, tile_expert, n_tiles, w1, w3, w2)
    y0 = jnp.take(ys, slot[:, 0], axis=0)
    y1 = jnp.take(ys, slot[:, 1], axis=0)
    return moe_combine(x, y0, y1, mod, 5, t_ctx, l_lat, g_final)


def _hy_conv_kernel(u_ref, up_ref, un_ref, cw_ref, cb_ref, x0_ref, vv_ref, *, t_ctx, l_ctx, l_lat):
    i = pl.program_id(0)
    has_prev, has_next = _seq_flags(i, TM, t_ctx, l_ctx, l_lat)
    c = _conv3(u_ref[...], up_ref[...], un_ref[...], cw_ref, has_prev, has_next) + cb_ref[...]
    d = c.shape[1] // 3
    x0_ref[...] = c[:, :d]
    vv_ref[...] = c[:, 2 * d:] * c[:, d:2 * d]


def hy_conv_gate(u, conv_w, conv_b, t_ctx, l_ctx, l_lat):
    t, c = u.shape
    d = c // 3
    cw = jnp.zeros((SUBLANES, c), F32).at[:conv_w.shape[0]].set(conv_w)
    row_spec = lambda w: pl.BlockSpec((TM, w), lambda i: (i, 0))
    return pl.pallas_call(
        functools.partial(_hy_conv_kernel, t_ctx=t_ctx, l_ctx=l_ctx, l_lat=l_lat),
        grid=(t // TM,),
        in_specs=_halo_specs(c, TM, t) + [_const_spec((SUBLANES, c)), _const_spec((1, c))],
        out_specs=[row_spec(d), row_spec(d)],
        out_shape=[jax.ShapeDtypeStruct((t, d), F32), jax.ShapeDtypeStruct((t, d), F32)],
        compiler_params=_cparams(("arbitrary",)),
        name="hy_conv_gate",
    )(u, u, u, cw, conv_b.reshape(1, c).astype(F32))


def _hy_features(n):
    t = np.linspace(0.0, 1.0, n)[:, None]
    bands = (HY_EMB - 1) // 2
    band_w = np.linspace(1e-4, bands - 1, bands)[None, :]
    ang = (2.0 * math.pi / n) * np.arange(n)[:, None] * band_w
    z = np.concatenate([t, np.cos(ang), -np.sin(ang)], axis=-1)
    zp = np.zeros((n, LANES))
    zp[:, :HY_EMB] = z
    return jnp.asarray(zp.astype(np.float32))


def _hy_hid_kernel(z_ref, f_ref, w1_ref, b1_ref, w2_ref, b2_ref, w3_ref, b3_ref, hid_ref):
    h = jnp.sin(f_ref[0:1, :] * (_dot(z_ref[...], w1_ref[...], HI) + b1_ref[...]))
    h = jnp.sin(f_ref[1:2, :] * (_dot(h, w2_ref[...], HI) + b2_ref[...]))
    h = jnp.sin(f_ref[2:3, :] * (_dot(h, w3_ref[...], HI) + b3_ref[...]))
    hid_ref[...] = h


def _hy_filt_kernel(hid_ref, w4_ref, z_ref, dl_ref, filt_ref):
    f = _dot(hid_ref[...], w4_ref[...], HI)
    f = f * jnp.exp(-z_ref[:, 0:1] * jnp.abs(dl_ref[...]))
    nrm = jnp.sum(jnp.abs(f), axis=0, keepdims=True)
    filt_ref[...] = f / nrm


def hy_filter(n, d, freq, w1, b1, w2, b2, w3, b3, w4):
    z = _hy_features(n)
    o = HY_ORDER
    w1p = jnp.zeros((LANES, o), F32).at[:HY_EMB].set(w1.astype(F32))
    fp = jnp.zeros((SUBLANES, o), F32).at[:3].set(freq.astype(F32))
    tr = min(n, TM)
    hid = pl.pallas_call(
        _hy_hid_kernel,
        grid=(n // tr,),
        in_specs=[pl.BlockSpec((tr, LANES), lambda i: (i, 0)), _const_spec((SUBLANES, o)),
                  _const_spec((LANES, o)), _const_spec((1, o)),
                  _const_spec((o, o)), _const_spec((1, o)),
                  _const_spec((o, o)), _const_spec((1, o))],
        out_specs=pl.BlockSpec((tr, o), lambda i: (i, 0)),
        out_shape=jax.ShapeDtypeStruct((n, o), F32),
        compiler_params=_cparams(("arbitrary",)),
        name="hy_filter_mlp",
    )(z, fp, w1p, b1.reshape(1, o).astype(F32), w2.astype(F32), b2.reshape(1, o).astype(F32),
      w3.astype(F32), b3.reshape(1, o).astype(F32))
    max_decay = math.log(HY_DECAY_TARGET) / HY_DECAY_SHORT_PCT
    min_decay = math.log(HY_DECAY_TARGET) / HY_DECAY_LONG_PCT
    deltas = np.tile(np.linspace(min_decay, max_decay, d), 2).astype(np.float32)[None, :]
    tc = 256
    return pl.pallas_call(
        _hy_filt_kernel,
        grid=(2 * d // tc,),
        in_specs=[_const_spec((n, o)), pl.BlockSpec((o, tc), lambda j: (0, j)),
                  _const_spec((n, LANES)), pl.BlockSpec((1, tc), lambda j: (0, j))],
        out_specs=pl.BlockSpec((n, tc), lambda j: (0, j)),
        out_shape=jax.ShapeDtypeStruct((n, 2 * d), F32),
        compiler_params=_cparams(("arbitrary",)),
        name="hy_filter_window",
    )(hid, w4.astype(F32), z, jnp.asarray(deltas))


def _dense_dft_tables(l):
    n = 2 * l
    k = np.arange(n)[:, None]
    j = np.arange(l)[None, :]
    th = 2.0 * math.pi * ((k * j) % n) / n
    fwd = np.concatenate([np.cos(th), -np.sin(th)], axis=0)
    inv = np.concatenate([np.cos(th).T, -np.sin(th).T], axis=1) / n
    return (jnp.asarray(fwd.astype(np.float32)).astype(BF16),
            jnp.asarray(inv.astype(np.float32)).astype(BF16))


def _dense_tf_kernel(hf_ref, hb_ref, fwd_ref, tf_ref):
    n = fwd_ref.shape[0] // 2
    hf = hf_ref[...]
    hb = hb_ref[...]
    tf_ref[0:n, :] = _dot(fwd_ref[0:n, :], (hf + hb).astype(BF16))
    tf_ref[n:, :] = _dot(fwd_ref[n:, :], (hf - hb).astype(BF16))


def _dense_conv_kernel(vv_ref, x0_ref, skip_ref, tf_ref, fwd_ref, inv_ref, out_ref):
    n = fwd_ref.shape[0] // 2
    vv = vv_ref[...]
    x = _dot(fwd_ref[...], vv.astype(BF16))
    xr, xi = x[:n], x[n:]
    tr, ti = tf_ref[0:n, :], tf_ref[n:, :]
    y = jnp.concatenate([xr * tr - xi * ti, xr * ti + xi * tr], axis=0).astype(BF16)
    conv = _dot(inv_ref[...], y)
    out_ref[...] = (conv + vv * skip_ref[...]) * x0_ref[...]


def hy_longconv_dense(vv, x0, skip, filt, n_seq, l):
    d = vv.shape[1]
    n = 2 * l
    fwd, inv = _dense_dft_tables(l)
    dc = 512
    tf = pl.pallas_call(
        _dense_tf_kernel,
        grid=(d // dc,),
        in_specs=[pl.BlockSpec((l, dc), lambda j: (0, j)),
                  pl.BlockSpec((l, dc), lambda j: (0, d // dc + j)),
                  _const_spec(fwd.shape)],
        out_specs=pl.BlockSpec((2 * n, dc), lambda j: (0, j)),
        out_shape=jax.ShapeDtypeStruct((2 * n, d), F32),
        compiler_params=_cparams(("arbitrary",)),
        name="hy_tf_dense",
    )(filt, filt, fwd)
    return pl.pallas_call(
        _dense_conv_kernel,
        grid=(d // dc, n_seq),
        in_specs=[pl.BlockSpec((l, dc), lambda j, s: (s, j)),
                  pl.BlockSpec((l, dc), lambda j, s: (s, j)),
                  pl.BlockSpec((1, dc), lambda j, s: (0, j)),
                  pl.BlockSpec((2 * n, dc), lambda j, s: (0, j)),
                  _const_spec(fwd.shape), _const_spec(inv.shape)],
        out_specs=pl.BlockSpec((l, dc), lambda j, s: (s, j)),
        out_shape=jax.ShapeDtypeStruct((n_seq * l, d), F32),
        compiler_params=_cparams(("arbitrary", "arbitrary")),
        name="hy_longconv_dense",
    )(vv, x0, skip.reshape(1, d).astype(F32), tf, fwd, inv)


def _two_stage_tables(l):
    n = 2 * l
    n2 = FFT_N2
    n1 = n // n2
    p = l // n2
    a = np.arange(p)[None, None, :]
    c = np.arange(n1)[None, :, None]
    b = np.arange(n2)[:, None, None]
    ph = 2.0 * math.pi * (((a * c) % n1) / n1 + ((b * c) % n) / n)
    f1 = np.concatenate([np.cos(ph), -np.sin(ph)], axis=1)
    pht = np.transpose(ph, (0, 2, 1))
    g1 = np.concatenate([np.cos(pht), -np.sin(pht)], axis=2) / n
    e = np.arange(n2)[:, None]
    bb = np.arange(n2)[None, :]
    th = 2.0 * math.pi * ((e * bb) % n2) / n2
    fr, fi = np.cos(th), -np.sin(th)
    f2 = np.block([[fr, -fi], [fi, fr]])
    f2c = np.block([[fr, fi], [-fi, fr]])
    cast = lambda m: jnp.asarray(m.astype(np.float32)).astype(BF16)
    return cast(f1), cast(f2), cast(f2c), cast(g1)


def _stage1(src_ref, f1_ref, w_scr, combine=None):
    n2, two_n1, p = f1_ref.shape

    def body(b, carry):
        if combine is None:
            xb = src_ref[pl.ds(b, p, stride=n2), :]
        else:
            xb = combine(src_ref[0][pl.ds(b, p, stride=n2), :], src_ref[1][pl.ds(b, p, stride=n2), :])
        r0 = pl.multiple_of(b * two_n1, two_n1)
        w_scr[pl.ds(r0, two_n1), :] = _dot(f1_ref[b], xb.astype(BF16))
        return carry

    lax.fori_loop(0, n2, body, 0)


def _load_slab(w_scr, c, n1, n2):
    re = w_scr[pl.ds(c, n2, stride=2 * n1), :]
    im = w_scr[pl.ds(n1 + c, n2, stride=2 * n1), :]
    return jnp.concatenate([re, im], axis=0).astype(BF16)


def _two_stage_tf_kernel(hf_ref, hb_ref, f1_ref, f2_ref, tfr_ref, tfi_ref, ws_scr, wd_scr):
    n2, two_n1, _ = f1_ref.shape
    n1 = two_n1 // 2
    _stage1((hf_ref, hb_ref), f1_ref, ws_scr, combine=lambda u, v: u + v)
    _stage1((hf_ref, hb_ref), f1_ref, wd_scr, combine=lambda u, v: u - v)

    def body(c, carry):
        r0 = pl.multiple_of(c * n2, n2)
        xs = _dot(f2_ref[0:n2, :], _load_slab(ws_scr, c, n1, n2))
        xd = _dot(f2_ref[n2:, :], _load_slab(wd_scr, c, n1, n2))
        tfr_ref[pl.ds(r0, n2), :] = xs
        tfi_ref[pl.ds(r0, n2), :] = xd
        return carry

    lax.fori_loop(0, n1, body, 0)


def _two_stage_conv_kernel(vv_ref, x0_ref, skip_ref, tfr_ref, tfi_ref,
                           f1_ref, f2_ref, f2c_ref, g1_ref, out_ref, w_scr):
    n2, two_n1, p = f1_ref.shape
    n1 = two_n1 // 2
    _stage1(vv_ref, f1_ref, w_scr)

    def body2(c, carry):
        r0 = pl.multiple_of(c * n2, n2)
        x = _dot(f2_ref[...], _load_slab(w_scr, c, n1, n2))
        xr, xi = x[:n2], x[n2:]
        tr = tfr_ref[pl.ds(r0, n2), :]
        ti = tfi_ref[pl.ds(r0, n2), :]
        y = jnp.concatenate([xr * tr - xi * ti, xr * ti + xi * tr], axis=0).astype(BF16)
        z = _dot(f2c_ref[...], y)
        w_scr[pl.ds(c, n2, stride=two_n1), :] = z[:n2]
        w_scr[pl.ds(n1 + c, n2, stride=two_n1), :] = z[n2:]
        return carry

    lax.fori_loop(0, n1, body2, 0)

    def body3(b, carry):
        r0 = pl.multiple_of(b * two_n1, two_n1)
        zb = w_scr[pl.ds(r0, two_n1), :].astype(BF16)
        out_ref[pl.ds(b, p, stride=n2), :] = _dot(g1_ref[b], zb)
        return carry

    lax.fori_loop(0, n2, body3, 0)
    vv = vv_ref[...]
    out_ref[...] = (out_ref[...] + vv * skip_ref[...]) * x0_ref[...]


def hy_longconv_two_stage(vv, x0, skip, filt, row0, n_seq, l):
    d = vv.shape[1]
    n = 2 * l
    f1, f2, f2c, g1 = _two_stage_tables(l)
    n2, two_n1, p = f1.shape
    dc = LANES
    tfr, tfi = pl.pallas_call(
        _two_stage_tf_kernel,
        grid=(d // dc,),
        in_specs=[pl.BlockSpec((l, dc), lambda j: (0, j)),
                  pl.BlockSpec((l, dc), lambda j: (0, d // dc + j)),
                  _const_spec(f1.shape), _const_spec(f2.shape)],
        out_specs=[pl.BlockSpec((n, dc), lambda j: (0, j)), pl.BlockSpec((n, dc), lambda j: (0, j))],
        out_shape=[jax.ShapeDtypeStruct((n, d), F32), jax.ShapeDtypeStruct((n, d), F32)],
        scratch_shapes=[pltpu.VMEM((n2 * two_n1, dc), F32), pltpu.VMEM((n2 * two_n1, dc), F32)],
        compiler_params=_cparams(("arbitrary",)),
        name="hy_tf_two_stage",
    )(filt, filt, f1, f2)
    base = row0 // l
    return pl.pallas_call(
        _two_stage_conv_kernel,
        grid=(d // dc, n_seq),
        in_specs=[pl.BlockSpec((l, dc), lambda j, s: (base + s, j)),
                  pl.BlockSpec((l, dc), lambda j, s: (base + s, j)),
                  pl.BlockSpec((1, dc), lambda j, s: (0, j)),
                  pl.BlockSpec((n, dc), lambda j, s: (0, j)),
                  pl.BlockSpec((n, dc), lambda j, s: (0, j)),
                  _const_spec(f1.shape), _const_spec(f2.shape), _const_spec(f2c.shape),
                  _const_spec(g1.shape)],
        out_specs=pl.BlockSpec((l, dc), lambda j, s: (s, j)),
        out_shape=jax.ShapeDtypeStruct((n_seq * l, d), F32),
        scratch_shapes=[pltpu.VMEM((n2 * two_n1, dc), F32)],
        compiler_params=_cparams(("arbitrary", "arbitrary")),
        name="hy_longconv_two_stage",
    )(vv, x0, skip.reshape(1, d).astype(F32), tfr, tfi, f1, f2, f2c, g1)


def _hy_out_kernel(y_ref, w_ref, b_ref, x_ref, mod_ref, out_ref, *, gate_idx):
    y = _dot(y_ref[...].astype(BF16), w_ref[...]) + b_ref[...]
    out_ref[...] = x_ref[...] + mod_ref[0, gate_idx:gate_idx + 1, :] * y


def hy_out(yg, w_out, b_out, x, mod, gate_idx, t_ctx, l_lat):
    t, d = x.shape
    grp = functools.partial(_group_of_tile, tm=TM, t_ctx=t_ctx, l_lat=l_lat)
    row_spec = lambda w: pl.BlockSpec((TM, w), lambda i: (i, 0))
    return pl.pallas_call(
        functools.partial(_hy_out_kernel, gate_idx=gate_idx),
        grid=(t // TM,),
        in_specs=[row_spec(d), _const_spec(w_out.shape), _const_spec((1, d)), row_spec(d),
                  pl.BlockSpec((1,) + mod.shape[1:], lambda i: (grp(i), 0, 0))],
        out_specs=row_spec(d),
        out_shape=jax.ShapeDtypeStruct((t, d), F32),
        compiler_params=_cparams(("arbitrary",)),
        name="hy_out",
    )(yg, w_out, b_out.reshape(1, d).astype(F32), x, mod)


def _grid_pos_embed(rows, d):
    r = np.repeat(np.arange(rows, dtype=np.float64), GRID_W)
    col = np.tile(np.arange(GRID_W, dtype=np.float64), rows)
    quarter = d // 4
    omega = 1.0 / (10000.0 ** (np.arange(quarter, dtype=np.float64) / quarter))
    ang_r = r[:, None] * omega
    ang_c = col[:, None] * omega
    pe = np.concatenate([np.sin(ang_r), np.cos(ang_r), np.sin(ang_c), np.cos(ang_c)], axis=-1)
    return jnp.asarray(pe.astype(np.float32))


def delta_layer(x, g, mod, state_lat, w_in, conv_w, a_log, dt_bias, g_head, w_out, dims):
    t_ctx, l_ctx, n_ctx, l_lat, n_lat = dims
    hk = DN_HEADS * DN_DK
    nqkv = 3 * hk
    w_qkv = w_in[:, :nqkv].astype(BF16)
    w_z = w_in[:, nqkv:nqkv + hk].astype(BF16)
    w_ab = jnp.zeros((w_in.shape[0], LANES), BF16).at[:, :4 * DN_HEADS].set(w_in[:, nqkv + hk:].astype(BF16))
    qkv, z, ab = norm_mod_matmul(x, g, mod, 0, 1, [w_qkv, w_z, w_ab], None, t_ctx, l_lat)
    q, k, v, gc, beta, gct = dn_prep(qkv, ab, conv_w.astype(F32), a_log, dt_bias, t_ctx, l_ctx, l_lat)
    outs = []
    ctx_states = []
    for direction in range(2):
        o_c, s_c = dn_chunk_scan(q, k, v, gc, beta, gct, None, direction=direction, row0=0,
                                 n_seq=n_ctx, seq_len=l_ctx, zero_init=True)
        o_l, _ = dn_chunk_scan(q, k, v, gc, beta, gct, state_lat.astype(F32), direction=direction,
                               row0=t_ctx, n_seq=n_lat, seq_len=l_lat, zero_init=False)
        outs.append(jnp.concatenate([o_c, o_l], axis=0))
        ctx_states.append(s_c)
    x = dn_out(outs[0], outs[1], z, g_head, w_out.astype(BF16), x, mod, 2, t_ctx, l_lat)
    return x, jnp.stack(ctx_states, axis=1)


def hyena_layer(x, g, mod, w_in, b_in, conv_w, conv_b, freq, fw1, fb1, fw2, fb2, fw3, fb3, fw4,
                skip, w_out, b_out, dims):
    t_ctx, l_ctx, n_ctx, l_lat, n_lat = dims
    d = x.shape[1]
    (u,) = norm_mod_matmul(x, g, mod, 0, 1, [w_in.astype(BF16)], [b_in.astype(F32)], t_ctx, l_lat)
    x0, vv = hy_conv_gate(u, conv_w.astype(F32), conv_b, t_ctx, l_ctx, l_lat)
    filt_c = hy_filter(l_ctx, d, freq, fw1, fb1, fw2, fb2, fw3, fb3, fw4)
    filt_l = hy_filter(l_lat, d, freq, fw1, fb1, fw2, fb2, fw3, fb3, fw4)
    y_c = hy_longconv_dense(vv, x0, skip, filt_c, n_ctx, l_ctx)
    y_l = hy_longconv_two_stage(vv, x0, skip, filt_l, t_ctx, n_lat, l_lat)
    yg = jnp.concatenate([y_c, y_l], axis=0)
    return hy_out(yg, w_out.astype(BF16), b_out, x, mod, 2, t_ctx, l_lat)


def kernel(x_prompt, x_sample, state_delta, c, c_ctx, w_ada, b_ada, g_norm, dn_w_in, dn_conv, dn_a_log, dn_dt_bias, dn_g_head, dn_w_out, hy_w_in, hy_b_in, hy_conv, hy_conv_b, hy_freq, hy_f_w1, hy_f_b1, hy_f_w2, hy_f_b2, hy_f_w3, hy_f_b3, hy_f_w4, hy_skip, hy_w_out, hy_b_out, w_router, router_bias, moe_w1, moe_w3, moe_w2, g_final):
    n_ctx, l_ctx, d = x_prompt.shape
    n_lat, l_lat, _ = x_sample.shape
    depth = w_ada.shape[0]
    t_ctx = n_ctx * l_ctx
    dims = (t_ctx, l_ctx, n_ctx, l_lat, n_lat)
    assert l_ctx % TM == 0 and l_lat % TM == 0 and t_ctx % l_lat == 0
    assert n_lat + 1 <= SUBLANES

    pos = _grid_pos_embed(l_lat // GRID_W, d)
    x = jnp.concatenate([x_prompt.reshape(t_ctx, d).astype(F32),
                         (x_sample.astype(F32) + pos[None]).reshape(n_lat * l_lat, d)], axis=0)
    cond = jnp.zeros((SUBLANES, d), F32).at[0].set(c_ctx.astype(F32)).at[1:1 + n_lat].set(c.astype(F32))
    mod_all = ada_modulation(cond, w_ada.astype(F32), b_ada.astype(F32))
    mod_all = mod_all.reshape(depth, SUBLANES, 6, d)

    ctx_states = []
    for i in range(depth):
        mod = mod_all[i]
        j = i // 2
        if i % 2 == 0:
            x, s_ctx = delta_layer(x, g_norm[i, 0], mod, state_delta[:, j], dn_w_in[j], dn_conv[j],
                                   dn_a_log[j], dn_dt_bias[j], dn_g_head[j], dn_w_out[j], dims)
            ctx_states.append(s_ctx.astype(x_prompt.dtype))
        else:
            x = hyena_layer(x, g_norm[i, 0], mod, hy_w_in[j], hy_b_in[j], hy_conv[j], hy_conv_b[j],
                            hy_freq[j], hy_f_w1[j], hy_f_b1[j], hy_f_w2[j], hy_f_b2[j], hy_f_w3[j],
                            hy_f_b3[j], hy_f_w4[j], hy_skip[j], hy_w_out[j], hy_b_out[j], dims)
        x = moe_layer(x, g_norm[i, 1], mod, w_router, router_bias, moe_w1[i], moe_w3[i], moe_w2[i],
                      t_ctx, l_lat, g_final if i == depth - 1 else None)
    y_prompt = x[:t_ctx].reshape(n_ctx, l_ctx, d).astype(x_prompt.dtype)
    y_sample = x[t_ctx:].reshape(n_lat, l_lat, d).astype(x_sample.dtype)
    new_state = jnp.stack(ctx_states, axis=1)
    return (y_prompt, y_sample, new_state)
```

```python
import functools
import math

import numpy as np
import jax
import jax.numpy as jnp
from jax import lax
from jax.experimental import pallas as pl
from jax.experimental.pallas import tpu as pltpu

F32 = jnp.float32
BF16 = jnp.bfloat16
HI = lax.Precision.HIGHEST

EPS = 1e-6
GRID_W = 64
DN_HEADS = 8
DN_DK = 128
DN_DV = 128
DN_CHUNK = 64
HY_EMB = 33
HY_ORDER = 64
HY_DECAY_SHORT_PCT = 0.3
HY_DECAY_LONG_PCT = 1.5
HY_DECAY_TARGET = 1e-2
N_EXPERTS = 32
N_GROUPS = 4
EXPERTS_PER_GROUP = N_EXPERTS // N_GROUPS

LANES = 128
SUBLANES = 8
TM = 256
MOE_TILE = 256
FFT_N2 = 128
VMEM_LIMIT = 48 * 1024 * 1024


def _cparams(sem):
    return pltpu.CompilerParams(dimension_semantics=sem, vmem_limit_bytes=VMEM_LIMIT)


def _sigmoid(x):
    return 1.0 / (1.0 + jnp.exp(-x))


def _silu(x):
    return x * _sigmoid(x)


def _dot(a, b, precision=None):
    return jnp.dot(a, b, preferred_element_type=F32, precision=precision)


def _const_spec(shape):
    nd = len(shape)
    return pl.BlockSpec(shape, lambda *_: (0,) * nd)


def _ada_kernel(cond_ref, w_ref, b_ref, out_ref):
    s = _silu(cond_ref[...])
    out_ref[0] = _dot(s, w_ref[0], HI) + b_ref[0]


def ada_modulation(cond, w_ada, b_ada):
    depth, d, n = w_ada.shape
    tn = 1536
    return pl.pallas_call(
        _ada_kernel,
        grid=(depth, n // tn),
        in_specs=[
            pl.BlockSpec((SUBLANES, d), lambda i, j: (0, 0)),
            pl.BlockSpec((1, d, tn), lambda i, j: (i, 0, j)),
            pl.BlockSpec((1, 1, tn), lambda i, j: (i, 0, j)),
        ],
        out_specs=pl.BlockSpec((1, SUBLANES, tn), lambda i, j: (i, 0, j)),
        out_shape=jax.ShapeDtypeStruct((depth, SUBLANES, n), F32),
        compiler_params=_cparams(("arbitrary", "arbitrary")),
        name="ada_modulation",
    )(cond, w_ada, b_ada.reshape(depth, 1, n))


def _norm_mod(x, g, mod_ref, shift_idx, scale_idx):
    ms = jnp.mean(x * x, axis=-1, keepdims=True)
    y = x * lax.rsqrt(ms + EPS) * g
    scale = mod_ref[0, scale_idx:scale_idx + 1, :]
    shift = mod_ref[0, shift_idx:shift_idx + 1, :]
    return y * (1.0 + scale) + shift


def _group_of_tile(i, tm, t_ctx, l_lat):
    r = i * tm
    return jnp.where(r < t_ctx, 0, 1 + (r - t_ctx) // l_lat)


def _nmm_kernel(x_ref, g_ref, mod_ref, *refs, n_out, has_bias, shift_idx, scale_idx):
    h = _norm_mod(x_ref[...], g_ref[...], mod_ref, shift_idx, scale_idx).astype(BF16)
    w_refs = refs[:n_out]
    b_refs = refs[n_out:n_out + (n_out if has_bias else 0)]
    o_refs = refs[n_out + len(b_refs):]
    for k in range(n_out):
        acc = _dot(h, w_refs[k][...])
        if has_bias:
            acc = acc + b_refs[k][...]
        o_refs[k][...] = acc


def norm_mod_matmul(x, g, mod, shift_idx, scale_idx, weights, biases, t_ctx, l_lat):
    t, d = x.shape
    n_out = len(weights)
    has_bias = biases is not None
    grp = functools.partial(_group_of_tile, tm=TM, t_ctx=t_ctx, l_lat=l_lat)
    in_specs = [
        pl.BlockSpec((TM, d), lambda i: (i, 0)),
        _const_spec((1, d)),
        pl.BlockSpec((1,) + mod.shape[1:], lambda i: (grp(i), 0, 0)),
    ]
    args = [x, g.reshape(1, d), mod]
    for w in weights:
        in_specs.append(_const_spec(w.shape))
        args.append(w)
    if has_bias:
        for b in biases:
            in_specs.append(_const_spec((1, b.shape[-1])))
            args.append(b.reshape(1, -1))
    out_specs = [pl.BlockSpec((TM, w.shape[1]), lambda i: (i, 0)) for w in weights]
    out_shape = [jax.ShapeDtypeStruct((t, w.shape[1]), F32) for w in weights]
    return pl.pallas_call(
        functools.partial(_nmm_kernel, n_out=n_out, has_bias=has_bias,
                          shift_idx=shift_idx, scale_idx=scale_idx),
        grid=(t // TM,),
        in_specs=in_specs,
        out_specs=out_specs,
        out_shape=out_shape,
        compiler_params=_cparams(("arbitrary",)),
        name="norm_mod_matmul",
    )(*args)


def _seq_flags(i, tm, t_ctx, l_ctx, l_lat):
    r = i * tm
    in_ctx = r < t_ctx
    pos = jnp.where(in_ctx, r % l_ctx, (r - t_ctx) % l_lat)
    length = jnp.where(in_ctx, l_ctx, l_lat)
    return pos != 0, (pos + tm) != length


def _conv3(x, prev8, next8, w_ref, has_prev, has_next):
    tm = x.shape[0]
    row = lax.broadcasted_iota(jnp.int32, (tm, 1), 0)
    halo_p = jnp.where(has_prev, prev8[SUBLANES - 1:SUBLANES, :], 0.0)
    halo_n = jnp.where(has_next, next8[0:1, :], 0.0)
    x_prev = jnp.where(row == 0, halo_p, pltpu.roll(x, 1, 0))
    x_next = jnp.where(row == tm - 1, halo_n, pltpu.roll(x, tm - 1, 0))
    return w_ref[0:1, :] * x_prev + w_ref[1:2, :] * x + w_ref[2:3, :] * x_next


def _halo_specs(c, tm, t):
    nb8 = t // SUBLANES
    per = tm // SUBLANES
    return [
        pl.BlockSpec((tm, c), lambda i: (i, 0)),
        pl.BlockSpec((SUBLANES, c), lambda i: (jnp.maximum(i * per - 1, 0), 0)),
        pl.BlockSpec((SUBLANES, c), lambda i: (jnp.minimum((i + 1) * per, nb8 - 1), 0)),
    ]


def _softplus(x):
    return jnp.maximum(x, 0.0) + jnp.log(1.0 + jnp.exp(-jnp.abs(x)))


def _dn_prep_kernel(x_ref, xp_ref, xn_ref, cw_ref, ab_ref, nega_ref, dtb_ref, tri_ref,
                    q_ref, k_ref, v_ref, gc_ref, beta_ref, gct_ref,
                    *, t_ctx, l_ctx, l_lat):
    i = pl.program_id(0)
    has_prev, has_next = _seq_flags(i, TM, t_ctx, l_ctx, l_lat)
    c = _conv3(x_ref[...], xp_ref[...], xn_ref[...], cw_ref, has_prev, has_next)
    s = _silu(c)
    hk = DN_HEADS * DN_DK
    for h in range(DN_HEADS):
        lo, hi = h * DN_DK, (h + 1) * DN_DK
        qh = s[:, lo:hi]
        kh = s[:, hk + lo:hk + hi]
        qn = lax.rsqrt(jnp.sum(qh * qh, axis=-1, keepdims=True) + EPS)
        kn = lax.rsqrt(jnp.sum(kh * kh, axis=-1, keepdims=True) + EPS)
        q_ref[:, lo:hi] = qh * (qn * (DN_DK ** -0.5))
        k_ref[:, lo:hi] = kh * kn
    v_ref[...] = s[:, 2 * hk:]
    ab = ab_ref[...]
    lg = nega_ref[...] * _softplus(ab + dtb_ref[...])
    beta_ref[...] = _sigmoid(ab)
    nch = TM // DN_CHUNK
    for d in range(2):
        gc = _dot(tri_ref[d], lg, HI)
        gc_ref[d] = gc
        gct = gc.T
        for ch in range(nch):
            gct_ref[d, ch] = gct[0:2 * DN_HEADS, ch * DN_CHUNK:(ch + 1) * DN_CHUNK]


def _chunk_tri():
    idx = np.arange(TM)
    same = (idx[:, None] // DN_CHUNK) == (idx[None, :] // DN_CHUNK)
    fwd = same & (idx[None, :] <= idx[:, None])
    bwd = same & (idx[None, :] >= idx[:, None])
    return jnp.asarray(np.stack([fwd, bwd]).astype(np.float32))


def dn_prep(qkv, ab, conv_w, a_log, dt_bias, t_ctx, l_ctx, l_lat):
    t, c = qkv.shape
    hd = DN_HEADS * DN_DK
    nega = jnp.zeros((1, LANES), F32).at[0, :2 * DN_HEADS].set(-jnp.exp(a_log.astype(F32)).reshape(-1))
    dtb = jnp.zeros((1, LANES), F32).at[0, :2 * DN_HEADS].set(dt_bias.astype(F32).reshape(-1))
    cw = jnp.zeros((SUBLANES, c), F32).at[:conv_w.shape[0]].set(conv_w)
    nch = TM // DN_CHUNK
    row_spec = lambda w: pl.BlockSpec((TM, w), lambda i: (i, 0))
    return pl.pallas_call(
        functools.partial(_dn_prep_kernel, t_ctx=t_ctx, l_ctx=l_ctx, l_lat=l_lat),
        grid=(t // TM,),
        in_specs=_halo_specs(c, TM, t) + [
            _const_spec((SUBLANES, c)),
            row_spec(LANES),
            _const_spec((1, LANES)),
            _const_spec((1, LANES)),
            _const_spec((2, TM, TM)),
        ],
        out_specs=[
            row_spec(hd), row_spec(hd), row_spec(hd),
            pl.BlockSpec((2, TM, LANES), lambda i: (0, i, 0)),
            row_spec(LANES),
            pl.BlockSpec((2, nch, 2 * DN_HEADS, DN_CHUNK), lambda i: (0, i, 0, 0)),
        ],
        out_shape=[
            jax.ShapeDtypeStruct((t, hd), F32),
            jax.ShapeDtypeStruct((t, hd), F32),
            jax.ShapeDtypeStruct((t, hd), F32),
            jax.ShapeDtypeStruct((2, t, LANES), F32),
            jax.ShapeDtypeStruct((t, LANES), F32),
            jax.ShapeDtypeStruct((2, t // DN_CHUNK, 2 * DN_HEADS, DN_CHUNK), F32),
        ],
        compiler_params=_cparams(("arbitrary",)),
        name="dn_prep",
    )(qkv, qkv, qkv, cw, ab, nega, dtb, _chunk_tri())


def _bmm(a, b):
    return _dot(a.astype(BF16), b.astype(BF16))


def _inv_unit_tri_batch(lms, in_blk, eye):
    dgs = [jnp.where(in_blk, lm, 0.0) for lm in lms]
    offs = [lm - dg for lm, dg in zip(lms, dgs)]
    n1 = [-dg for dg in dgs]
    n2 = [_bmm(a, a) for a in n1]
    p = [eye + a for a in n1]
    n4 = [_bmm(a, a) for a in n2]
    p = [x + _bmm(x, a) for x, a in zip(p, n2)]
    n8 = [_bmm(a, a) for a in n4]
    p = [x + _bmm(x, a) for x, a in zip(p, n4)]
    p = [x + _bmm(x, a) for x, a in zip(p, n8)]
    m = [_bmm(x, o) for x, o in zip(p, offs)]
    m2 = [_bmm(a, a) for a in m]
    t1 = [eye - a for a in m]
    t1 = [x + _bmm(x, a) for x, a in zip(t1, m2)]
    return [_bmm(x, y) for x, y in zip(t1, p)]


def _dn_chunk_kernel(q_ref, k_ref, v_ref, gc_ref, beta_ref, gct_ref, s0_ref,
                     o_ref, sfin_ref, s_scr, attn_scr, u_scr, qw_scr, kd_scr, gl_scr,
                     *, direction, nblk, zero_init):
    j = pl.program_id(1)
    c_sz = DN_CHUNK
    nch = TM // c_sz
    heads = range(DN_HEADS)

    @pl.when(j == 0)
    def _():
        if zero_init:
            s_scr[...] = jnp.zeros_like(s_scr)
        else:
            s_scr[...] = s0_ref[0, 0]

    row = lax.broadcasted_iota(jnp.int32, (c_sz, c_sz), 0)
    col = lax.broadcasted_iota(jnp.int32, (c_sz, c_sz), 1)
    if direction == 0:
        incl, strict = row >= col, row > col
        last = c_sz - 1
    else:
        incl, strict = row <= col, row < col
        last = 0
    in_blk = (row // 16) == (col // 16)
    eye = (row == col).astype(F32)

    def lanes(h):
        return slice(h * DN_DK, (h + 1) * DN_DK)

    def prep_body(ch, carry):
        r0 = pl.multiple_of(ch * c_sz, c_sz)
        rows = pl.ds(r0, c_sz)
        gct = gct_ref[0, ch]
        gls = [direction * DN_HEADS + h for h in heads]
        qs = [q_ref[rows, lanes(h)] for h in heads]
        ks = [k_ref[rows, lanes(h)] for h in heads]
        gcc = [gc_ref[0, rows, g:g + 1] for g in gls]
        beta = [beta_ref[rows, 2 * DN_HEADS + g:2 * DN_HEADS + g + 1] for g in gls]
        dec = [jnp.exp(jnp.where(incl, gcc[h] - gct[gls[h]:gls[h] + 1, :], -jnp.inf)) for h in heads]
        kb = [ks[h] * beta[h] for h in heads]
        prod = [lax.dot_general(jnp.concatenate([qs[h], kb[h]], axis=0).astype(BF16), ks[h].astype(BF16),
                                (((1,), (1,)), ((), ())), preferred_element_type=F32)
                for h in heads]
        for h in heads:
            attn_scr[ch, h] = (prod[h][:c_sz] * dec[h]).astype(BF16)
        lms = [jnp.where(strict, prod[h][c_sz:] * dec[h], 0.0) for h in heads]
        tinv = _inv_unit_tri_batch(lms, in_blk, eye)
        eg = [jnp.exp(gcc[h]) for h in heads]
        uw = [_bmm(tinv[h], jnp.concatenate([v_ref[rows, lanes(h)] * beta[h], kb[h] * eg[h]], axis=1))
              for h in heads]
        for h in heads:
            u_scr[rows, lanes(h)] = uw[h][:, :DN_DV]
            qw_scr[ch, h] = jnp.concatenate([qs[h] * eg[h], uw[h][:, DN_DV:]], axis=0).astype(BF16)
            g_last = gcc[h][last:last + 1, :]
            kd_scr[ch, h] = (ks[h] * jnp.exp(g_last - gcc[h])).T.astype(BF16)
            gl_scr[ch, h] = jnp.broadcast_to(jnp.exp(g_last), (1, DN_DV))
        return carry

    lax.fori_loop(0, nch, prep_body, 0)

    def scan_body(ci, carry):
        ch = ci if direction == 0 else nch - 1 - ci
        r0 = pl.multiple_of(ch * c_sz, c_sz)
        rows = pl.ds(r0, c_sz)
        s = [s_scr[h] for h in heads]
        qs_ws = [_dot(qw_scr[ch, h], s[h].astype(BF16)) for h in heads]
        vb = [(u_scr[rows, lanes(h)] - qs_ws[h][c_sz:]).astype(BF16) for h in heads]
        for h in heads:
            o_ref[rows, lanes(h)] = qs_ws[h][:c_sz] + _dot(attn_scr[ch, h], vb[h])
            s_scr[h] = s[h] * gl_scr[ch, h] + _dot(kd_scr[ch, h], vb[h])
        return carry

    lax.fori_loop(0, nch, scan_body, 0)

    @pl.when(j == nblk - 1)
    def _():
        sfin_ref[0, 0] = s_scr[...]


def dn_chunk_scan(q, k, v, gc, beta, gct, s0, *, direction, row0, n_seq, seq_len, zero_init):
    t, hd = q.shape
    nblk = seq_len // TM
    base = row0 // TM
    nch = TM // DN_CHUNK

    def blk(s, j):
        jj = j if direction == 0 else nblk - 1 - j
        return base + s * nblk + jj

    row_spec = lambda w: pl.BlockSpec((TM, w), lambda s, j: (blk(s, j), 0))
    if zero_init:
        s0 = jnp.zeros((1, 1, DN_HEADS, DN_DK, DN_DV), F32)
        s0_spec = pl.BlockSpec((1, 1, DN_HEADS, DN_DK, DN_DV), lambda s, j: (0, 0, 0, 0, 0))
    else:
        s0_spec = pl.BlockSpec((1, 1, DN_HEADS, DN_DK, DN_DV), lambda s, j: (s, direction, 0, 0, 0))
    o, sfin = pl.pallas_call(
        functools.partial(_dn_chunk_kernel, direction=direction, nblk=nblk, zero_init=zero_init),
        grid=(n_seq, nblk),
        in_specs=[
            row_spec(hd), row_spec(hd), row_spec(hd),
            pl.BlockSpec((1, TM, LANES), lambda s, j: (direction, blk(s, j), 0)),
            row_spec(LANES),
            pl.BlockSpec((1, nch, 2 * DN_HEADS, DN_CHUNK), lambda s, j: (direction, blk(s, j), 0, 0)),
            s0_spec,
        ],
        out_specs=[
            pl.BlockSpec((TM, hd), lambda s, j: (s * nblk + (j if direction == 0 else nblk - 1 - j), 0)),
            pl.BlockSpec((1, 1, DN_HEADS, DN_DK, DN_DV), lambda s, j: (s, 0, 0, 0, 0)),
        ],
        out_shape=[
            jax.ShapeDtypeStruct((n_seq * seq_len, hd), F32),
            jax.ShapeDtypeStruct((n_seq, 1, DN_HEADS, DN_DK, DN_DV), F32),
        ],
        scratch_shapes=[
            pltpu.VMEM((DN_HEADS, DN_DK, DN_DV), F32),
            pltpu.VMEM((nch, DN_HEADS, DN_CHUNK, DN_CHUNK), BF16),
            pltpu.VMEM((TM, hd), F32),
            pltpu.VMEM((nch, DN_HEADS, 2 * DN_CHUNK, DN_DK), BF16),
            pltpu.VMEM((nch, DN_HEADS, DN_DK, DN_CHUNK), BF16),
            pltpu.VMEM((nch, DN_HEADS, 1, DN_DV), F32),
        ],
        compiler_params=_cparams(("arbitrary", "arbitrary")),
        name="dn_chunk_scan_d%d" % direction,
    )(q, k, v, gc, beta, gct, s0)
    return o, sfin[:, 0]


def _dn_out_kernel(of_ref, ob_ref, z_ref, gh_ref, w_ref, x_ref, mod_ref, out_ref, *, gate_idx):
    o = of_ref[...] + ob_ref[...]
    z = z_ref[...]
    parts = []
    for h in range(DN_HEADS):
        lanes = slice(h * DN_DV, (h + 1) * DN_DV)
        oh = o[:, lanes]
        ms = jnp.mean(oh * oh, axis=-1, keepdims=True)
        parts.append(oh * lax.rsqrt(ms + EPS) * gh_ref[...])
    on = jnp.concatenate(parts, axis=1) * _silu(z)
    y = _dot(on.astype(BF16), w_ref[...])
    out_ref[...] = x_ref[...] + mod_ref[0, gate_idx:gate_idx + 1, :] * y


def dn_out(o_f, o_b, z, g_head, w_out, x, mod, gate_idx, t_ctx, l_lat):
    t, d = x.shape
    hd = o_f.shape[1]
    grp = functools.partial(_group_of_tile, tm=TM, t_ctx=t_ctx, l_lat=l_lat)
    row_spec = lambda w: pl.BlockSpec((TM, w), lambda i: (i, 0))
    return pl.pallas_call(
        functools.partial(_dn_out_kernel, gate_idx=gate_idx),
        grid=(t // TM,),
        in_specs=[
            row_spec(hd), row_spec(hd), row_spec(hd),
            _const_spec((1, DN_DV)),
            _const_spec(w_out.shape),
            row_spec(d),
            pl.BlockSpec((1,) + mod.shape[1:], lambda i: (grp(i), 0, 0)),
        ],
        out_specs=row_spec(d),
        out_shape=jax.ShapeDtypeStruct((t, d), F32),
        compiler_params=_cparams(("arbitrary",)),
        name="dn_out",
    )(o_f, o_b, z, g_head.reshape(1, DN_DV).astype(F32), w_out, x, mod)


def _first_argmax(vals, lane, valid):
    masked = jnp.where(valid, vals, -jnp.inf)
    m = jnp.max(masked, axis=-1, keepdims=True)
    idx = jnp.min(jnp.where(valid & (masked == m), lane, LANES), axis=-1, keepdims=True)
    return m, idx


def _router_kernel(x_ref, g_ref, mod_ref, wr_ref, rb_ref, h_ref, idx_ref, wt_ref,
                   *, shift_idx, scale_idx):
    h = _norm_mod(x_ref[...], g_ref[...], mod_ref, shift_idx, scale_idx)
    h_ref[...] = h.astype(BF16)
    scores = _sigmoid(_dot(h, wr_ref[...], HI))
    sel = scores + rb_ref[...]
    tm = scores.shape[0]
    lane = lax.broadcasted_iota(jnp.int32, (tm, LANES), 1)
    best_score = None
    for g in range(N_GROUPS):
        in_g = (lane >= g * EXPERTS_PER_GROUP) & (lane < (g + 1) * EXPERTS_PER_GROUP)
        m1, i1 = _first_argmax(sel, lane, in_g)
        m2, i2 = _first_argmax(sel, lane, in_g & (lane != i1))
        gs = m1 + m2
        if best_score is None:
            best_score, b1, b2 = gs, i1, i2
        else:
            better = gs > best_score
            best_score = jnp.where(better, gs, best_score)
            b1 = jnp.where(better, i1, b1)
            b2 = jnp.where(better, i2, b2)
    w1 = jnp.sum(jnp.where(lane == b1, scores, 0.0), axis=-1, keepdims=True)
    w2 = jnp.sum(jnp.where(lane == b2, scores, 0.0), axis=-1, keepdims=True)
    tot = w1 + w2
    idx_ref[...] = jnp.where(lane == 0, b1, jnp.where(lane == 1, b2, 0))
    wt_ref[...] = jnp.where(lane == 0, w1 / tot, jnp.where(lane == 1, w2 / tot, 0.0))


def moe_router(x, g, mod, shift_idx, scale_idx, w_router, router_bias, t_ctx, l_lat):
    t, d = x.shape
    grp = functools.partial(_group_of_tile, tm=TM, t_ctx=t_ctx, l_lat=l_lat)
    wr = jnp.zeros((d, LANES), F32).at[:, :N_EXPERTS].set(w_router.astype(F32))
    rb = jnp.zeros((1, LANES), F32).at[0, :N_EXPERTS].set(router_bias.astype(F32))
    row_spec = lambda w: pl.BlockSpec((TM, w), lambda i: (i, 0))
    return pl.pallas_call(
        functools.partial(_router_kernel, shift_idx=shift_idx, scale_idx=scale_idx),
        grid=(t // TM,),
        in_specs=[
            row_spec(d),
            _const_spec((1, d)),
            pl.BlockSpec((1,) + mod.shape[1:], lambda i: (grp(i), 0, 0)),
            _const_spec((d, LANES)),
            _const_spec((1, LANES)),
        ],
        out_specs=[row_spec(d), row_spec(LANES), row_spec(LANES)],
        out_shape=[
            jax.ShapeDtypeStruct((t, d), BF16),
            jax.ShapeDtypeStruct((t, LANES), jnp.int32),
            jax.ShapeDtypeStruct((t, LANES), F32),
        ],
        compiler_params=_cparams(("arbitrary",)),
        name="moe_router",
    )(x, g.reshape(1, d), mod, wr, rb)


def _expert_kernel(te_ref, nt_ref, xs_ref, gw_ref, w1_ref, w3_ref, w2_ref, ys_ref):
    i = pl.program_id(0)

    @pl.when(i < nt_ref[0])
    def _():
        xs = xs_ref[...]
        a = _dot(xs, w1_ref[0].astype(BF16))
        b = _dot(xs, w3_ref[0].astype(BF16))
        hid = (_silu(a) * b).astype(BF16)
        ys_ref[...] = gw_ref[...] * _dot(hid, w2_ref[0].astype(BF16))

    @pl.when(i >= nt_ref[0])
    def _():
        ys_ref[...] = jnp.zeros_like(ys_ref)


def moe_experts(xs, gw, tile_expert, n_tiles, w1, w3, w2):
    p, d = xs.shape
    de = w1.shape[-1]
    ntile = p // MOE_TILE
    grid_spec = pltpu.PrefetchScalarGridSpec(
        num_scalar_prefetch=2,
        grid=(ntile,),
        in_specs=[
            pl.BlockSpec((MOE_TILE, d), lambda i, te, nt: (i, 0)),
            pl.BlockSpec((MOE_TILE, 1), lambda i, te, nt: (i, 0)),
            pl.BlockSpec((1, d, de), lambda i, te, nt: (te[i], 0, 0)),
            pl.BlockSpec((1, d, de), lambda i, te, nt: (te[i], 0, 0)),
            pl.BlockSpec((1, de, d), lambda i, te, nt: (te[i], 0, 0)),
        ],
        out_specs=pl.BlockSpec((MOE_TILE, d), lambda i, te, nt: (i, 0)),
    )
    return pl.pallas_call(
        _expert_kernel,
        grid_spec=grid_spec,
        out_shape=jax.ShapeDtypeStruct((p, d), F32),
        compiler_params=_cparams(("arbitrary",)),
        name="moe_experts",
    )(tile_expert, n_tiles, xs, gw, w1, w3, w2)


def moe_dispatch_plan(idx, wts):
    t = idx.shape[0]
    n_assign = 2 * t
    p = n_assign + N_EXPERTS * MOE_TILE
    p = -(-p // MOE_TILE) * MOE_TILE
    e_flat = idx.reshape(-1)
    onehot = (e_flat[:, None] == jnp.arange(N_EXPERTS, dtype=jnp.int32)[None, :]).astype(jnp.int32)
    rank = jnp.cumsum(onehot, axis=0) - onehot
    counts = jnp.sum(onehot, axis=0)
    padded = ((counts + MOE_TILE - 1) // MOE_TILE) * MOE_TILE
    starts = jnp.cumsum(padded) - padded
    slot = jnp.sum(onehot * (starts[None, :] + rank), axis=1)
    token = jnp.arange(n_assign, dtype=jnp.int32) // 2
    slot_token = jnp.zeros((p,), jnp.int32).at[slot].set(token)
    slot_gate = jnp.zeros((p,), F32).at[slot].set(wts.reshape(-1))
    ends = jnp.cumsum(padded)
    tile_start = jnp.arange(p // MOE_TILE, dtype=jnp.int32) * MOE_TILE
    tile_expert = jnp.sum((tile_start[:, None] >= ends[None, :]).astype(jnp.int32), axis=1)
    n_tiles = (ends[-1] // MOE_TILE).astype(jnp.int32).reshape(1)
    last_used = jnp.maximum(n_tiles[0] - 1, 0)
    tile_expert = jnp.where(jnp.arange(p // MOE_TILE) < n_tiles[0], tile_expert,
                            tile_expert[last_used]).astype(jnp.int32)
    return slot_token, slot_gate.reshape(p, 1), tile_expert, n_tiles, slot.reshape(t, 2)


def _combine_kernel(x_ref, y0_ref, y1_ref, mod_ref, *refs, gate_idx, final_norm):
    x = x_ref[...] + mod_ref[0, gate_idx:gate_idx + 1, :] * (y0_ref[...] + y1_ref[...])
    if final_norm:
        gf_ref, out_ref = refs
        ms = jnp.mean(x * x, axis=-1, keepdims=True)
        out_ref[...] = x * lax.rsqrt(ms + EPS) * gf_ref[...]
    else:
        (out_ref,) = refs
        out_ref[...] = x


def moe_combine(x, y0, y1, mod, gate_idx, t_ctx, l_lat, g_final=None):
    t, d = x.shape
    grp = functools.partial(_group_of_tile, tm=TM, t_ctx=t_ctx, l_lat=l_lat)
    row_spec = lambda w: pl.BlockSpec((TM, w), lambda i: (i, 0))
    in_specs = [row_spec(d), row_spec(d), row_spec(d),
                pl.BlockSpec((1,) + mod.shape[1:], lambda i: (grp(i), 0, 0))]
    args = [x, y0, y1, mod]
    if g_final is not None:
        in_specs.append(_const_spec((1, d)))
        args.append(g_final.reshape(1, d).astype(F32))
    return pl.pallas_call(
        functools.partial(_combine_kernel, gate_idx=gate_idx, final_norm=g_final is not None),
        grid=(t // TM,),
        in_specs=in_specs,
        out_specs=row_spec(d),
        out_shape=jax.ShapeDtypeStruct((t, d), F32),
        compiler_params=_cparams(("arbitrary",)),
        name="moe_combine",
    )(*args)


def moe_layer(x, g, mod, w_router, router_bias, w1, w3, w2, t_ctx, l_lat, g_final=None):
    h, idx_pad, wt_pad = moe_router(x, g, mod, 3, 4, w_router, router_bias, t_ctx, l_lat)
    idx = idx_pad[:, :2]
    wts = wt_pad[:, :2]
    slot_token, slot_gate, tile_expert, n_tiles, slot = moe_dispatch_plan(idx, wts)
    xs = jnp.take(h, slot_token, axis=0)
    ys = moe_experts(xs, slot_gate, tile_expert, n_tiles, w1, w3, w2)
    y0 = jnp.take(ys, slot[:, 0], axis=0)
    y1 = jnp.take(ys, slot[:, 1], axis=0)
    return moe_combine(x, y0, y1, mod, 5, t_ctx, l_lat, g_final)


def _hy_conv_kernel(u_ref, up_ref, un_ref, cw_ref, cb_ref, x0_ref, vv_ref, *, t_ctx, l_ctx, l_lat):
    i = pl.program_id(0)
    has_prev, has_next = _seq_flags(i, TM, t_ctx, l_ctx, l_lat)
    c = _conv3(u_ref[...], up_ref[...], un_ref[...], cw_ref, has_prev, has_next) + cb_ref[...]
    d = c.shape[1] // 3
    x0_ref[...] = c[:, :d]
    vv_ref[...] = c[:, 2 * d:] * c[:, d:2 * d]


def hy_conv_gate(u, conv_w, conv_b, t_ctx, l_ctx, l_lat):
    t, c = u.shape
    d = c // 3
    cw = jnp.zeros((SUBLANES, c), F32).at[:conv_w.shape[0]].set(conv_w)
    row_spec = lambda w: pl.BlockSpec((TM, w), lambda i: (i, 0))
    return pl.pallas_call(
        functools.partial(_hy_conv_kernel, t_ctx=t_ctx, l_ctx=l_ctx, l_lat=l_lat),
        grid=(t // TM,),
        in_specs=_halo_specs(c, TM, t) + [_const_spec((SUBLANES, c)), _const_spec((1, c))],
        out_specs=[row_spec(d), row_spec(d)],
        out_shape=[jax.ShapeDtypeStruct((t, d), F32), jax.ShapeDtypeStruct((t, d), F32)],
        compiler_params=_cparams(("arbitrary",)),
        name="hy_conv_gate",
    )(u, u, u, cw, conv_b.reshape(1, c).astype(F32))


def _hy_features(n):
    t = np.linspace(0.0, 1.0, n)[:, None]
    bands = (HY_EMB - 1) // 2
    band_w = np.linspace(1e-4, bands - 1, bands)[None, :]
    ang = (2.0 * math.pi / n) * np.arange(n)[:, None] * band_w
    z = np.concatenate([t, np.cos(ang), -np.sin(ang)], axis=-1)
    zp = np.zeros((n, LANES))
    zp[:, :HY_EMB] = z
    return jnp.asarray(zp.astype(np.float32))


def _hy_hid_kernel(z_ref, f_ref, w1_ref, b1_ref, w2_ref, b2_ref, w3_ref, b3_ref, hid_ref):
    h = jnp.sin(f_ref[0:1, :] * (_dot(z_ref[...], w1_ref[...], HI) + b1_ref[...]))
    h = jnp.sin(f_ref[1:2, :] * (_dot(h, w2_ref[...], HI) + b2_ref[...]))
    h = jnp.sin(f_ref[2:3, :] * (_dot(h, w3_ref[...], HI) + b3_ref[...]))
    hid_ref[...] = h


def _hy_filt_kernel(hid_ref, w4_ref, z_ref, dl_ref, filt_ref):
    f = _dot(hid_ref[...], w4_ref[...], HI)
    f = f * jnp.exp(-z_ref[:, 0:1] * jnp.abs(dl_ref[...]))
    nrm = jnp.sum(jnp.abs(f), axis=0, keepdims=True)
    filt_ref[...] = f / nrm


def hy_filter(n, d, freq, w1, b1, w2, b2, w3, b3, w4):
    z = _hy_features(n)
    o = HY_ORDER
    w1p = jnp.zeros((LANES, o), F32).at[:HY_EMB].set(w1.astype(F32))
    fp = jnp.zeros((SUBLANES, o), F32).at[:3].set(freq.astype(F32))
    tr = min(n, TM)
    hid = pl.pallas_call(
        _hy_hid_kernel,
        grid=(n // tr,),
        in_specs=[pl.BlockSpec((tr, LANES), lambda i: (i, 0)), _const_spec((SUBLANES, o)),
                  _const_spec((LANES, o)), _const_spec((1, o)),
                  _const_spec((o, o)), _const_spec((1, o)),
                  _const_spec((o, o)), _const_spec((1, o))],
        out_specs=pl.BlockSpec((tr, o), lambda i: (i, 0)),
        out_shape=jax.ShapeDtypeStruct((n, o), F32),
        compiler_params=_cparams(("arbitrary",)),
        name="hy_filter_mlp",
    )(z, fp, w1p, b1.reshape(1, o).astype(F32), w2.astype(F32), b2.reshape(1, o).astype(F32),
      w3.astype(F32), b3.reshape(1, o).astype(F32))
    max_decay = math.log(HY_DECAY_TARGET) / HY_DECAY_SHORT_PCT
    min_decay = math.log(HY_DECAY_TARGET) / HY_DECAY_LONG_PCT
    deltas = np.tile(np.linspace(min_decay, max_decay, d), 2).astype(np.float32)[None, :]
    tc = 256
    return pl.pallas_call(
        _hy_filt_kernel,
        grid=(2 * d // tc,),
        in_specs=[_const_spec((n, o)), pl.BlockSpec((o, tc), lambda j: (0, j)),
                  _const_spec((n, LANES)), pl.BlockSpec((1, tc), lambda j: (0, j))],
        out_specs=pl.BlockSpec((n, tc), lambda j: (0, j)),
        out_shape=jax.ShapeDtypeStruct((n, 2 * d), F32),
        compiler_params=_cparams(("arbitrary",)),
        name="hy_filter_window",
    )(hid, w4.astype(F32), z, jnp.asarray(deltas))


def _dense_dft_tables(l):
    n = 2 * l
    k = np.arange(n)[:, None]
    j = np.arange(l)[None, :]
    th = 2.0 * math.pi * ((k * j) % n) / n
    fwd = np.concatenate([np.cos(th), -np.sin(th)], axis=0)
    inv = np.concatenate([np.cos(th).T, -np.sin(th).T], axis=1) / n
    return (jnp.asarray(fwd.astype(np.float32)).astype(BF16),
            jnp.asarray(inv.astype(np.float32)).astype(BF16))


def _dense_tf_kernel(hf_ref, hb_ref, fwd_ref, tf_ref):
    n = fwd_ref.shape[0] // 2
    hf = hf_ref[...]
    hb = hb_ref[...]
    tf_ref[0:n, :] = _dot(fwd_ref[0:n, :], (hf + hb).astype(BF16))
    tf_ref[n:, :] = _dot(fwd_ref[n:, :], (hf - hb).astype(BF16))


def _dense_conv_kernel(vv_ref, x0_ref, skip_ref, tf_ref, fwd_ref, inv_ref, out_ref):
    n = fwd_ref.shape[0] // 2
    vv = vv_ref[...]
    x = _dot(fwd_ref[...], vv.astype(BF16))
    xr, xi = x[:n], x[n:]
    tr, ti = tf_ref[0:n, :], tf_ref[n:, :]
    y = jnp.concatenate([xr * tr - xi * ti, xr * ti + xi * tr], axis=0).astype(BF16)
    conv = _dot(inv_ref[...], y)
    out_ref[...] = (conv + vv * skip_ref[...]) * x0_ref[...]


def hy_longconv_dense(vv, x0, skip, filt, n_seq, l):
    d = vv.shape[1]
    n = 2 * l
    fwd, inv = _dense_dft_tables(l)
    dc = 512
    tf = pl.pallas_call(
        _dense_tf_kernel,
        grid=(d // dc,),
        in_specs=[pl.BlockSpec((l, dc), lambda j: (0, j)),
                  pl.BlockSpec((l, dc), lambda j: (0, d // dc + j)),
                  _const_spec(fwd.shape)],
        out_specs=pl.BlockSpec((2 * n, dc), lambda j: (0, j)),
        out_shape=jax.ShapeDtypeStruct((2 * n, d), F32),
        compiler_params=_cparams(("arbitrary",)),
        name="hy_tf_dense",
    )(filt, filt, fwd)
    return pl.pallas_call(
        _dense_conv_kernel,
        grid=(d // dc, n_seq),
        in_specs=[pl.BlockSpec((l, dc), lambda j, s: (s, j)),
                  pl.BlockSpec((l, dc), lambda j, s: (s, j)),
                  pl.BlockSpec((1, dc), lambda j, s: (0, j)),
                  pl.BlockSpec((2 * n, dc), lambda j, s: (0, j)),
                  _const_spec(fwd.shape), _const_spec(inv.shape)],
        out_specs=pl.BlockSpec((l, dc), lambda j, s: (s, j)),
        out_shape=jax.ShapeDtypeStruct((n_seq * l, d), F32),
        compiler_params=_cparams(("arbitrary", "arbitrary")),
        name="hy_longconv_dense",
    )(vv, x0, skip.reshape(1, d).astype(F32), tf, fwd, inv)


def _two_stage_tables(l):
    n = 2 * l
    n2 = FFT_N2
    n1 = n // n2
    p = l // n2
    a = np.arange(p)[None, None, :]
    c = np.arange(n1)[None, :, None]
    b = np.arange(n2)[:, None, None]
    ph = 2.0 * math.pi * (((a * c) % n1) / n1 + ((b * c) % n) / n)
    f1 = np.concatenate([np.cos(ph), -np.sin(ph)], axis=1)
    pht = np.transpose(ph, (0, 2, 1))
    g1 = np.concatenate([np.cos(pht), -np.sin(pht)], axis=2) / n
    e = np.arange(n2)[:, None]
    bb = np.arange(n2)[None, :]
    th = 2.0 * math.pi * ((e * bb) % n2) / n2
    fr, fi = np.cos(th), -np.sin(th)
    f2 = np.block([[fr, -fi], [fi, fr]])
    f2c = np.block([[fr, fi], [-fi, fr]])
    cast = lambda m: jnp.asarray(m.astype(np.float32)).astype(BF16)
    return cast(f1), cast(f2), cast(f2c), cast(g1)


def _unrolled_loop(n, unroll, fn):
    def body(i, carry):
        fn([i * unroll + u for u in range(unroll)])
        return carry

    lax.fori_loop(0, n // unroll, body, 0)


FFT_UNROLL_SMALL = 8
FFT_UNROLL_SLAB = 2


def _stage1(src_ref, f1_ref, w_scr, combine=None):
    n2, two_n1, p = f1_ref.shape

    def step(bs):
        if combine is None:
            xs = [src_ref[pl.ds(b, p, stride=n2), :] for b in bs]
        else:
            xs = [combine(src_ref[0][pl.ds(b, p, stride=n2), :], src_ref[1][pl.ds(b, p, stride=n2), :])
                  for b in bs]
        res = [_dot(f1_ref[b], x.astype(BF16)) for b, x in zip(bs, xs)]
        for b, r in zip(bs, res):
            w_scr[pl.ds(pl.multiple_of(b * two_n1, two_n1), two_n1), :] = r

    _unrolled_loop(n2, FFT_UNROLL_SMALL, step)


def _load_slab(w_scr, c, n1, n2):
    re = w_scr[pl.ds(c, n2, stride=2 * n1), :]
    im = w_scr[pl.ds(n1 + c, n2, stride=2 * n1), :]
    return jnp.concatenate([re, im], axis=0).astype(BF16)


def _two_stage_tf_kernel(hf_ref, hb_ref, f1_ref, f2_ref, tfr_ref, tfi_ref, ws_scr, wd_scr):
    n2, two_n1, _ = f1_ref.shape
    n1 = two_n1 // 2
    _stage1((hf_ref, hb_ref), f1_ref, ws_scr, combine=lambda u, v: u + v)
    _stage1((hf_ref, hb_ref), f1_ref, wd_scr, combine=lambda u, v: u - v)

    def step(cs):
        xs = [_dot(f2_ref[0:n2, :], _load_slab(ws_scr, c, n1, n2)) for c in cs]
        xd = [_dot(f2_ref[n2:, :], _load_slab(wd_scr, c, n1, n2)) for c in cs]
        for c, a, b in zip(cs, xs, xd):
            r0 = pl.multiple_of(c * n2, n2)
            tfr_ref[pl.ds(r0, n2), :] = a
            tfi_ref[pl.ds(r0, n2), :] = b

    _unrolled_loop(n1, FFT_UNROLL_SLAB, step)


def _two_stage_conv_kernel(vv_ref, x0_ref, skip_ref, tfr_ref, tfi_ref,
                           f1_ref, f2_ref, f2c_ref, g1_ref, out_ref, w_scr):
    n2, two_n1, p = f1_ref.shape
    n1 = two_n1 // 2
    _stage1(vv_ref, f1_ref, w_scr)

    def step2(cs):
        xs = [_dot(f2_ref[...], _load_slab(w_scr, c, n1, n2)) for c in cs]
        ys = []
        for c, x in zip(cs, xs):
            r0 = pl.multiple_of(c * n2, n2)
            xr, xi = x[:n2], x[n2:]
            tr = tfr_ref[pl.ds(r0, n2), :]
            ti = tfi_ref[pl.ds(r0, n2), :]
            ys.append(jnp.concatenate([xr * tr - xi * ti, xr * ti + xi * tr], axis=0).astype(BF16))
        zs = [_dot(f2c_ref[...], y) for y in ys]
        for c, z in zip(cs, zs):
            w_scr[pl.ds(c, n2, stride=two_n1), :] = z[:n2]
            w_scr[pl.ds(n1 + c, n2, stride=two_n1), :] = z[n2:]

    _unrolled_loop(n1, FFT_UNROLL_SLAB, step2)

    def step3(bs):
        zb = [w_scr[pl.ds(pl.multiple_of(b * two_n1, two_n1), two_n1), :].astype(BF16) for b in bs]
        res = [_dot(g1_ref[b], z) for b, z in zip(bs, zb)]
        for b, r in zip(bs, res):
            out_ref[pl.ds(b, p, stride=n2), :] = r

    _unrolled_loop(n2, FFT_UNROLL_SMALL, step3)
    vv = vv_ref[...]
    out_ref[...] = (out_ref[...] + vv * skip_ref[...]) * x0_ref[...]


def hy_longconv_two_stage(vv, x0, skip, filt, row0, n_seq, l):
    d = vv.shape[1]
    n = 2 * l
    f1, f2, f2c, g1 = _two_stage_tables(l)
    n2, two_n1, p = f1.shape
    dc = LANES
    tfr, tfi = pl.pallas_call(
        _two_stage_tf_kernel,
        grid=(d // dc,),
        in_specs=[pl.BlockSpec((l, dc), lambda j: (0, j)),
                  pl.BlockSpec((l, dc), lambda j: (0, d // dc + j)),
                  _const_spec(f1.shape), _const_spec(f2.shape)],
        out_specs=[pl.BlockSpec((n, dc), lambda j: (0, j)), pl.BlockSpec((n, dc), lambda j: (0, j))],
        out_shape=[jax.ShapeDtypeStruct((n, d), F32), jax.ShapeDtypeStruct((n, d), F32)],
        scratch_shapes=[pltpu.VMEM((n2 * two_n1, dc), F32), pltpu.VMEM((n2 * two_n1, dc), F32)],
        compiler_params=_cparams(("arbitrary",)),
        name="hy_tf_two_stage",
    )(filt, filt, f1, f2)
    base = row0 // l
    return pl.pallas_call(
        _two_stage_conv_kernel,
        grid=(d // dc, n_seq),
        in_specs=[pl.BlockSpec((l, dc), lambda j, s: (base + s, j)),
                  pl.BlockSpec((l, dc), lambda j, s: (base + s, j)),
                  pl.BlockSpec((1, dc), lambda j, s: (0, j)),
                  pl.BlockSpec((n, dc), lambda j, s: (0, j)),
                  pl.BlockSpec((n, dc), lambda j, s: (0, j)),
                  _const_spec(f1.shape), _const_spec(f2.shape), _const_spec(f2c.shape),
                  _const_spec(g1.shape)],
        out_specs=pl.BlockSpec((l, dc), lambda j, s: (s, j)),
        out_shape=jax.ShapeDtypeStruct((n_seq * l, d), F32),
        scratch_shapes=[pltpu.VMEM((n2 * two_n1, dc), F32)],
        compiler_params=_cparams(("arbitrary", "arbitrary")),
        name="hy_longconv_two_stage",
    )(vv, x0, skip.reshape(1, d).astype(F32), tfr, tfi, f1, f2, f2c, g1)


def _hy_out_kernel(y_ref, w_ref, b_ref, x_ref, mod_ref, out_ref, *, gate_idx):
    y = _dot(y_ref[...].astype(BF16), w_ref[...]) + b_ref[...]
    out_ref[...] = x_ref[...] + mod_ref[0, gate_idx:gate_idx + 1, :] * y


def hy_out(yg, w_out, b_out, x, mod, gate_idx, t_ctx, l_lat):
    t, d = x.shape
    grp = functools.partial(_group_of_tile, tm=TM, t_ctx=t_ctx, l_lat=l_lat)
    row_spec = lambda w: pl.BlockSpec((TM, w), lambda i: (i, 0))
    return pl.pallas_call(
        functools.partial(_hy_out_kernel, gate_idx=gate_idx),
        grid=(t // TM,),
        in_specs=[row_spec(d), _const_spec(w_out.shape), _const_spec((1, d)), row_spec(d),
                  pl.BlockSpec((1,) + mod.shape[1:], lambda i: (grp(i), 0, 0))],
        out_specs=row_spec(d),
        out_shape=jax.ShapeDtypeStruct((t, d), F32),
        compiler_params=_cparams(("arbitrary",)),
        name="hy_out",
    )(yg, w_out, b_out.reshape(1, d).astype(F32), x, mod)


def _grid_pos_embed(rows, d):
    r = np.repeat(np.arange(rows, dtype=np.float64), GRID_W)
    col = np.tile(np.arange(GRID_W, dtype=np.float64), rows)
    quarter = d // 4
    omega = 1.0 / (10000.0 ** (np.arange(quarter, dtype=np.float64) / quarter))
    ang_r = r[:, None] * omega
    ang_c = col[:, None] * omega
    pe = np.concatenate([np.sin(ang_r), np.cos(ang_r), np.sin(ang_c), np.cos(ang_c)], axis=-1)
    return jnp.asarray(pe.astype(np.float32))


def delta_layer(x, g, mod, state_lat, w_in, conv_w, a_log, dt_bias, g_head, w_out, dims):
    t_ctx, l_ctx, n_ctx, l_lat, n_lat = dims
    hk = DN_HEADS * DN_DK
    nqkv = 3 * hk
    w_qkv = w_in[:, :nqkv].astype(BF16)
    w_z = w_in[:, nqkv:nqkv + hk].astype(BF16)
    w_ab = jnp.zeros((w_in.shape[0], LANES), BF16).at[:, :4 * DN_HEADS].set(w_in[:, nqkv + hk:].astype(BF16))
    qkv, z, ab = norm_mod_matmul(x, g, mod, 0, 1, [w_qkv, w_z, w_ab], None, t_ctx, l_lat)
    q, k, v, gc, beta, gct = dn_prep(qkv, ab, conv_w.astype(F32), a_log, dt_bias, t_ctx, l_ctx, l_lat)
    outs = []
    ctx_states = []
    for direction in range(2):
        o_c, s_c = dn_chunk_scan(q, k, v, gc, beta, gct, None, direction=direction, row0=0,
                                 n_seq=n_ctx, seq_len=l_ctx, zero_init=True)
        o_l, _ = dn_chunk_scan(q, k, v, gc, beta, gct, state_lat.astype(F32), direction=direction,
                               row0=t_ctx, n_seq=n_lat, seq_len=l_lat, zero_init=False)
        outs.append(jnp.concatenate([o_c, o_l], axis=0))
        ctx_states.append(s_c)
    x = dn_out(outs[0], outs[1], z, g_head, w_out.astype(BF16), x, mod, 2, t_ctx, l_lat)
    return x, jnp.stack(ctx_states, axis=1)


def hyena_layer(x, g, mod, w_in, b_in, conv_w, conv_b, freq, fw1, fb1, fw2, fb2, fw3, fb3, fw4,
                skip, w_out, b_out, dims):
    t_ctx, l_ctx, n_ctx, l_lat, n_lat = dims
    d = x.shape[1]
    (u,) = norm_mod_matmul(x, g, mod, 0, 1, [w_in.astype(BF16)], [b_in.astype(F32)], t_ctx, l_lat)
    x0, vv = hy_conv_gate(u, conv_w.astype(F32), conv_b, t_ctx, l_ctx, l_lat)
    filt_c = hy_filter(l_ctx, d, freq, fw1, fb1, fw2, fb2, fw3, fb3, fw4)
    filt_l = hy_filter(l_lat, d, freq, fw1, fb1, fw2, fb2, fw3, fb3, fw4)
    y_c = hy_longconv_dense(vv, x0, skip, filt_c, n_ctx, l_ctx)
    y_l = hy_longconv_two_stage(vv, x0, skip, filt_l, t_ctx, n_lat, l_lat)
    yg = jnp.concatenate([y_c, y_l], axis=0)
    return hy_out(yg, w_out.astype(BF16), b_out, x, mod, 2, t_ctx, l_lat)


def kernel(x_prompt, x_sample, state_delta, c, c_ctx, w_ada, b_ada, g_norm, dn_w_in, dn_conv, dn_a_log, dn_dt_bias, dn_g_head, dn_w_out, hy_w_in, hy_b_in, hy_conv, hy_conv_b, hy_freq, hy_f_w1, hy_f_b1, hy_f_w2, hy_f_b2, hy_f_w3, hy_f_b3, hy_f_w4, hy_skip, hy_w_out, hy_b_out, w_router, router_bias, moe_w1, moe_w3, moe_w2, g_final):
    n_ctx, l_ctx, d = x_prompt.shape
    n_lat, l_lat, _ = x_sample.shape
    depth = w_ada.shape[0]
    t_ctx = n_ctx * l_ctx
    dims = (t_ctx, l_ctx, n_ctx, l_lat, n_lat)
    assert l_ctx % TM == 0 and l_lat % TM == 0 and t_ctx % l_lat == 0
    assert n_lat + 1 <= SUBLANES

    pos = _grid_pos_embed(l_lat // GRID_W, d)
    x = jnp.concatenate([x_prompt.reshape(t_ctx, d).astype(F32),
                         (x_sample.astype(F32) + pos[None]).reshape(n_lat * l_lat, d)], axis=0)
    cond = jnp.zeros((SUBLANES, d), F32).at[0].set(c_ctx.astype(F32)).at[1:1 + n_lat].set(c.astype(F32))
    mod_all = ada_modulation(cond, w_ada.astype(F32), b_ada.astype(F32))
    mod_all = mod_all.reshape(depth, SUBLANES, 6, d)

    ctx_states = []
    for i in range(depth):
        mod = mod_all[i]
        j = i // 2
        if i % 2 == 0:
            x, s_ctx = delta_layer(x, g_norm[i, 0], mod, state_delta[:, j], dn_w_in[j], dn_conv[j],
                                   dn_a_log[j], dn_dt_bias[j], dn_g_head[j], dn_w_out[j], dims)
            ctx_states.append(s_ctx.astype(x_prompt.dtype))
        else:
            x = hyena_layer(x, g_norm[i, 0], mod, hy_w_in[j], hy_b_in[j], hy_conv[j], hy_conv_b[j],
                            hy_freq[j], hy_f_w1[j], hy_f_b1[j], hy_f_w2[j], hy_f_b2[j], hy_f_w3[j],
                            hy_f_b3[j], hy_f_w4[j], hy_skip[j], hy_w_out[j], hy_b_out[j], dims)
        x = moe_layer(x, g_norm[i, 1], mod, w_router, router_bias, moe_w1[i], moe_w3[i], moe_w2[i],
                      t_ctx, l_lat, g_final if i == depth - 1 else None)
    y_prompt = x[:t_ctx].reshape(n_ctx, l_ctx, d).astype(x_prompt.dtype)
    y_sample = x[t_ctx:].reshape(n_lat, l_lat, d).astype(x_sample.dtype)
    new_state = jnp.stack(ctx_states, axis=1)
    return (y_prompt, y_sample, new_state)
```

```python
import functools
import math

import numpy as np
import jax
import jax.numpy as jnp
from jax import lax
from jax.experimental import pallas as pl
from jax.experimental.pallas import tpu as pltpu

F32 = jnp.float32
BF16 = jnp.bfloat16
HI = lax.Precision.HIGHEST

EPS = 1e-6
GRID_W = 64
DN_HEADS = 8
DN_DK = 128
DN_DV = 128
DN_CHUNK = 64
HY_EMB = 33
HY_ORDER = 64
HY_DECAY_SHORT_PCT = 0.3
HY_DECAY_LONG_PCT = 1.5
HY_DECAY_TARGET = 1e-2
N_EXPERTS = 32
N_GROUPS = 4
EXPERTS_PER_GROUP = N_EXPERTS // N_GROUPS

LANES = 128
SUBLANES = 8
TM = 256
MOE_TILE = 256
FFT_N2 = 128
VMEM_LIMIT = 48 * 1024 * 1024


def _cparams(sem):
    return pltpu.CompilerParams(dimension_semantics=sem, vmem_limit_bytes=VMEM_LIMIT)


def _sigmoid(x):
    return 1.0 / (1.0 + jnp.exp(-x))


def _silu(x):
    return x * _sigmoid(x)


def _dot(a, b, precision=None):
    return jnp.dot(a, b, preferred_element_type=F32, precision=precision)


def _const_spec(shape):
    nd = len(shape)
    return pl.BlockSpec(shape, lambda *_: (0,) * nd)


def _ada_kernel(cond_ref, w_ref, b_ref, out_ref):
    s = _silu(cond_ref[...])
    out_ref[0] = _dot(s, w_ref[0], HI) + b_ref[0]


def ada_modulation(cond, w_ada, b_ada):
    depth, d, n = w_ada.shape
    tn = 1536
    return pl.pallas_call(
        _ada_kernel,
        grid=(depth, n // tn),
        in_specs=[
            pl.BlockSpec((SUBLANES, d), lambda i, j: (0, 0)),
            pl.BlockSpec((1, d, tn), lambda i, j: (i, 0, j)),
            pl.BlockSpec((1, 1, tn), lambda i, j: (i, 0, j)),
        ],
        out_specs=pl.BlockSpec((1, SUBLANES, tn), lambda i, j: (i, 0, j)),
        out_shape=jax.ShapeDtypeStruct((depth, SUBLANES, n), F32),
        compiler_params=_cparams(("arbitrary", "arbitrary")),
        name="ada_modulation",
    )(cond, w_ada, b_ada.reshape(depth, 1, n))


def _norm_mod(x, g, mod_ref, shift_idx, scale_idx):
    ms = jnp.mean(x * x, axis=-1, keepdims=True)
    y = x * lax.rsqrt(ms + EPS) * g
    scale = mod_ref[0, scale_idx:scale_idx + 1, :]
    shift = mod_ref[0, shift_idx:shift_idx + 1, :]
    return y * (1.0 + scale) + shift


def _group_of_tile(i, tm, t_ctx, l_lat):
    r = i * tm
    return jnp.where(r < t_ctx, 0, 1 + (r - t_ctx) // l_lat)


def _nmm_kernel(x_ref, g_ref, mod_ref, *refs, n_out, has_bias, shift_idx, scale_idx):
    h = _norm_mod(x_ref[...], g_ref[...], mod_ref, shift_idx, scale_idx).astype(BF16)
    w_refs = refs[:n_out]
    b_refs = refs[n_out:n_out + (n_out if has_bias else 0)]
    o_refs = refs[n_out + len(b_refs):]
    for k in range(n_out):
        acc = _dot(h, w_refs[k][...])
        if has_bias:
            acc = acc + b_refs[k][...]
        o_refs[k][...] = acc


def norm_mod_matmul(x, g, mod, shift_idx, scale_idx, weights, biases, t_ctx, l_lat):
    t, d = x.shape
    n_out = len(weights)
    has_bias = biases is not None
    grp = functools.partial(_group_of_tile, tm=TM, t_ctx=t_ctx, l_lat=l_lat)
    in_specs = [
        pl.BlockSpec((TM, d), lambda i: (i, 0)),
        _const_spec((1, d)),
        pl.BlockSpec((1,) + mod.shape[1:], lambda i: (grp(i), 0, 0)),
    ]
    args = [x, g.reshape(1, d), mod]
    for w in weights:
        in_specs.append(_const_spec(w.shape))
        args.append(w)
    if has_bias:
        for b in biases:
            in_specs.append(_const_spec((1, b.shape[-1])))
            args.append(b.reshape(1, -1))
    out_specs = [pl.BlockSpec((TM, w.shape[1]), lambda i: (i, 0)) for w in weights]
    out_shape = [jax.ShapeDtypeStruct((t, w.shape[1]), F32) for w in weights]
    return pl.pallas_call(
        functools.partial(_nmm_kernel, n_out=n_out, has_bias=has_bias,
                          shift_idx=shift_idx, scale_idx=scale_idx),
        grid=(t // TM,),
        in_specs=in_specs,
        out_specs=out_specs,
        out_shape=out_shape,
        compiler_params=_cparams(("arbitrary",)),
        name="norm_mod_matmul",
    )(*args)


def _seq_flags(i, tm, t_ctx, l_ctx, l_lat):
    r = i * tm
    in_ctx = r < t_ctx
    pos = jnp.where(in_ctx, r % l_ctx, (r - t_ctx) % l_lat)
    length = jnp.where(in_ctx, l_ctx, l_lat)
    return pos != 0, (pos + tm) != length


def _conv3(x, prev8, next8, w_ref, has_prev, has_next):
    tm = x.shape[0]
    row = lax.broadcasted_iota(jnp.int32, (tm, 1), 0)
    halo_p = jnp.where(has_prev, prev8[SUBLANES - 1:SUBLANES, :], 0.0)
    halo_n = jnp.where(has_next, next8[0:1, :], 0.0)
    x_prev = jnp.where(row == 0, halo_p, pltpu.roll(x, 1, 0))
    x_next = jnp.where(row == tm - 1, halo_n, pltpu.roll(x, tm - 1, 0))
    return w_ref[0:1, :] * x_prev + w_ref[1:2, :] * x + w_ref[2:3, :] * x_next


def _halo_specs(c, tm, t):
    nb8 = t // SUBLANES
    per = tm // SUBLANES
    return [
        pl.BlockSpec((tm, c), lambda i: (i, 0)),
        pl.BlockSpec((SUBLANES, c), lambda i: (jnp.maximum(i * per - 1, 0), 0)),
        pl.BlockSpec((SUBLANES, c), lambda i: (jnp.minimum((i + 1) * per, nb8 - 1), 0)),
    ]


def _softplus(x):
    return jnp.maximum(x, 0.0) + jnp.log(1.0 + jnp.exp(-jnp.abs(x)))


def _dn_prep_kernel(x_ref, xp_ref, xn_ref, cw_ref, ab_ref, nega_ref, dtb_ref, tri_ref,
                    q_ref, k_ref, v_ref, gc_ref, beta_ref, gct_ref,
                    *, t_ctx, l_ctx, l_lat):
    i = pl.program_id(0)
    has_prev, has_next = _seq_flags(i, TM, t_ctx, l_ctx, l_lat)
    c = _conv3(x_ref[...], xp_ref[...], xn_ref[...], cw_ref, has_prev, has_next)
    s = _silu(c)
    hk = DN_HEADS * DN_DK
    for h in range(DN_HEADS):
        lo, hi = h * DN_DK, (h + 1) * DN_DK
        qh = s[:, lo:hi]
        kh = s[:, hk + lo:hk + hi]
        qn = lax.rsqrt(jnp.sum(qh * qh, axis=-1, keepdims=True) + EPS)
        kn = lax.rsqrt(jnp.sum(kh * kh, axis=-1, keepdims=True) + EPS)
        q_ref[:, lo:hi] = qh * (qn * (DN_DK ** -0.5))
        k_ref[:, lo:hi] = kh * kn
    v_ref[...] = s[:, 2 * hk:]
    ab = ab_ref[...]
    lg = nega_ref[...] * _softplus(ab + dtb_ref[...])
    beta_ref[...] = _sigmoid(ab)
    nch = TM // DN_CHUNK
    for d in range(2):
        gc = _dot(tri_ref[d], lg, HI)
        gc_ref[d] = gc
        gct = gc.T
        for ch in range(nch):
            gct_ref[d, ch] = gct[0:2 * DN_HEADS, ch * DN_CHUNK:(ch + 1) * DN_CHUNK]


def _chunk_tri():
    idx = np.arange(TM)
    same = (idx[:, None] // DN_CHUNK) == (idx[None, :] // DN_CHUNK)
    fwd = same & (idx[None, :] <= idx[:, None])
    bwd = same & (idx[None, :] >= idx[:, None])
    return jnp.asarray(np.stack([fwd, bwd]).astype(np.float32))


def dn_prep(qkv, ab, conv_w, a_log, dt_bias, t_ctx, l_ctx, l_lat):
    t, c = qkv.shape
    hd = DN_HEADS * DN_DK
    nega = jnp.zeros((1, LANES), F32).at[0, :2 * DN_HEADS].set(-jnp.exp(a_log.astype(F32)).reshape(-1))
    dtb = jnp.zeros((1, LANES), F32).at[0, :2 * DN_HEADS].set(dt_bias.astype(F32).reshape(-1))
    cw = jnp.zeros((SUBLANES, c), F32).at[:conv_w.shape[0]].set(conv_w)
    nch = TM // DN_CHUNK
    row_spec = lambda w: pl.BlockSpec((TM, w), lambda i: (i, 0))
    return pl.pallas_call(
        functools.partial(_dn_prep_kernel, t_ctx=t_ctx, l_ctx=l_ctx, l_lat=l_lat),
        grid=(t // TM,),
        in_specs=_halo_specs(c, TM, t) + [
            _const_spec((SUBLANES, c)),
            row_spec(LANES),
            _const_spec((1, LANES)),
            _const_spec((1, LANES)),
            _const_spec((2, TM, TM)),
        ],
        out_specs=[
            row_spec(hd), row_spec(hd), row_spec(hd),
            pl.BlockSpec((2, TM, LANES), lambda i: (0, i, 0)),
            row_spec(LANES),
            pl.BlockSpec((2, nch, 2 * DN_HEADS, DN_CHUNK), lambda i: (0, i, 0, 0)),
        ],
        out_shape=[
            jax.ShapeDtypeStruct((t, hd), F32),
            jax.ShapeDtypeStruct((t, hd), F32),
            jax.ShapeDtypeStruct((t, hd), F32),
            jax.ShapeDtypeStruct((2, t, LANES), F32),
            jax.ShapeDtypeStruct((t, LANES), F32),
            jax.ShapeDtypeStruct((2, t // DN_CHUNK, 2 * DN_HEADS, DN_CHUNK), F32),
        ],
        compiler_params=_cparams(("arbitrary",)),
        name="dn_prep",
    )(qkv, qkv, qkv, cw, ab, nega, dtb, _chunk_tri())


def _bmm(a, b):
    return _dot(a.astype(BF16), b.astype(BF16))


def _inv_unit_tri_batch(lms, in_blk, eye):
    dgs = [jnp.where(in_blk, lm, 0.0) for lm in lms]
    offs = [lm - dg for lm, dg in zip(lms, dgs)]
    n1 = [-dg for dg in dgs]
    n2 = [_bmm(a, a) for a in n1]
    p = [eye + a for a in n1]
    n4 = [_bmm(a, a) for a in n2]
    p = [x + _bmm(x, a) for x, a in zip(p, n2)]
    n8 = [_bmm(a, a) for a in n4]
    p = [x + _bmm(x, a) for x, a in zip(p, n4)]
    p = [x + _bmm(x, a) for x, a in zip(p, n8)]
    m = [_bmm(x, o) for x, o in zip(p, offs)]
    m2 = [_bmm(a, a) for a in m]
    t1 = [eye - a for a in m]
    t1 = [x + _bmm(x, a) for x, a in zip(t1, m2)]
    return [_bmm(x, y) for x, y in zip(t1, p)]


def _dn_chunk_kernel(blk_ref, seq_ref, first_ref, last_ref,
                     q_ref, k_ref, v_ref, gc_ref, beta_ref, gct_ref, s0_ref,
                     o_ref, sfin_ref, s_scr, attn_scr, u_scr, qw_scr, kd_scr, gl_scr,
                     *, direction, n_zero_init):
    step = pl.program_id(0)
    c_sz = DN_CHUNK
    nch = TM // c_sz
    heads = range(DN_HEADS)

    @pl.when(first_ref[step] == 1)
    def _():
        s_scr[...] = jnp.where(seq_ref[step] >= n_zero_init, s0_ref[0, 0], 0.0)

    row = lax.broadcasted_iota(jnp.int32, (c_sz, c_sz), 0)
    col = lax.broadcasted_iota(jnp.int32, (c_sz, c_sz), 1)
    if direction == 0:
        incl, strict = row >= col, row > col
        last = c_sz - 1
    else:
        incl, strict = row <= col, row < col
        last = 0
    in_blk = (row // 16) == (col // 16)
    eye = (row == col).astype(F32)

    def lanes(h):
        return slice(h * DN_DK, (h + 1) * DN_DK)

    def prep_body(ch, carry):
        r0 = pl.multiple_of(ch * c_sz, c_sz)
        rows = pl.ds(r0, c_sz)
        gct = gct_ref[0, ch]
        gls = [direction * DN_HEADS + h for h in heads]
        qs = [q_ref[rows, lanes(h)] for h in heads]
        ks = [k_ref[rows, lanes(h)] for h in heads]
        gcc = [gc_ref[0, rows, g:g + 1] for g in gls]
        beta = [beta_ref[rows, 2 * DN_HEADS + g:2 * DN_HEADS + g + 1] for g in gls]
        dec = [jnp.exp(jnp.where(incl, gcc[h] - gct[gls[h]:gls[h] + 1, :], -jnp.inf)) for h in heads]
        kb = [ks[h] * beta[h] for h in heads]
        prod = [lax.dot_general(jnp.concatenate([qs[h], kb[h]], axis=0).astype(BF16), ks[h].astype(BF16),
                                (((1,), (1,)), ((), ())), preferred_element_type=F32)
                for h in heads]
        for h in heads:
            attn_scr[ch, h] = (prod[h][:c_sz] * dec[h]).astype(BF16)
        lms = [jnp.where(strict, prod[h][c_sz:] * dec[h], 0.0) for h in heads]
        tinv = _inv_unit_tri_batch(lms, in_blk, eye)
        eg = [jnp.exp(gcc[h]) for h in heads]
        uw = [_bmm(tinv[h], jnp.concatenate([v_ref[rows, lanes(h)] * beta[h], kb[h] * eg[h]], axis=1))
              for h in heads]
        for h in heads:
            u_scr[rows, lanes(h)] = uw[h][:, :DN_DV]
            qw_scr[ch, h] = jnp.concatenate([qs[h] * eg[h], uw[h][:, DN_DV:]], axis=0).astype(BF16)
            g_last = gcc[h][last:last + 1, :]
            kd_scr[ch, h] = (ks[h] * jnp.exp(g_last - gcc[h])).T.astype(BF16)
            gl_scr[ch, h] = jnp.broadcast_to(jnp.exp(g_last), (1, DN_DV))
        return carry

    lax.fori_loop(0, nch, prep_body, 0)

    def scan_body(ci, carry):
        ch = ci if direction == 0 else nch - 1 - ci
        r0 = pl.multiple_of(ch * c_sz, c_sz)
        rows = pl.ds(r0, c_sz)
        s = [s_scr[h] for h in heads]
        qs_ws = [_dot(qw_scr[ch, h], s[h].astype(BF16)) for h in heads]
        vb = [(u_scr[rows, lanes(h)] - qs_ws[h][c_sz:]).astype(BF16) for h in heads]
        for h in heads:
            o_ref[rows, lanes(h)] = qs_ws[h][:c_sz] + _dot(attn_scr[ch, h], vb[h])
            s_scr[h] = s[h] * gl_scr[ch, h] + _dot(kd_scr[ch, h], vb[h])
        return carry

    lax.fori_loop(0, nch, scan_body, 0)

    @pl.when(last_ref[step] == 1)
    def _():
        sfin_ref[0, 0] = s_scr[...]


def _scan_tables(direction, seq_lens):
    blk, seq, first, last = [], [], [], []
    base = 0
    for s, length in enumerate(seq_lens):
        nblk = length // TM
        order = range(nblk) if direction == 0 else range(nblk - 1, -1, -1)
        for pos, jj in enumerate(order):
            blk.append(base + jj)
            seq.append(s)
            first.append(int(pos == 0))
            last.append(int(pos == nblk - 1))
        base += nblk
    return [jnp.asarray(np.asarray(a, np.int32)) for a in (blk, seq, first, last)]


def dn_chunk_scan(q, k, v, gc, beta, gct, s0, *, direction, seq_lens, n_zero_init):
    t, hd = q.shape
    nch = TM // DN_CHUNK
    n_seq = len(seq_lens)
    tables = _scan_tables(direction, seq_lens)
    n_steps = int(tables[0].shape[0])
    row_spec = lambda w: pl.BlockSpec((TM, w), lambda i, blk, seq, fst, lst: (blk[i], 0))
    state_blk = (1, 1, DN_HEADS, DN_DK, DN_DV)
    grid_spec = pltpu.PrefetchScalarGridSpec(
        num_scalar_prefetch=4,
        grid=(n_steps,),
        in_specs=[
            row_spec(hd), row_spec(hd), row_spec(hd),
            pl.BlockSpec((1, TM, LANES), lambda i, blk, seq, fst, lst: (direction, blk[i], 0)),
            row_spec(LANES),
            pl.BlockSpec((1, nch, 2 * DN_HEADS, DN_CHUNK),
                         lambda i, blk, seq, fst, lst: (direction, blk[i], 0, 0)),
            pl.BlockSpec(state_blk, lambda i, blk, seq, fst, lst:
                         (jnp.maximum(seq[i] - n_zero_init, 0), direction, 0, 0, 0)),
        ],
        out_specs=[
            row_spec(hd),
            pl.BlockSpec(state_blk, lambda i, blk, seq, fst, lst: (seq[i], 0, 0, 0, 0)),
        ],
        scratch_shapes=[
            pltpu.VMEM((DN_HEADS, DN_DK, DN_DV), F32),
            pltpu.VMEM((nch, DN_HEADS, DN_CHUNK, DN_CHUNK), BF16),
            pltpu.VMEM((TM, hd), F32),
            pltpu.VMEM((nch, DN_HEADS, 2 * DN_CHUNK, DN_DK), BF16),
            pltpu.VMEM((nch, DN_HEADS, DN_DK, DN_CHUNK), BF16),
            pltpu.VMEM((nch, DN_HEADS, 1, DN_DV), F32),
        ],
    )
    o, sfin = pl.pallas_call(
        functools.partial(_dn_chunk_kernel, direction=direction, n_zero_init=n_zero_init),
        grid_spec=grid_spec,
        out_shape=[
            jax.ShapeDtypeStruct((t, hd), F32),
            jax.ShapeDtypeStruct((n_seq,) + state_blk[1:], F32),
        ],
        compiler_params=_cparams(("arbitrary",)),
        name="dn_chunk_scan_d%d" % direction,
    )(*tables, q, k, v, gc, beta, gct, s0)
    return o, sfin[:, 0]


def _dn_out_kernel(of_ref, ob_ref, z_ref, gh_ref, w_ref, x_ref, mod_ref, out_ref, *, gate_idx):
    o = of_ref[...] + ob_ref[...]
    z = z_ref[...]
    parts = []
    for h in range(DN_HEADS):
        lanes = slice(h * DN_DV, (h + 1) * DN_DV)
        oh = o[:, lanes]
        ms = jnp.mean(oh * oh, axis=-1, keepdims=True)
        parts.append(oh * lax.rsqrt(ms + EPS) * gh_ref[...])
    on = jnp.concatenate(parts, axis=1) * _silu(z)
    y = _dot(on.astype(BF16), w_ref[...])
    out_ref[...] = x_ref[...] + mod_ref[0, gate_idx:gate_idx + 1, :] * y


def dn_out(o_f, o_b, z, g_head, w_out, x, mod, gate_idx, t_ctx, l_lat):
    t, d = x.shape
    hd = o_f.shape[1]
    grp = functools.partial(_group_of_tile, tm=TM, t_ctx=t_ctx, l_lat=l_lat)
    row_spec = lambda w: pl.BlockSpec((TM, w), lambda i: (i, 0))
    return pl.pallas_call(
        functools.partial(_dn_out_kernel, gate_idx=gate_idx),
        grid=(t // TM,),
        in_specs=[
            row_spec(hd), row_spec(hd), row_spec(hd),
            _const_spec((1, DN_DV)),
            _const_spec(w_out.shape),
            row_spec(d),
            pl.BlockSpec((1,) + mod.shape[1:], lambda i: (grp(i), 0, 0)),
        ],
        out_specs=row_spec(d),
        out_shape=jax.ShapeDtypeStruct((t, d), F32),
        compiler_params=_cparams(("arbitrary",)),
        name="dn_out",
    )(o_f, o_b, z, g_head.reshape(1, DN_DV).astype(F32), w_out, x, mod)


def _first_argmax(vals, lane, valid):
    masked = jnp.where(valid, vals, -jnp.inf)
    m = jnp.max(masked, axis=-1, keepdims=True)
    idx = jnp.min(jnp.where(valid & (masked == m), lane, float(LANES)), axis=-1, keepdims=True)
    return m, idx


def _router_kernel(x_ref, g_ref, mod_ref, wh_ref, wl_ref, rb_ref, h_ref, idx_ref, wt_ref,
                   *, shift_idx, scale_idx):
    h = _norm_mod(x_ref[...], g_ref[...], mod_ref, shift_idx, scale_idx)
    hb = h.astype(BF16)
    h_ref[...] = hb
    hl = (h - hb.astype(F32)).astype(BF16)
    wh = wh_ref[...]
    logits = _dot(hb, wh) + (_dot(hl, wh) + _dot(hb, wl_ref[...]))
    scores = _sigmoid(logits)
    sel = scores + rb_ref[...]
    tm = scores.shape[0]
    lane = lax.broadcasted_iota(jnp.int32, (tm, LANES), 1).astype(F32)
    best_score = None
    for g in range(N_GROUPS):
        in_g = (lane >= g * EXPERTS_PER_GROUP) & (lane < (g + 1) * EXPERTS_PER_GROUP)
        m1, i1 = _first_argmax(sel, lane, in_g)
        m2, i2 = _first_argmax(sel, lane, in_g & (lane != i1))
        gs = m1 + m2
        if best_score is None:
            best_score, b1, b2 = gs, i1, i2
        else:
            better = gs > best_score
            best_score = jnp.where(better, gs, best_score)
            b1 = jnp.where(better, i1, b1)
            b2 = jnp.where(better, i2, b2)
    w1 = jnp.sum(jnp.where(lane == b1, scores, 0.0), axis=-1, keepdims=True)
    w2 = jnp.sum(jnp.where(lane == b2, scores, 0.0), axis=-1, keepdims=True)
    tot = w1 + w2
    idx_ref[...] = jnp.where(lane == 0.0, b1, jnp.where(lane == 1.0, b2, 0.0)).astype(jnp.int32)
    wt_ref[...] = jnp.where(lane == 0.0, w1 / tot, jnp.where(lane == 1.0, w2 / tot, 0.0))


def moe_router(x, g, mod, shift_idx, scale_idx, w_router, router_bias, t_ctx, l_lat):
    t, d = x.shape
    grp = functools.partial(_group_of_tile, tm=TM, t_ctx=t_ctx, l_lat=l_lat)
    wr = jnp.zeros((d, LANES), F32).at[:, :N_EXPERTS].set(w_router.astype(F32))
    wh = wr.astype(BF16)
    wl = (wr - wh.astype(F32)).astype(BF16)
    rb = jnp.zeros((1, LANES), F32).at[0, :N_EXPERTS].set(router_bias.astype(F32))
    row_spec = lambda w: pl.BlockSpec((TM, w), lambda i: (i, 0))
    return pl.pallas_call(
        functools.partial(_router_kernel, shift_idx=shift_idx, scale_idx=scale_idx),
        grid=(t // TM,),
        in_specs=[
            row_spec(d),
            _const_spec((1, d)),
            pl.BlockSpec((1,) + mod.shape[1:], lambda i: (grp(i), 0, 0)),
            _const_spec((d, LANES)),
            _const_spec((d, LANES)),
            _const_spec((1, LANES)),
        ],
        out_specs=[row_spec(d), row_spec(LANES), row_spec(LANES)],
        out_shape=[
            jax.ShapeDtypeStruct((t, d), BF16),
            jax.ShapeDtypeStruct((t, LANES), jnp.int32),
            jax.ShapeDtypeStruct((t, LANES), F32),
        ],
        compiler_params=_cparams(("arbitrary",)),
        name="moe_router",
    )(x, g.reshape(1, d), mod, wh, wl, rb)


def _expert_kernel(te_ref, nt_ref, first_ref, nxt_ref, par_ref,
                   xs_ref, w1_hbm, w3_hbm, w2_hbm, ys_ref,
                   w1_buf, w3_buf, w2_buf, w1_bf, w3_bf, w2_bf, sem, *, layer):
    i = pl.program_id(0)
    active = i < nt_ref[0]

    def weight_copies(e, slot):
        return (pltpu.make_async_copy(w1_hbm.at[layer, e], w1_buf.at[slot], sem.at[slot, 0]),
                pltpu.make_async_copy(w3_hbm.at[layer, e], w3_buf.at[slot], sem.at[slot, 1]),
                pltpu.make_async_copy(w2_hbm.at[layer, e], w2_buf.at[slot], sem.at[slot, 2]))

    @pl.when(active & (i == 0))
    def _():
        for cp in weight_copies(te_ref[0], 0):
            cp.start()

    @pl.when(active & (first_ref[i] == 1))
    def _():
        slot = par_ref[i]
        for cp in weight_copies(te_ref[i], slot):
            cp.wait()

        @pl.when(nxt_ref[i] >= 0)
        def _():
            for cp in weight_copies(nxt_ref[i], 1 - slot):
                cp.start()

        w1_bf[...] = w1_buf[slot].astype(BF16)
        w3_bf[...] = w3_buf[slot].astype(BF16)
        w2_bf[...] = w2_buf[slot].astype(BF16)

    @pl.when(active)
    def _():
        xs = xs_ref[...]
        a = _dot(xs, w1_bf[...])
        b = _dot(xs, w3_bf[...])
        hid = (_silu(a) * b).astype(BF16)
        ys_ref[...] = _dot(hid, w2_bf[...])

    @pl.when(jnp.logical_not(active))
    def _():
        ys_ref[...] = jnp.zeros_like(ys_ref)


def moe_experts(xs, plan, w1, w3, w2, layer):
    p, d = xs.shape
    de = w1.shape[-1]
    ntile = p // MOE_TILE
    tile_map = lambda i, *_: (i, 0)
    grid_spec = pltpu.PrefetchScalarGridSpec(
        num_scalar_prefetch=5,
        grid=(ntile,),
        in_specs=[
            pl.BlockSpec((MOE_TILE, d), tile_map),
            pl.BlockSpec(memory_space=pl.ANY),
            pl.BlockSpec(memory_space=pl.ANY),
            pl.BlockSpec(memory_space=pl.ANY),
        ],
        out_specs=pl.BlockSpec((MOE_TILE, d), tile_map),
        scratch_shapes=[
            pltpu.VMEM((2, d, de), F32), pltpu.VMEM((2, d, de), F32), pltpu.VMEM((2, de, d), F32),
            pltpu.VMEM((d, de), BF16), pltpu.VMEM((d, de), BF16), pltpu.VMEM((de, d), BF16),
            pltpu.SemaphoreType.DMA((2, 3)),
        ],
    )
    return pl.pallas_call(
        functools.partial(_expert_kernel, layer=layer),
        grid_spec=grid_spec,
        out_shape=jax.ShapeDtypeStruct((p, d), F32),
        compiler_params=_cparams(("arbitrary",)),
        name="moe_experts",
    )(plan["tile_expert"], plan["n_tiles"], plan["first"], plan["next_expert"], plan["parity"],
      xs, w1, w3, w2)


def moe_dispatch_plan(idx):
    t = idx.shape[0]
    n_assign = 2 * t
    p = n_assign + N_EXPERTS * MOE_TILE
    p = -(-p // MOE_TILE) * MOE_TILE
    ntile = p // MOE_TILE
    experts = jnp.arange(N_EXPERTS, dtype=jnp.int32)
    e_flat = idx.reshape(-1)
    onehot = (e_flat[:, None] == experts[None, :]).astype(jnp.int32)
    rank = jnp.cumsum(onehot, axis=0) - onehot
    counts = jnp.sum(onehot, axis=0)
    padded = ((counts + MOE_TILE - 1) // MOE_TILE) * MOE_TILE
    ends = jnp.cumsum(padded)
    starts = ends - padded
    slot = jnp.sum(onehot * (starts[None, :] + rank), axis=1)
    token = jnp.arange(n_assign, dtype=jnp.int32) // 2
    slot_token = jnp.zeros((p,), jnp.int32).at[slot].set(token, unique_indices=True,
                                                          mode="promise_in_bounds")
    tile_id = jnp.arange(ntile, dtype=jnp.int32)
    tile_expert = jnp.sum((tile_id[:, None] * MOE_TILE >= ends[None, :]).astype(jnp.int32), axis=1)
    n_tiles = (ends[-1] // MOE_TILE).astype(jnp.int32)
    used = tile_id < n_tiles
    tile_expert = jnp.where(used, tile_expert, tile_expert[jnp.maximum(n_tiles - 1, 0)])
    tile_expert = jnp.minimum(tile_expert, N_EXPERTS - 1).astype(jnp.int32)
    prev = jnp.concatenate([jnp.full((1,), -1, jnp.int32), tile_expert[:-1]])
    first = (used & (tile_expert != prev)).astype(jnp.int32)
    parity = ((jnp.cumsum(first) - 1) % 2).astype(jnp.int32)
    cand = jnp.where(counts > 0, experts, N_EXPERTS)
    later = lax.cummin(cand[::-1])[::-1]
    nxt_e = jnp.concatenate([later[1:], jnp.full((1,), N_EXPERTS, jnp.int32)])
    nxt_e = jnp.where(nxt_e >= N_EXPERTS, -1, nxt_e).astype(jnp.int32)
    next_expert = nxt_e[tile_expert]
    return dict(slot_token=slot_token, slot=slot.reshape(t, 2), tile_expert=tile_expert,
                n_tiles=n_tiles.reshape(1), first=first, next_expert=next_expert, parity=parity)


def _combine_kernel(x_ref, y0_ref, y1_ref, wt_ref, mod_ref, *refs, gate_idx, final_norm):
    moe = wt_ref[:, 0:1] * y0_ref[...] + wt_ref[:, 1:2] * y1_ref[...]
    x = x_ref[...] + mod_ref[0, gate_idx:gate_idx + 1, :] * moe
    if final_norm:
        gf_ref, out_ref = refs
        ms = jnp.mean(x * x, axis=-1, keepdims=True)
        out_ref[...] = x * lax.rsqrt(ms + EPS) * gf_ref[...]
    else:
        (out_ref,) = refs
        out_ref[...] = x


def moe_combine(x, y0, y1, wt, mod, gate_idx, t_ctx, l_lat, g_final=None):
    t, d = x.shape
    grp = functools.partial(_group_of_tile, tm=TM, t_ctx=t_ctx, l_lat=l_lat)
    row_spec = lambda w: pl.BlockSpec((TM, w), lambda i: (i, 0))
    in_specs = [row_spec(d), row_spec(d), row_spec(d), row_spec(LANES),
                pl.BlockSpec((1,) + mod.shape[1:], lambda i: (grp(i), 0, 0))]
    args = [x, y0, y1, wt, mod]
    if g_final is not None:
        in_specs.append(_const_spec((1, d)))
        args.append(g_final.reshape(1, d).astype(F32))
    return pl.pallas_call(
        functools.partial(_combine_kernel, gate_idx=gate_idx, final_norm=g_final is not None),
        grid=(t // TM,),
        in_specs=in_specs,
        out_specs=row_spec(d),
        out_shape=jax.ShapeDtypeStruct((t, d), F32),
        compiler_params=_cparams(("arbitrary",)),
        name="moe_combine",
    )(*args)


def moe_layer(x, g, mod, w_router, router_bias, w1, w3, w2, layer, t_ctx, l_lat, g_final=None):
    h, idx_pad, wt_pad = moe_router(x, g, mod, 3, 4, w_router, router_bias, t_ctx, l_lat)
    plan = moe_dispatch_plan(idx_pad[:, :2])
    take = lambda rows, index: rows.at[index].get(mode="promise_in_bounds")
    xs = take(h, plan["slot_token"])
    ys = moe_experts(xs, plan, w1, w3, w2, layer)
    y0 = take(ys, plan["slot"][:, 0])
    y1 = take(ys, plan["slot"][:, 1])
    return moe_combine(x, y0, y1, wt_pad, mod, 5, t_ctx, l_lat, g_final)


def _hy_conv_kernel(u_ref, up_ref, un_ref, cw_ref, cb_ref, x0_ref, vv_ref, *, t_ctx, l_ctx, l_lat):
    i = pl.program_id(0)
    has_prev, has_next = _seq_flags(i, TM, t_ctx, l_ctx, l_lat)
    c = _conv3(u_ref[...], up_ref[...], un_ref[...], cw_ref, has_prev, has_next) + cb_ref[...]
    d = c.shape[1] // 3
    x0_ref[...] = c[:, :d]
    vv_ref[...] = c[:, 2 * d:] * c[:, d:2 * d]


def hy_conv_gate(u, conv_w, conv_b, t_ctx, l_ctx, l_lat):
    t, c = u.shape
    d = c // 3
    cw = jnp.zeros((SUBLANES, c), F32).at[:conv_w.shape[0]].set(conv_w)
    row_spec = lambda w: pl.BlockSpec((TM, w), lambda i: (i, 0))
    return pl.pallas_call(
        functools.partial(_hy_conv_kernel, t_ctx=t_ctx, l_ctx=l_ctx, l_lat=l_lat),
        grid=(t // TM,),
        in_specs=_halo_specs(c, TM, t) + [_const_spec((SUBLANES, c)), _const_spec((1, c))],
        out_specs=[row_spec(d), row_spec(d)],
        out_shape=[jax.ShapeDtypeStruct((t, d), F32), jax.ShapeDtypeStruct((t, d), F32)],
        compiler_params=_cparams(("arbitrary",)),
        name="hy_conv_gate",
    )(u, u, u, cw, conv_b.reshape(1, c).astype(F32))


def _hy_features(n):
    t = np.linspace(0.0, 1.0, n)[:, None]
    bands = (HY_EMB - 1) // 2
    band_w = np.linspace(1e-4, bands - 1, bands)[None, :]
    ang = (2.0 * math.pi / n) * np.arange(n)[:, None] * band_w
    z = np.concatenate([t, np.cos(ang), -np.sin(ang)], axis=-1)
    zp = np.zeros((n, LANES))
    zp[:, :HY_EMB] = z
    return jnp.asarray(zp.astype(np.float32))


def _hy_hid_kernel(z_ref, f_ref, w1_ref, b1_ref, w2_ref, b2_ref, w3_ref, b3_ref, hid_ref):
    h = jnp.sin(f_ref[0:1, :] * (_dot(z_ref[...], w1_ref[...], HI) + b1_ref[...]))
    h = jnp.sin(f_ref[1:2, :] * (_dot(h, w2_ref[...], HI) + b2_ref[...]))
    h = jnp.sin(f_ref[2:3, :] * (_dot(h, w3_ref[...], HI) + b3_ref[...]))
    hid_ref[...] = h


def _hy_filt_kernel(hid_ref, w4_ref, z_ref, dl_ref, filt_ref):
    f = _dot(hid_ref[...], w4_ref[...], HI)
    f = f * jnp.exp(-z_ref[:, 0:1] * jnp.abs(dl_ref[...]))
    nrm = jnp.sum(jnp.abs(f), axis=0, keepdims=True)
    filt_ref[...] = f / nrm


def hy_filter(n, d, freq, w1, b1, w2, b2, w3, b3, w4):
    z = _hy_features(n)
    o = HY_ORDER
    w1p = jnp.zeros((LANES, o), F32).at[:HY_EMB].set(w1.astype(F32))
    fp = jnp.zeros((SUBLANES, o), F32).at[:3].set(freq.astype(F32))
    tr = min(n, TM)
    hid = pl.pallas_call(
        _hy_hid_kernel,
        grid=(n // tr,),
        in_specs=[pl.BlockSpec((tr, LANES), lambda i: (i, 0)), _const_spec((SUBLANES, o)),
                  _const_spec((LANES, o)), _const_spec((1, o)),
                  _const_spec((o, o)), _const_spec((1, o)),
                  _const_spec((o, o)), _const_spec((1, o))],
        out_specs=pl.BlockSpec((tr, o), lambda i: (i, 0)),
        out_shape=jax.ShapeDtypeStruct((n, o), F32),
        compiler_params=_cparams(("arbitrary",)),
        name="hy_filter_mlp",
    )(z, fp, w1p, b1.reshape(1, o).astype(F32), w2.astype(F32), b2.reshape(1, o).astype(F32),
      w3.astype(F32), b3.reshape(1, o).astype(F32))
    max_decay = math.log(HY_DECAY_TARGET) / HY_DECAY_SHORT_PCT
    min_decay = math.log(HY_DECAY_TARGET) / HY_DECAY_LONG_PCT
    deltas = np.tile(np.linspace(min_decay, max_decay, d), 2).astype(np.float32)[None, :]
    tc = 256
    return pl.pallas_call(
        _hy_filt_kernel,
        grid=(2 * d // tc,),
        in_specs=[_const_spec((n, o)), pl.BlockSpec((o, tc), lambda j: (0, j)),
                  _const_spec((n, LANES)), pl.BlockSpec((1, tc), lambda j: (0, j))],
        out_specs=pl.BlockSpec((n, tc), lambda j: (0, j)),
        out_shape=jax.ShapeDtypeStruct((n, 2 * d), F32),
        compiler_params=_cparams(("arbitrary",)),
        name="hy_filter_window",
    )(hid, w4.astype(F32), z, jnp.asarray(deltas))


def _dense_dft_tables(l):
    n = 2 * l
    k = np.arange(n)[:, None]
    j = np.arange(l)[None, :]
    th = 2.0 * math.pi * ((k * j) % n) / n
    fwd = np.concatenate([np.cos(th), -np.sin(th)], axis=0)
    inv = np.concatenate([np.cos(th).T, -np.sin(th).T], axis=1) / n
    return (jnp.asarray(fwd.astype(np.float32)).astype(BF16),
            jnp.asarray(inv.astype(np.float32)).astype(BF16))


def _dense_tf_kernel(hf_ref, hb_ref, fwd_ref, tf_ref):
    n = fwd_ref.shape[0] // 2
    hf = hf_ref[...]
    hb = hb_ref[...]
    tf_ref[0:n, :] = _dot(fwd_ref[0:n, :], (hf + hb).astype(BF16))
    tf_ref[n:, :] = _dot(fwd_ref[n:, :], (hf - hb).astype(BF16))


def _dense_conv_kernel(vv_ref, x0_ref, skip_ref, tf_ref, fwd_ref, inv_ref, out_ref):
    n = fwd_ref.shape[0] // 2
    vv = vv_ref[...]
    x = _dot(fwd_ref[...], vv.astype(BF16))
    xr, xi = x[:n], x[n:]
    tr, ti = tf_ref[0:n, :], tf_ref[n:, :]
    y = jnp.concatenate([xr * tr - xi * ti, xr * ti + xi * tr], axis=0).astype(BF16)
    conv = _dot(inv_ref[...], y)
    out_ref[...] = (conv + vv * skip_ref[...]) * x0_ref[...]


def hy_longconv_dense(vv, x0, skip, filt, n_seq, l):
    d = vv.shape[1]
    n = 2 * l
    fwd, inv = _dense_dft_tables(l)
    dc = 512
    tf = pl.pallas_call(
        _dense_tf_kernel,
        grid=(d // dc,),
        in_specs=[pl.BlockSpec((l, dc), lambda j: (0, j)),
                  pl.BlockSpec((l, dc), lambda j: (0, d // dc + j)),
                  _const_spec(fwd.shape)],
        out_specs=pl.BlockSpec((2 * n, dc), lambda j: (0, j)),
        out_shape=jax.ShapeDtypeStruct((2 * n, d), F32),
        compiler_params=_cparams(("arbitrary",)),
        name="hy_tf_dense",
    )(filt, filt, fwd)
    return pl.pallas_call(
        _dense_conv_kernel,
        grid=(d // dc, n_seq),
        in_specs=[pl.BlockSpec((l, dc), lambda j, s: (s, j)),
                  pl.BlockSpec((l, dc), lambda j, s: (s, j)),
                  pl.BlockSpec((1, dc), lambda j, s: (0, j)),
                  pl.BlockSpec((2 * n, dc), lambda j, s: (0, j)),
                  _const_spec(fwd.shape), _const_spec(inv.shape)],
        out_specs=pl.BlockSpec((l, dc), lambda j, s: (s, j)),
        out_shape=jax.ShapeDtypeStruct((n_seq * l, d), F32),
        compiler_params=_cparams(("arbitrary", "arbitrary")),
        name="hy_longconv_dense",
    )(vv, x0, skip.reshape(1, d).astype(F32), tf, fwd, inv)


def _two_stage_tables(l):
    n = 2 * l
    n2 = FFT_N2
    n1 = n // n2
    p = l // n2
    a = np.arange(p)[None, None, :]
    c = np.arange(n1)[None, :, None]
    b = np.arange(n2)[:, None, None]
    ph = 2.0 * math.pi * (((a * c) % n1) / n1 + ((b * c) % n) / n)
    f1 = np.concatenate([np.cos(ph), -np.sin(ph)], axis=1)
    pht = np.transpose(ph, (0, 2, 1))
    g1 = np.concatenate([np.cos(pht), -np.sin(pht)], axis=2) / n
    e = np.arange(n2)[:, None]
    bb = np.arange(n2)[None, :]
    th = 2.0 * math.pi * ((e * bb) % n2) / n2
    fr, fi = np.cos(th), -np.sin(th)
    f2 = np.block([[fr, -fi], [fi, fr]])
    f2c = np.block([[fr, fi], [-fi, fr]])
    cast = lambda m: jnp.asarray(m.astype(np.float32)).astype(BF16)
    return cast(f1), cast(f2), cast(f2c), cast(g1)


def _unrolled_loop(n, unroll, fn):
    def body(i, carry):
        fn([i * unroll + u for u in range(unroll)])
        return carry

    lax.fori_loop(0, n // unroll, body, 0)


FFT_UNROLL_SMALL = 8
FFT_UNROLL_SLAB = 4


def _stage1(src_ref, f1_ref, w_scr, combine=None):
    n2, two_n1, p = f1_ref.shape

    def step(bs):
        if combine is None:
            xs = [src_ref[pl.ds(b, p, stride=n2), :] for b in bs]
        else:
            xs = [combine(src_ref[0][pl.ds(b, p, stride=n2), :], src_ref[1][pl.ds(b, p, stride=n2), :])
                  for b in bs]
        res = [_dot(f1_ref[b], x.astype(BF16)) for b, x in zip(bs, xs)]
        for b, r in zip(bs, res):
            w_scr[pl.ds(pl.multiple_of(b * two_n1, two_n1), two_n1), :] = r

    _unrolled_loop(n2, FFT_UNROLL_SMALL, step)


def _load_slab(w_scr, c, n1, n2):
    re = w_scr[pl.ds(c, n2, stride=2 * n1), :]
    im = w_scr[pl.ds(n1 + c, n2, stride=2 * n1), :]
    return jnp.concatenate([re, im], axis=0).astype(BF16)


def _two_stage_tf_kernel(hf_ref, hb_ref, f1_ref, f2_ref, tfr_ref, tfi_ref, ws_scr, wd_scr):
    n2, two_n1, _ = f1_ref.shape
    n1 = two_n1 // 2
    _stage1((hf_ref, hb_ref), f1_ref, ws_scr, combine=lambda u, v: u + v)
    _stage1((hf_ref, hb_ref), f1_ref, wd_scr, combine=lambda u, v: u - v)

    def step(cs):
        xs = [_dot(f2_ref[0:n2, :], _load_slab(ws_scr, c, n1, n2)) for c in cs]
        xd = [_dot(f2_ref[n2:, :], _load_slab(wd_scr, c, n1, n2)) for c in cs]
        for c, a, b in zip(cs, xs, xd):
            r0 = pl.multiple_of(c * n2, n2)
            tfr_ref[pl.ds(r0, n2), :] = a
            tfi_ref[pl.ds(r0, n2), :] = b

    _unrolled_loop(n1, FFT_UNROLL_SLAB, step)


def _two_stage_conv_kernel(vv_ref, x0_ref, skip_ref, tfr_ref, tfi_ref,
                           f1_ref, f2_ref, f2c_ref, g1_ref, out_ref, w_scr):
    n2, two_n1, p = f1_ref.shape
    n1 = two_n1 // 2
    _stage1(vv_ref, f1_ref, w_scr)

    def step2(cs):
        xs = [_dot(f2_ref[...], _load_slab(w_scr, c, n1, n2)) for c in cs]
        ys = []
        for c, x in zip(cs, xs):
            r0 = pl.multiple_of(c * n2, n2)
            xr, xi = x[:n2], x[n2:]
            tr = tfr_ref[pl.ds(r0, n2), :]
            ti = tfi_ref[pl.ds(r0, n2), :]
            ys.append(jnp.concatenate([xr * tr - xi * ti, xr * ti + xi * tr], axis=0).astype(BF16))
        zs = [_dot(f2c_ref[...], y) for y in ys]
        for c, z in zip(cs, zs):
            w_scr[pl.ds(c, n2, stride=two_n1), :] = z[:n2]
            w_scr[pl.ds(n1 + c, n2, stride=two_n1), :] = z[n2:]

    _unrolled_loop(n1, FFT_UNROLL_SLAB, step2)

    def step3(bs):
        zb = [w_scr[pl.ds(pl.multiple_of(b * two_n1, two_n1), two_n1), :].astype(BF16) for b in bs]
        res = [_dot(g1_ref[b], z) for b, z in zip(bs, zb)]
        for b, r in zip(bs, res):
            out_ref[pl.ds(b, p, stride=n2), :] = r

    _unrolled_loop(n2, FFT_UNROLL_SMALL, step3)
    vv = vv_ref[...]
    out_ref[...] = (out_ref[...] + vv * skip_ref[...]) * x0_ref[...]


def hy_longconv_two_stage(vv, x0, skip, filt, row0, n_seq, l):
    d = vv.shape[1]
    n = 2 * l
    f1, f2, f2c, g1 = _two_stage_tables(l)
    n2, two_n1, p = f1.shape
    dc = LANES
    tfr, tfi = pl.pallas_call(
        _two_stage_tf_kernel,
        grid=(d // dc,),
        in_specs=[pl.BlockSpec((l, dc), lambda j: (0, j)),
                  pl.BlockSpec((l, dc), lambda j: (0, d // dc + j)),
                  _const_spec(f1.shape), _const_spec(f2.shape)],
        out_specs=[pl.BlockSpec((n, dc), lambda j: (0, j)), pl.BlockSpec((n, dc), lambda j: (0, j))],
        out_shape=[jax.ShapeDtypeStruct((n, d), F32), jax.ShapeDtypeStruct((n, d), F32)],
        scratch_shapes=[pltpu.VMEM((n2 * two_n1, dc), F32), pltpu.VMEM((n2 * two_n1, dc), F32)],
        compiler_params=_cparams(("arbitrary",)),
        name="hy_tf_two_stage",
    )(filt, filt, f1, f2)
    base = row0 // l
    return pl.pallas_call(
        _two_stage_conv_kernel,
        grid=(d // dc, n_seq),
        in_specs=[pl.BlockSpec((l, dc), lambda j, s: (base + s, j)),
                  pl.BlockSpec((l, dc), lambda j, s: (base + s, j)),
                  pl.BlockSpec((1, dc), lambda j, s: (0, j)),
                  pl.BlockSpec((n, dc), lambda j, s: (0, j)),
                  pl.BlockSpec((n, dc), lambda j, s: (0, j)),
                  _const_spec(f1.shape), _const_spec(f2.shape), _const_spec(f2c.shape),
                  _const_spec(g1.shape)],
        out_specs=pl.BlockSpec((l, dc), lambda j, s: (s, j)),
        out_shape=jax.ShapeDtypeStruct((n_seq * l, d), F32),
        scratch_shapes=[pltpu.VMEM((n2 * two_n1, dc), F32)],
        compiler_params=_cparams(("arbitrary", "arbitrary")),
        name="hy_longconv_two_stage",
    )(vv, x0, skip.reshape(1, d).astype(F32), tfr, tfi, f1, f2, f2c, g1)


def _hy_out_kernel(y_ref, w_ref, b_ref, x_ref, mod_ref, out_ref, *, gate_idx):
    y = _dot(y_ref[...].astype(BF16), w_ref[...]) + b_ref[...]
    out_ref[...] = x_ref[...] + mod_ref[0, gate_idx:gate_idx + 1, :] * y


def hy_out(yg, w_out, b_out, x, mod, gate_idx, t_ctx, l_lat):
    t, d = x.shape
    grp = functools.partial(_group_of_tile, tm=TM, t_ctx=t_ctx, l_lat=l_lat)
    row_spec = lambda w: pl.BlockSpec((TM, w), lambda i: (i, 0))
    return pl.pallas_call(
        functools.partial(_hy_out_kernel, gate_idx=gate_idx),
        grid=(t // TM,),
        in_specs=[row_spec(d), _const_spec(w_out.shape), _const_spec((1, d)), row_spec(d),
                  pl.BlockSpec((1,) + mod.shape[1:], lambda i: (grp(i), 0, 0))],
        out_specs=row_spec(d),
        out_shape=jax.ShapeDtypeStruct((t, d), F32),
        compiler_params=_cparams(("arbitrary",)),
        name="hy_out",
    )(yg, w_out, b_out.reshape(1, d).astype(F32), x, mod)


def _grid_pos_embed(rows, d):
    r = np.repeat(np.arange(rows, dtype=np.float64), GRID_W)
    col = np.tile(np.arange(GRID_W, dtype=np.float64), rows)
    quarter = d // 4
    omega = 1.0 / (10000.0 ** (np.arange(quarter, dtype=np.float64) / quarter))
    ang_r = r[:, None] * omega
    ang_c = col[:, None] * omega
    pe = np.concatenate([np.sin(ang_r), np.cos(ang_r), np.sin(ang_c), np.cos(ang_c)], axis=-1)
    return jnp.asarray(pe.astype(np.float32))


def delta_layer(x, g, mod, state_lat, w_in, conv_w, a_log, dt_bias, g_head, w_out, dims):
    t_ctx, l_ctx, n_ctx, l_lat, n_lat = dims
    hk = DN_HEADS * DN_DK
    nqkv = 3 * hk
    w_qkv = w_in[:, :nqkv].astype(BF16)
    w_z = w_in[:, nqkv:nqkv + hk].astype(BF16)
    w_ab = jnp.zeros((w_in.shape[0], LANES), BF16).at[:, :4 * DN_HEADS].set(w_in[:, nqkv + hk:].astype(BF16))
    qkv, z, ab = norm_mod_matmul(x, g, mod, 0, 1, [w_qkv, w_z, w_ab], None, t_ctx, l_lat)
    q, k, v, gc, beta, gct = dn_prep(qkv, ab, conv_w.astype(F32), a_log, dt_bias, t_ctx, l_ctx, l_lat)
    outs = []
    ctx_states = []
    seq_lens = (l_ctx,) * n_ctx + (l_lat,) * n_lat
    for direction in range(2):
        o, s_all = dn_chunk_scan(q, k, v, gc, beta, gct, state_lat.astype(F32), direction=direction,
                                 seq_lens=seq_lens, n_zero_init=n_ctx)
        outs.append(o)
        ctx_states.append(s_all[:n_ctx])
    x = dn_out(outs[0], outs[1], z, g_head, w_out.astype(BF16), x, mod, 2, t_ctx, l_lat)
    return x, jnp.stack(ctx_states, axis=1)


def hyena_layer(x, g, mod, w_in, b_in, conv_w, conv_b, freq, fw1, fb1, fw2, fb2, fw3, fb3, fw4,
                skip, w_out, b_out, dims):
    t_ctx, l_ctx, n_ctx, l_lat, n_lat = dims
    d = x.shape[1]
    (u,) = norm_mod_matmul(x, g, mod, 0, 1, [w_in.astype(BF16)], [b_in.astype(F32)], t_ctx, l_lat)
    x0, vv = hy_conv_gate(u, conv_w.astype(F32), conv_b, t_ctx, l_ctx, l_lat)
    filt_c = hy_filter(l_ctx, d, freq, fw1, fb1, fw2, fb2, fw3, fb3, fw4)
    filt_l = hy_filter(l_lat, d, freq, fw1, fb1, fw2, fb2, fw3, fb3, fw4)
    y_c = hy_longconv_dense(vv, x0, skip, filt_c, n_ctx, l_ctx)
    y_l = hy_longconv_two_stage(vv, x0, skip, filt_l, t_ctx, n_lat, l_lat)
    yg = jnp.concatenate([y_c, y_l], axis=0)
    return hy_out(yg, w_out.astype(BF16), b_out, x, mod, 2, t_ctx, l_lat)


def kernel(x_prompt, x_sample, state_delta, c, c_ctx, w_ada, b_ada, g_norm, dn_w_in, dn_conv, dn_a_log, dn_dt_bias, dn_g_head, dn_w_out, hy_w_in, hy_b_in, hy_conv, hy_conv_b, hy_freq, hy_f_w1, hy_f_b1, hy_f_w2, hy_f_b2, hy_f_w3, hy_f_b3, hy_f_w4, hy_skip, hy_w_out, hy_b_out, w_router, router_bias, moe_w1, moe_w3, moe_w2, g_final):
    n_ctx, l_ctx, d = x_prompt.shape
    n_lat, l_lat, _ = x_sample.shape
    depth = w_ada.shape[0]
    t_ctx = n_ctx * l_ctx
    dims = (t_ctx, l_ctx, n_ctx, l_lat, n_lat)
    assert l_ctx % TM == 0 and l_lat % TM == 0 and t_ctx % l_lat == 0
    assert n_lat + 1 <= SUBLANES

    pos = _grid_pos_embed(l_lat // GRID_W, d)
    x = jnp.concatenate([x_prompt.reshape(t_ctx, d).astype(F32),
                         (x_sample.astype(F32) + pos[None]).reshape(n_lat * l_lat, d)], axis=0)
    cond = jnp.zeros((SUBLANES, d), F32).at[0].set(c_ctx.astype(F32)).at[1:1 + n_lat].set(c.astype(F32))
    mod_all = ada_modulation(cond, w_ada.astype(F32), b_ada.astype(F32))
    mod_all = mod_all.reshape(depth, SUBLANES, 6, d)

    ctx_states = []
    for i in range(depth):
        mod = mod_all[i]
        j = i // 2
        if i % 2 == 0:
            x, s_ctx = delta_layer(x, g_norm[i, 0], mod, state_delta[:, j], dn_w_in[j], dn_conv[j],
                                   dn_a_log[j], dn_dt_bias[j], dn_g_head[j], dn_w_out[j], dims)
            ctx_states.append(s_ctx.astype(x_prompt.dtype))
        else:
            x = hyena_layer(x, g_norm[i, 0], mod, hy_w_in[j], hy_b_in[j], hy_conv[j], hy_conv_b[j],
                            hy_freq[j], hy_f_w1[j], hy_f_b1[j], hy_f_w2[j], hy_f_b2[j], hy_f_w3[j],
                            hy_f_b3[j], hy_f_w4[j], hy_skip[j], hy_w_out[j], hy_b_out[j], dims)
        x = moe_layer(x, g_norm[i, 1], mod, w_router, router_bias, moe_w1, moe_w3, moe_w2, i,
                      t_ctx, l_lat, g_final if i == depth - 1 else None)
    y_prompt = x[:t_ctx].reshape(n_ctx, l_ctx, d).astype(x_prompt.dtype)
    y_sample = x[t_ctx:].reshape(n_lat, l_lat, d).astype(x_sample.dtype)
    new_state = jnp.stack(ctx_states, axis=1)
    return (y_prompt, y_sample, new_state)
```

```python
import functools
import math

import numpy as np
import jax
import jax.numpy as jnp
from jax import lax
from jax.experimental import pallas as pl
from jax.experimental.pallas import tpu as pltpu

F32 = jnp.float32
BF16 = jnp.bfloat16
HI = lax.Precision.HIGHEST

EPS = 1e-6
GRID_W = 64
DN_HEADS = 8
DN_DK = 128
DN_DV = 128
DN_CHUNK = 64
HY_EMB = 33
HY_ORDER = 64
HY_DECAY_SHORT_PCT = 0.3
HY_DECAY_LONG_PCT = 1.5
HY_DECAY_TARGET = 1e-2
N_EXPERTS = 32
N_GROUPS = 4
EXPERTS_PER_GROUP = N_EXPERTS // N_GROUPS

LANES = 128
SUBLANES = 8
TM = 256
MOE_TILE = 256
FFT_N2 = 128
DN_PREP_CHUNKS = 2
VMEM_LIMIT = 48 * 1024 * 1024


def _cparams(sem):
    return pltpu.CompilerParams(dimension_semantics=sem, vmem_limit_bytes=VMEM_LIMIT)


def _sigmoid(x):
    return 1.0 / (1.0 + jnp.exp(-x))


def _silu(x):
    return x * _sigmoid(x)


def _dot(a, b, precision=None):
    return jnp.dot(a, b, preferred_element_type=F32, precision=precision)


def _const_spec(shape):
    nd = len(shape)
    return pl.BlockSpec(shape, lambda *_: (0,) * nd)


def _ada_kernel(cond_ref, w_ref, b_ref, out_ref):
    s = _silu(cond_ref[...])
    out_ref[0] = _dot(s, w_ref[0], HI) + b_ref[0]


def ada_modulation(cond, w_ada, b_ada):
    depth, d, n = w_ada.shape
    tn = 1536
    return pl.pallas_call(
        _ada_kernel,
        grid=(depth, n // tn),
        in_specs=[
            pl.BlockSpec((SUBLANES, d), lambda i, j: (0, 0)),
            pl.BlockSpec((1, d, tn), lambda i, j: (i, 0, j)),
            pl.BlockSpec((1, 1, tn), lambda i, j: (i, 0, j)),
        ],
        out_specs=pl.BlockSpec((1, SUBLANES, tn), lambda i, j: (i, 0, j)),
        out_shape=jax.ShapeDtypeStruct((depth, SUBLANES, n), F32),
        compiler_params=_cparams(("arbitrary", "arbitrary")),
        name="ada_modulation",
    )(cond, w_ada, b_ada.reshape(depth, 1, n))


def _norm_mod(x, g, mod_ref, shift_idx, scale_idx):
    ms = jnp.mean(x * x, axis=-1, keepdims=True)
    y = x * lax.rsqrt(ms + EPS) * g
    scale = mod_ref[0, scale_idx:scale_idx + 1, :]
    shift = mod_ref[0, shift_idx:shift_idx + 1, :]
    return y * (1.0 + scale) + shift


def _group_of_tile(i, tm, t_ctx, l_lat):
    r = i * tm
    return jnp.where(r < t_ctx, 0, 1 + (r - t_ctx) // l_lat)


def _nmm_kernel(x_ref, g_ref, mod_ref, *refs, n_out, has_bias, shift_idx, scale_idx):
    h = _norm_mod(x_ref[...], g_ref[...], mod_ref, shift_idx, scale_idx).astype(BF16)
    w_refs = refs[:n_out]
    b_refs = refs[n_out:n_out + (n_out if has_bias else 0)]
    o_refs = refs[n_out + len(b_refs):]
    for k in range(n_out):
        acc = _dot(h, w_refs[k][...])
        if has_bias:
            acc = acc + b_refs[k][...]
        o_refs[k][...] = acc


def norm_mod_matmul(x, g, mod, shift_idx, scale_idx, weights, biases, t_ctx, l_lat):
    t, d = x.shape
    n_out = len(weights)
    has_bias = biases is not None
    grp = functools.partial(_group_of_tile, tm=TM, t_ctx=t_ctx, l_lat=l_lat)
    in_specs = [
        pl.BlockSpec((TM, d), lambda i: (i, 0)),
        _const_spec((1, d)),
        pl.BlockSpec((1,) + mod.shape[1:], lambda i: (grp(i), 0, 0)),
    ]
    args = [x, g.reshape(1, d), mod]
    for w in weights:
        in_specs.append(_const_spec(w.shape))
        args.append(w)
    if has_bias:
        for b in biases:
            in_specs.append(_const_spec((1, b.shape[-1])))
            args.append(b.reshape(1, -1))
    out_specs = [pl.BlockSpec((TM, w.shape[1]), lambda i: (i, 0)) for w in weights]
    out_shape = [jax.ShapeDtypeStruct((t, w.shape[1]), F32) for w in weights]
    return pl.pallas_call(
        functools.partial(_nmm_kernel, n_out=n_out, has_bias=has_bias,
                          shift_idx=shift_idx, scale_idx=scale_idx),
        grid=(t // TM,),
        in_specs=in_specs,
        out_specs=out_specs,
        out_shape=out_shape,
        compiler_params=_cparams(("arbitrary",)),
        name="norm_mod_matmul",
    )(*args)


def _seq_flags(i, tm, t_ctx, l_ctx, l_lat):
    r = i * tm
    in_ctx = r < t_ctx
    pos = jnp.where(in_ctx, r % l_ctx, (r - t_ctx) % l_lat)
    length = jnp.where(in_ctx, l_ctx, l_lat)
    return pos != 0, (pos + tm) != length


def _conv3(x, prev8, next8, w_ref, has_prev, has_next):
    tm = x.shape[0]
    row = lax.broadcasted_iota(jnp.int32, (tm, 1), 0)
    halo_p = jnp.where(has_prev, prev8[SUBLANES - 1:SUBLANES, :], 0.0)
    halo_n = jnp.where(has_next, next8[0:1, :], 0.0)
    x_prev = jnp.where(row == 0, halo_p, pltpu.roll(x, 1, 0))
    x_next = jnp.where(row == tm - 1, halo_n, pltpu.roll(x, tm - 1, 0))
    return w_ref[0:1, :] * x_prev + w_ref[1:2, :] * x + w_ref[2:3, :] * x_next


def _halo_specs(c, tm, t):
    nb8 = t // SUBLANES
    per = tm // SUBLANES
    return [
        pl.BlockSpec((tm, c), lambda i: (i, 0)),
        pl.BlockSpec((SUBLANES, c), lambda i: (jnp.maximum(i * per - 1, 0), 0)),
        pl.BlockSpec((SUBLANES, c), lambda i: (jnp.minimum((i + 1) * per, nb8 - 1), 0)),
    ]


def _softplus(x):
    return jnp.maximum(x, 0.0) + jnp.log(1.0 + jnp.exp(-jnp.abs(x)))


def _dn_prep_kernel(x_ref, xp_ref, xn_ref, cw_ref, ab_ref, nega_ref, dtb_ref, tri_ref,
                    q_ref, k_ref, v_ref, gc_ref, beta_ref, gct_ref,
                    *, t_ctx, l_ctx, l_lat):
    i = pl.program_id(0)
    has_prev, has_next = _seq_flags(i, TM, t_ctx, l_ctx, l_lat)
    c = _conv3(x_ref[...], xp_ref[...], xn_ref[...], cw_ref, has_prev, has_next)
    s = _silu(c)
    hk = DN_HEADS * DN_DK
    for h in range(DN_HEADS):
        lo, hi = h * DN_DK, (h + 1) * DN_DK
        qh = s[:, lo:hi]
        kh = s[:, hk + lo:hk + hi]
        qn = lax.rsqrt(jnp.sum(qh * qh, axis=-1, keepdims=True) + EPS)
        kn = lax.rsqrt(jnp.sum(kh * kh, axis=-1, keepdims=True) + EPS)
        q_ref[:, lo:hi] = (qh * (qn * (DN_DK ** -0.5))).astype(BF16)
        k_ref[:, lo:hi] = (kh * kn).astype(BF16)
    v_ref[...] = s[:, 2 * hk:].astype(BF16)
    ab = ab_ref[...]
    lg = nega_ref[...] * _softplus(ab + dtb_ref[...])
    beta_ref[...] = _sigmoid(ab)
    nch = TM // DN_CHUNK
    for d in range(2):
        gc = _dot(tri_ref[d], lg, HI)
        gc_ref[d] = gc
        gct = gc.T
        for ch in range(nch):
            gct_ref[d, ch] = gct[0:2 * DN_HEADS, ch * DN_CHUNK:(ch + 1) * DN_CHUNK]


def _chunk_tri():
    idx = np.arange(TM)
    same = (idx[:, None] // DN_CHUNK) == (idx[None, :] // DN_CHUNK)
    fwd = same & (idx[None, :] <= idx[:, None])
    bwd = same & (idx[None, :] >= idx[:, None])
    return jnp.asarray(np.stack([fwd, bwd]).astype(np.float32))


def dn_prep(qkv, ab, conv_w, a_log, dt_bias, t_ctx, l_ctx, l_lat):
    t, c = qkv.shape
    hd = DN_HEADS * DN_DK
    nega = jnp.zeros((1, LANES), F32).at[0, :2 * DN_HEADS].set(-jnp.exp(a_log.astype(F32)).reshape(-1))
    dtb = jnp.zeros((1, LANES), F32).at[0, :2 * DN_HEADS].set(dt_bias.astype(F32).reshape(-1))
    cw = jnp.zeros((SUBLANES, c), F32).at[:conv_w.shape[0]].set(conv_w)
    nch = TM // DN_CHUNK
    row_spec = lambda w: pl.BlockSpec((TM, w), lambda i: (i, 0))
    return pl.pallas_call(
        functools.partial(_dn_prep_kernel, t_ctx=t_ctx, l_ctx=l_ctx, l_lat=l_lat),
        grid=(t // TM,),
        in_specs=_halo_specs(c, TM, t) + [
            _const_spec((SUBLANES, c)),
            row_spec(LANES),
            _const_spec((1, LANES)),
            _const_spec((1, LANES)),
            _const_spec((2, TM, TM)),
        ],
        out_specs=[
            row_spec(hd), row_spec(hd), row_spec(hd),
            pl.BlockSpec((2, TM, LANES), lambda i: (0, i, 0)),
            row_spec(LANES),
            pl.BlockSpec((2, nch, 2 * DN_HEADS, DN_CHUNK), lambda i: (0, i, 0, 0)),
        ],
        out_shape=[
            jax.ShapeDtypeStruct((t, hd), BF16),
            jax.ShapeDtypeStruct((t, hd), BF16),
            jax.ShapeDtypeStruct((t, hd), BF16),
            jax.ShapeDtypeStruct((2, t, LANES), F32),
            jax.ShapeDtypeStruct((t, LANES), F32),
            jax.ShapeDtypeStruct((2, t // DN_CHUNK, 2 * DN_HEADS, DN_CHUNK), F32),
        ],
        compiler_params=_cparams(("arbitrary",)),
        name="dn_prep",
    )(qkv, qkv, qkv, cw, ab, nega, dtb, _chunk_tri())


def _bmm(a, b):
    return _dot(a.astype(BF16), b.astype(BF16))


def _inv_unit_tri_batch(lms, in_blk, eye):
    dgs = [jnp.where(in_blk, lm, 0.0) for lm in lms]
    offs = [lm - dg for lm, dg in zip(lms, dgs)]
    n1 = [-dg for dg in dgs]
    n2 = [_bmm(a, a) for a in n1]
    p = [eye + a for a in n1]
    n4 = [_bmm(a, a) for a in n2]
    p = [x + _bmm(x, a) for x, a in zip(p, n2)]
    n8 = [_bmm(a, a) for a in n4]
    p = [x + _bmm(x, a) for x, a in zip(p, n4)]
    p = [x + _bmm(x, a) for x, a in zip(p, n8)]
    m = [_bmm(x, o) for x, o in zip(p, offs)]
    m2 = [_bmm(a, a) for a in m]
    t1 = [eye - a for a in m]
    t1 = [x + _bmm(x, a) for x, a in zip(t1, m2)]
    return [_bmm(x, y) for x, y in zip(t1, p)]


def _dn_chunk_kernel(blk_ref, seq_ref, first_ref, last_ref,
                     q_ref, k_ref, v_ref, gc_ref, beta_ref, gct_ref, s0_ref,
                     o_ref, sfin_ref, s_scr, attn_scr, u_scr, qw_scr, kd_scr, gl_scr,
                     *, direction, n_zero_init):
    step = pl.program_id(0)
    c_sz = DN_CHUNK
    nch = TM // c_sz
    heads = range(DN_HEADS)

    @pl.when(first_ref[step] == 1)
    def _():
        s_scr[...] = jnp.where(seq_ref[step] >= n_zero_init, s0_ref[0, 0], 0.0)

    row = lax.broadcasted_iota(jnp.int32, (c_sz, c_sz), 0)
    col = lax.broadcasted_iota(jnp.int32, (c_sz, c_sz), 1)
    if direction == 0:
        incl, strict = row >= col, row > col
        last = c_sz - 1
    else:
        incl, strict = row <= col, row < col
        last = 0
    in_blk = (row // 16) == (col // 16)
    eye = (row == col).astype(F32)

    def lanes(h):
        return slice(h * DN_DK, (h + 1) * DN_DK)

    def prep_body(it, carry):
        units = [(it * DN_PREP_CHUNKS + c, h) for c in range(DN_PREP_CHUNKS) for h in heads]
        idx = range(len(units))
        rows = [pl.ds(pl.multiple_of(ch * c_sz, c_sz), c_sz) for ch, _ in units]
        gls = [direction * DN_HEADS + h for _, h in units]
        qs = [q_ref[rows[i], lanes(units[i][1])].astype(F32) for i in idx]
        ks = [k_ref[rows[i], lanes(units[i][1])].astype(F32) for i in idx]
        gcc = [gc_ref[0, rows[i], gls[i]:gls[i] + 1] for i in idx]
        beta = [beta_ref[rows[i], 2 * DN_HEADS + gls[i]:2 * DN_HEADS + gls[i] + 1] for i in idx]
        gcr = [gct_ref[0, units[i][0]][gls[i]:gls[i] + 1, :] for i in idx]
        dec = [jnp.exp(jnp.where(incl, gcc[i] - gcr[i], -jnp.inf)) for i in idx]
        kb = [ks[i] * beta[i] for i in idx]
        prod = [lax.dot_general(jnp.concatenate([qs[i], kb[i]], axis=0).astype(BF16), ks[i].astype(BF16),
                                (((1,), (1,)), ((), ())), preferred_element_type=F32)
                for i in idx]
        for i, (ch, h) in enumerate(units):
            attn_scr[ch, h] = (prod[i][:c_sz] * dec[i]).astype(BF16)
        lms = [jnp.where(strict, prod[i][c_sz:] * dec[i], 0.0) for i in idx]
        tinv = _inv_unit_tri_batch(lms, in_blk, eye)
        eg = [jnp.exp(gcc[i]) for i in idx]
        uw = [_bmm(tinv[i], jnp.concatenate(
                  [v_ref[rows[i], lanes(units[i][1])].astype(F32) * beta[i], kb[i] * eg[i]], axis=1))
              for i in idx]
        for i, (ch, h) in enumerate(units):
            u_scr[rows[i], lanes(h)] = uw[i][:, :DN_DV]
            qw_scr[ch, h] = jnp.concatenate([qs[i] * eg[i], uw[i][:, DN_DV:]], axis=0).astype(BF16)
            g_last = gcc[i][last:last + 1, :]
            kd_scr[ch, h] = (ks[i] * jnp.exp(g_last - gcc[i])).T.astype(BF16)
            gl_scr[ch, h] = jnp.broadcast_to(jnp.exp(g_last), (1, DN_DV))
        return carry

    lax.fori_loop(0, nch // DN_PREP_CHUNKS, prep_body, 0)

    def scan_body(ci, carry):
        ch = ci if direction == 0 else nch - 1 - ci
        r0 = pl.multiple_of(ch * c_sz, c_sz)
        rows = pl.ds(r0, c_sz)
        s = [s_scr[h] for h in heads]
        qs_ws = [_dot(qw_scr[ch, h], s[h].astype(BF16)) for h in heads]
        vb = [(u_scr[rows, lanes(h)] - qs_ws[h][c_sz:]).astype(BF16) for h in heads]
        for h in heads:
            o_ref[rows, lanes(h)] = qs_ws[h][:c_sz] + _dot(attn_scr[ch, h], vb[h])
            s_scr[h] = s[h] * gl_scr[ch, h] + _dot(kd_scr[ch, h], vb[h])
        return carry

    lax.fori_loop(0, nch, scan_body, 0)

    @pl.when(last_ref[step] == 1)
    def _():
        sfin_ref[0, 0] = s_scr[...]


def _scan_tables(direction, seq_lens):
    blk, seq, first, last = [], [], [], []
    base = 0
    for s, length in enumerate(seq_lens):
        nblk = length // TM
        order = range(nblk) if direction == 0 else range(nblk - 1, -1, -1)
        for pos, jj in enumerate(order):
            blk.append(base + jj)
            seq.append(s)
            first.append(int(pos == 0))
            last.append(int(pos == nblk - 1))
        base += nblk
    return [jnp.asarray(np.asarray(a, np.int32)) for a in (blk, seq, first, last)]


def dn_chunk_scan(q, k, v, gc, beta, gct, s0, *, direction, seq_lens, n_zero_init):
    t, hd = q.shape
    nch = TM // DN_CHUNK
    n_seq = len(seq_lens)
    tables = _scan_tables(direction, seq_lens)
    n_steps = int(tables[0].shape[0])
    row_spec = lambda w: pl.BlockSpec((TM, w), lambda i, blk, seq, fst, lst: (blk[i], 0))
    state_blk = (1, 1, DN_HEADS, DN_DK, DN_DV)
    grid_spec = pltpu.PrefetchScalarGridSpec(
        num_scalar_prefetch=4,
        grid=(n_steps,),
        in_specs=[
            row_spec(hd), row_spec(hd), row_spec(hd),
            pl.BlockSpec((1, TM, LANES), lambda i, blk, seq, fst, lst: (direction, blk[i], 0)),
            row_spec(LANES),
            pl.BlockSpec((1, nch, 2 * DN_HEADS, DN_CHUNK),
                         lambda i, blk, seq, fst, lst: (direction, blk[i], 0, 0)),
            pl.BlockSpec(state_blk, lambda i, blk, seq, fst, lst:
                         (jnp.maximum(seq[i] - n_zero_init, 0), direction, 0, 0, 0)),
        ],
        out_specs=[
            row_spec(hd),
            pl.BlockSpec(state_blk, lambda i, blk, seq, fst, lst: (seq[i], 0, 0, 0, 0)),
        ],
        scratch_shapes=[
            pltpu.VMEM((DN_HEADS, DN_DK, DN_DV), F32),
            pltpu.VMEM((nch, DN_HEADS, DN_CHUNK, DN_CHUNK), BF16),
            pltpu.VMEM((TM, hd), F32),
            pltpu.VMEM((nch, DN_HEADS, 2 * DN_CHUNK, DN_DK), BF16),
            pltpu.VMEM((nch, DN_HEADS, DN_DK, DN_CHUNK), BF16),
            pltpu.VMEM((nch, DN_HEADS, 1, DN_DV), F32),
        ],
    )
    o, sfin = pl.pallas_call(
        functools.partial(_dn_chunk_kernel, direction=direction, n_zero_init=n_zero_init),
        grid_spec=grid_spec,
        out_shape=[
            jax.ShapeDtypeStruct((t, hd), F32),
            jax.ShapeDtypeStruct((n_seq,) + state_blk[1:], F32),
        ],
        compiler_params=_cparams(("arbitrary",)),
        name="dn_chunk_scan_d%d" % direction,
    )(*tables, q, k, v, gc, beta, gct, s0)
    return o, sfin[:, 0]


def _dn_out_kernel(of_ref, ob_ref, z_ref, gh_ref, w_ref, x_ref, mod_ref, out_ref, *, gate_idx):
    o = of_ref[...] + ob_ref[...]
    z = z_ref[...]
    parts = []
    for h in range(DN_HEADS):
        lanes = slice(h * DN_DV, (h + 1) * DN_DV)
        oh = o[:, lanes]
        ms = jnp.mean(oh * oh, axis=-1, keepdims=True)
        parts.append(oh * lax.rsqrt(ms + EPS) * gh_ref[...])
    on = jnp.concatenate(parts, axis=1) * _silu(z)
    y = _dot(on.astype(BF16), w_ref[...])
    out_ref[...] = x_ref[...] + mod_ref[0, gate_idx:gate_idx + 1, :] * y


def dn_out(o_f, o_b, z, g_head, w_out, x, mod, gate_idx, t_ctx, l_lat):
    t, d = x.shape
    hd = o_f.shape[1]
    grp = functools.partial(_group_of_tile, tm=TM, t_ctx=t_ctx, l_lat=l_lat)
    row_spec = lambda w: pl.BlockSpec((TM, w), lambda i: (i, 0))
    return pl.pallas_call(
        functools.partial(_dn_out_kernel, gate_idx=gate_idx),
        grid=(t // TM,),
        in_specs=[
            row_spec(hd), row_spec(hd), row_spec(hd),
            _const_spec((1, DN_DV)),
            _const_spec(w_out.shape),
            row_spec(d),
            pl.BlockSpec((1,) + mod.shape[1:], lambda i: (grp(i), 0, 0)),
        ],
        out_specs=row_spec(d),
        out_shape=jax.ShapeDtypeStruct((t, d), F32),
        compiler_params=_cparams(("arbitrary",)),
        name="dn_out",
    )(o_f, o_b, z, g_head.reshape(1, DN_DV).astype(F32), w_out, x, mod)


def _first_argmax(vals, lane, valid):
    masked = jnp.where(valid, vals, -jnp.inf)
    m = jnp.max(masked, axis=-1, keepdims=True)
    idx = jnp.min(jnp.where(valid & (masked == m), lane, float(LANES)), axis=-1, keepdims=True)
    return m, idx


def _router_kernel(x_ref, g_ref, mod_ref, wh_ref, wl_ref, rb_ref, h_ref, idx_ref, wt_ref,
                   *, shift_idx, scale_idx):
    h = _norm_mod(x_ref[...], g_ref[...], mod_ref, shift_idx, scale_idx)
    hb = h.astype(BF16)
    h_ref[...] = hb
    hl = (h - hb.astype(F32)).astype(BF16)
    wh = wh_ref[...]
    logits = _dot(hb, wh) + (_dot(hl, wh) + _dot(hb, wl_ref[...]))
    scores = _sigmoid(logits)
    sel = scores + rb_ref[...]
    tm = scores.shape[0]
    lane = lax.broadcasted_iota(jnp.int32, (tm, LANES), 1).astype(F32)
    best_score = None
    for g in range(N_GROUPS):
        in_g = (lane >= g * EXPERTS_PER_GROUP) & (lane < (g + 1) * EXPERTS_PER_GROUP)
        m1, i1 = _first_argmax(sel, lane, in_g)
        m2, i2 = _first_argmax(sel, lane, in_g & (lane != i1))
        gs = m1 + m2
        if best_score is None:
            best_score, b1, b2 = gs, i1, i2
        else:
            better = gs > best_score
            best_score = jnp.where(better, gs, best_score)
            b1 = jnp.where(better, i1, b1)
            b2 = jnp.where(better, i2, b2)
    w1 = jnp.sum(jnp.where(lane == b1, scores, 0.0), axis=-1, keepdims=True)
    w2 = jnp.sum(jnp.where(lane == b2, scores, 0.0), axis=-1, keepdims=True)
    tot = w1 + w2
    idx_ref[...] = jnp.where(lane == 0.0, b1, jnp.where(lane == 1.0, b2, 0.0)).astype(jnp.int32)
    wt_ref[...] = jnp.where(lane == 0.0, w1 / tot, jnp.where(lane == 1.0, w2 / tot, 0.0))


def moe_router(x, g, mod, shift_idx, scale_idx, w_router, router_bias, t_ctx, l_lat):
    t, d = x.shape
    grp = functools.partial(_group_of_tile, tm=TM, t_ctx=t_ctx, l_lat=l_lat)
    wr = jnp.zeros((d, LANES), F32).at[:, :N_EXPERTS].set(w_router.astype(F32))
    wh = wr.astype(BF16)
    wl = (wr - wh.astype(F32)).astype(BF16)
    rb = jnp.zeros((1, LANES), F32).at[0, :N_EXPERTS].set(router_bias.astype(F32))
    row_spec = lambda w: pl.BlockSpec((TM, w), lambda i: (i, 0))
    return pl.pallas_call(
        functools.partial(_router_kernel, shift_idx=shift_idx, scale_idx=scale_idx),
        grid=(t // TM,),
        in_specs=[
            row_spec(d),
            _const_spec((1, d)),
            pl.BlockSpec((1,) + mod.shape[1:], lambda i: (grp(i), 0, 0)),
            _const_spec((d, LANES)),
            _const_spec((d, LANES)),
            _const_spec((1, LANES)),
        ],
        out_specs=[row_spec(d), row_spec(LANES), row_spec(LANES)],
        out_shape=[
            jax.ShapeDtypeStruct((t, d), BF16),
            jax.ShapeDtypeStruct((t, LANES), jnp.int32),
            jax.ShapeDtypeStruct((t, LANES), F32),
        ],
        compiler_params=_cparams(("arbitrary",)),
        name="moe_router",
    )(x, g.reshape(1, d), mod, wh, wl, rb)


def _expert_kernel(te_ref, nt_ref, first_ref, nxt_ref, par_ref,
                   xs_ref, w1_hbm, w3_hbm, w2_hbm, ys_ref,
                   w1_buf, w3_buf, w2_buf, w1_bf, w3_bf, w2_bf, sem, *, layer):
    i = pl.program_id(0)
    active = i < nt_ref[0]

    def weight_copies(e, slot):
        return (pltpu.make_async_copy(w1_hbm.at[layer, e], w1_buf.at[slot], sem.at[slot, 0]),
                pltpu.make_async_copy(w3_hbm.at[layer, e], w3_buf.at[slot], sem.at[slot, 1]),
                pltpu.make_async_copy(w2_hbm.at[layer, e], w2_buf.at[slot], sem.at[slot, 2]))

    @pl.when(active & (i == 0))
    def _():
        for cp in weight_copies(te_ref[0], 0):
            cp.start()

    @pl.when(active & (first_ref[i] == 1))
    def _():
        slot = par_ref[i]
        for cp in weight_copies(te_ref[i], slot):
            cp.wait()

        @pl.when(nxt_ref[i] >= 0)
        def _():
            for cp in weight_copies(nxt_ref[i], 1 - slot):
                cp.start(priority=1)

        w1_bf[...] = w1_buf[slot].astype(BF16)
        w3_bf[...] = w3_buf[slot].astype(BF16)
        w2_bf[...] = w2_buf[slot].astype(BF16)

    @pl.when(active)
    def _():
        xs = xs_ref[...]
        a = _dot(xs, w1_bf[...])
        b = _dot(xs, w3_bf[...])
        hid = (_silu(a) * b).astype(BF16)
        ys_ref[...] = _dot(hid, w2_bf[...])

    @pl.when(jnp.logical_not(active))
    def _():
        ys_ref[...] = jnp.zeros_like(ys_ref)


def moe_experts(xs, plan, w1, w3, w2, layer):
    p, d = xs.shape
    de = w1.shape[-1]
    ntile = p // MOE_TILE
    tile_map = lambda i, *_: (i, 0)
    grid_spec = pltpu.PrefetchScalarGridSpec(
        num_scalar_prefetch=5,
        grid=(ntile,),
        in_specs=[
            pl.BlockSpec((MOE_TILE, d), tile_map),
            pl.BlockSpec(memory_space=pl.ANY),
            pl.BlockSpec(memory_space=pl.ANY),
            pl.BlockSpec(memory_space=pl.ANY),
        ],
        out_specs=pl.BlockSpec((MOE_TILE, d), tile_map),
        scratch_shapes=[
            pltpu.VMEM((2, d, de), F32), pltpu.VMEM((2, d, de), F32), pltpu.VMEM((2, de, d), F32),
            pltpu.VMEM((d, de), BF16), pltpu.VMEM((d, de), BF16), pltpu.VMEM((de, d), BF16),
            pltpu.SemaphoreType.DMA((2, 3)),
        ],
    )
    return pl.pallas_call(
        functools.partial(_expert_kernel, layer=layer),
        grid_spec=grid_spec,
        out_shape=jax.ShapeDtypeStruct((p, d), F32),
        compiler_params=_cparams(("arbitrary",)),
        name="moe_experts",
    )(plan["tile_expert"], plan["n_tiles"], plan["first"], plan["next_expert"], plan["parity"],
      xs, w1, w3, w2)


def moe_dispatch_plan(idx):
    t = idx.shape[0]
    n_assign = 2 * t
    p = n_assign + N_EXPERTS * MOE_TILE
    p = -(-p // MOE_TILE) * MOE_TILE
    ntile = p // MOE_TILE
    experts = jnp.arange(N_EXPERTS, dtype=jnp.int32)
    e_flat = idx.reshape(-1)
    onehot = (e_flat[:, None] == experts[None, :]).astype(jnp.int32)
    rank = jnp.cumsum(onehot, axis=0) - onehot
    counts = jnp.sum(onehot, axis=0)
    padded = ((counts + MOE_TILE - 1) // MOE_TILE) * MOE_TILE
    ends = jnp.cumsum(padded)
    starts = ends - padded
    slot = jnp.sum(onehot * (starts[None, :] + rank), axis=1)
    token = jnp.arange(n_assign, dtype=jnp.int32) // 2
    slot_token = jnp.zeros((p,), jnp.int32).at[slot].set(token, unique_indices=True,
                                                          mode="promise_in_bounds")
    tile_id = jnp.arange(ntile, dtype=jnp.int32)
    tile_expert = jnp.sum((tile_id[:, None] * MOE_TILE >= ends[None, :]).astype(jnp.int32), axis=1)
    n_tiles = (ends[-1] // MOE_TILE).astype(jnp.int32)
    used = tile_id < n_tiles
    tile_expert = jnp.where(used, tile_expert, tile_expert[jnp.maximum(n_tiles - 1, 0)])
    tile_expert = jnp.minimum(tile_expert, N_EXPERTS - 1).astype(jnp.int32)
    prev = jnp.concatenate([jnp.full((1,), -1, jnp.int32), tile_expert[:-1]])
    first = (used & (tile_expert != prev)).astype(jnp.int32)
    parity = ((jnp.cumsum(first) - 1) % 2).astype(jnp.int32)
    cand = jnp.where(counts > 0, experts, N_EXPERTS)
    later = lax.cummin(cand[::-1])[::-1]
    nxt_e = jnp.concatenate([later[1:], jnp.full((1,), N_EXPERTS, jnp.int32)])
    nxt_e = jnp.where(nxt_e >= N_EXPERTS, -1, nxt_e).astype(jnp.int32)
    next_expert = nxt_e[tile_expert]
    return dict(slot_token=slot_token, slot=slot.reshape(t, 2), tile_expert=tile_expert,
                n_tiles=n_tiles.reshape(1), first=first, next_expert=next_expert, parity=parity)


def _combine_kernel(x_ref, y0_ref, y1_ref, wt_ref, mod_ref, *refs, gate_idx, final_norm):
    moe = wt_ref[:, 0:1] * y0_ref[...] + wt_ref[:, 1:2] * y1_ref[...]
    x = x_ref[...] + mod_ref[0, gate_idx:gate_idx + 1, :] * moe
    if final_norm:
        gf_ref, out_ref = refs
        ms = jnp.mean(x * x, axis=-1, keepdims=True)
        out_ref[...] = x * lax.rsqrt(ms + EPS) * gf_ref[...]
    else:
        (out_ref,) = refs
        out_ref[...] = x


def moe_combine(x, y0, y1, wt, mod, gate_idx, t_ctx, l_lat, g_final=None):
    t, d = x.shape
    grp = functools.partial(_group_of_tile, tm=TM, t_ctx=t_ctx, l_lat=l_lat)
    row_spec = lambda w: pl.BlockSpec((TM, w), lambda i: (i, 0))
    in_specs = [row_spec(d), row_spec(d), row_spec(d), row_spec(LANES),
                pl.BlockSpec((1,) + mod.shape[1:], lambda i: (grp(i), 0, 0))]
    args = [x, y0, y1, wt, mod]
    if g_final is not None:
        in_specs.append(_const_spec((1, d)))
        args.append(g_final.reshape(1, d).astype(F32))
    return pl.pallas_call(
        functools.partial(_combine_kernel, gate_idx=gate_idx, final_norm=g_final is not None),
        grid=(t // TM,),
        in_specs=in_specs,
        out_specs=row_spec(d),
        out_shape=jax.ShapeDtypeStruct((t, d), F32),
        compiler_params=_cparams(("arbitrary",)),
        name="moe_combine",
    )(*args)


def moe_layer(x, g, mod, w_router, router_bias, w1, w3, w2, layer, t_ctx, l_lat, g_final=None):
    h, idx_pad, wt_pad = moe_router(x, g, mod, 3, 4, w_router, router_bias, t_ctx, l_lat)
    plan = moe_dispatch_plan(idx_pad[:, :2])
    take = lambda rows, index: rows.at[index].get(mode="promise_in_bounds")
    xs = take(h, plan["slot_token"])
    ys = moe_experts(xs, plan, w1, w3, w2, layer)
    y0 = take(ys, plan["slot"][:, 0])
    y1 = take(ys, plan["slot"][:, 1])
    return moe_combine(x, y0, y1, wt_pad, mod, 5, t_ctx, l_lat, g_final)


def _hy_conv_kernel(u_ref, up_ref, un_ref, cw_ref, cb_ref, x0_ref, vv_ref, *, t_ctx, l_ctx, l_lat):
    i = pl.program_id(0)
    has_prev, has_next = _seq_flags(i, TM, t_ctx, l_ctx, l_lat)
    c = _conv3(u_ref[...], up_ref[...], un_ref[...], cw_ref, has_prev, has_next) + cb_ref[...]
    d = c.shape[1] // 3
    x0_ref[...] = c[:, :d]
    vv_ref[...] = c[:, 2 * d:] * c[:, d:2 * d]


def hy_conv_gate(u, conv_w, conv_b, t_ctx, l_ctx, l_lat):
    t, c = u.shape
    d = c // 3
    cw = jnp.zeros((SUBLANES, c), F32).at[:conv_w.shape[0]].set(conv_w)
    row_spec = lambda w: pl.BlockSpec((TM, w), lambda i: (i, 0))
    return pl.pallas_call(
        functools.partial(_hy_conv_kernel, t_ctx=t_ctx, l_ctx=l_ctx, l_lat=l_lat),
        grid=(t // TM,),
        in_specs=_halo_specs(c, TM, t) + [_const_spec((SUBLANES, c)), _const_spec((1, c))],
        out_specs=[row_spec(d), row_spec(d)],
        out_shape=[jax.ShapeDtypeStruct((t, d), F32), jax.ShapeDtypeStruct((t, d), F32)],
        compiler_params=_cparams(("arbitrary",)),
        name="hy_conv_gate",
    )(u, u, u, cw, conv_b.reshape(1, c).astype(F32))


def _hy_features(n):
    t = np.linspace(0.0, 1.0, n)[:, None]
    bands = (HY_EMB - 1) // 2
    band_w = np.linspace(1e-4, bands - 1, bands)[None, :]
    ang = (2.0 * math.pi / n) * np.arange(n)[:, None] * band_w
    z = np.concatenate([t, np.cos(ang), -np.sin(ang)], axis=-1)
    zp = np.zeros((n, LANES))
    zp[:, :HY_EMB] = z
    return jnp.asarray(zp.astype(np.float32))


def _hy_hid_kernel(z_ref, f_ref, w1_ref, b1_ref, w2_ref, b2_ref, w3_ref, b3_ref, hid_ref):
    h = jnp.sin(f_ref[0:1, :] * (_dot(z_ref[...], w1_ref[...], HI) + b1_ref[...]))
    h = jnp.sin(f_ref[1:2, :] * (_dot(h, w2_ref[...], HI) + b2_ref[...]))
    h = jnp.sin(f_ref[2:3, :] * (_dot(h, w3_ref[...], HI) + b3_ref[...]))
    hid_ref[...] = h


def _hy_filt_kernel(hid_ref, w4_ref, z_ref, dl_ref, filt_ref):
    f = _dot(hid_ref[...], w4_ref[...], HI)
    f = f * jnp.exp(-z_ref[:, 0:1] * jnp.abs(dl_ref[...]))
    nrm = jnp.sum(jnp.abs(f), axis=0, keepdims=True)
    filt_ref[...] = f / nrm


def hy_filter(n, d, freq, w1, b1, w2, b2, w3, b3, w4):
    z = _hy_features(n)
    o = HY_ORDER
    w1p = jnp.zeros((LANES, o), F32).at[:HY_EMB].set(w1.astype(F32))
    fp = jnp.zeros((SUBLANES, o), F32).at[:3].set(freq.astype(F32))
    tr = min(n, TM)
    hid = pl.pallas_call(
        _hy_hid_kernel,
        grid=(n // tr,),
        in_specs=[pl.BlockSpec((tr, LANES), lambda i: (i, 0)), _const_spec((SUBLANES, o)),
                  _const_spec((LANES, o)), _const_spec((1, o)),
                  _const_spec((o, o)), _const_spec((1, o)),
                  _const_spec((o, o)), _const_spec((1, o))],
        out_specs=pl.BlockSpec((tr, o), lambda i: (i, 0)),
        out_shape=jax.ShapeDtypeStruct((n, o), F32),
        compiler_params=_cparams(("arbitrary",)),
        name="hy_filter_mlp",
    )(z, fp, w1p, b1.reshape(1, o).astype(F32), w2.astype(F32), b2.reshape(1, o).astype(F32),
      w3.astype(F32), b3.reshape(1, o).astype(F32))
    max_decay = math.log(HY_DECAY_TARGET) / HY_DECAY_SHORT_PCT
    min_decay = math.log(HY_DECAY_TARGET) / HY_DECAY_LONG_PCT
    deltas = np.tile(np.linspace(min_decay, max_decay, d), 2).astype(np.float32)[None, :]
    tc = 256
    return pl.pallas_call(
        _hy_filt_kernel,
        grid=(2 * d // tc,),
        in_specs=[_const_spec((n, o)), pl.BlockSpec((o, tc), lambda j: (0, j)),
                  _const_spec((n, LANES)), pl.BlockSpec((1, tc), lambda j: (0, j))],
        out_specs=pl.BlockSpec((n, tc), lambda j: (0, j)),
        out_shape=jax.ShapeDtypeStruct((n, 2 * d), F32),
        compiler_params=_cparams(("arbitrary",)),
        name="hy_filter_window",
    )(hid, w4.astype(F32), z, jnp.asarray(deltas))


def _dense_dft_tables(l):
    n = 2 * l
    k = np.arange(n)[:, None]
    j = np.arange(l)[None, :]
    th = 2.0 * math.pi * ((k * j) % n) / n
    fwd = np.concatenate([np.cos(th), -np.sin(th)], axis=0)
    inv = np.concatenate([np.cos(th).T, -np.sin(th).T], axis=1) / n
    return (jnp.asarray(fwd.astype(np.float32)).astype(BF16),
            jnp.asarray(inv.astype(np.float32)).astype(BF16))


def _dense_tf_kernel(hf_ref, hb_ref, fwd_ref, tf_ref):
    n = fwd_ref.shape[0] // 2
    hf = hf_ref[...]
    hb = hb_ref[...]
    tf_ref[0:n, :] = _dot(fwd_ref[0:n, :], (hf + hb).astype(BF16))
    tf_ref[n:, :] = _dot(fwd_ref[n:, :], (hf - hb).astype(BF16))


def _dense_conv_kernel(vv_ref, x0_ref, skip_ref, tf_ref, fwd_ref, inv_ref, out_ref):
    n = fwd_ref.shape[0] // 2
    vv = vv_ref[...]
    x = _dot(fwd_ref[...], vv.astype(BF16))
    xr, xi = x[:n], x[n:]
    tr, ti = tf_ref[0:n, :], tf_ref[n:, :]
    y = jnp.concatenate([xr * tr - xi * ti, xr * ti + xi * tr], axis=0).astype(BF16)
    conv = _dot(inv_ref[...], y)
    out_ref[...] = (conv + vv * skip_ref[...]) * x0_ref[...]


def hy_longconv_dense(vv, x0, skip, filt, n_seq, l):
    d = vv.shape[1]
    n = 2 * l
    fwd, inv = _dense_dft_tables(l)
    dc = 512
    tf = pl.pallas_call(
        _dense_tf_kernel,
        grid=(d // dc,),
        in_specs=[pl.BlockSpec((l, dc), lambda j: (0, j)),
                  pl.BlockSpec((l, dc), lambda j: (0, d // dc + j)),
                  _const_spec(fwd.shape)],
        out_specs=pl.BlockSpec((2 * n, dc), lambda j: (0, j)),
        out_shape=jax.ShapeDtypeStruct((2 * n, d), F32),
        compiler_params=_cparams(("arbitrary",)),
        name="hy_tf_dense",
    )(filt, filt, fwd)
    return pl.pallas_call(
        _dense_conv_kernel,
        grid=(d // dc, n_seq),
        in_specs=[pl.BlockSpec((l, dc), lambda j, s: (s, j)),
                  pl.BlockSpec((l, dc), lambda j, s: (s, j)),
                  pl.BlockSpec((1, dc), lambda j, s: (0, j)),
                  pl.BlockSpec((2 * n, dc), lambda j, s: (0, j)),
                  _const_spec(fwd.shape), _const_spec(inv.shape)],
        out_specs=pl.BlockSpec((l, dc), lambda j, s: (s, j)),
        out_shape=jax.ShapeDtypeStruct((n_seq * l, d), F32),
        compiler_params=_cparams(("arbitrary", "arbitrary")),
        name="hy_longconv_dense",
    )(vv, x0, skip.reshape(1, d).astype(F32), tf, fwd, inv)


def _two_stage_tables(l):
    n = 2 * l
    n2 = FFT_N2
    n1_full = n // n2
    n_slab = n1_full // 2 + 1
    n1 = -(-n_slab // FFT_UNROLL_SLAB) * FFT_UNROLL_SLAB
    p = l // n2
    a = np.arange(p)[None, None, :]
    c = np.arange(n1)[None, :, None]
    b = np.arange(n2)[:, None, None]
    live = (c < n_slab).astype(np.float64)
    weight = np.where((c == 0) | (c == n1_full // 2), 1.0, 2.0) * live
    ph = 2.0 * math.pi * (((a * c) % n1_full) / n1_full + ((b * c) % n) / n)
    f1 = np.concatenate([np.cos(ph) * live, -np.sin(ph) * live], axis=1)
    pht = np.transpose(ph, (0, 2, 1))
    wt = np.transpose(weight, (0, 2, 1)) / n
    g1 = np.concatenate([np.cos(pht) * wt, -np.sin(pht) * wt], axis=2)
    e = np.arange(n2)[:, None]
    bb = np.arange(n2)[None, :]
    th = 2.0 * math.pi * ((e * bb) % n2) / n2
    fr, fi = np.cos(th), -np.sin(th)
    f2 = np.block([[fr, -fi], [fi, fr]])
    f2c = np.block([[fr, fi], [-fi, fr]])
    cast = lambda m: jnp.asarray(m.astype(np.float32)).astype(BF16)
    return cast(f1), cast(f2), cast(f2c), cast(g1)


def _unrolled_loop(n, unroll, fn):
    def body(i, carry):
        fn([i * unroll + u for u in range(unroll)])
        return carry

    lax.fori_loop(0, n // unroll, body, 0)


FFT_UNROLL_SMALL = 8
FFT_UNROLL_SLAB = 4


def _stage1(src_ref, f1_ref, w_scr, combine=None):
    n2, two_n1, p = f1_ref.shape

    def step(bs):
        if combine is None:
            xs = [src_ref[pl.ds(b, p, stride=n2), :] for b in bs]
        else:
            xs = [combine(src_ref[0][pl.ds(b, p, stride=n2), :], src_ref[1][pl.ds(b, p, stride=n2), :])
                  for b in bs]
        res = [_dot(f1_ref[b], x.astype(BF16)) for b, x in zip(bs, xs)]
        for b, r in zip(bs, res):
            w_scr[pl.ds(pl.multiple_of(b * two_n1, two_n1), two_n1), :] = r

    _unrolled_loop(n2, FFT_UNROLL_SMALL, step)


def _load_slab(w_scr, c, n1, n2):
    re = w_scr[pl.ds(c, n2, stride=2 * n1), :]
    im = w_scr[pl.ds(n1 + c, n2, stride=2 * n1), :]
    return jnp.concatenate([re, im], axis=0).astype(BF16)


def _two_stage_tf_kernel(hf_ref, hb_ref, f1_ref, f2_ref, tfr_ref, tfi_ref, ws_scr, wd_scr):
    n2, two_n1, _ = f1_ref.shape
    n1 = two_n1 // 2
    _stage1((hf_ref, hb_ref), f1_ref, ws_scr, combine=lambda u, v: u + v)
    _stage1((hf_ref, hb_ref), f1_ref, wd_scr, combine=lambda u, v: u - v)

    def step(cs):
        xs = [_dot(f2_ref[0:n2, :], _load_slab(ws_scr, c, n1, n2)) for c in cs]
        xd = [_dot(f2_ref[n2:, :], _load_slab(wd_scr, c, n1, n2)) for c in cs]
        for c, a, b in zip(cs, xs, xd):
            r0 = pl.multiple_of(c * n2, n2)
            tfr_ref[pl.ds(r0, n2), :] = a
            tfi_ref[pl.ds(r0, n2), :] = b

    _unrolled_loop(n1, FFT_UNROLL_SLAB, step)


def _two_stage_conv_kernel(vv_ref, x0_ref, skip_ref, tfr_ref, tfi_ref,
                           f1_ref, f2_ref, f2c_ref, g1_ref, out_ref, w_scr):
    n2, two_n1, p = f1_ref.shape
    n1 = two_n1 // 2
    _stage1(vv_ref, f1_ref, w_scr)

    def step2(cs):
        xs = [_dot(f2_ref[...], _load_slab(w_scr, c, n1, n2)) for c in cs]
        ys = []
        for c, x in zip(cs, xs):
            r0 = pl.multiple_of(c * n2, n2)
            xr, xi = x[:n2], x[n2:]
            tr = tfr_ref[pl.ds(r0, n2), :]
            ti = tfi_ref[pl.ds(r0, n2), :]
            ys.append(jnp.concatenate([xr * tr - xi * ti, xr * ti + xi * tr], axis=0).astype(BF16))
        zs = [_dot(f2c_ref[...], y) for y in ys]
        for c, z in zip(cs, zs):
            w_scr[pl.ds(c, n2, stride=two_n1), :] = z[:n2]
            w_scr[pl.ds(n1 + c, n2, stride=two_n1), :] = z[n2:]

    _unrolled_loop(n1, FFT_UNROLL_SLAB, step2)

    def step3(bs):
        zb = [w_scr[pl.ds(pl.multiple_of(b * two_n1, two_n1), two_n1), :].astype(BF16) for b in bs]
        res = [_dot(g1_ref[b], z) for b, z in zip(bs, zb)]
        for b, r in zip(bs, res):
            out_ref[pl.ds(b, p, stride=n2), :] = r

    _unrolled_loop(n2, FFT_UNROLL_SMALL, step3)
    vv = vv_ref[...]
    out_ref[...] = (out_ref[...] + vv * skip_ref[...]) * x0_ref[...]


def hy_longconv_two_stage(vv, x0, skip, filt, row0, n_seq, l):
    d = vv.shape[1]
    f1, f2, f2c, g1 = _two_stage_tables(l)
    n2, two_n1, p = f1.shape
    n = (two_n1 // 2) * n2
    dc = LANES
    tfr, tfi = pl.pallas_call(
        _two_stage_tf_kernel,
        grid=(d // dc,),
        in_specs=[pl.BlockSpec((l, dc), lambda j: (0, j)),
                  pl.BlockSpec((l, dc), lambda j: (0, d // dc + j)),
                  _const_spec(f1.shape), _const_spec(f2.shape)],
        out_specs=[pl.BlockSpec((n, dc), lambda j: (0, j)), pl.BlockSpec((n, dc), lambda j: (0, j))],
        out_shape=[jax.ShapeDtypeStruct((n, d), F32), jax.ShapeDtypeStruct((n, d), F32)],
        scratch_shapes=[pltpu.VMEM((n2 * two_n1, dc), F32), pltpu.VMEM((n2 * two_n1, dc), F32)],
        compiler_params=_cparams(("arbitrary",)),
        name="hy_tf_two_stage",
    )(filt, filt, f1, f2)
    base = row0 // l
    return pl.pallas_call(
        _two_stage_conv_kernel,
        grid=(d // dc, n_seq),
        in_specs=[pl.BlockSpec((l, dc), lambda j, s: (base + s, j)),
                  pl.BlockSpec((l, dc), lambda j, s: (base + s, j)),
                  pl.BlockSpec((1, dc), lambda j, s: (0, j)),
                  pl.BlockSpec((n, dc), lambda j, s: (0, j)),
                  pl.BlockSpec((n, dc), lambda j, s: (0, j)),
                  _const_spec(f1.shape), _const_spec(f2.shape), _const_spec(f2c.shape),
                  _const_spec(g1.shape)],
        out_specs=pl.BlockSpec((l, dc), lambda j, s: (s, j)),
        out_shape=jax.ShapeDtypeStruct((n_seq * l, d), F32),
        scratch_shapes=[pltpu.VMEM((n2 * two_n1, dc), F32)],
        compiler_params=_cparams(("arbitrary", "arbitrary")),
        name="hy_longconv_two_stage",
    )(vv, x0, skip.reshape(1, d).astype(F32), tfr, tfi, f1, f2, f2c, g1)


def _hy_out_kernel(y_ref, w_ref, b_ref, x_ref, mod_ref, out_ref, *, gate_idx):
    y = _dot(y_ref[...].astype(BF16), w_ref[...]) + b_ref[...]
    out_ref[...] = x_ref[...] + mod_ref[0, gate_idx:gate_idx + 1, :] * y


def hy_out(yg, w_out, b_out, x, mod, gate_idx, t_ctx, l_lat):
    t, d = x.shape
    grp = functools.partial(_group_of_tile, tm=TM, t_ctx=t_ctx, l_lat=l_lat)
    row_spec = lambda w: pl.BlockSpec((TM, w), lambda i: (i, 0))
    return pl.pallas_call(
        functools.partial(_hy_out_kernel, gate_idx=gate_idx),
        grid=(t // TM,),
        in_specs=[row_spec(d), _const_spec(w_out.shape), _const_spec((1, d)), row_spec(d),
                  pl.BlockSpec((1,) + mod.shape[1:], lambda i: (grp(i), 0, 0))],
        out_specs=row_spec(d),
        out_shape=jax.ShapeDtypeStruct((t, d), F32),
        compiler_params=_cparams(("arbitrary",)),
        name="hy_out",
    )(yg, w_out, b_out.reshape(1, d).astype(F32), x, mod)


def _grid_pos_embed(rows, d):
    r = np.repeat(np.arange(rows, dtype=np.float64), GRID_W)
    col = np.tile(np.arange(GRID_W, dtype=np.float64), rows)
    quarter = d // 4
    omega = 1.0 / (10000.0 ** (np.arange(quarter, dtype=np.float64) / quarter))
    ang_r = r[:, None] * omega
    ang_c = col[:, None] * omega
    pe = np.concatenate([np.sin(ang_r), np.cos(ang_r), np.sin(ang_c), np.cos(ang_c)], axis=-1)
    return jnp.asarray(pe.astype(np.float32))


def delta_layer(x, g, mod, state_lat, w_in, conv_w, a_log, dt_bias, g_head, w_out, dims):
    t_ctx, l_ctx, n_ctx, l_lat, n_lat = dims
    hk = DN_HEADS * DN_DK
    nqkv = 3 * hk
    w_qkv = w_in[:, :nqkv].astype(BF16)
    w_z = w_in[:, nqkv:nqkv + hk].astype(BF16)
    w_ab = jnp.zeros((w_in.shape[0], LANES), BF16).at[:, :4 * DN_HEADS].set(w_in[:, nqkv + hk:].astype(BF16))
    qkv, z, ab = norm_mod_matmul(x, g, mod, 0, 1, [w_qkv, w_z, w_ab], None, t_ctx, l_lat)
    q, k, v, gc, beta, gct = dn_prep(qkv, ab, conv_w.astype(F32), a_log, dt_bias, t_ctx, l_ctx, l_lat)
    outs = []
    ctx_states = []
    seq_lens = (l_ctx,) * n_ctx + (l_lat,) * n_lat
    for direction in range(2):
        o, s_all = dn_chunk_scan(q, k, v, gc, beta, gct, state_lat.astype(F32), direction=direction,
                                 seq_lens=seq_lens, n_zero_init=n_ctx)
        outs.append(o)
        ctx_states.append(s_all[:n_ctx])
    x = dn_out(outs[0], outs[1], z, g_head, w_out.astype(BF16), x, mod, 2, t_ctx, l_lat)
    return x, jnp.stack(ctx_states, axis=1)


def hyena_layer(x, g, mod, w_in, b_in, conv_w, conv_b, freq, fw1, fb1, fw2, fb2, fw3, fb3, fw4,
                skip, w_out, b_out, dims):
    t_ctx, l_ctx, n_ctx, l_lat, n_lat = dims
    d = x.shape[1]
    (u,) = norm_mod_matmul(x, g, mod, 0, 1, [w_in.astype(BF16)], [b_in.astype(F32)], t_ctx, l_lat)
    x0, vv = hy_conv_gate(u, conv_w.astype(F32), conv_b, t_ctx, l_ctx, l_lat)
    filt_c = hy_filter(l_ctx, d, freq, fw1, fb1, fw2, fb2, fw3, fb3, fw4)
    filt_l = hy_filter(l_lat, d, freq, fw1, fb1, fw2, fb2, fw3, fb3, fw4)
    y_c = hy_longconv_dense(vv, x0, skip, filt_c, n_ctx, l_ctx)
    y_l = hy_longconv_two_stage(vv, x0, skip, filt_l, t_ctx, n_lat, l_lat)
    yg = jnp.concatenate([y_c, y_l], axis=0)
    return hy_out(yg, w_out.astype(BF16), b_out, x, mod, 2, t_ctx, l_lat)


def kernel(x_prompt, x_sample, state_delta, c, c_ctx, w_ada, b_ada, g_norm, dn_w_in, dn_conv, dn_a_log, dn_dt_bias, dn_g_head, dn_w_out, hy_w_in, hy_b_in, hy_conv, hy_conv_b, hy_freq, hy_f_w1, hy_f_b1, hy_f_w2, hy_f_b2, hy_f_w3, hy_f_b3, hy_f_w4, hy_skip, hy_w_out, hy_b_out, w_router, router_bias, moe_w1, moe_w3, moe_w2, g_final):
    n_ctx, l_ctx, d = x_prompt.shape
    n_lat, l_lat, _ = x_sample.shape
    depth = w_ada.shape[0]
    t_ctx = n_ctx * l_ctx
    dims = (t_ctx, l_ctx, n_ctx, l_lat, n_lat)
    assert l_ctx % TM == 0 and l_lat % TM == 0 and t_ctx % l_lat == 0
    assert n_lat + 1 <= SUBLANES

    pos = _grid_pos_embed(l_lat // GRID_W, d)
    x = jnp.concatenate([x_prompt.reshape(t_ctx, d).astype(F32),
                         (x_sample.astype(F32) + pos[None]).reshape(n_lat * l_lat, d)], axis=0)
    cond = jnp.zeros((SUBLANES, d), F32).at[0].set(c_ctx.astype(F32)).at[1:1 + n_lat].set(c.astype(F32))
    mod_all = ada_modulation(cond, w_ada.astype(F32), b_ada.astype(F32))
    mod_all = mod_all.reshape(depth, SUBLANES, 6, d)

    ctx_states = []
    for i in range(depth):
        mod = mod_all[i]
        j = i // 2
        if i % 2 == 0:
            x, s_ctx = delta_layer(x, g_norm[i, 0], mod, state_delta[:, j], dn_w_in[j], dn_conv[j],
                                   dn_a_log[j], dn_dt_bias[j], dn_g_head[j], dn_w_out[j], dims)
            ctx_states.append(s_ctx.astype(x_prompt.dtype))
        else:
            x = hyena_layer(x, g_norm[i, 0], mod, hy_w_in[j], hy_b_in[j], hy_conv[j], hy_conv_b[j],
                            hy_freq[j], hy_f_w1[j], hy_f_b1[j], hy_f_w2[j], hy_f_b2[j], hy_f_w3[j],
                            hy_f_b3[j], hy_f_w4[j], hy_skip[j], hy_w_out[j], hy_b_out[j], dims)
        x = moe_layer(x, g_norm[i, 1], mod, w_router, router_bias, moe_w1, moe_w3, moe_w2, i,
                      t_ctx, l_lat, g_final if i == depth - 1 else None)
    y_prompt = x[:t_ctx].reshape(n_ctx, l_ctx, d).astype(x_prompt.dtype)
    y_sample = x[t_ctx:].reshape(n_lat, l_lat, d).astype(x_sample.dtype)
    new_state = jnp.stack(ctx_states, axis=1)
    return (y_prompt, y_sample, new_state)
```

```python
import functools
import math

import numpy as np
import jax
import jax.numpy as jnp
from jax import lax
from jax.experimental import pallas as pl
from jax.experimental.pallas import tpu as pltpu

F32 = jnp.float32
BF16 = jnp.bfloat16
HI = lax.Precision.HIGHEST

EPS = 1e-6
GRID_W = 64
DN_HEADS = 8
DN_DK = 128
DN_DV = 128
DN_CHUNK = 64
HY_EMB = 33
HY_ORDER = 64
HY_DECAY_SHORT_PCT = 0.3
HY_DECAY_LONG_PCT = 1.5
HY_DECAY_TARGET = 1e-2
N_EXPERTS = 32
N_GROUPS = 4
EXPERTS_PER_GROUP = N_EXPERTS // N_GROUPS

LANES = 128
SUBLANES = 8
TM = 256
HALO_ROWS = 16
MOE_TILE = 256
FFT_N2 = 128
DN_PREP_CHUNKS = 4
VMEM_LIMIT = 48 * 1024 * 1024


def _cparams(sem):
    return pltpu.CompilerParams(dimension_semantics=sem, vmem_limit_bytes=VMEM_LIMIT)


def _sigmoid(x):
    return 1.0 / (1.0 + jnp.exp(-x))


def _silu(x):
    return x * _sigmoid(x)


def _dot(a, b, precision=None):
    return jnp.dot(a, b, preferred_element_type=F32, precision=precision)


def _const_spec(shape):
    nd = len(shape)
    return pl.BlockSpec(shape, lambda *_: (0,) * nd)


def _ada_kernel(cond_ref, w_ref, b_ref, out_ref):
    s = _silu(cond_ref[...])
    out_ref[0] = _dot(s, w_ref[0], HI) + b_ref[0]


def ada_modulation(cond, w_ada, b_ada):
    depth, d, n = w_ada.shape
    tn = 1536
    return pl.pallas_call(
        _ada_kernel,
        grid=(depth, n // tn),
        in_specs=[
            pl.BlockSpec((SUBLANES, d), lambda i, j: (0, 0)),
            pl.BlockSpec((1, d, tn), lambda i, j: (i, 0, j)),
            pl.BlockSpec((1, 1, tn), lambda i, j: (i, 0, j)),
        ],
        out_specs=pl.BlockSpec((1, SUBLANES, tn), lambda i, j: (i, 0, j)),
        out_shape=jax.ShapeDtypeStruct((depth, SUBLANES, n), F32),
        compiler_params=_cparams(("arbitrary", "arbitrary")),
        name="ada_modulation",
    )(cond, w_ada, b_ada.reshape(depth, 1, n))


def _norm_mod(x, g, mod_ref, shift_idx, scale_idx):
    ms = jnp.mean(x * x, axis=-1, keepdims=True)
    y = x * lax.rsqrt(ms + EPS) * g
    scale = mod_ref[0, scale_idx:scale_idx + 1, :]
    shift = mod_ref[0, shift_idx:shift_idx + 1, :]
    return y * (1.0 + scale) + shift


def _group_of_tile(i, tm, t_ctx, l_lat):
    r = i * tm
    return jnp.where(r < t_ctx, 0, 1 + (r - t_ctx) // l_lat)


def _nmm_kernel(x_ref, g_ref, mod_ref, *refs, n_out, has_bias, shift_idx, scale_idx):
    h = _norm_mod(x_ref[...], g_ref[...], mod_ref, shift_idx, scale_idx).astype(BF16)
    w_refs = refs[:n_out]
    b_refs = refs[n_out:n_out + (n_out if has_bias else 0)]
    o_refs = refs[n_out + len(b_refs):]
    for k in range(n_out):
        acc = _dot(h, w_refs[k][...])
        if has_bias:
            acc = acc + b_refs[k][...]
        o_refs[k][...] = acc.astype(o_refs[k].dtype)


def norm_mod_matmul(x, g, mod, shift_idx, scale_idx, weights, biases, out_dtypes, t_ctx, l_lat):
    t, d = x.shape
    n_out = len(weights)
    has_bias = biases is not None
    grp = functools.partial(_group_of_tile, tm=TM, t_ctx=t_ctx, l_lat=l_lat)
    in_specs = [
        pl.BlockSpec((TM, d), lambda i: (i, 0)),
        _const_spec((1, d)),
        pl.BlockSpec((1,) + mod.shape[1:], lambda i: (grp(i), 0, 0)),
    ]
    args = [x, g.reshape(1, d), mod]
    for w in weights:
        in_specs.append(_const_spec(w.shape))
        args.append(w)
    if has_bias:
        for b in biases:
            in_specs.append(_const_spec((1, b.shape[-1])))
            args.append(b.reshape(1, -1))
    out_specs = [pl.BlockSpec((TM, w.shape[1]), lambda i: (i, 0)) for w in weights]
    out_shape = [jax.ShapeDtypeStruct((t, w.shape[1]), dt) for w, dt in zip(weights, out_dtypes)]
    return pl.pallas_call(
        functools.partial(_nmm_kernel, n_out=n_out, has_bias=has_bias,
                          shift_idx=shift_idx, scale_idx=scale_idx),
        grid=(t // TM,),
        in_specs=in_specs,
        out_specs=out_specs,
        out_shape=out_shape,
        compiler_params=_cparams(("arbitrary",)),
        name="norm_mod_matmul",
    )(*args)


def _seq_flags(i, tm, t_ctx, l_ctx, l_lat):
    r = i * tm
    in_ctx = r < t_ctx
    pos = jnp.where(in_ctx, r % l_ctx, (r - t_ctx) % l_lat)
    length = jnp.where(in_ctx, l_ctx, l_lat)
    return pos != 0, (pos + tm) != length


def _conv3(x, prev_rows, next_rows, w_ref, has_prev, has_next):
    tm = x.shape[0]
    x = x.astype(F32)
    row = lax.broadcasted_iota(jnp.int32, (tm, 1), 0)
    halo_p = jnp.where(has_prev, prev_rows[HALO_ROWS - 1:HALO_ROWS, :].astype(F32), 0.0)
    halo_n = jnp.where(has_next, next_rows[0:1, :].astype(F32), 0.0)
    x_prev = jnp.where(row == 0, halo_p, pltpu.roll(x, 1, 0))
    x_next = jnp.where(row == tm - 1, halo_n, pltpu.roll(x, tm - 1, 0))
    return w_ref[0:1, :] * x_prev + w_ref[1:2, :] * x + w_ref[2:3, :] * x_next


def _halo_specs(c, tm, t):
    nblk = t // HALO_ROWS
    per = tm // HALO_ROWS
    return [
        pl.BlockSpec((tm, c), lambda i: (i, 0)),
        pl.BlockSpec((HALO_ROWS, c), lambda i: (jnp.maximum(i * per - 1, 0), 0)),
        pl.BlockSpec((HALO_ROWS, c), lambda i: (jnp.minimum((i + 1) * per, nblk - 1), 0)),
    ]


def _softplus(x):
    return jnp.maximum(x, 0.0) + jnp.log(1.0 + jnp.exp(-jnp.abs(x)))


def _dn_prep_kernel(x_ref, xp_ref, xn_ref, cw_ref, ab_ref, nega_ref, dtb_ref, tri_ref,
                    q_ref, k_ref, v_ref, gc_ref, beta_ref, gct_ref,
                    *, t_ctx, l_ctx, l_lat):
    i = pl.program_id(0)
    has_prev, has_next = _seq_flags(i, TM, t_ctx, l_ctx, l_lat)
    c = _conv3(x_ref[...], xp_ref[...], xn_ref[...], cw_ref, has_prev, has_next)
    s = _silu(c)
    hk = DN_HEADS * DN_DK
    for h in range(DN_HEADS):
        lo, hi = h * DN_DK, (h + 1) * DN_DK
        qh = s[:, lo:hi]
        kh = s[:, hk + lo:hk + hi]
        qn = lax.rsqrt(jnp.sum(qh * qh, axis=-1, keepdims=True) + EPS)
        kn = lax.rsqrt(jnp.sum(kh * kh, axis=-1, keepdims=True) + EPS)
        q_ref[:, lo:hi] = (qh * (qn * (DN_DK ** -0.5))).astype(BF16)
        k_ref[:, lo:hi] = (kh * kn).astype(BF16)
    v_ref[...] = s[:, 2 * hk:].astype(BF16)
    ab = ab_ref[...]
    lg = nega_ref[...] * _softplus(ab + dtb_ref[...])
    beta_ref[...] = _sigmoid(ab)
    nch = TM // DN_CHUNK
    for d in range(2):
        gc = _dot(tri_ref[d], lg, HI)
        gc_ref[d] = gc
        gct = gc.T
        for ch in range(nch):
            gct_ref[d, ch] = gct[0:2 * DN_HEADS, ch * DN_CHUNK:(ch + 1) * DN_CHUNK]


def _chunk_tri():
    idx = np.arange(TM)
    same = (idx[:, None] // DN_CHUNK) == (idx[None, :] // DN_CHUNK)
    fwd = same & (idx[None, :] <= idx[:, None])
    bwd = same & (idx[None, :] >= idx[:, None])
    return jnp.asarray(np.stack([fwd, bwd]).astype(np.float32))


def dn_prep(qkv, ab, conv_w, a_log, dt_bias, t_ctx, l_ctx, l_lat):
    t, c = qkv.shape
    hd = DN_HEADS * DN_DK
    nega = jnp.zeros((1, LANES), F32).at[0, :2 * DN_HEADS].set(-jnp.exp(a_log.astype(F32)).reshape(-1))
    dtb = jnp.zeros((1, LANES), F32).at[0, :2 * DN_HEADS].set(dt_bias.astype(F32).reshape(-1))
    cw = jnp.zeros((SUBLANES, c), F32).at[:conv_w.shape[0]].set(conv_w)
    nch = TM // DN_CHUNK
    row_spec = lambda w: pl.BlockSpec((TM, w), lambda i: (i, 0))
    return pl.pallas_call(
        functools.partial(_dn_prep_kernel, t_ctx=t_ctx, l_ctx=l_ctx, l_lat=l_lat),
        grid=(t // TM,),
        in_specs=_halo_specs(c, TM, t) + [
            _const_spec((SUBLANES, c)),
            row_spec(LANES),
            _const_spec((1, LANES)),
            _const_spec((1, LANES)),
            _const_spec((2, TM, TM)),
        ],
        out_specs=[
            row_spec(hd), row_spec(hd), row_spec(hd),
            pl.BlockSpec((2, TM, LANES), lambda i: (0, i, 0)),
            row_spec(LANES),
            pl.BlockSpec((2, nch, 2 * DN_HEADS, DN_CHUNK), lambda i: (0, i, 0, 0)),
        ],
        out_shape=[
            jax.ShapeDtypeStruct((t, hd), BF16),
            jax.ShapeDtypeStruct((t, hd), BF16),
            jax.ShapeDtypeStruct((t, hd), BF16),
            jax.ShapeDtypeStruct((2, t, LANES), F32),
            jax.ShapeDtypeStruct((t, LANES), F32),
            jax.ShapeDtypeStruct((2, t // DN_CHUNK, 2 * DN_HEADS, DN_CHUNK), F32),
        ],
        compiler_params=_cparams(("arbitrary",)),
        name="dn_prep",
    )(qkv, qkv, qkv, cw, ab, nega, dtb, _chunk_tri())


def _bmm(a, b):
    return _dot(a.astype(BF16), b.astype(BF16))


def _inv_unit_tri_batch(lms, in_blk, eye):
    dgs = [jnp.where(in_blk, lm, 0.0) for lm in lms]
    offs = [lm - dg for lm, dg in zip(lms, dgs)]
    n1 = [-dg for dg in dgs]
    n2 = [_bmm(a, a) for a in n1]
    p = [eye + a for a in n1]
    n4 = [_bmm(a, a) for a in n2]
    p = [x + _bmm(x, a) for x, a in zip(p, n2)]
    n8 = [_bmm(a, a) for a in n4]
    p = [x + _bmm(x, a) for x, a in zip(p, n4)]
    p = [x + _bmm(x, a) for x, a in zip(p, n8)]
    m = [_bmm(x, o) for x, o in zip(p, offs)]
    m2 = [_bmm(a, a) for a in m]
    t1 = [eye - a for a in m]
    t1 = [x + _bmm(x, a) for x, a in zip(t1, m2)]
    return [_bmm(x, y) for x, y in zip(t1, p)]


def _dn_chunk_kernel(blk_ref, seq_ref, first_ref, last_ref,
                     q_ref, k_ref, v_ref, gc_ref, beta_ref, gct_ref, s0_ref,
                     o_ref, sfin_ref, s_scr, attn_scr, u_scr, qw_scr, kd_scr, gl_scr,
                     *, direction, n_zero_init):
    step = pl.program_id(0)
    c_sz = DN_CHUNK
    nch = TM // c_sz
    heads = range(DN_HEADS)

    @pl.when(first_ref[step] == 1)
    def _():
        s_scr[...] = jnp.where(seq_ref[step] >= n_zero_init, s0_ref[0, 0], 0.0)

    row = lax.broadcasted_iota(jnp.int32, (c_sz, c_sz), 0)
    col = lax.broadcasted_iota(jnp.int32, (c_sz, c_sz), 1)
    if direction == 0:
        incl, strict = row >= col, row > col
        last = c_sz - 1
    else:
        incl, strict = row <= col, row < col
        last = 0
    in_blk = (row // 16) == (col // 16)
    eye = (row == col).astype(F32)

    def lanes(h):
        return slice(h * DN_DK, (h + 1) * DN_DK)

    def prep_body(it, carry):
        units = [(it * DN_PREP_CHUNKS + c, h) for c in range(DN_PREP_CHUNKS) for h in heads]
        idx = range(len(units))
        rows = [pl.ds(pl.multiple_of(ch * c_sz, c_sz), c_sz) for ch, _ in units]
        gls = [direction * DN_HEADS + h for _, h in units]
        qs = [q_ref[rows[i], lanes(units[i][1])].astype(F32) for i in idx]
        ks = [k_ref[rows[i], lanes(units[i][1])].astype(F32) for i in idx]
        gcc = [gc_ref[0, rows[i], gls[i]:gls[i] + 1] for i in idx]
        beta = [beta_ref[rows[i], 2 * DN_HEADS + gls[i]:2 * DN_HEADS + gls[i] + 1] for i in idx]
        gcr = [gct_ref[0, units[i][0]][gls[i]:gls[i] + 1, :] for i in idx]
        dec = [jnp.exp(jnp.where(incl, gcc[i] - gcr[i], -jnp.inf)) for i in idx]
        kb = [ks[i] * beta[i] for i in idx]
        prod = [lax.dot_general(jnp.concatenate([qs[i], kb[i]], axis=0).astype(BF16), ks[i].astype(BF16),
                                (((1,), (1,)), ((), ())), preferred_element_type=F32)
                for i in idx]
        for i, (ch, h) in enumerate(units):
            attn_scr[ch, h] = (prod[i][:c_sz] * dec[i]).astype(BF16)
        lms = [jnp.where(strict, prod[i][c_sz:] * dec[i], 0.0) for i in idx]
        tinv = _inv_unit_tri_batch(lms, in_blk, eye)
        eg = [jnp.exp(gcc[i]) for i in idx]
        uw = [_bmm(tinv[i], jnp.concatenate(
                  [v_ref[rows[i], lanes(units[i][1])].astype(F32) * beta[i], kb[i] * eg[i]], axis=1))
              for i in idx]
        for i, (ch, h) in enumerate(units):
            u_scr[rows[i], lanes(h)] = uw[i][:, :DN_DV]
            qw_scr[ch, h] = jnp.concatenate([qs[i] * eg[i], uw[i][:, DN_DV:]], axis=0).astype(BF16)
            g_last = gcc[i][last:last + 1, :]
            kd_scr[ch, h] = (ks[i] * jnp.exp(g_last - gcc[i])).T.astype(BF16)
            gl_scr[ch, h] = jnp.broadcast_to(jnp.exp(g_last), (1, DN_DV))
        return carry

    lax.fori_loop(0, nch // DN_PREP_CHUNKS, prep_body, 0)

    def scan_body(ci, carry):
        ch = ci if direction == 0 else nch - 1 - ci
        r0 = pl.multiple_of(ch * c_sz, c_sz)
        rows = pl.ds(r0, c_sz)
        s = [s_scr[h] for h in heads]
        qs_ws = [_dot(qw_scr[ch, h], s[h].astype(BF16)) for h in heads]
        vb = [(u_scr[rows, lanes(h)] - qs_ws[h][c_sz:]).astype(BF16) for h in heads]
        for h in heads:
            o_ref[rows, lanes(h)] = (qs_ws[h][:c_sz] + _dot(attn_scr[ch, h], vb[h])).astype(o_ref.dtype)
            s_scr[h] = s[h] * gl_scr[ch, h] + _dot(kd_scr[ch, h], vb[h])
        return carry

    lax.fori_loop(0, nch, scan_body, 0)

    @pl.when(last_ref[step] == 1)
    def _():
        sfin_ref[0, 0] = s_scr[...]


def _scan_tables(direction, seq_lens):
    blk, seq, first, last = [], [], [], []
    base = 0
    for s, length in enumerate(seq_lens):
        nblk = length // TM
        order = range(nblk) if direction == 0 else range(nblk - 1, -1, -1)
        for pos, jj in enumerate(order):
            blk.append(base + jj)
            seq.append(s)
            first.append(int(pos == 0))
            last.append(int(pos == nblk - 1))
        base += nblk
    return [jnp.asarray(np.asarray(a, np.int32)) for a in (blk, seq, first, last)]


def dn_chunk_scan(q, k, v, gc, beta, gct, s0, *, direction, seq_lens, n_zero_init):
    t, hd = q.shape
    nch = TM // DN_CHUNK
    n_seq = len(seq_lens)
    tables = _scan_tables(direction, seq_lens)
    n_steps = int(tables[0].shape[0])
    row_spec = lambda w: pl.BlockSpec((TM, w), lambda i, blk, seq, fst, lst: (blk[i], 0))
    state_blk = (1, 1, DN_HEADS, DN_DK, DN_DV)
    grid_spec = pltpu.PrefetchScalarGridSpec(
        num_scalar_prefetch=4,
        grid=(n_steps,),
        in_specs=[
            row_spec(hd), row_spec(hd), row_spec(hd),
            pl.BlockSpec((1, TM, LANES), lambda i, blk, seq, fst, lst: (direction, blk[i], 0)),
            row_spec(LANES),
            pl.BlockSpec((1, nch, 2 * DN_HEADS, DN_CHUNK),
                         lambda i, blk, seq, fst, lst: (direction, blk[i], 0, 0)),
            pl.BlockSpec(state_blk, lambda i, blk, seq, fst, lst:
                         (jnp.maximum(seq[i] - n_zero_init, 0), direction, 0, 0, 0)),
        ],
        out_specs=[
            row_spec(hd),
            pl.BlockSpec(state_blk, lambda i, blk, seq, fst, lst: (seq[i], 0, 0, 0, 0)),
        ],
        scratch_shapes=[
            pltpu.VMEM((DN_HEADS, DN_DK, DN_DV), F32),
            pltpu.VMEM((nch, DN_HEADS, DN_CHUNK, DN_CHUNK), BF16),
            pltpu.VMEM((TM, hd), F32),
            pltpu.VMEM((nch, DN_HEADS, 2 * DN_CHUNK, DN_DK), BF16),
            pltpu.VMEM((nch, DN_HEADS, DN_DK, DN_CHUNK), BF16),
            pltpu.VMEM((nch, DN_HEADS, 1, DN_DV), F32),
        ],
    )
    o, sfin = pl.pallas_call(
        functools.partial(_dn_chunk_kernel, direction=direction, n_zero_init=n_zero_init),
        grid_spec=grid_spec,
        out_shape=[
            jax.ShapeDtypeStruct((t, hd), BF16),
            jax.ShapeDtypeStruct((n_seq,) + state_blk[1:], F32),
        ],
        compiler_params=_cparams(("arbitrary",)),
        name="dn_chunk_scan_d%d" % direction,
    )(*tables, q, k, v, gc, beta, gct, s0)
    return o, sfin[:, 0]


def _dn_out_kernel(of_ref, ob_ref, z_ref, gh_ref, w_ref, x_ref, mod_ref, out_ref, *, gate_idx):
    o = of_ref[...].astype(F32) + ob_ref[...].astype(F32)
    z = z_ref[...].astype(F32)
    parts = []
    for h in range(DN_HEADS):
        lanes = slice(h * DN_DV, (h + 1) * DN_DV)
        oh = o[:, lanes]
        ms = jnp.mean(oh * oh, axis=-1, keepdims=True)
        parts.append(oh * lax.rsqrt(ms + EPS) * gh_ref[...])
    on = jnp.concatenate(parts, axis=1) * _silu(z)
    y = _dot(on.astype(BF16), w_ref[...])
    out_ref[...] = x_ref[...] + mod_ref[0, gate_idx:gate_idx + 1, :] * y


def dn_out(o_f, o_b, z, g_head, w_out, x, mod, gate_idx, t_ctx, l_lat):
    t, d = x.shape
    hd = o_f.shape[1]
    grp = functools.partial(_group_of_tile, tm=TM, t_ctx=t_ctx, l_lat=l_lat)
    row_spec = lambda w: pl.BlockSpec((TM, w), lambda i: (i, 0))
    return pl.pallas_call(
        functools.partial(_dn_out_kernel, gate_idx=gate_idx),
        grid=(t // TM,),
        in_specs=[
            row_spec(hd), row_spec(hd), row_spec(hd),
            _const_spec((1, DN_DV)),
            _const_spec(w_out.shape),
            row_spec(d),
            pl.BlockSpec((1,) + mod.shape[1:], lambda i: (grp(i), 0, 0)),
        ],
        out_specs=row_spec(d),
        out_shape=jax.ShapeDtypeStruct((t, d), F32),
        compiler_params=_cparams(("arbitrary",)),
        name="dn_out",
    )(o_f, o_b, z, g_head.reshape(1, DN_DV).astype(F32), w_out, x, mod)


def _first_argmax(vals, lane, valid):
    masked = jnp.where(valid, vals, -jnp.inf)
    m = jnp.max(masked, axis=-1, keepdims=True)
    idx = jnp.min(jnp.where(valid & (masked == m), lane, float(LANES)), axis=-1, keepdims=True)
    return m, idx


def _router_kernel(x_ref, g_ref, mod_ref, wh_ref, wl_ref, rb_ref, h_ref, idx_ref, wt_ref,
                   *, shift_idx, scale_idx):
    h = _norm_mod(x_ref[...], g_ref[...], mod_ref, shift_idx, scale_idx)
    hb = h.astype(BF16)
    h_ref[...] = hb
    hl = (h - hb.astype(F32)).astype(BF16)
    wh = wh_ref[...]
    logits = _dot(hb, wh) + (_dot(hl, wh) + _dot(hb, wl_ref[...]))
    scores = _sigmoid(logits)
    sel = scores + rb_ref[...]
    tm = scores.shape[0]
    lane = lax.broadcasted_iota(jnp.int32, (tm, LANES), 1).astype(F32)
    best_score = None
    for g in range(N_GROUPS):
        in_g = (lane >= g * EXPERTS_PER_GROUP) & (lane < (g + 1) * EXPERTS_PER_GROUP)
        m1, i1 = _first_argmax(sel, lane, in_g)
        m2, i2 = _first_argmax(sel, lane, in_g & (lane != i1))
        gs = m1 + m2
        if best_score is None:
            best_score, b1, b2 = gs, i1, i2
        else:
            better = gs > best_score
            best_score = jnp.where(better, gs, best_score)
            b1 = jnp.where(better, i1, b1)
            b2 = jnp.where(better, i2, b2)
    w1 = jnp.sum(jnp.where(lane == b1, scores, 0.0), axis=-1, keepdims=True)
    w2 = jnp.sum(jnp.where(lane == b2, scores, 0.0), axis=-1, keepdims=True)
    tot = w1 + w2
    idx_ref[...] = jnp.where(lane == 0.0, b1, jnp.where(lane == 1.0, b2, 0.0)).astype(jnp.int32)
    wt_ref[...] = jnp.where(lane == 0.0, w1 / tot, jnp.where(lane == 1.0, w2 / tot, 0.0))


def moe_router(x, g, mod, shift_idx, scale_idx, w_router, router_bias, t_ctx, l_lat):
    t, d = x.shape
    grp = functools.partial(_group_of_tile, tm=TM, t_ctx=t_ctx, l_lat=l_lat)
    wr = jnp.zeros((d, LANES), F32).at[:, :N_EXPERTS].set(w_router.astype(F32))
    wh = wr.astype(BF16)
    wl = (wr - wh.astype(F32)).astype(BF16)
    rb = jnp.zeros((1, LANES), F32).at[0, :N_EXPERTS].set(router_bias.astype(F32))
    row_spec = lambda w: pl.BlockSpec((TM, w), lambda i: (i, 0))
    return pl.pallas_call(
        functools.partial(_router_kernel, shift_idx=shift_idx, scale_idx=scale_idx),
        grid=(t // TM,),
        in_specs=[
            row_spec(d),
            _const_spec((1, d)),
            pl.BlockSpec((1,) + mod.shape[1:], lambda i: (grp(i), 0, 0)),
            _const_spec((d, LANES)),
            _const_spec((d, LANES)),
            _const_spec((1, LANES)),
        ],
        out_specs=[row_spec(d), row_spec(LANES), row_spec(LANES)],
        out_shape=[
            jax.ShapeDtypeStruct((t, d), BF16),
            jax.ShapeDtypeStruct((t, LANES), jnp.int32),
            jax.ShapeDtypeStruct((t, LANES), F32),
        ],
        compiler_params=_cparams(("arbitrary",)),
        name="moe_router",
    )(x, g.reshape(1, d), mod, wh, wl, rb)


def _expert_kernel(te_ref, nt_ref, first_ref, nxt_ref, par_ref,
                   xs_ref, w1_hbm, w3_hbm, w2_hbm, ys_ref,
                   w1_buf, w3_buf, w2_buf, w1_bf, w3_bf, w2_bf, sem, *, layer):
    i = pl.program_id(0)
    active = i < nt_ref[0]

    def weight_copies(e, slot):
        return (pltpu.make_async_copy(w1_hbm.at[layer, e], w1_buf.at[slot], sem.at[slot, 0]),
                pltpu.make_async_copy(w3_hbm.at[layer, e], w3_buf.at[slot], sem.at[slot, 1]),
                pltpu.make_async_copy(w2_hbm.at[layer, e], w2_buf.at[slot], sem.at[slot, 2]))

    @pl.when(active & (i == 0))
    def _():
        for cp in weight_copies(te_ref[0], 0):
            cp.start()

    @pl.when(active & (first_ref[i] == 1))
    def _():
        slot = par_ref[i]
        for cp in weight_copies(te_ref[i], slot):
            cp.wait()

        @pl.when(nxt_ref[i] >= 0)
        def _():
            for cp in weight_copies(nxt_ref[i], 1 - slot):
                cp.start(priority=1)

        w1_bf[...] = w1_buf[slot].astype(BF16)
        w3_bf[...] = w3_buf[slot].astype(BF16)
        w2_bf[...] = w2_buf[slot].astype(BF16)

    @pl.when(active)
    def _():
        xs = xs_ref[...]
        a = _dot(xs, w1_bf[...])
        b = _dot(xs, w3_bf[...])
        hid = (_silu(a) * b).astype(BF16)
        ys_ref[...] = _dot(hid, w2_bf[...]).astype(ys_ref.dtype)

    @pl.when(jnp.logical_not(active))
    def _():
        ys_ref[...] = jnp.zeros_like(ys_ref)


def moe_experts(xs, plan, w1, w3, w2, layer):
    p, d = xs.shape
    de = w1.shape[-1]
    ntile = p // MOE_TILE
    tile_map = lambda i, *_: (i, 0)
    grid_spec = pltpu.PrefetchScalarGridSpec(
        num_scalar_prefetch=5,
        grid=(ntile,),
        in_specs=[
            pl.BlockSpec((MOE_TILE, d), tile_map),
            pl.BlockSpec(memory_space=pl.ANY),
            pl.BlockSpec(memory_space=pl.ANY),
            pl.BlockSpec(memory_space=pl.ANY),
        ],
        out_specs=pl.BlockSpec((MOE_TILE, d), tile_map),
        scratch_shapes=[
            pltpu.VMEM((2, d, de), F32), pltpu.VMEM((2, d, de), F32), pltpu.VMEM((2, de, d), F32),
            pltpu.VMEM((d, de), BF16), pltpu.VMEM((d, de), BF16), pltpu.VMEM((de, d), BF16),
            pltpu.SemaphoreType.DMA((2, 3)),
        ],
    )
    return pl.pallas_call(
        functools.partial(_expert_kernel, layer=layer),
        grid_spec=grid_spec,
        out_shape=jax.ShapeDtypeStruct((p, d), BF16),
        compiler_params=_cparams(("arbitrary",)),
        name="moe_experts",
    )(plan["tile_expert"], plan["n_tiles"], plan["first"], plan["next_expert"], plan["parity"],
      xs, w1, w3, w2)


def moe_dispatch_plan(idx):
    t = idx.shape[0]
    n_assign = 2 * t
    p = n_assign + N_EXPERTS * MOE_TILE
    p = -(-p // MOE_TILE) * MOE_TILE
    ntile = p // MOE_TILE
    experts = jnp.arange(N_EXPERTS, dtype=jnp.int32)
    e_flat = idx.reshape(-1)
    onehot = (e_flat[:, None] == experts[None, :]).astype(jnp.int32)
    rank = jnp.cumsum(onehot, axis=0) - onehot
    counts = jnp.sum(onehot, axis=0)
    padded = ((counts + MOE_TILE - 1) // MOE_TILE) * MOE_TILE
    ends = jnp.cumsum(padded)
    starts = ends - padded
    slot = jnp.sum(onehot * (starts[None, :] + rank), axis=1)
    token = jnp.arange(n_assign, dtype=jnp.int32) // 2
    slot_token = jnp.zeros((p,), jnp.int32).at[slot].set(token, unique_indices=True,
                                                          mode="promise_in_bounds")
    tile_id = jnp.arange(ntile, dtype=jnp.int32)
    tile_expert = jnp.sum((tile_id[:, None] * MOE_TILE >= ends[None, :]).astype(jnp.int32), axis=1)
    n_tiles = (ends[-1] // MOE_TILE).astype(jnp.int32)
    used = tile_id < n_tiles
    tile_expert = jnp.where(used, tile_expert, tile_expert[jnp.maximum(n_tiles - 1, 0)])
    tile_expert = jnp.minimum(tile_expert, N_EXPERTS - 1).astype(jnp.int32)
    prev = jnp.concatenate([jnp.full((1,), -1, jnp.int32), tile_expert[:-1]])
    first = (used & (tile_expert != prev)).astype(jnp.int32)
    parity = ((jnp.cumsum(first) - 1) % 2).astype(jnp.int32)
    cand = jnp.where(counts > 0, experts, N_EXPERTS)
    later = lax.cummin(cand[::-1])[::-1]
    nxt_e = jnp.concatenate([later[1:], jnp.full((1,), N_EXPERTS, jnp.int32)])
    nxt_e = jnp.where(nxt_e >= N_EXPERTS, -1, nxt_e).astype(jnp.int32)
    next_expert = nxt_e[tile_expert]
    return dict(slot_token=slot_token, slot=slot.reshape(t, 2), tile_expert=tile_expert,
                n_tiles=n_tiles.reshape(1), first=first, next_expert=next_expert, parity=parity)


def _combine_kernel(x_ref, y0_ref, y1_ref, wt_ref, mod_ref, *refs, gate_idx, final_norm):
    moe = wt_ref[:, 0:1] * y0_ref[...].astype(F32) + wt_ref[:, 1:2] * y1_ref[...].astype(F32)
    x = x_ref[...] + mod_ref[0, gate_idx:gate_idx + 1, :] * moe
    if final_norm:
        gf_ref, out_ref = refs
        ms = jnp.mean(x * x, axis=-1, keepdims=True)
        out_ref[...] = x * lax.rsqrt(ms + EPS) * gf_ref[...]
    else:
        (out_ref,) = refs
        out_ref[...] = x


def moe_combine(x, y0, y1, wt, mod, gate_idx, t_ctx, l_lat, g_final=None):
    t, d = x.shape
    grp = functools.partial(_group_of_tile, tm=TM, t_ctx=t_ctx, l_lat=l_lat)
    row_spec = lambda w: pl.BlockSpec((TM, w), lambda i: (i, 0))
    in_specs = [row_spec(d), row_spec(d), row_spec(d), row_spec(LANES),
                pl.BlockSpec((1,) + mod.shape[1:], lambda i: (grp(i), 0, 0))]
    args = [x, y0, y1, wt, mod]
    if g_final is not None:
        in_specs.append(_const_spec((1, d)))
        args.append(g_final.reshape(1, d).astype(F32))
    return pl.pallas_call(
        functools.partial(_combine_kernel, gate_idx=gate_idx, final_norm=g_final is not None),
        grid=(t // TM,),
        in_specs=in_specs,
        out_specs=row_spec(d),
        out_shape=jax.ShapeDtypeStruct((t, d), F32),
        compiler_params=_cparams(("arbitrary",)),
        name="moe_combine",
    )(*args)


def moe_layer(x, g, mod, w_router, router_bias, w1, w3, w2, layer, t_ctx, l_lat, g_final=None):
    h, idx_pad, wt_pad = moe_router(x, g, mod, 3, 4, w_router, router_bias, t_ctx, l_lat)
    plan = moe_dispatch_plan(idx_pad[:, :2])
    take = lambda rows, index: rows.at[index].get(mode="promise_in_bounds")
    xs = take(h, plan["slot_token"])
    ys = moe_experts(xs, plan, w1, w3, w2, layer)
    y0 = take(ys, plan["slot"][:, 0])
    y1 = take(ys, plan["slot"][:, 1])
    return moe_combine(x, y0, y1, wt_pad, mod, 5, t_ctx, l_lat, g_final)


def _hy_conv_kernel(u_ref, up_ref, un_ref, cw_ref, cb_ref, x0_ref, vv_ref, *, t_ctx, l_ctx, l_lat):
    i = pl.program_id(0)
    has_prev, has_next = _seq_flags(i, TM, t_ctx, l_ctx, l_lat)
    c = _conv3(u_ref[...], up_ref[...], un_ref[...], cw_ref, has_prev, has_next) + cb_ref[...]
    d = c.shape[1] // 3
    x0_ref[...] = c[:, :d].astype(x0_ref.dtype)
    vv_ref[...] = (c[:, 2 * d:] * c[:, d:2 * d]).astype(vv_ref.dtype)


def hy_conv_gate(u, conv_w, conv_b, t_ctx, l_ctx, l_lat):
    t, c = u.shape
    d = c // 3
    cw = jnp.zeros((SUBLANES, c), F32).at[:conv_w.shape[0]].set(conv_w)
    row_spec = lambda w: pl.BlockSpec((TM, w), lambda i: (i, 0))
    return pl.pallas_call(
        functools.partial(_hy_conv_kernel, t_ctx=t_ctx, l_ctx=l_ctx, l_lat=l_lat),
        grid=(t // TM,),
        in_specs=_halo_specs(c, TM, t) + [_const_spec((SUBLANES, c)), _const_spec((1, c))],
        out_specs=[row_spec(d), row_spec(d)],
        out_shape=[jax.ShapeDtypeStruct((t, d), BF16), jax.ShapeDtypeStruct((t, d), BF16)],
        compiler_params=_cparams(("arbitrary",)),
        name="hy_conv_gate",
    )(u, u, u, cw, conv_b.reshape(1, c).astype(F32))


def _hy_features(n):
    t = np.linspace(0.0, 1.0, n)[:, None]
    bands = (HY_EMB - 1) // 2
    band_w = np.linspace(1e-4, bands - 1, bands)[None, :]
    ang = (2.0 * math.pi / n) * np.arange(n)[:, None] * band_w
    z = np.concatenate([t, np.cos(ang), -np.sin(ang)], axis=-1)
    zp = np.zeros((n, LANES))
    zp[:, :HY_EMB] = z
    return jnp.asarray(zp.astype(np.float32))


def _hy_hid_kernel(z_ref, f_ref, w1_ref, b1_ref, w2_ref, b2_ref, w3_ref, b3_ref, hid_ref):
    h = jnp.sin(f_ref[0:1, :] * (_dot(z_ref[...], w1_ref[...], HI) + b1_ref[...]))
    h = jnp.sin(f_ref[1:2, :] * (_dot(h, w2_ref[...], HI) + b2_ref[...]))
    h = jnp.sin(f_ref[2:3, :] * (_dot(h, w3_ref[...], HI) + b3_ref[...]))
    hid_ref[...] = h


def _hy_filt_kernel(hid_ref, w4_ref, z_ref, dl_ref, filt_ref):
    f = _dot(hid_ref[...], w4_ref[...], HI)
    f = f * jnp.exp(-z_ref[:, 0:1] * jnp.abs(dl_ref[...]))
    nrm = jnp.sum(jnp.abs(f), axis=0, keepdims=True)
    filt_ref[...] = f / nrm


def hy_filter(n, d, freq, w1, b1, w2, b2, w3, b3, w4):
    z = _hy_features(n)
    o = HY_ORDER
    w1p = jnp.zeros((LANES, o), F32).at[:HY_EMB].set(w1.astype(F32))
    fp = jnp.zeros((SUBLANES, o), F32).at[:3].set(freq.astype(F32))
    tr = min(n, TM)
    hid = pl.pallas_call(
        _hy_hid_kernel,
        grid=(n // tr,),
        in_specs=[pl.BlockSpec((tr, LANES), lambda i: (i, 0)), _const_spec((SUBLANES, o)),
                  _const_spec((LANES, o)), _const_spec((1, o)),
                  _const_spec((o, o)), _const_spec((1, o)),
                  _const_spec((o, o)), _const_spec((1, o))],
        out_specs=pl.BlockSpec((tr, o), lambda i: (i, 0)),
        out_shape=jax.ShapeDtypeStruct((n, o), F32),
        compiler_params=_cparams(("arbitrary",)),
        name="hy_filter_mlp",
    )(z, fp, w1p, b1.reshape(1, o).astype(F32), w2.astype(F32), b2.reshape(1, o).astype(F32),
      w3.astype(F32), b3.reshape(1, o).astype(F32))
    max_decay = math.log(HY_DECAY_TARGET) / HY_DECAY_SHORT_PCT
    min_decay = math.log(HY_DECAY_TARGET) / HY_DECAY_LONG_PCT
    deltas = np.tile(np.linspace(min_decay, max_decay, d), 2).astype(np.float32)[None, :]
    tc = 256
    return pl.pallas_call(
        _hy_filt_kernel,
        grid=(2 * d // tc,),
        in_specs=[_const_spec((n, o)), pl.BlockSpec((o, tc), lambda j: (0, j)),
                  _const_spec((n, LANES)), pl.BlockSpec((1, tc), lambda j: (0, j))],
        out_specs=pl.BlockSpec((n, tc), lambda j: (0, j)),
        out_shape=jax.ShapeDtypeStruct((n, 2 * d), F32),
        compiler_params=_cparams(("arbitrary",)),
        name="hy_filter_window",
    )(hid, w4.astype(F32), z, jnp.asarray(deltas))


def _dense_dft_tables(l):
    n = 2 * l
    k = np.arange(n)[:, None]
    j = np.arange(l)[None, :]
    th = 2.0 * math.pi * ((k * j) % n) / n
    fwd = np.concatenate([np.cos(th), -np.sin(th)], axis=0)
    inv = np.concatenate([np.cos(th).T, -np.sin(th).T], axis=1) / n
    return (jnp.asarray(fwd.astype(np.float32)).astype(BF16),
            jnp.asarray(inv.astype(np.float32)).astype(BF16))


def _dense_tf_kernel(hf_ref, hb_ref, fwd_ref, tf_ref):
    n = fwd_ref.shape[0] // 2
    hf = hf_ref[...]
    hb = hb_ref[...]
    tf_ref[0:n, :] = _dot(fwd_ref[0:n, :], (hf + hb).astype(BF16))
    tf_ref[n:, :] = _dot(fwd_ref[n:, :], (hf - hb).astype(BF16))


def _dense_conv_kernel(vv_ref, x0_ref, skip_ref, tf_ref, fwd_ref, inv_ref, out_ref):
    n = fwd_ref.shape[0] // 2
    vv = vv_ref[...]
    x = _dot(fwd_ref[...], vv.astype(BF16))
    xr, xi = x[:n], x[n:]
    tr, ti = tf_ref[0:n, :], tf_ref[n:, :]
    y = jnp.concatenate([xr * tr - xi * ti, xr * ti + xi * tr], axis=0).astype(BF16)
    conv = _dot(inv_ref[...], y)
    out = (conv + vv.astype(F32) * skip_ref[...]) * x0_ref[...].astype(F32)
    out_ref[...] = out.astype(out_ref.dtype)


def hy_longconv_dense(vv, x0, skip, filt, n_seq, l):
    d = vv.shape[1]
    n = 2 * l
    fwd, inv = _dense_dft_tables(l)
    dc = 512
    tf = pl.pallas_call(
        _dense_tf_kernel,
        grid=(d // dc,),
        in_specs=[pl.BlockSpec((l, dc), lambda j: (0, j)),
                  pl.BlockSpec((l, dc), lambda j: (0, d // dc + j)),
                  _const_spec(fwd.shape)],
        out_specs=pl.BlockSpec((2 * n, dc), lambda j: (0, j)),
        out_shape=jax.ShapeDtypeStruct((2 * n, d), F32),
        compiler_params=_cparams(("arbitrary",)),
        name="hy_tf_dense",
    )(filt, filt, fwd)
    return pl.pallas_call(
        _dense_conv_kernel,
        grid=(d // dc, n_seq),
        in_specs=[pl.BlockSpec((l, dc), lambda j, s: (s, j)),
                  pl.BlockSpec((l, dc), lambda j, s: (s, j)),
                  pl.BlockSpec((1, dc), lambda j, s: (0, j)),
                  pl.BlockSpec((2 * n, dc), lambda j, s: (0, j)),
                  _const_spec(fwd.shape), _const_spec(inv.shape)],
        out_specs=pl.BlockSpec((l, dc), lambda j, s: (s, j)),
        out_shape=jax.ShapeDtypeStruct((n_seq * l, d), BF16),
        compiler_params=_cparams(("arbitrary", "arbitrary")),
        name="hy_longconv_dense",
    )(vv, x0, skip.reshape(1, d).astype(F32), tf, fwd, inv)


def _two_stage_tables(l):
    n = 2 * l
    n2 = FFT_N2
    n1_full = n // n2
    n_slab = n1_full // 2 + 1
    n1 = -(-n_slab // FFT_UNROLL_SLAB) * FFT_UNROLL_SLAB
    p = l // n2
    a = np.arange(p)[None, None, :]
    c = np.arange(n1)[None, :, None]
    b = np.arange(n2)[:, None, None]
    live = (c < n_slab).astype(np.float64)
    weight = np.where((c == 0) | (c == n1_full // 2), 1.0, 2.0) * live
    ph = 2.0 * math.pi * (((a * c) % n1_full) / n1_full + ((b * c) % n) / n)
    f1 = np.concatenate([np.cos(ph) * live, -np.sin(ph) * live], axis=1)
    pht = np.transpose(ph, (0, 2, 1))
    wt = np.transpose(weight, (0, 2, 1)) / n
    g1 = np.concatenate([np.cos(pht) * wt, -np.sin(pht) * wt], axis=2)
    e = np.arange(n2)[:, None]
    bb = np.arange(n2)[None, :]
    th = 2.0 * math.pi * ((e * bb) % n2) / n2
    fr, fi = np.cos(th), -np.sin(th)
    f2 = np.block([[fr, -fi], [fi, fr]])
    f2c = np.block([[fr, fi], [-fi, fr]])
    cast = lambda m: jnp.asarray(m.astype(np.float32)).astype(BF16)
    return cast(f1), cast(f2), cast(f2c), cast(g1)


def _unrolled_loop(n, unroll, fn):
    def body(i, carry):
        fn([i * unroll + u for u in range(unroll)])
        return carry

    lax.fori_loop(0, n // unroll, body, 0)


FFT_UNROLL_SMALL = 8
FFT_UNROLL_SLAB = 4


def _stage1(src_ref, f1_ref, w_scr, combine=None):
    n2, two_n1, p = f1_ref.shape

    def step(bs):
        if combine is None:
            xs = [src_ref[pl.ds(b, p, stride=n2), :] for b in bs]
        else:
            xs = [combine(src_ref[0][pl.ds(b, p, stride=n2), :], src_ref[1][pl.ds(b, p, stride=n2), :])
                  for b in bs]
        res = [_dot(f1_ref[b], x.astype(BF16)) for b, x in zip(bs, xs)]
        for b, r in zip(bs, res):
            w_scr[pl.ds(pl.multiple_of(b * two_n1, two_n1), two_n1), :] = r

    _unrolled_loop(n2, FFT_UNROLL_SMALL, step)


def _load_slab(w_scr, c, n1, n2):
    re = w_scr[pl.ds(c, n2, stride=2 * n1), :]
    im = w_scr[pl.ds(n1 + c, n2, stride=2 * n1), :]
    return jnp.concatenate([re, im], axis=0).astype(BF16)


def _two_stage_tf_kernel(hf_ref, hb_ref, f1_ref, f2_ref, tfr_ref, tfi_ref, ws_scr, wd_scr):
    n2, two_n1, _ = f1_ref.shape
    n1 = two_n1 // 2
    _stage1((hf_ref, hb_ref), f1_ref, ws_scr, combine=lambda u, v: u + v)
    _stage1((hf_ref, hb_ref), f1_ref, wd_scr, combine=lambda u, v: u - v)

    def step(cs):
        xs = [_dot(f2_ref[0:n2, :], _load_slab(ws_scr, c, n1, n2)) for c in cs]
        xd = [_dot(f2_ref[n2:, :], _load_slab(wd_scr, c, n1, n2)) for c in cs]
        for c, a, b in zip(cs, xs, xd):
            r0 = pl.multiple_of(c * n2, n2)
            tfr_ref[pl.ds(r0, n2), :] = a
            tfi_ref[pl.ds(r0, n2), :] = b

    _unrolled_loop(n1, FFT_UNROLL_SLAB, step)


def _two_stage_conv_kernel(vv_ref, x0_ref, skip_ref, tfr_ref, tfi_ref,
                           f1_ref, f2_ref, f2c_ref, g1_ref, out_ref, w_scr, t_scr):
    n2, two_n1, p = f1_ref.shape
    n1 = two_n1 // 2
    t_scr[...] = vv_ref[...].astype(F32)
    _stage1(t_scr, f1_ref, w_scr)

    def step2(cs):
        xs = [_dot(f2_ref[...], _load_slab(w_scr, c, n1, n2)) for c in cs]
        ys = []
        for c, x in zip(cs, xs):
            r0 = pl.multiple_of(c * n2, n2)
            xr, xi = x[:n2], x[n2:]
            tr = tfr_ref[pl.ds(r0, n2), :]
            ti = tfi_ref[pl.ds(r0, n2), :]
            ys.append(jnp.concatenate([xr * tr - xi * ti, xr * ti + xi * tr], axis=0).astype(BF16))
        zs = [_dot(f2c_ref[...], y) for y in ys]
        for c, z in zip(cs, zs):
            w_scr[pl.ds(c, n2, stride=two_n1), :] = z[:n2]
            w_scr[pl.ds(n1 + c, n2, stride=two_n1), :] = z[n2:]

    _unrolled_loop(n1, FFT_UNROLL_SLAB, step2)

    def step3(bs):
        zb = [w_scr[pl.ds(pl.multiple_of(b * two_n1, two_n1), two_n1), :].astype(BF16) for b in bs]
        res = [_dot(g1_ref[b], z) for b, z in zip(bs, zb)]
        for b, r in zip(bs, res):
            t_scr[pl.ds(b, p, stride=n2), :] = r

    _unrolled_loop(n2, FFT_UNROLL_SMALL, step3)
    out = (t_scr[...] + vv_ref[...].astype(F32) * skip_ref[...]) * x0_ref[...].astype(F32)
    out_ref[...] = out.astype(out_ref.dtype)


def hy_longconv_two_stage(vv, x0, skip, filt, row0, n_seq, l):
    d = vv.shape[1]
    f1, f2, f2c, g1 = _two_stage_tables(l)
    n2, two_n1, p = f1.shape
    n = (two_n1 // 2) * n2
    dc = LANES
    tfr, tfi = pl.pallas_call(
        _two_stage_tf_kernel,
        grid=(d // dc,),
        in_specs=[pl.BlockSpec((l, dc), lambda j: (0, j)),
                  pl.BlockSpec((l, dc), lambda j: (0, d // dc + j)),
                  _const_spec(f1.shape), _const_spec(f2.shape)],
        out_specs=[pl.BlockSpec((n, dc), lambda j: (0, j)), pl.BlockSpec((n, dc), lambda j: (0, j))],
        out_shape=[jax.ShapeDtypeStruct((n, d), F32), jax.ShapeDtypeStruct((n, d), F32)],
        scratch_shapes=[pltpu.VMEM((n2 * two_n1, dc), F32), pltpu.VMEM((n2 * two_n1, dc), F32)],
        compiler_params=_cparams(("arbitrary",)),
        name="hy_tf_two_stage",
    )(filt, filt, f1, f2)
    base = row0 // l
    return pl.pallas_call(
        _two_stage_conv_kernel,
        grid=(d // dc, n_seq),
        in_specs=[pl.BlockSpec((l, dc), lambda j, s: (base + s, j)),
                  pl.BlockSpec((l, dc), lambda j, s: (base + s, j)),
                  pl.BlockSpec((1, dc), lambda j, s: (0, j)),
                  pl.BlockSpec((n, dc), lambda j, s: (0, j)),
                  pl.BlockSpec((n, dc), lambda j, s: (0, j)),
                  _const_spec(f1.shape), _const_spec(f2.shape), _const_spec(f2c.shape),
                  _const_spec(g1.shape)],
        out_specs=pl.BlockSpec((l, dc), lambda j, s: (s, j)),
        out_shape=jax.ShapeDtypeStruct((n_seq * l, d), BF16),
        scratch_shapes=[pltpu.VMEM((n2 * two_n1, dc), F32), pltpu.VMEM((l, dc), F32)],
        compiler_params=_cparams(("arbitrary", "arbitrary")),
        name="hy_longconv_two_stage",
    )(vv, x0, skip.reshape(1, d).astype(F32), tfr, tfi, f1, f2, f2c, g1)


def _hy_out_kernel(y_ref, w_ref, b_ref, x_ref, mod_ref, out_ref, *, gate_idx):
    y = _dot(y_ref[...].astype(BF16), w_ref[...]) + b_ref[...]
    out_ref[...] = x_ref[...] + mod_ref[0, gate_idx:gate_idx + 1, :] * y


def hy_out(yg, w_out, b_out, x, mod, gate_idx, t_ctx, l_lat):
    t, d = x.shape
    grp = functools.partial(_group_of_tile, tm=TM, t_ctx=t_ctx, l_lat=l_lat)
    row_spec = lambda w: pl.BlockSpec((TM, w), lambda i: (i, 0))
    return pl.pallas_call(
        functools.partial(_hy_out_kernel, gate_idx=gate_idx),
        grid=(t // TM,),
        in_specs=[row_spec(d), _const_spec(w_out.shape), _const_spec((1, d)), row_spec(d),
                  pl.BlockSpec((1,) + mod.shape[1:], lambda i: (grp(i), 0, 0))],
        out_specs=row_spec(d),
        out_shape=jax.ShapeDtypeStruct((t, d), F32),
        compiler_params=_cparams(("arbitrary",)),
        name="hy_out",
    )(yg, w_out, b_out.reshape(1, d).astype(F32), x, mod)


def _grid_pos_embed(rows, d):
    r = np.repeat(np.arange(rows, dtype=np.float64), GRID_W)
    col = np.tile(np.arange(GRID_W, dtype=np.float64), rows)
    quarter = d // 4
    omega = 1.0 / (10000.0 ** (np.arange(quarter, dtype=np.float64) / quarter))
    ang_r = r[:, None] * omega
    ang_c = col[:, None] * omega
    pe = np.concatenate([np.sin(ang_r), np.cos(ang_r), np.sin(ang_c), np.cos(ang_c)], axis=-1)
    return jnp.asarray(pe.astype(np.float32))


def delta_layer(x, g, mod, state_lat, w_in, conv_w, a_log, dt_bias, g_head, w_out, dims):
    t_ctx, l_ctx, n_ctx, l_lat, n_lat = dims
    hk = DN_HEADS * DN_DK
    nqkv = 3 * hk
    w_qkv = w_in[:, :nqkv].astype(BF16)
    w_z = w_in[:, nqkv:nqkv + hk].astype(BF16)
    w_ab = jnp.zeros((w_in.shape[0], LANES), BF16).at[:, :4 * DN_HEADS].set(w_in[:, nqkv + hk:].astype(BF16))
    qkv, z, ab = norm_mod_matmul(x, g, mod, 0, 1, [w_qkv, w_z, w_ab], None, [BF16, BF16, F32],
                                 t_ctx, l_lat)
    q, k, v, gc, beta, gct = dn_prep(qkv, ab, conv_w.astype(F32), a_log, dt_bias, t_ctx, l_ctx, l_lat)
    outs = []
    ctx_states = []
    seq_lens = (l_ctx,) * n_ctx + (l_lat,) * n_lat
    for direction in range(2):
        o, s_all = dn_chunk_scan(q, k, v, gc, beta, gct, state_lat.astype(F32), direction=direction,
                                 seq_lens=seq_lens, n_zero_init=n_ctx)
        outs.append(o)
        ctx_states.append(s_all[:n_ctx])
    x = dn_out(outs[0], outs[1], z, g_head, w_out.astype(BF16), x, mod, 2, t_ctx, l_lat)
    return x, jnp.stack(ctx_states, axis=1)


def hyena_layer(x, g, mod, w_in, b_in, conv_w, conv_b, freq, fw1, fb1, fw2, fb2, fw3, fb3, fw4,
                skip, w_out, b_out, dims):
    t_ctx, l_ctx, n_ctx, l_lat, n_lat = dims
    d = x.shape[1]
    (u,) = norm_mod_matmul(x, g, mod, 0, 1, [w_in.astype(BF16)], [b_in.astype(F32)], [BF16],
                           t_ctx, l_lat)
    x0, vv = hy_conv_gate(u, conv_w.astype(F32), conv_b, t_ctx, l_ctx, l_lat)
    filt_c = hy_filter(l_ctx, d, freq, fw1, fb1, fw2, fb2, fw3, fb3, fw4)
    filt_l = hy_filter(l_lat, d, freq, fw1, fb1, fw2, fb2, fw3, fb3, fw4)
    y_c = hy_longconv_dense(vv, x0, skip, filt_c, n_ctx, l_ctx)
    y_l = hy_longconv_two_stage(vv, x0, skip, filt_l, t_ctx, n_lat, l_lat)
    yg = jnp.concatenate([y_c, y_l], axis=0)
    return hy_out(yg, w_out.astype(BF16), b_out, x, mod, 2, t_ctx, l_lat)


def kernel(x_prompt, x_sample, state_delta, c, c_ctx, w_ada, b_ada, g_norm, dn_w_in, dn_conv, dn_a_log, dn_dt_bias, dn_g_head, dn_w_out, hy_w_in, hy_b_in, hy_conv, hy_conv_b, hy_freq, hy_f_w1, hy_f_b1, hy_f_w2, hy_f_b2, hy_f_w3, hy_f_b3, hy_f_w4, hy_skip, hy_w_out, hy_b_out, w_router, router_bias, moe_w1, moe_w3, moe_w2, g_final):
    n_ctx, l_ctx, d = x_prompt.shape
    n_lat, l_lat, _ = x_sample.shape
    depth = w_ada.shape[0]
    t_ctx = n_ctx * l_ctx
    dims = (t_ctx, l_ctx, n_ctx, l_lat, n_lat)
    assert l_ctx % TM == 0 and l_lat % TM == 0 and t_ctx % l_lat == 0
    assert n_lat + 1 <= SUBLANES

    pos = _grid_pos_embed(l_lat // GRID_W, d)
    x = jnp.concatenate([x_prompt.reshape(t_ctx, d).astype(F32),
                         (x_sample.astype(F32) + pos[None]).reshape(n_lat * l_lat, d)], axis=0)
    cond = jnp.zeros((SUBLANES, d), F32).at[0].set(c_ctx.astype(F32)).at[1:1 + n_lat].set(c.astype(F32))
    mod_all = ada_modulation(cond, w_ada.astype(F32), b_ada.astype(F32))
    mod_all = mod_all.reshape(depth, SUBLANES, 6, d)

    ctx_states = []
    for i in range(depth):
        mod = mod_all[i]
        j = i // 2
        if i % 2 == 0:
            x, s_ctx = delta_layer(x, g_norm[i, 0], mod, state_delta[:, j], dn_w_in[j], dn_conv[j],
                                   dn_a_log[j], dn_dt_bias[j], dn_g_head[j], dn_w_out[j], dims)
            ctx_states.append(s_ctx.astype(x_prompt.dtype))
        else:
            x = hyena_layer(x, g_norm[i, 0], mod, hy_w_in[j], hy_b_in[j], hy_conv[j], hy_conv_b[j],
                            hy_freq[j], hy_f_w1[j], hy_f_b1[j], hy_f_w2[j], hy_f_b2[j], hy_f_w3[j],
                            hy_f_b3[j], hy_f_w4[j], hy_skip[j], hy_w_out[j], hy_b_out[j], dims)
        x = moe_layer(x, g_norm[i, 1], mod, w_router, router_bias, moe_w1, moe_w3, moe_w2, i,
                      t_ctx, l_lat, g_final if i == depth - 1 else None)
    y_prompt = x[:t_ctx].reshape(n_ctx, l_ctx, d).astype(x_prompt.dtype)
    y_sample = x[t_ctx:].reshape(n_lat, l_lat, d).astype(x_sample.dtype)
    new_state = jnp.stack(ctx_states, axis=1)
    return (y_prompt, y_sample, new_state)
```

```python
import functools
import math

import numpy as np
import jax
import jax.numpy as jnp
from jax import lax
from jax.experimental import pallas as pl
from jax.experimental.pallas import tpu as pltpu

F32 = jnp.float32
BF16 = jnp.bfloat16
HI = lax.Precision.HIGHEST

EPS = 1e-6
GRID_W = 64
DN_HEADS = 8
DN_DK = 128
DN_DV = 128
DN_CHUNK = 64
HY_EMB = 33
HY_ORDER = 64
HY_DECAY_SHORT_PCT = 0.3
HY_DECAY_LONG_PCT = 1.5
HY_DECAY_TARGET = 1e-2
N_EXPERTS = 32
N_GROUPS = 4
EXPERTS_PER_GROUP = N_EXPERTS // N_GROUPS

LANES = 128
SUBLANES = 8
TM = 256
HALO_ROWS = 16
MOE_TILE = 256
FFT_N2 = 128
DN_PREP_CHUNKS = 4
VMEM_LIMIT = 48 * 1024 * 1024


def _cparams(sem):
    return pltpu.CompilerParams(dimension_semantics=sem, vmem_limit_bytes=VMEM_LIMIT)


def _sigmoid(x):
    return 1.0 / (1.0 + jnp.exp(-x))


def _silu(x):
    return x * _sigmoid(x)


def _dot(a, b, precision=None):
    return jnp.dot(a, b, preferred_element_type=F32, precision=precision)


def _const_spec(shape):
    nd = len(shape)
    return pl.BlockSpec(shape, lambda *_: (0,) * nd)


def _ada_kernel(cond_ref, w_ref, b_ref, out_ref):
    s = _silu(cond_ref[...])
    out_ref[0] = _dot(s, w_ref[0], HI) + b_ref[0]


def ada_modulation(cond, w_ada, b_ada):
    depth, d, n = w_ada.shape
    tn = 1536
    return pl.pallas_call(
        _ada_kernel,
        grid=(depth, n // tn),
        in_specs=[
            pl.BlockSpec((SUBLANES, d), lambda i, j: (0, 0)),
            pl.BlockSpec((1, d, tn), lambda i, j: (i, 0, j)),
            pl.BlockSpec((1, 1, tn), lambda i, j: (i, 0, j)),
        ],
        out_specs=pl.BlockSpec((1, SUBLANES, tn), lambda i, j: (i, 0, j)),
        out_shape=jax.ShapeDtypeStruct((depth, SUBLANES, n), F32),
        compiler_params=_cparams(("arbitrary", "arbitrary")),
        name="ada_modulation",
    )(cond, w_ada, b_ada.reshape(depth, 1, n))


def _norm_mod(x, g, mod_ref, shift_idx, scale_idx):
    ms = jnp.mean(x * x, axis=-1, keepdims=True)
    y = x * lax.rsqrt(ms + EPS) * g
    scale = mod_ref[0, scale_idx:scale_idx + 1, :]
    shift = mod_ref[0, shift_idx:shift_idx + 1, :]
    return y * (1.0 + scale) + shift


def _group_of_tile(i, tm, t_ctx, l_lat):
    r = i * tm
    return jnp.where(r < t_ctx, 0, 1 + (r - t_ctx) // l_lat)


def _nmm_kernel(x_ref, g_ref, mod_ref, *refs, n_out, has_bias, shift_idx, scale_idx):
    h = _norm_mod(x_ref[...], g_ref[...], mod_ref, shift_idx, scale_idx).astype(BF16)
    w_refs = refs[:n_out]
    b_refs = refs[n_out:n_out + (n_out if has_bias else 0)]
    o_refs = refs[n_out + len(b_refs):]
    for k in range(n_out):
        acc = _dot(h, w_refs[k][...])
        if has_bias:
            acc = acc + b_refs[k][...]
        o_refs[k][...] = acc.astype(o_refs[k].dtype)


def norm_mod_matmul(x, g, mod, shift_idx, scale_idx, weights, biases, out_dtypes, t_ctx, l_lat):
    t, d = x.shape
    n_out = len(weights)
    has_bias = biases is not None
    grp = functools.partial(_group_of_tile, tm=TM, t_ctx=t_ctx, l_lat=l_lat)
    in_specs = [
        pl.BlockSpec((TM, d), lambda i: (i, 0)),
        _const_spec((1, d)),
        pl.BlockSpec((1,) + mod.shape[1:], lambda i: (grp(i), 0, 0)),
    ]
    args = [x, g.reshape(1, d), mod]
    for w in weights:
        in_specs.append(_const_spec(w.shape))
        args.append(w)
    if has_bias:
        for b in biases:
            in_specs.append(_const_spec((1, b.shape[-1])))
            args.append(b.reshape(1, -1))
    out_specs = [pl.BlockSpec((TM, w.shape[1]), lambda i: (i, 0)) for w in weights]
    out_shape = [jax.ShapeDtypeStruct((t, w.shape[1]), dt) for w, dt in zip(weights, out_dtypes)]
    return pl.pallas_call(
        functools.partial(_nmm_kernel, n_out=n_out, has_bias=has_bias,
                          shift_idx=shift_idx, scale_idx=scale_idx),
        grid=(t // TM,),
        in_specs=in_specs,
        out_specs=out_specs,
        out_shape=out_shape,
        compiler_params=_cparams(("arbitrary",)),
        name="norm_mod_matmul",
    )(*args)


def _seq_flags(i, tm, t_ctx, l_ctx, l_lat):
    r = i * tm
    in_ctx = r < t_ctx
    pos = jnp.where(in_ctx, r % l_ctx, (r - t_ctx) % l_lat)
    length = jnp.where(in_ctx, l_ctx, l_lat)
    return pos != 0, (pos + tm) != length


def _conv3(x, prev_rows, next_rows, w_ref, has_prev, has_next):
    tm = x.shape[0]
    x = x.astype(F32)
    halo_p = jnp.where(has_prev, prev_rows[HALO_ROWS - 1:HALO_ROWS, :].astype(F32), 0.0)
    halo_n = jnp.where(has_next, next_rows[0:1, :].astype(F32), 0.0)
    w0, w1, w2 = w_ref[0:1, :], w_ref[1:2, :], w_ref[2:3, :]
    c = w0 * pltpu.roll(x, 1, 0) + w1 * x + w2 * pltpu.roll(x, tm - 1, 0)
    row = lax.broadcasted_iota(jnp.int32, (SUBLANES, 1), 0)
    top = c[:SUBLANES] + jnp.where(row == 0, w0 * (halo_p - x[tm - 1:tm]), 0.0)
    bot = c[tm - SUBLANES:] + jnp.where(row == SUBLANES - 1, w2 * (halo_n - x[0:1]), 0.0)
    return jnp.concatenate([top, c[SUBLANES:tm - SUBLANES], bot], axis=0)


def _halo_specs(c, tm, t):
    nblk = t // HALO_ROWS
    per = tm // HALO_ROWS
    return [
        pl.BlockSpec((tm, c), lambda i: (i, 0)),
        pl.BlockSpec((HALO_ROWS, c), lambda i: (jnp.maximum(i * per - 1, 0), 0)),
        pl.BlockSpec((HALO_ROWS, c), lambda i: (jnp.minimum((i + 1) * per, nblk - 1), 0)),
    ]


def _softplus(x):
    return jnp.maximum(x, 0.0) + jnp.log(1.0 + jnp.exp(-jnp.abs(x)))


def _dn_prep_kernel(x_ref, xp_ref, xn_ref, cw_ref, ab_ref, nega_ref, dtb_ref, tri_ref,
                    q_ref, k_ref, v_ref, gc_ref, beta_ref, gct_ref,
                    *, t_ctx, l_ctx, l_lat):
    i = pl.program_id(0)
    has_prev, has_next = _seq_flags(i, TM, t_ctx, l_ctx, l_lat)
    c = _conv3(x_ref[...], xp_ref[...], xn_ref[...], cw_ref, has_prev, has_next)
    s = _silu(c)
    hk = DN_HEADS * DN_DK
    for h in range(DN_HEADS):
        lo, hi = h * DN_DK, (h + 1) * DN_DK
        qh = s[:, lo:hi]
        kh = s[:, hk + lo:hk + hi]
        qn = lax.rsqrt(jnp.sum(qh * qh, axis=-1, keepdims=True) + EPS)
        kn = lax.rsqrt(jnp.sum(kh * kh, axis=-1, keepdims=True) + EPS)
        q_ref[:, lo:hi] = (qh * (qn * (DN_DK ** -0.5))).astype(BF16)
        k_ref[:, lo:hi] = (kh * kn).astype(BF16)
    v_ref[...] = s[:, 2 * hk:].astype(BF16)
    ab = ab_ref[...]
    lg = nega_ref[...] * _softplus(ab + dtb_ref[...])
    beta_ref[...] = _sigmoid(ab)
    nch = TM // DN_CHUNK
    for d in range(2):
        gc = _dot(tri_ref[d], lg, HI)
        gc_ref[d] = gc
        gct = gc.T
        for ch in range(nch):
            gct_ref[d, ch] = gct[0:2 * DN_HEADS, ch * DN_CHUNK:(ch + 1) * DN_CHUNK]


def _chunk_tri():
    idx = np.arange(TM)
    same = (idx[:, None] // DN_CHUNK) == (idx[None, :] // DN_CHUNK)
    fwd = same & (idx[None, :] <= idx[:, None])
    bwd = same & (idx[None, :] >= idx[:, None])
    return jnp.asarray(np.stack([fwd, bwd]).astype(np.float32))


def dn_prep(qkv, ab, conv_w, a_log, dt_bias, t_ctx, l_ctx, l_lat):
    t, c = qkv.shape
    hd = DN_HEADS * DN_DK
    nega = jnp.zeros((1, LANES), F32).at[0, :2 * DN_HEADS].set(-jnp.exp(a_log.astype(F32)).reshape(-1))
    dtb = jnp.zeros((1, LANES), F32).at[0, :2 * DN_HEADS].set(dt_bias.astype(F32).reshape(-1))
    cw = jnp.zeros((SUBLANES, c), F32).at[:conv_w.shape[0]].set(conv_w)
    nch = TM // DN_CHUNK
    row_spec = lambda w: pl.BlockSpec((TM, w), lambda i: (i, 0))
    return pl.pallas_call(
        functools.partial(_dn_prep_kernel, t_ctx=t_ctx, l_ctx=l_ctx, l_lat=l_lat),
        grid=(t // TM,),
        in_specs=_halo_specs(c, TM, t) + [
            _const_spec((SUBLANES, c)),
            row_spec(LANES),
            _const_spec((1, LANES)),
            _const_spec((1, LANES)),
            _const_spec((2, TM, TM)),
        ],
        out_specs=[
            row_spec(hd), row_spec(hd), row_spec(hd),
            pl.BlockSpec((2, TM, LANES), lambda i: (0, i, 0)),
            row_spec(LANES),
            pl.BlockSpec((2, nch, 2 * DN_HEADS, DN_CHUNK), lambda i: (0, i, 0, 0)),
        ],
        out_shape=[
            jax.ShapeDtypeStruct((t, hd), BF16),
            jax.ShapeDtypeStruct((t, hd), BF16),
            jax.ShapeDtypeStruct((t, hd), BF16),
            jax.ShapeDtypeStruct((2, t, LANES), F32),
            jax.ShapeDtypeStruct((t, LANES), F32),
            jax.ShapeDtypeStruct((2, t // DN_CHUNK, 2 * DN_HEADS, DN_CHUNK), F32),
        ],
        compiler_params=_cparams(("arbitrary",)),
        name="dn_prep",
    )(qkv, qkv, qkv, cw, ab, nega, dtb, _chunk_tri())


def _bmm(a, b):
    return _dot(a.astype(BF16), b.astype(BF16))


def _inv_unit_tri_batch(lms, in_blk, eye):
    dgs = [jnp.where(in_blk, lm, 0.0) for lm in lms]
    offs = [lm - dg for lm, dg in zip(lms, dgs)]
    n1 = [-dg for dg in dgs]
    n2 = [_bmm(a, a) for a in n1]
    p = [eye + a for a in n1]
    n4 = [_bmm(a, a) for a in n2]
    p = [x + _bmm(x, a) for x, a in zip(p, n2)]
    n8 = [_bmm(a, a) for a in n4]
    p = [x + _bmm(x, a) for x, a in zip(p, n4)]
    p = [x + _bmm(x, a) for x, a in zip(p, n8)]
    m = [_bmm(x, o) for x, o in zip(p, offs)]
    m2 = [_bmm(a, a) for a in m]
    t1 = [eye - a for a in m]
    t1 = [x + _bmm(x, a) for x, a in zip(t1, m2)]
    return [_bmm(x, y) for x, y in zip(t1, p)]


def _dn_chunk_kernel(blk_ref, seq_ref, first_ref, last_ref,
                     q_ref, k_ref, v_ref, gc_ref, beta_ref, gct_ref, s0_ref,
                     o_ref, sfin_ref, s_scr, attn_scr, u_scr, qw_scr, kd_scr, gl_scr,
                     *, direction, n_zero_init):
    step = pl.program_id(0)
    c_sz = DN_CHUNK
    nch = TM // c_sz
    heads = range(DN_HEADS)

    @pl.when(first_ref[step] == 1)
    def _():
        s_scr[...] = jnp.where(seq_ref[step] >= n_zero_init, s0_ref[0, 0], 0.0)

    row = lax.broadcasted_iota(jnp.int32, (c_sz, c_sz), 0)
    col = lax.broadcasted_iota(jnp.int32, (c_sz, c_sz), 1)
    if direction == 0:
        incl, strict = row >= col, row > col
        last = c_sz - 1
    else:
        incl, strict = row <= col, row < col
        last = 0
    in_blk = (row // 16) == (col // 16)
    eye = (row == col).astype(F32)

    def lanes(h):
        return slice(h * DN_DK, (h + 1) * DN_DK)

    def prep_body(it, carry):
        units = [(it * DN_PREP_CHUNKS + c, h) for c in range(DN_PREP_CHUNKS) for h in heads]
        idx = range(len(units))
        rows = [pl.ds(pl.multiple_of(ch * c_sz, c_sz), c_sz) for ch, _ in units]
        gls = [direction * DN_HEADS + h for _, h in units]
        qs = [q_ref[rows[i], lanes(units[i][1])].astype(F32) for i in idx]
        ks = [k_ref[rows[i], lanes(units[i][1])].astype(F32) for i in idx]
        gcc = [gc_ref[0, rows[i], gls[i]:gls[i] + 1] for i in idx]
        beta = [beta_ref[rows[i], 2 * DN_HEADS + gls[i]:2 * DN_HEADS + gls[i] + 1] for i in idx]
        gcr = [gct_ref[0, units[i][0]][gls[i]:gls[i] + 1, :] for i in idx]
        dec = [jnp.exp(jnp.where(incl, gcc[i] - gcr[i], -jnp.inf)) for i in idx]
        kb = [ks[i] * beta[i] for i in idx]
        prod = [lax.dot_general(jnp.concatenate([qs[i], kb[i]], axis=0).astype(BF16), ks[i].astype(BF16),
                                (((1,), (1,)), ((), ())), preferred_element_type=F32)
                for i in idx]
        for i, (ch, h) in enumerate(units):
            attn_scr[ch, h] = (prod[i][:c_sz] * dec[i]).astype(BF16)
        lms = [jnp.where(strict, prod[i][c_sz:] * dec[i], 0.0) for i in idx]
        tinv = _inv_unit_tri_batch(lms, in_blk, eye)
        eg = [jnp.exp(gcc[i]) for i in idx]
        uw = [_bmm(tinv[i], jnp.concatenate(
                  [v_ref[rows[i], lanes(units[i][1])].astype(F32) * beta[i], kb[i] * eg[i]], axis=1))
              for i in idx]
        for i, (ch, h) in enumerate(units):
            u_scr[rows[i], lanes(h)] = uw[i][:, :DN_DV]
            qw_scr[ch, h] = jnp.concatenate([qs[i] * eg[i], uw[i][:, DN_DV:]], axis=0).astype(BF16)
            g_last = gcc[i][last:last + 1, :]
            kd_scr[ch, h] = (ks[i] * jnp.exp(g_last - gcc[i])).T.astype(BF16)
            gl_scr[ch, h] = jnp.broadcast_to(jnp.exp(g_last), (1, DN_DV))
        return carry

    lax.fori_loop(0, nch // DN_PREP_CHUNKS, prep_body, 0)

    def scan_body(ci, carry):
        ch = ci if direction == 0 else nch - 1 - ci
        r0 = pl.multiple_of(ch * c_sz, c_sz)
        rows = pl.ds(r0, c_sz)
        s = [s_scr[h] for h in heads]
        qs_ws = [_dot(qw_scr[ch, h], s[h].astype(BF16)) for h in heads]
        vb = [(u_scr[rows, lanes(h)] - qs_ws[h][c_sz:]).astype(BF16) for h in heads]
        for h in heads:
            o_ref[rows, lanes(h)] = (qs_ws[h][:c_sz] + _dot(attn_scr[ch, h], vb[h])).astype(o_ref.dtype)
            s_scr[h] = s[h] * gl_scr[ch, h] + _dot(kd_scr[ch, h], vb[h])
        return carry

    lax.fori_loop(0, nch, scan_body, 0)

    @pl.when(last_ref[step] == 1)
    def _():
        sfin_ref[0, 0] = s_scr[...]


def _scan_tables(direction, seq_lens):
    blk, seq, first, last = [], [], [], []
    base = 0
    for s, length in enumerate(seq_lens):
        nblk = length // TM
        order = range(nblk) if direction == 0 else range(nblk - 1, -1, -1)
        for pos, jj in enumerate(order):
            blk.append(base + jj)
            seq.append(s)
            first.append(int(pos == 0))
            last.append(int(pos == nblk - 1))
        base += nblk
    return [jnp.asarray(np.asarray(a, np.int32)) for a in (blk, seq, first, last)]


def dn_chunk_scan(q, k, v, gc, beta, gct, s0, *, direction, seq_lens, n_zero_init):
    t, hd = q.shape
    nch = TM // DN_CHUNK
    n_seq = len(seq_lens)
    tables = _scan_tables(direction, seq_lens)
    n_steps = int(tables[0].shape[0])
    row_spec = lambda w: pl.BlockSpec((TM, w), lambda i, blk, seq, fst, lst: (blk[i], 0))
    state_blk = (1, 1, DN_HEADS, DN_DK, DN_DV)
    grid_spec = pltpu.PrefetchScalarGridSpec(
        num_scalar_prefetch=4,
        grid=(n_steps,),
        in_specs=[
            row_spec(hd), row_spec(hd), row_spec(hd),
            pl.BlockSpec((1, TM, LANES), lambda i, blk, seq, fst, lst: (direction, blk[i], 0)),
            row_spec(LANES),
            pl.BlockSpec((1, nch, 2 * DN_HEADS, DN_CHUNK),
                         lambda i, blk, seq, fst, lst: (direction, blk[i], 0, 0)),
            pl.BlockSpec(state_blk, lambda i, blk, seq, fst, lst:
                         (jnp.maximum(seq[i] - n_zero_init, 0), direction, 0, 0, 0)),
        ],
        out_specs=[
            row_spec(hd),
            pl.BlockSpec(state_blk, lambda i, blk, seq, fst, lst: (seq[i], 0, 0, 0, 0)),
        ],
        scratch_shapes=[
            pltpu.VMEM((DN_HEADS, DN_DK, DN_DV), F32),
            pltpu.VMEM((nch, DN_HEADS, DN_CHUNK, DN_CHUNK), BF16),
            pltpu.VMEM((TM, hd), F32),
            pltpu.VMEM((nch, DN_HEADS, 2 * DN_CHUNK, DN_DK), BF16),
            pltpu.VMEM((nch, DN_HEADS, DN_DK, DN_CHUNK), BF16),
            pltpu.VMEM((nch, DN_HEADS, 1, DN_DV), F32),
        ],
    )
    o, sfin = pl.pallas_call(
        functools.partial(_dn_chunk_kernel, direction=direction, n_zero_init=n_zero_init),
        grid_spec=grid_spec,
        out_shape=[
            jax.ShapeDtypeStruct((t, hd), BF16),
            jax.ShapeDtypeStruct((n_seq,) + state_blk[1:], F32),
        ],
        compiler_params=_cparams(("arbitrary",)),
        name="dn_chunk_scan_d%d" % direction,
    )(*tables, q, k, v, gc, beta, gct, s0)
    return o, sfin[:, 0]


def _dn_out_kernel(of_ref, ob_ref, z_ref, gh_ref, w_ref, x_ref, mod_ref, out_ref, *, gate_idx):
    o = of_ref[...].astype(F32) + ob_ref[...].astype(F32)
    z = z_ref[...].astype(F32)
    parts = []
    for h in range(DN_HEADS):
        lanes = slice(h * DN_DV, (h + 1) * DN_DV)
        oh = o[:, lanes]
        ms = jnp.mean(oh * oh, axis=-1, keepdims=True)
        parts.append(oh * lax.rsqrt(ms + EPS) * gh_ref[...])
    on = jnp.concatenate(parts, axis=1) * _silu(z)
    y = _dot(on.astype(BF16), w_ref[...])
    out_ref[...] = x_ref[...] + mod_ref[0, gate_idx:gate_idx + 1, :] * y


def dn_out(o_f, o_b, z, g_head, w_out, x, mod, gate_idx, t_ctx, l_lat):
    t, d = x.shape
    hd = o_f.shape[1]
    grp = functools.partial(_group_of_tile, tm=TM, t_ctx=t_ctx, l_lat=l_lat)
    row_spec = lambda w: pl.BlockSpec((TM, w), lambda i: (i, 0))
    return pl.pallas_call(
        functools.partial(_dn_out_kernel, gate_idx=gate_idx),
        grid=(t // TM,),
        in_specs=[
            row_spec(hd), row_spec(hd), row_spec(hd),
            _const_spec((1, DN_DV)),
            _const_spec(w_out.shape),
            row_spec(d),
            pl.BlockSpec((1,) + mod.shape[1:], lambda i: (grp(i), 0, 0)),
        ],
        out_specs=row_spec(d),
        out_shape=jax.ShapeDtypeStruct((t, d), F32),
        compiler_params=_cparams(("arbitrary",)),
        name="dn_out",
    )(o_f, o_b, z, g_head.reshape(1, DN_DV).astype(F32), w_out, x, mod)


ROUTE_ROWS = 8


def _first_argmax_rows(vals, row):
    m = jnp.max(vals, axis=0, keepdims=True)
    idx = jnp.min(jnp.where(vals == m, row, float(SUBLANES)), axis=0, keepdims=True)
    return m, idx


def _router_kernel(x_ref, g_ref, mod_ref, wh_ref, wl_ref, rb_ref, tri_ref,
                   h_ref, route_ref, count_ref, run_scr, *, shift_idx, scale_idx):
    i = pl.program_id(0)

    @pl.when(i == 0)
    def _():
        run_scr[...] = jnp.zeros_like(run_scr)

    h = _norm_mod(x_ref[...], g_ref[...], mod_ref, shift_idx, scale_idx)
    hb = h.astype(BF16)
    h_ref[...] = hb
    hl = (h - hb.astype(F32)).astype(BF16)
    nt = (((1,), (1,)), ((), ()))
    wh = wh_ref[...]
    logits = (lax.dot_general(wh, hb, nt, preferred_element_type=F32)
              + (lax.dot_general(wh, hl, nt, preferred_element_type=F32)
                 + lax.dot_general(wl_ref[...], hb, nt, preferred_element_type=F32)))
    scores = _sigmoid(logits)
    sel = scores + rb_ref[:, 0:1]
    tm = scores.shape[1]
    row8 = lax.broadcasted_iota(jnp.int32, (SUBLANES, tm), 0).astype(F32)
    best_score = None
    for g in range(N_GROUPS):
        s_g = sel[g * EXPERTS_PER_GROUP:(g + 1) * EXPERTS_PER_GROUP, :]
        m1, i1 = _first_argmax_rows(s_g, row8)
        m2, i2 = _first_argmax_rows(jnp.where(row8 == i1, -jnp.inf, s_g), row8)
        gs = m1 + m2
        e1, e2 = i1 + float(g * EXPERTS_PER_GROUP), i2 + float(g * EXPERTS_PER_GROUP)
        if best_score is None:
            best_score, b1, b2 = gs, e1, e2
        else:
            better = gs > best_score
            best_score = jnp.where(better, gs, best_score)
            b1 = jnp.where(better, e1, b1)
            b2 = jnp.where(better, e2, b2)
    row_e = lax.broadcasted_iota(jnp.int32, (N_EXPERTS, tm), 0).astype(F32)
    pick1, pick2 = row_e == b1, row_e == b2
    w1 = jnp.sum(jnp.where(pick1, scores, 0.0), axis=0, keepdims=True)
    w2 = jnp.sum(jnp.where(pick2, scores, 0.0), axis=0, keepdims=True)
    tot = w1 + w2
    chosen = jnp.where(pick1 | pick2, 1.0, 0.0)
    before = run_scr[:, 0:1] + _dot(chosen.astype(BF16), tri_ref[...])
    r1 = jnp.sum(jnp.where(pick1, before, 0.0), axis=0, keepdims=True)
    r2 = jnp.sum(jnp.where(pick2, before, 0.0), axis=0, keepdims=True)
    run_scr[...] = run_scr[...] + jnp.sum(chosen, axis=1, keepdims=True)
    count_ref[...] = run_scr[...]
    zero = jnp.zeros_like(w1)
    route_ref[...] = jnp.concatenate([b1, b2, w1 / tot, w2 / tot, r1, r2, zero, zero], axis=0)


def moe_router(x, g, mod, shift_idx, scale_idx, w_router, router_bias, t_ctx, l_lat):
    t, d = x.shape
    grp = functools.partial(_group_of_tile, tm=TM, t_ctx=t_ctx, l_lat=l_lat)
    wr = w_router.astype(F32).T
    wh = wr.astype(BF16)
    wl = (wr - wh.astype(F32)).astype(BF16)
    rb = jnp.broadcast_to(router_bias.astype(F32)[:, None], (N_EXPERTS, LANES))
    idx = np.arange(TM)
    tri = jnp.asarray((idx[:, None] < idx[None, :]).astype(np.float32)).astype(BF16)
    return pl.pallas_call(
        functools.partial(_router_kernel, shift_idx=shift_idx, scale_idx=scale_idx),
        grid=(t // TM,),
        in_specs=[
            pl.BlockSpec((TM, d), lambda i: (i, 0)),
            _const_spec((1, d)),
            pl.BlockSpec((1,) + mod.shape[1:], lambda i: (grp(i), 0, 0)),
            _const_spec((N_EXPERTS, d)),
            _const_spec((N_EXPERTS, d)),
            _const_spec((N_EXPERTS, LANES)),
            _const_spec((TM, TM)),
        ],
        out_specs=[pl.BlockSpec((TM, d), lambda i: (i, 0)),
                   pl.BlockSpec((ROUTE_ROWS, TM), lambda i: (0, i)),
                   _const_spec((N_EXPERTS, LANES))],
        out_shape=[
            jax.ShapeDtypeStruct((t, d), BF16),
            jax.ShapeDtypeStruct((ROUTE_ROWS, t), F32),
            jax.ShapeDtypeStruct((N_EXPERTS, LANES), F32),
        ],
        scratch_shapes=[pltpu.VMEM((N_EXPERTS, LANES), F32)],
        compiler_params=_cparams(("arbitrary",)),
        name="moe_router",
    )(x, g.reshape(1, d), mod, wh, wl, rb, tri)


def _expert_kernel(te_ref, nt_ref, first_ref, nxt_ref, par_ref,
                   xs_ref, w1_hbm, w3_hbm, w2_hbm, ys_ref,
                   w1_buf, w3_buf, w2_buf, w1_bf, w3_bf, w2_bf, sem, *, layer):
    i = pl.program_id(0)
    active = i < nt_ref[0]

    def weight_copies(e, slot):
        return (pltpu.make_async_copy(w1_hbm.at[layer, e], w1_buf.at[slot], sem.at[slot, 0]),
                pltpu.make_async_copy(w3_hbm.at[layer, e], w3_buf.at[slot], sem.at[slot, 1]),
                pltpu.make_async_copy(w2_hbm.at[layer, e], w2_buf.at[slot], sem.at[slot, 2]))

    @pl.when(active & (i == 0))
    def _():
        for cp in weight_copies(te_ref[0], 0):
            cp.start()

    @pl.when(active & (first_ref[i] == 1))
    def _():
        slot = par_ref[i]
        for cp in weight_copies(te_ref[i], slot):
            cp.wait()

        @pl.when(nxt_ref[i] >= 0)
        def _():
            for cp in weight_copies(nxt_ref[i], 1 - slot):
                cp.start(priority=1)

        w1_bf[...] = w1_buf[slot].astype(BF16)
        w3_bf[...] = w3_buf[slot].astype(BF16)
        w2_bf[...] = w2_buf[slot].astype(BF16)

    @pl.when(active)
    def _():
        xs = xs_ref[...]
        a = _dot(xs, w1_bf[...])
        b = _dot(xs, w3_bf[...])
        hid = (_silu(a) * b).astype(BF16)
        ys_ref[...] = _dot(hid, w2_bf[...]).astype(ys_ref.dtype)

    @pl.when(jnp.logical_not(active))
    def _():
        ys_ref[...] = jnp.zeros_like(ys_ref)


def moe_experts(xs, plan, w1, w3, w2, layer):
    p, d = xs.shape
    de = w1.shape[-1]
    ntile = p // MOE_TILE
    tile_map = lambda i, *_: (i, 0)
    grid_spec = pltpu.PrefetchScalarGridSpec(
        num_scalar_prefetch=5,
        grid=(ntile,),
        in_specs=[
            pl.BlockSpec((MOE_TILE, d), tile_map),
            pl.BlockSpec(memory_space=pl.ANY),
            pl.BlockSpec(memory_space=pl.ANY),
            pl.BlockSpec(memory_space=pl.ANY),
        ],
        out_specs=pl.BlockSpec((MOE_TILE, d), tile_map),
        scratch_shapes=[
            pltpu.VMEM((2, d, de), F32), pltpu.VMEM((2, d, de), F32), pltpu.VMEM((2, de, d), F32),
            pltpu.VMEM((d, de), BF16), pltpu.VMEM((d, de), BF16), pltpu.VMEM((de, d), BF16),
            pltpu.SemaphoreType.DMA((2, 3)),
        ],
    )
    return pl.pallas_call(
        functools.partial(_expert_kernel, layer=layer),
        grid_spec=grid_spec,
        out_shape=jax.ShapeDtypeStruct((p, d), BF16),
        compiler_params=_cparams(("arbitrary",)),
        name="moe_experts",
    )(plan["tile_expert"], plan["n_tiles"], plan["first"], plan["next_expert"], plan["parity"],
      xs, w1, w3, w2)


def moe_dispatch_plan(route, counts):
    t = route.shape[1]
    p = 2 * t + N_EXPERTS * MOE_TILE
    p = -(-p // MOE_TILE) * MOE_TILE
    ntile = p // MOE_TILE
    experts = jnp.arange(N_EXPERTS, dtype=jnp.int32)
    expert = route[0:2].astype(jnp.int32)
    rank = route[4:6].astype(jnp.int32)
    padded = ((counts + MOE_TILE - 1) // MOE_TILE) * MOE_TILE
    ends = jnp.cumsum(padded)
    starts = ends - padded
    slot = starts[expert] + rank
    token = jnp.tile(jnp.arange(t, dtype=jnp.int32), 2)
    slot_token = jnp.zeros((p,), jnp.int32).at[slot.reshape(-1)].set(
        token, unique_indices=True, mode="promise_in_bounds")
    tile_id = jnp.arange(ntile, dtype=jnp.int32)
    tile_expert = jnp.sum((tile_id[:, None] * MOE_TILE >= ends[None, :]).astype(jnp.int32), axis=1)
    n_tiles = (ends[-1] // MOE_TILE).astype(jnp.int32)
    used = tile_id < n_tiles
    tile_expert = jnp.where(used, tile_expert, tile_expert[jnp.maximum(n_tiles - 1, 0)])
    tile_expert = jnp.minimum(tile_expert, N_EXPERTS - 1).astype(jnp.int32)
    prev = jnp.concatenate([jnp.full((1,), -1, jnp.int32), tile_expert[:-1]])
    first = (used & (tile_expert != prev)).astype(jnp.int32)
    parity = ((jnp.cumsum(first) - 1) % 2).astype(jnp.int32)
    cand = jnp.where(counts > 0, experts, N_EXPERTS)
    later = lax.cummin(cand[::-1])[::-1]
    nxt_e = jnp.concatenate([later[1:], jnp.full((1,), N_EXPERTS, jnp.int32)])
    nxt_e = jnp.where(nxt_e >= N_EXPERTS, -1, nxt_e).astype(jnp.int32)
    next_expert = nxt_e[tile_expert]
    return dict(slot_token=slot_token, slot=slot, tile_expert=tile_expert,
                n_tiles=n_tiles.reshape(1), first=first, next_expert=next_expert, parity=parity)


def _combine_kernel(x_ref, y0_ref, y1_ref, wt_ref, mod_ref, *refs, gate_idx, final_norm):
    moe = wt_ref[:, 0:1] * y0_ref[...].astype(F32) + wt_ref[:, 1:2] * y1_ref[...].astype(F32)
    x = x_ref[...] + mod_ref[0, gate_idx:gate_idx + 1, :] * moe
    if final_norm:
        gf_ref, out_ref = refs
        ms = jnp.mean(x * x, axis=-1, keepdims=True)
        out_ref[...] = x * lax.rsqrt(ms + EPS) * gf_ref[...]
    else:
        (out_ref,) = refs
        out_ref[...] = x


def moe_combine(x, y0, y1, wt, mod, gate_idx, t_ctx, l_lat, g_final=None):
    t, d = x.shape
    grp = functools.partial(_group_of_tile, tm=TM, t_ctx=t_ctx, l_lat=l_lat)
    row_spec = lambda w: pl.BlockSpec((TM, w), lambda i: (i, 0))
    in_specs = [row_spec(d), row_spec(d), row_spec(d), row_spec(wt.shape[1]),
                pl.BlockSpec((1,) + mod.shape[1:], lambda i: (grp(i), 0, 0))]
    args = [x, y0, y1, wt, mod]
    if g_final is not None:
        in_specs.append(_const_spec((1, d)))
        args.append(g_final.reshape(1, d).astype(F32))
    return pl.pallas_call(
        functools.partial(_combine_kernel, gate_idx=gate_idx, final_norm=g_final is not None),
        grid=(t // TM,),
        in_specs=in_specs,
        out_specs=row_spec(d),
        out_shape=jax.ShapeDtypeStruct((t, d), F32),
        compiler_params=_cparams(("arbitrary",)),
        name="moe_combine",
    )(*args)


def moe_layer(x, g, mod, w_router, router_bias, w1, w3, w2, layer, t_ctx, l_lat, g_final=None):
    h, route, counts = moe_router(x, g, mod, 3, 4, w_router, router_bias, t_ctx, l_lat)
    plan = moe_dispatch_plan(route, counts[:, 0].astype(jnp.int32))
    take = lambda rows, index: rows.at[index].get(mode="promise_in_bounds")
    xs = take(h, plan["slot_token"])
    ys = moe_experts(xs, plan, w1, w3, w2, layer)
    y0 = take(ys, plan["slot"][0])
    y1 = take(ys, plan["slot"][1])
    return moe_combine(x, y0, y1, route[2:4].T, mod, 5, t_ctx, l_lat, g_final)


def _hy_conv_kernel(u_ref, up_ref, un_ref, cw_ref, cb_ref, x0_ref, vv_ref, *, t_ctx, l_ctx, l_lat):
    i = pl.program_id(0)
    has_prev, has_next = _seq_flags(i, TM, t_ctx, l_ctx, l_lat)
    c = _conv3(u_ref[...], up_ref[...], un_ref[...], cw_ref, has_prev, has_next) + cb_ref[...]
    d = c.shape[1] // 3
    x0_ref[...] = c[:, :d].astype(x0_ref.dtype)
    vv_ref[...] = (c[:, 2 * d:] * c[:, d:2 * d]).astype(vv_ref.dtype)


def hy_conv_gate(u, conv_w, conv_b, t_ctx, l_ctx, l_lat):
    t, c = u.shape
    d = c // 3
    cw = jnp.zeros((SUBLANES, c), F32).at[:conv_w.shape[0]].set(conv_w)
    row_spec = lambda w: pl.BlockSpec((TM, w), lambda i: (i, 0))
    return pl.pallas_call(
        functools.partial(_hy_conv_kernel, t_ctx=t_ctx, l_ctx=l_ctx, l_lat=l_lat),
        grid=(t // TM,),
        in_specs=_halo_specs(c, TM, t) + [_const_spec((SUBLANES, c)), _const_spec((1, c))],
        out_specs=[row_spec(d), row_spec(d)],
        out_shape=[jax.ShapeDtypeStruct((t, d), BF16), jax.ShapeDtypeStruct((t, d), BF16)],
        compiler_params=_cparams(("arbitrary",)),
        name="hy_conv_gate",
    )(u, u, u, cw, conv_b.reshape(1, c).astype(F32))


def _hy_features(n):
    t = np.linspace(0.0, 1.0, n)[:, None]
    bands = (HY_EMB - 1) // 2
    band_w = np.linspace(1e-4, bands - 1, bands)[None, :]
    ang = (2.0 * math.pi / n) * np.arange(n)[:, None] * band_w
    z = np.concatenate([t, np.cos(ang), -np.sin(ang)], axis=-1)
    zp = np.zeros((n, LANES))
    zp[:, :HY_EMB] = z
    return jnp.asarray(zp.astype(np.float32))


def _hy_hid_kernel(z_ref, f_ref, w1_ref, b1_ref, w2_ref, b2_ref, w3_ref, b3_ref, hid_ref):
    h = jnp.sin(f_ref[0:1, :] * (_dot(z_ref[...], w1_ref[...], HI) + b1_ref[...]))
    h = jnp.sin(f_ref[1:2, :] * (_dot(h, w2_ref[...], HI) + b2_ref[...]))
    h = jnp.sin(f_ref[2:3, :] * (_dot(h, w3_ref[...], HI) + b3_ref[...]))
    hid_ref[...] = h


def _hy_filt_kernel(hid_ref, w4_ref, z_ref, dl_ref, filt_ref):
    f = _dot(hid_ref[...], w4_ref[...], HI)
    f = f * jnp.exp(-z_ref[:, 0:1] * jnp.abs(dl_ref[...]))
    nrm = jnp.sum(jnp.abs(f), axis=0, keepdims=True)
    filt_ref[...] = f / nrm


def hy_filter(n, d, freq, w1, b1, w2, b2, w3, b3, w4):
    z = _hy_features(n)
    o = HY_ORDER
    w1p = jnp.zeros((LANES, o), F32).at[:HY_EMB].set(w1.astype(F32))
    fp = jnp.zeros((SUBLANES, o), F32).at[:3].set(freq.astype(F32))
    tr = min(n, TM)
    hid = pl.pallas_call(
        _hy_hid_kernel,
        grid=(n // tr,),
        in_specs=[pl.BlockSpec((tr, LANES), lambda i: (i, 0)), _const_spec((SUBLANES, o)),
                  _const_spec((LANES, o)), _const_spec((1, o)),
                  _const_spec((o, o)), _const_spec((1, o)),
                  _const_spec((o, o)), _const_spec((1, o))],
        out_specs=pl.BlockSpec((tr, o), lambda i: (i, 0)),
        out_shape=jax.ShapeDtypeStruct((n, o), F32),
        compiler_params=_cparams(("arbitrary",)),
        name="hy_filter_mlp",
    )(z, fp, w1p, b1.reshape(1, o).astype(F32), w2.astype(F32), b2.reshape(1, o).astype(F32),
      w3.astype(F32), b3.reshape(1, o).astype(F32))
    max_decay = math.log(HY_DECAY_TARGET) / HY_DECAY_SHORT_PCT
    min_decay = math.log(HY_DECAY_TARGET) / HY_DECAY_LONG_PCT
    deltas = np.tile(np.linspace(min_decay, max_decay, d), 2).astype(np.float32)[None, :]
    tc = 256
    return pl.pallas_call(
        _hy_filt_kernel,
        grid=(2 * d // tc,),
        in_specs=[_const_spec((n, o)), pl.BlockSpec((o, tc), lambda j: (0, j)),
                  _const_spec((n, LANES)), pl.BlockSpec((1, tc), lambda j: (0, j))],
        out_specs=pl.BlockSpec((n, tc), lambda j: (0, j)),
        out_shape=jax.ShapeDtypeStruct((n, 2 * d), F32),
        compiler_params=_cparams(("arbitrary",)),
        name="hy_filter_window",
    )(hid, w4.astype(F32), z, jnp.asarray(deltas))


def _dense_dft_tables(l):
    n = 2 * l
    k = np.arange(n)[:, None]
    j = np.arange(l)[None, :]
    th = 2.0 * math.pi * ((k * j) % n) / n
    fwd = np.concatenate([np.cos(th), -np.sin(th)], axis=0)
    inv = np.concatenate([np.cos(th).T, -np.sin(th).T], axis=1) / n
    return (jnp.asarray(fwd.astype(np.float32)).astype(BF16),
            jnp.asarray(inv.astype(np.float32)).astype(BF16))


def _dense_tf_kernel(hf_ref, hb_ref, fwd_ref, tf_ref):
    n = fwd_ref.shape[0] // 2
    hf = hf_ref[...]
    hb = hb_ref[...]
    tf_ref[0:n, :] = _dot(fwd_ref[0:n, :], (hf + hb).astype(BF16))
    tf_ref[n:, :] = _dot(fwd_ref[n:, :], (hf - hb).astype(BF16))


def _dense_conv_kernel(vv_ref, x0_ref, skip_ref, tf_ref, fwd_ref, inv_ref, out_ref):
    n = fwd_ref.shape[0] // 2
    vv = vv_ref[...]
    x = _dot(fwd_ref[...], vv.astype(BF16))
    xr, xi = x[:n], x[n:]
    tr, ti = tf_ref[0:n, :], tf_ref[n:, :]
    y = jnp.concatenate([xr * tr - xi * ti, xr * ti + xi * tr], axis=0).astype(BF16)
    conv = _dot(inv_ref[...], y)
    out = (conv + vv.astype(F32) * skip_ref[...]) * x0_ref[...].astype(F32)
    out_ref[...] = out.astype(out_ref.dtype)


def hy_longconv_dense(vv, x0, skip, filt, n_seq, l):
    d = vv.shape[1]
    n = 2 * l
    fwd, inv = _dense_dft_tables(l)
    dc = 512
    tf = pl.pallas_call(
        _dense_tf_kernel,
        grid=(d // dc,),
        in_specs=[pl.BlockSpec((l, dc), lambda j: (0, j)),
                  pl.BlockSpec((l, dc), lambda j: (0, d // dc + j)),
                  _const_spec(fwd.shape)],
        out_specs=pl.BlockSpec((2 * n, dc), lambda j: (0, j)),
        out_shape=jax.ShapeDtypeStruct((2 * n, d), F32),
        compiler_params=_cparams(("arbitrary",)),
        name="hy_tf_dense",
    )(filt, filt, fwd)
    return pl.pallas_call(
        _dense_conv_kernel,
        grid=(d // dc, n_seq),
        in_specs=[pl.BlockSpec((l, dc), lambda j, s: (s, j)),
                  pl.BlockSpec((l, dc), lambda j, s: (s, j)),
                  pl.BlockSpec((1, dc), lambda j, s: (0, j)),
                  pl.BlockSpec((2 * n, dc), lambda j, s: (0, j)),
                  _const_spec(fwd.shape), _const_spec(inv.shape)],
        out_specs=pl.BlockSpec((l, dc), lambda j, s: (s, j)),
        out_shape=jax.ShapeDtypeStruct((n_seq * l, d), BF16),
        compiler_params=_cparams(("arbitrary", "arbitrary")),
        name="hy_longconv_dense",
    )(vv, x0, skip.reshape(1, d).astype(F32), tf, fwd, inv)


def _two_stage_tables(l):
    n = 2 * l
    n2 = FFT_N2
    n1_full = n // n2
    n_slab = n1_full // 2 + 1
    n1 = -(-n_slab // FFT_UNROLL_SLAB) * FFT_UNROLL_SLAB
    p = l // n2
    a = np.arange(p)[None, None, :]
    c = np.arange(n1)[None, :, None]
    b = np.arange(n2)[:, None, None]
    live = (c < n_slab).astype(np.float64)
    weight = np.where((c == 0) | (c == n1_full // 2), 1.0, 2.0) * live
    ph = 2.0 * math.pi * (((a * c) % n1_full) / n1_full + ((b * c) % n) / n)
    f1 = np.concatenate([np.cos(ph) * live, -np.sin(ph) * live], axis=1)
    pht = np.transpose(ph, (0, 2, 1))
    wt = np.transpose(weight, (0, 2, 1)) / n
    g1 = np.concatenate([np.cos(pht) * wt, -np.sin(pht) * wt], axis=2)
    e = np.arange(n2)[:, None]
    bb = np.arange(n2)[None, :]
    th = 2.0 * math.pi * ((e * bb) % n2) / n2
    fr, fi = np.cos(th), -np.sin(th)
    f2 = np.block([[fr, -fi], [fi, fr]])
    f2c = np.block([[fr, fi], [-fi, fr]])
    cast = lambda m: jnp.asarray(m.astype(np.float32)).astype(BF16)
    return cast(f1), cast(f2), cast(f2c), cast(g1)


def _unrolled_loop(n, unroll, fn):
    def body(i, carry):
        fn([i * unroll + u for u in range(unroll)])
        return carry

    lax.fori_loop(0, n // unroll, body, 0)


FFT_UNROLL_SMALL = 16
FFT_UNROLL_SLAB = 4


def _stage1(src_ref, f1_ref, w_scr, combine=None):
    n2, two_n1, p = f1_ref.shape

    def step(bs):
        if combine is None:
            xs = [src_ref[pl.ds(b, p, stride=n2), :] for b in bs]
        else:
            xs = [combine(src_ref[0][pl.ds(b, p, stride=n2), :], src_ref[1][pl.ds(b, p, stride=n2), :])
                  for b in bs]
        res = [_dot(f1_ref[b], x.astype(BF16)) for b, x in zip(bs, xs)]
        for b, r in zip(bs, res):
            w_scr[pl.ds(pl.multiple_of(b * two_n1, two_n1), two_n1), :] = r

    _unrolled_loop(n2, FFT_UNROLL_SMALL, step)


def _load_slab(w_scr, c, n1, n2):
    re = w_scr[pl.ds(c, n2, stride=2 * n1), :]
    im = w_scr[pl.ds(n1 + c, n2, stride=2 * n1), :]
    return jnp.concatenate([re, im], axis=0).astype(BF16)


def _two_stage_tf_kernel(hf_ref, hb_ref, f1_ref, f2_ref, tfr_ref, tfi_ref, ws_scr, wd_scr):
    n2, two_n1, _ = f1_ref.shape
    n1 = two_n1 // 2
    _stage1((hf_ref, hb_ref), f1_ref, ws_scr, combine=lambda u, v: u + v)
    _stage1((hf_ref, hb_ref), f1_ref, wd_scr, combine=lambda u, v: u - v)

    def step(cs):
        xs = [_dot(f2_ref[0:n2, :], _load_slab(ws_scr, c, n1, n2)) for c in cs]
        xd = [_dot(f2_ref[n2:, :], _load_slab(wd_scr, c, n1, n2)) for c in cs]
        for c, a, b in zip(cs, xs, xd):
            r0 = pl.multiple_of(c * n2, n2)
            tfr_ref[pl.ds(r0, n2), :] = a
            tfi_ref[pl.ds(r0, n2), :] = b

    _unrolled_loop(n1, FFT_UNROLL_SLAB, step)


def _two_stage_conv_kernel(vv_ref, x0_ref, skip_ref, tfr_ref, tfi_ref,
                           f1_ref, f2_ref, f2c_ref, g1_ref, out_ref, w_scr, t_scr):
    n2, two_n1, p = f1_ref.shape
    n1 = two_n1 // 2
    t_scr[...] = vv_ref[...].astype(F32)
    _stage1(t_scr, f1_ref, w_scr)

    def step2(cs):
        xs = [_dot(f2_ref[...], _load_slab(w_scr, c, n1, n2)) for c in cs]
        ys = []
        for c, x in zip(cs, xs):
            r0 = pl.multiple_of(c * n2, n2)
            xr, xi = x[:n2], x[n2:]
            tr = tfr_ref[pl.ds(r0, n2), :]
            ti = tfi_ref[pl.ds(r0, n2), :]
            ys.append(jnp.concatenate([xr * tr - xi * ti, xr * ti + xi * tr], axis=0).astype(BF16))
        zs = [_dot(f2c_ref[...], y) for y in ys]
        for c, z in zip(cs, zs):
            w_scr[pl.ds(c, n2, stride=two_n1), :] = z[:n2]
            w_scr[pl.ds(n1 + c, n2, stride=two_n1), :] = z[n2:]

    _unrolled_loop(n1, FFT_UNROLL_SLAB, step2)

    def step3(bs):
        zb = [w_scr[pl.ds(pl.multiple_of(b * two_n1, two_n1), two_n1), :].astype(BF16) for b in bs]
        res = [_dot(g1_ref[b], z) for b, z in zip(bs, zb)]
        for b, r in zip(bs, res):
            t_scr[pl.ds(b, p, stride=n2), :] = r

    _unrolled_loop(n2, FFT_UNROLL_SMALL, step3)
    out = (t_scr[...] + vv_ref[...].astype(F32) * skip_ref[...]) * x0_ref[...].astype(F32)
    out_ref[...] = out.astype(out_ref.dtype)


def hy_longconv_two_stage(vv, x0, skip, filt, row0, n_seq, l):
    d = vv.shape[1]
    f1, f2, f2c, g1 = _two_stage_tables(l)
    n2, two_n1, p = f1.shape
    n = (two_n1 // 2) * n2
    dc = LANES
    tfr, tfi = pl.pallas_call(
        _two_stage_tf_kernel,
        grid=(d // dc,),
        in_specs=[pl.BlockSpec((l, dc), lambda j: (0, j)),
                  pl.BlockSpec((l, dc), lambda j: (0, d // dc + j)),
                  _const_spec(f1.shape), _const_spec(f2.shape)],
        out_specs=[pl.BlockSpec((n, dc), lambda j: (0, j)), pl.BlockSpec((n, dc), lambda j: (0, j))],
        out_shape=[jax.ShapeDtypeStruct((n, d), F32), jax.ShapeDtypeStruct((n, d), F32)],
        scratch_shapes=[pltpu.VMEM((n2 * two_n1, dc), F32), pltpu.VMEM((n2 * two_n1, dc), F32)],
        compiler_params=_cparams(("arbitrary",)),
        name="hy_tf_two_stage",
    )(filt, filt, f1, f2)
    base = row0 // l
    return pl.pallas_call(
        _two_stage_conv_kernel,
        grid=(d // dc, n_seq),
        in_specs=[pl.BlockSpec((l, dc), lambda j, s: (base + s, j)),
                  pl.BlockSpec((l, dc), lambda j, s: (base + s, j)),
                  pl.BlockSpec((1, dc), lambda j, s: (0, j)),
                  pl.BlockSpec((n, dc), lambda j, s: (0, j)),
                  pl.BlockSpec((n, dc), lambda j, s: (0, j)),
                  _const_spec(f1.shape), _const_spec(f2.shape), _const_spec(f2c.shape),
                  _const_spec(g1.shape)],
        out_specs=pl.BlockSpec((l, dc), lambda j, s: (s, j)),
        out_shape=jax.ShapeDtypeStruct((n_seq * l, d), BF16),
        scratch_shapes=[pltpu.VMEM((n2 * two_n1, dc), F32), pltpu.VMEM((l, dc), F32)],
        compiler_params=_cparams(("arbitrary", "arbitrary")),
        name="hy_longconv_two_stage",
    )(vv, x0, skip.reshape(1, d).astype(F32), tfr, tfi, f1, f2, f2c, g1)


def _hy_out_kernel(y_ref, w_ref, b_ref, x_ref, mod_ref, out_ref, *, gate_idx):
    y = _dot(y_ref[...].astype(BF16), w_ref[...]) + b_ref[...]
    out_ref[...] = x_ref[...] + mod_ref[0, gate_idx:gate_idx + 1, :] * y


def hy_out(yg, w_out, b_out, x, mod, gate_idx, t_ctx, l_lat):
    t, d = x.shape
    grp = functools.partial(_group_of_tile, tm=TM, t_ctx=t_ctx, l_lat=l_lat)
    row_spec = lambda w: pl.BlockSpec((TM, w), lambda i: (i, 0))
    return pl.pallas_call(
        functools.partial(_hy_out_kernel, gate_idx=gate_idx),
        grid=(t // TM,),
        in_specs=[row_spec(d), _const_spec(w_out.shape), _const_spec((1, d)), row_spec(d),
                  pl.BlockSpec((1,) + mod.shape[1:], lambda i: (grp(i), 0, 0))],
        out_specs=row_spec(d),
        out_shape=jax.ShapeDtypeStruct((t, d), F32),
        compiler_params=_cparams(("arbitrary",)),
        name="hy_out",
    )(yg, w_out, b_out.reshape(1, d).astype(F32), x, mod)


def _grid_pos_embed(rows, d):
    r = np.repeat(np.arange(rows, dtype=np.float64), GRID_W)
    col = np.tile(np.arange(GRID_W, dtype=np.float64), rows)
    quarter = d // 4
    omega = 1.0 / (10000.0 ** (np.arange(quarter, dtype=np.float64) / quarter))
    ang_r = r[:, None] * omega
    ang_c = col[:, None] * omega
    pe = np.concatenate([np.sin(ang_r), np.cos(ang_r), np.sin(ang_c), np.cos(ang_c)], axis=-1)
    return jnp.asarray(pe.astype(np.float32))


def delta_layer(x, g, mod, state_lat, w_in, conv_w, a_log, dt_bias, g_head, w_out, dims):
    t_ctx, l_ctx, n_ctx, l_lat, n_lat = dims
    hk = DN_HEADS * DN_DK
    nqkv = 3 * hk
    w_qkv = w_in[:, :nqkv].astype(BF16)
    w_z = w_in[:, nqkv:nqkv + hk].astype(BF16)
    w_ab = jnp.zeros((w_in.shape[0], LANES), BF16).at[:, :4 * DN_HEADS].set(w_in[:, nqkv + hk:].astype(BF16))
    qkv, z, ab = norm_mod_matmul(x, g, mod, 0, 1, [w_qkv, w_z, w_ab], None, [BF16, BF16, F32],
                                 t_ctx, l_lat)
    q, k, v, gc, beta, gct = dn_prep(qkv, ab, conv_w.astype(F32), a_log, dt_bias, t_ctx, l_ctx, l_lat)
    outs = []
    ctx_states = []
    seq_lens = (l_ctx,) * n_ctx + (l_lat,) * n_lat
    for direction in range(2):
        o, s_all = dn_chunk_scan(q, k, v, gc, beta, gct, state_lat.astype(F32), direction=direction,
                                 seq_lens=seq_lens, n_zero_init=n_ctx)
        outs.append(o)
        ctx_states.append(s_all[:n_ctx])
    x = dn_out(outs[0], outs[1], z, g_head, w_out.astype(BF16), x, mod, 2, t_ctx, l_lat)
    return x, jnp.stack(ctx_states, axis=1)


def hyena_layer(x, g, mod, w_in, b_in, conv_w, conv_b, freq, fw1, fb1, fw2, fb2, fw3, fb3, fw4,
                skip, w_out, b_out, dims):
    t_ctx, l_ctx, n_ctx, l_lat, n_lat = dims
    d = x.shape[1]
    (u,) = norm_mod_matmul(x, g, mod, 0, 1, [w_in.astype(BF16)], [b_in.astype(F32)], [BF16],
                           t_ctx, l_lat)
    x0, vv = hy_conv_gate(u, conv_w.astype(F32), conv_b, t_ctx, l_ctx, l_lat)
    filt_c = hy_filter(l_ctx, d, freq, fw1, fb1, fw2, fb2, fw3, fb3, fw4)
    filt_l = hy_filter(l_lat, d, freq, fw1, fb1, fw2, fb2, fw3, fb3, fw4)
    y_c = hy_longconv_dense(vv, x0, skip, filt_c, n_ctx, l_ctx)
    y_l = hy_longconv_two_stage(vv, x0, skip, filt_l, t_ctx, n_lat, l_lat)
    yg = jnp.concatenate([y_c, y_l], axis=0)
    return hy_out(yg, w_out.astype(BF16), b_out, x, mod, 2, t_ctx, l_lat)


def kernel(x_prompt, x_sample, state_delta, c, c_ctx, w_ada, b_ada, g_norm, dn_w_in, dn_conv, dn_a_log, dn_dt_bias, dn_g_head, dn_w_out, hy_w_in, hy_b_in, hy_conv, hy_conv_b, hy_freq, hy_f_w1, hy_f_b1, hy_f_w2, hy_f_b2, hy_f_w3, hy_f_b3, hy_f_w4, hy_skip, hy_w_out, hy_b_out, w_router, router_bias, moe_w1, moe_w3, moe_w2, g_final):
    n_ctx, l_ctx, d = x_prompt.shape
    n_lat, l_lat, _ = x_sample.shape
    depth = w_ada.shape[0]
    t_ctx = n_ctx * l_ctx
    dims = (t_ctx, l_ctx, n_ctx, l_lat, n_lat)
    assert l_ctx % TM == 0 and l_lat % TM == 0 and t_ctx % l_lat == 0
    assert n_lat + 1 <= SUBLANES

    pos = _grid_pos_embed(l_lat // GRID_W, d)
    x = jnp.concatenate([x_prompt.reshape(t_ctx, d).astype(F32),
                         (x_sample.astype(F32) + pos[None]).reshape(n_lat * l_lat, d)], axis=0)
    cond = jnp.zeros((SUBLANES, d), F32).at[0].set(c_ctx.astype(F32)).at[1:1 + n_lat].set(c.astype(F32))
    mod_all = ada_modulation(cond, w_ada.astype(F32), b_ada.astype(F32))
    mod_all = mod_all.reshape(depth, SUBLANES, 6, d)

    ctx_states = []
    for i in range(depth):
        mod = mod_all[i]
        j = i // 2
        if i % 2 == 0:
            x, s_ctx = delta_layer(x, g_norm[i, 0], mod, state_delta[:, j], dn_w_in[j], dn_conv[j],
                                   dn_a_log[j], dn_dt_bias[j], dn_g_head[j], dn_w_out[j], dims)
            ctx_states.append(s_ctx.astype(x_prompt.dtype))
        else:
            x = hyena_layer(x, g_norm[i, 0], mod, hy_w_in[j], hy_b_in[j], hy_conv[j], hy_conv_b[j],
                            hy_freq[j], hy_f_w1[j], hy_f_b1[j], hy_f_w2[j], hy_f_b2[j], hy_f_w3[j],
                            hy_f_b3[j], hy_f_w4[j], hy_skip[j], hy_w_out[j], hy_b_out[j], dims)
        x = moe_layer(x, g_norm[i, 1], mod, w_router, router_bias, moe_w1, moe_w3, moe_w2, i,
                      t_ctx, l_lat, g_final if i == depth - 1 else None)
    y_prompt = x[:t_ctx].reshape(n_ctx, l_ctx, d).astype(x_prompt.dtype)
    y_sample = x[t_ctx:].reshape(n_lat, l_lat, d).astype(x_sample.dtype)
    new_state = jnp.stack(ctx_states, axis=1)
    return (y_prompt, y_sample, new_state)
```

```python
import functools
import math

import numpy as np
import jax
import jax.numpy as jnp
from jax import lax
from jax.experimental import pallas as pl
from jax.experimental.pallas import tpu as pltpu

F32 = jnp.float32
BF16 = jnp.bfloat16
HI = lax.Precision.HIGHEST

EPS = 1e-6
GRID_W = 64
DN_HEADS = 8
DN_DK = 128
DN_DV = 128
DN_CHUNK = 64
HY_EMB = 33
HY_ORDER = 64
HY_DECAY_SHORT_PCT = 0.3
HY_DECAY_LONG_PCT = 1.5
HY_DECAY_TARGET = 1e-2
N_EXPERTS = 32
N_GROUPS = 4
EXPERTS_PER_GROUP = N_EXPERTS // N_GROUPS

LANES = 128
SUBLANES = 8
TM = 256
HALO_ROWS = 16
MOE_TILE = 256
FFT_N2 = 128
DN_PREP_CHUNKS = 4
VMEM_LIMIT = 48 * 1024 * 1024


def _cparams(sem):
    return pltpu.CompilerParams(dimension_semantics=sem, vmem_limit_bytes=VMEM_LIMIT)


def _sigmoid(x):
    return 1.0 / (1.0 + jnp.exp(-x))


def _silu(x):
    return x * _sigmoid(x)


def _dot(a, b, precision=None):
    return jnp.dot(a, b, preferred_element_type=F32, precision=precision)


def _const_spec(shape):
    nd = len(shape)
    return pl.BlockSpec(shape, lambda *_: (0,) * nd)


def _ada_kernel(cond_ref, w_ref, b_ref, out_ref):
    s = _silu(cond_ref[...])
    out_ref[0] = _dot(s, w_ref[0], HI) + b_ref[0]


def ada_modulation(cond, w_ada, b_ada):
    depth, d, n = w_ada.shape
    tn = 1536
    return pl.pallas_call(
        _ada_kernel,
        grid=(depth, n // tn),
        in_specs=[
            pl.BlockSpec((SUBLANES, d), lambda i, j: (0, 0)),
            pl.BlockSpec((1, d, tn), lambda i, j: (i, 0, j)),
            pl.BlockSpec((1, 1, tn), lambda i, j: (i, 0, j)),
        ],
        out_specs=pl.BlockSpec((1, SUBLANES, tn), lambda i, j: (i, 0, j)),
        out_shape=jax.ShapeDtypeStruct((depth, SUBLANES, n), F32),
        compiler_params=_cparams(("arbitrary", "arbitrary")),
        name="ada_modulation",
    )(cond, w_ada, b_ada.reshape(depth, 1, n))


def _norm_mod(x, g, mod_ref, shift_idx, scale_idx):
    ms = jnp.mean(x * x, axis=-1, keepdims=True)
    y = x * lax.rsqrt(ms + EPS) * g
    scale = mod_ref[0, scale_idx:scale_idx + 1, :]
    shift = mod_ref[0, shift_idx:shift_idx + 1, :]
    return y * (1.0 + scale) + shift


def _group_of_tile(i, tm, t_ctx, l_lat):
    r = i * tm
    return jnp.where(r < t_ctx, 0, 1 + (r - t_ctx) // l_lat)


def _nmm_kernel(x_ref, g_ref, mod_ref, *refs, n_out, has_bias, shift_idx, scale_idx):
    h = _norm_mod(x_ref[...], g_ref[...], mod_ref, shift_idx, scale_idx).astype(BF16)
    w_refs = refs[:n_out]
    b_refs = refs[n_out:n_out + (n_out if has_bias else 0)]
    o_refs = refs[n_out + len(b_refs):]
    for k in range(n_out):
        acc = _dot(h, w_refs[k][...])
        if has_bias:
            acc = acc + b_refs[k][...]
        o_refs[k][...] = acc.astype(o_refs[k].dtype)


def norm_mod_matmul(x, g, mod, shift_idx, scale_idx, weights, biases, out_dtypes, t_ctx, l_lat):
    t, d = x.shape
    n_out = len(weights)
    has_bias = biases is not None
    grp = functools.partial(_group_of_tile, tm=TM, t_ctx=t_ctx, l_lat=l_lat)
    in_specs = [
        pl.BlockSpec((TM, d), lambda i: (i, 0)),
        _const_spec((1, d)),
        pl.BlockSpec((1,) + mod.shape[1:], lambda i: (grp(i), 0, 0)),
    ]
    args = [x, g.reshape(1, d), mod]
    for w in weights:
        in_specs.append(_const_spec(w.shape))
        args.append(w)
    if has_bias:
        for b in biases:
            in_specs.append(_const_spec((1, b.shape[-1])))
            args.append(b.reshape(1, -1))
    out_specs = [pl.BlockSpec((TM, w.shape[1]), lambda i: (i, 0)) for w in weights]
    out_shape = [jax.ShapeDtypeStruct((t, w.shape[1]), dt) for w, dt in zip(weights, out_dtypes)]
    return pl.pallas_call(
        functools.partial(_nmm_kernel, n_out=n_out, has_bias=has_bias,
                          shift_idx=shift_idx, scale_idx=scale_idx),
        grid=(t // TM,),
        in_specs=in_specs,
        out_specs=out_specs,
        out_shape=out_shape,
        compiler_params=_cparams(("arbitrary",)),
        name="norm_mod_matmul",
    )(*args)


def _seq_flags(i, tm, t_ctx, l_ctx, l_lat):
    r = i * tm
    in_ctx = r < t_ctx
    pos = jnp.where(in_ctx, r % l_ctx, (r - t_ctx) % l_lat)
    length = jnp.where(in_ctx, l_ctx, l_lat)
    return pos != 0, (pos + tm) != length


def _conv3(x, prev_rows, next_rows, w_ref, has_prev, has_next):
    tm = x.shape[0]
    x = x.astype(F32)
    halo_p = jnp.where(has_prev, prev_rows[HALO_ROWS - 1:HALO_ROWS, :].astype(F32), 0.0)
    halo_n = jnp.where(has_next, next_rows[0:1, :].astype(F32), 0.0)
    w0, w1, w2 = w_ref[0:1, :], w_ref[1:2, :], w_ref[2:3, :]
    c = w0 * pltpu.roll(x, 1, 0) + w1 * x + w2 * pltpu.roll(x, tm - 1, 0)
    row = lax.broadcasted_iota(jnp.int32, (SUBLANES, 1), 0)
    top = c[:SUBLANES] + jnp.where(row == 0, w0 * (halo_p - x[tm - 1:tm]), 0.0)
    bot = c[tm - SUBLANES:] + jnp.where(row == SUBLANES - 1, w2 * (halo_n - x[0:1]), 0.0)
    return jnp.concatenate([top, c[SUBLANES:tm - SUBLANES], bot], axis=0)


def _halo_specs(c, tm, t):
    nblk = t // HALO_ROWS
    per = tm // HALO_ROWS
    return [
        pl.BlockSpec((tm, c), lambda i: (i, 0)),
        pl.BlockSpec((HALO_ROWS, c), lambda i: (jnp.maximum(i * per - 1, 0), 0)),
        pl.BlockSpec((HALO_ROWS, c), lambda i: (jnp.minimum((i + 1) * per, nblk - 1), 0)),
    ]


def _softplus(x):
    return jnp.maximum(x, 0.0) + jnp.log(1.0 + jnp.exp(-jnp.abs(x)))


def _dn_prep_kernel(x_ref, xp_ref, xn_ref, cw_ref, ab_ref, nega_ref, dtb_ref, tri_ref,
                    q_ref, k_ref, v_ref, gc_ref, beta_ref, gct_ref,
                    *, t_ctx, l_ctx, l_lat):
    i = pl.program_id(0)
    has_prev, has_next = _seq_flags(i, TM, t_ctx, l_ctx, l_lat)
    c = _conv3(x_ref[...], xp_ref[...], xn_ref[...], cw_ref, has_prev, has_next)
    s = _silu(c)
    hk = DN_HEADS * DN_DK
    for h in range(DN_HEADS):
        lo, hi = h * DN_DK, (h + 1) * DN_DK
        qh = s[:, lo:hi]
        kh = s[:, hk + lo:hk + hi]
        qn = lax.rsqrt(jnp.sum(qh * qh, axis=-1, keepdims=True) + EPS)
        kn = lax.rsqrt(jnp.sum(kh * kh, axis=-1, keepdims=True) + EPS)
        q_ref[:, lo:hi] = (qh * (qn * (DN_DK ** -0.5))).astype(BF16)
        k_ref[:, lo:hi] = (kh * kn).astype(BF16)
    v_ref[...] = s[:, 2 * hk:].astype(BF16)
    ab = ab_ref[...]
    lg = nega_ref[...] * _softplus(ab + dtb_ref[...])
    beta_ref[...] = _sigmoid(ab)
    nch = TM // DN_CHUNK
    for d in range(2):
        gc = _dot(tri_ref[d], lg, HI)
        gc_ref[d] = gc
        gct = gc.T
        for ch in range(nch):
            gct_ref[d, ch] = gct[0:2 * DN_HEADS, ch * DN_CHUNK:(ch + 1) * DN_CHUNK]


def _chunk_tri():
    idx = np.arange(TM)
    same = (idx[:, None] // DN_CHUNK) == (idx[None, :] // DN_CHUNK)
    fwd = same & (idx[None, :] <= idx[:, None])
    bwd = same & (idx[None, :] >= idx[:, None])
    return jnp.asarray(np.stack([fwd, bwd]).astype(np.float32))


def dn_prep(qkv, ab, conv_w, a_log, dt_bias, t_ctx, l_ctx, l_lat):
    t, c = qkv.shape
    hd = DN_HEADS * DN_DK
    nega = jnp.zeros((1, LANES), F32).at[0, :2 * DN_HEADS].set(-jnp.exp(a_log.astype(F32)).reshape(-1))
    dtb = jnp.zeros((1, LANES), F32).at[0, :2 * DN_HEADS].set(dt_bias.astype(F32).reshape(-1))
    cw = jnp.zeros((SUBLANES, c), F32).at[:conv_w.shape[0]].set(conv_w)
    nch = TM // DN_CHUNK
    row_spec = lambda w: pl.BlockSpec((TM, w), lambda i: (i, 0))
    return pl.pallas_call(
        functools.partial(_dn_prep_kernel, t_ctx=t_ctx, l_ctx=l_ctx, l_lat=l_lat),
        grid=(t // TM,),
        in_specs=_halo_specs(c, TM, t) + [
            _const_spec((SUBLANES, c)),
            row_spec(LANES),
            _const_spec((1, LANES)),
            _const_spec((1, LANES)),
            _const_spec((2, TM, TM)),
        ],
        out_specs=[
            row_spec(hd), row_spec(hd), row_spec(hd),
            pl.BlockSpec((2, TM, LANES), lambda i: (0, i, 0)),
            row_spec(LANES),
            pl.BlockSpec((2, nch, 2 * DN_HEADS, DN_CHUNK), lambda i: (0, i, 0, 0)),
        ],
        out_shape=[
            jax.ShapeDtypeStruct((t, hd), BF16),
            jax.ShapeDtypeStruct((t, hd), BF16),
            jax.ShapeDtypeStruct((t, hd), BF16),
            jax.ShapeDtypeStruct((2, t, LANES), F32),
            jax.ShapeDtypeStruct((t, LANES), F32),
            jax.ShapeDtypeStruct((2, t // DN_CHUNK, 2 * DN_HEADS, DN_CHUNK), F32),
        ],
        compiler_params=_cparams(("arbitrary",)),
        name="dn_prep",
    )(qkv, qkv, qkv, cw, ab, nega, dtb, _chunk_tri())


def _bmm(a, b):
    return _dot(a.astype(BF16), b.astype(BF16))


def _inv_unit_tri_batch(lms, in_blk, eye):
    dgs = [jnp.where(in_blk, lm, 0.0) for lm in lms]
    offs = [lm - dg for lm, dg in zip(lms, dgs)]
    n1 = [-dg for dg in dgs]
    n2 = [_bmm(a, a) for a in n1]
    p = [eye + a for a in n1]
    n4 = [_bmm(a, a) for a in n2]
    p = [x + _bmm(x, a) for x, a in zip(p, n2)]
    n8 = [_bmm(a, a) for a in n4]
    p = [x + _bmm(x, a) for x, a in zip(p, n4)]
    p = [x + _bmm(x, a) for x, a in zip(p, n8)]
    m = [_bmm(x, o) for x, o in zip(p, offs)]
    m2 = [_bmm(a, a) for a in m]
    t1 = [eye - a for a in m]
    t1 = [x + _bmm(x, a) for x, a in zip(t1, m2)]
    return [_bmm(x, y) for x, y in zip(t1, p)]


def _dn_chunk_kernel(blk_ref, seq_ref, first_ref, last_ref,
                     q_ref, k_ref, v_ref, gc_ref, beta_ref, gct_ref, s0_ref,
                     o_ref, sfin_ref, s_scr, attn_scr, u_scr, qw_scr, kd_scr, gl_scr,
                     *, direction, n_zero_init):
    step = pl.program_id(0)
    c_sz = DN_CHUNK
    nch = TM // c_sz
    heads = range(DN_HEADS)

    @pl.when(first_ref[step] == 1)
    def _():
        s_scr[...] = jnp.where(seq_ref[step] >= n_zero_init, s0_ref[0, 0], 0.0)

    row = lax.broadcasted_iota(jnp.int32, (c_sz, c_sz), 0)
    col = lax.broadcasted_iota(jnp.int32, (c_sz, c_sz), 1)
    if direction == 0:
        incl, strict = row >= col, row > col
        last = c_sz - 1
    else:
        incl, strict = row <= col, row < col
        last = 0
    in_blk = (row // 16) == (col // 16)
    eye = (row == col).astype(F32)

    def lanes(h):
        return slice(h * DN_DK, (h + 1) * DN_DK)

    def prep_body(it, carry):
        units = [(it * DN_PREP_CHUNKS + c, h) for c in range(DN_PREP_CHUNKS) for h in heads]
        idx = range(len(units))
        rows = [pl.ds(pl.multiple_of(ch * c_sz, c_sz), c_sz) for ch, _ in units]
        gls = [direction * DN_HEADS + h for _, h in units]
        qs = [q_ref[rows[i], lanes(units[i][1])].astype(F32) for i in idx]
        ks = [k_ref[rows[i], lanes(units[i][1])].astype(F32) for i in idx]
        gcc = [gc_ref[0, rows[i], gls[i]:gls[i] + 1] for i in idx]
        beta = [beta_ref[rows[i], 2 * DN_HEADS + gls[i]:2 * DN_HEADS + gls[i] + 1] for i in idx]
        gcr = [gct_ref[0, units[i][0]][gls[i]:gls[i] + 1, :] for i in idx]
        dec = [jnp.exp(jnp.where(incl, gcc[i] - gcr[i], -jnp.inf)) for i in idx]
        kb = [ks[i] * beta[i] for i in idx]
        prod = [lax.dot_general(jnp.concatenate([qs[i], kb[i]], axis=0).astype(BF16), ks[i].astype(BF16),
                                (((1,), (1,)), ((), ())), preferred_element_type=F32)
                for i in idx]
        for i, (ch, h) in enumerate(units):
            attn_scr[ch, h] = (prod[i][:c_sz] * dec[i]).astype(BF16)
        lms = [jnp.where(strict, prod[i][c_sz:] * dec[i], 0.0) for i in idx]
        tinv = _inv_unit_tri_batch(lms, in_blk, eye)
        eg = [jnp.exp(gcc[i]) for i in idx]
        uw = [_bmm(tinv[i], jnp.concatenate(
                  [v_ref[rows[i], lanes(units[i][1])].astype(F32) * beta[i], kb[i] * eg[i]], axis=1))
              for i in idx]
        for i, (ch, h) in enumerate(units):
            u_scr[rows[i], lanes(h)] = uw[i][:, :DN_DV]
            qw_scr[ch, h] = jnp.concatenate([qs[i] * eg[i], uw[i][:, DN_DV:]], axis=0).astype(BF16)
            g_last = gcc[i][last:last + 1, :]
            kd_scr[ch, h] = (ks[i] * jnp.exp(g_last - gcc[i])).T.astype(BF16)
            gl_scr[ch, h] = jnp.broadcast_to(jnp.exp(g_last), (1, DN_DV))
        return carry

    lax.fori_loop(0, nch // DN_PREP_CHUNKS, prep_body, 0)

    def scan_body(ci, carry):
        ch = ci if direction == 0 else nch - 1 - ci
        r0 = pl.multiple_of(ch * c_sz, c_sz)
        rows = pl.ds(r0, c_sz)
        s = [s_scr[h] for h in heads]
        qs_ws = [_dot(qw_scr[ch, h], s[h].astype(BF16)) for h in heads]
        vb = [(u_scr[rows, lanes(h)] - qs_ws[h][c_sz:]).astype(BF16) for h in heads]
        for h in heads:
            o_ref[rows, lanes(h)] = (qs_ws[h][:c_sz] + _dot(attn_scr[ch, h], vb[h])).astype(o_ref.dtype)
            s_scr[h] = s[h] * gl_scr[ch, h] + _dot(kd_scr[ch, h], vb[h])
        return carry

    lax.fori_loop(0, nch, scan_body, 0)

    @pl.when(last_ref[step] == 1)
    def _():
        sfin_ref[0, 0] = s_scr[...]


def _scan_tables(direction, seq_lens):
    blk, seq, first, last = [], [], [], []
    base = 0
    for s, length in enumerate(seq_lens):
        nblk = length // TM
        order = range(nblk) if direction == 0 else range(nblk - 1, -1, -1)
        for pos, jj in enumerate(order):
            blk.append(base + jj)
            seq.append(s)
            first.append(int(pos == 0))
            last.append(int(pos == nblk - 1))
        base += nblk
    return [jnp.asarray(np.asarray(a, np.int32)) for a in (blk, seq, first, last)]


def dn_chunk_scan(q, k, v, gc, beta, gct, s0, *, direction, seq_lens, n_zero_init):
    t, hd = q.shape
    nch = TM // DN_CHUNK
    n_seq = len(seq_lens)
    tables = _scan_tables(direction, seq_lens)
    n_steps = int(tables[0].shape[0])
    row_spec = lambda w: pl.BlockSpec((TM, w), lambda i, blk, seq, fst, lst: (blk[i], 0))
    state_blk = (1, 1, DN_HEADS, DN_DK, DN_DV)
    grid_spec = pltpu.PrefetchScalarGridSpec(
        num_scalar_prefetch=4,
        grid=(n_steps,),
        in_specs=[
            row_spec(hd), row_spec(hd), row_spec(hd),
            pl.BlockSpec((1, TM, LANES), lambda i, blk, seq, fst, lst: (direction, blk[i], 0)),
            row_spec(LANES),
            pl.BlockSpec((1, nch, 2 * DN_HEADS, DN_CHUNK),
                         lambda i, blk, seq, fst, lst: (direction, blk[i], 0, 0)),
            pl.BlockSpec(state_blk, lambda i, blk, seq, fst, lst:
                         (jnp.maximum(seq[i] - n_zero_init, 0), direction, 0, 0, 0)),
        ],
        out_specs=[
            row_spec(hd),
            pl.BlockSpec(state_blk, lambda i, blk, seq, fst, lst: (seq[i], 0, 0, 0, 0)),
        ],
        scratch_shapes=[
            pltpu.VMEM((DN_HEADS, DN_DK, DN_DV), F32),
            pltpu.VMEM((nch, DN_HEADS, DN_CHUNK, DN_CHUNK), BF16),
            pltpu.VMEM((TM, hd), F32),
            pltpu.VMEM((nch, DN_HEADS, 2 * DN_CHUNK, DN_DK), BF16),
            pltpu.VMEM((nch, DN_HEADS, DN_DK, DN_CHUNK), BF16),
            pltpu.VMEM((nch, DN_HEADS, 1, DN_DV), F32),
        ],
    )
    o, sfin = pl.pallas_call(
        functools.partial(_dn_chunk_kernel, direction=direction, n_zero_init=n_zero_init),
        grid_spec=grid_spec,
        out_shape=[
            jax.ShapeDtypeStruct((t, hd), BF16),
            jax.ShapeDtypeStruct((n_seq,) + state_blk[1:], F32),
        ],
        compiler_params=_cparams(("arbitrary",)),
        name="dn_chunk_scan_d%d" % direction,
    )(*tables, q, k, v, gc, beta, gct, s0)
    return o, sfin[:, 0]


def _dn_out_kernel(of_ref, ob_ref, z_ref, gh_ref, w_ref, x_ref, mod_ref, out_ref, *, gate_idx):
    o = of_ref[...].astype(F32) + ob_ref[...].astype(F32)
    z = z_ref[...].astype(F32)
    parts = []
    for h in range(DN_HEADS):
        lanes = slice(h * DN_DV, (h + 1) * DN_DV)
        oh = o[:, lanes]
        ms = jnp.mean(oh * oh, axis=-1, keepdims=True)
        parts.append(oh * lax.rsqrt(ms + EPS) * gh_ref[...])
    on = jnp.concatenate(parts, axis=1) * _silu(z)
    y = _dot(on.astype(BF16), w_ref[...])
    out_ref[...] = x_ref[...] + mod_ref[0, gate_idx:gate_idx + 1, :] * y


def dn_out(o_f, o_b, z, g_head, w_out, x, mod, gate_idx, t_ctx, l_lat):
    t, d = x.shape
    hd = o_f.shape[1]
    grp = functools.partial(_group_of_tile, tm=TM, t_ctx=t_ctx, l_lat=l_lat)
    row_spec = lambda w: pl.BlockSpec((TM, w), lambda i: (i, 0))
    return pl.pallas_call(
        functools.partial(_dn_out_kernel, gate_idx=gate_idx),
        grid=(t // TM,),
        in_specs=[
            row_spec(hd), row_spec(hd), row_spec(hd),
            _const_spec((1, DN_DV)),
            _const_spec(w_out.shape),
            row_spec(d),
            pl.BlockSpec((1,) + mod.shape[1:], lambda i: (grp(i), 0, 0)),
        ],
        out_specs=row_spec(d),
        out_shape=jax.ShapeDtypeStruct((t, d), F32),
        compiler_params=_cparams(("arbitrary",)),
        name="dn_out",
    )(o_f, o_b, z, g_head.reshape(1, DN_DV).astype(F32), w_out, x, mod)


ROUTE_ROWS = 8


def _first_argmax_rows(vals, row):
    m = jnp.max(vals, axis=0, keepdims=True)
    idx = jnp.min(jnp.where(vals == m, row, float(SUBLANES)), axis=0, keepdims=True)
    return m, idx


def _router_kernel(x_ref, g_ref, mod_ref, wh_ref, wl_ref, rb_ref, tri_ref,
                   h_ref, route_ref, count_ref, run_scr, *, shift_idx, scale_idx):
    i = pl.program_id(0)

    @pl.when(i == 0)
    def _():
        run_scr[...] = jnp.zeros_like(run_scr)

    h = _norm_mod(x_ref[...], g_ref[...], mod_ref, shift_idx, scale_idx)
    hb = h.astype(BF16)
    h_ref[...] = hb
    hl = (h - hb.astype(F32)).astype(BF16)
    nt = (((1,), (1,)), ((), ()))
    wh = wh_ref[...]
    logits = (lax.dot_general(wh, hb, nt, preferred_element_type=F32)
              + (lax.dot_general(wh, hl, nt, preferred_element_type=F32)
                 + lax.dot_general(wl_ref[...], hb, nt, preferred_element_type=F32)))
    scores = _sigmoid(logits)
    sel = scores + rb_ref[:, 0:1]
    tm = scores.shape[1]
    row8 = lax.broadcasted_iota(jnp.int32, (SUBLANES, tm), 0).astype(F32)
    best_score = None
    for g in range(N_GROUPS):
        s_g = sel[g * EXPERTS_PER_GROUP:(g + 1) * EXPERTS_PER_GROUP, :]
        m1, i1 = _first_argmax_rows(s_g, row8)
        m2, i2 = _first_argmax_rows(jnp.where(row8 == i1, -jnp.inf, s_g), row8)
        gs = m1 + m2
        e1, e2 = i1 + float(g * EXPERTS_PER_GROUP), i2 + float(g * EXPERTS_PER_GROUP)
        if best_score is None:
            best_score, b1, b2 = gs, e1, e2
        else:
            better = gs > best_score
            best_score = jnp.where(better, gs, best_score)
            b1 = jnp.where(better, e1, b1)
            b2 = jnp.where(better, e2, b2)
    row_e = lax.broadcasted_iota(jnp.int32, (N_EXPERTS, tm), 0).astype(F32)
    pick1, pick2 = row_e == b1, row_e == b2
    w1 = jnp.sum(jnp.where(pick1, scores, 0.0), axis=0, keepdims=True)
    w2 = jnp.sum(jnp.where(pick2, scores, 0.0), axis=0, keepdims=True)
    tot = w1 + w2
    chosen = jnp.where(pick1 | pick2, 1.0, 0.0)
    before = run_scr[:, 0:1] + _dot(chosen.astype(BF16), tri_ref[...])
    r1 = jnp.sum(jnp.where(pick1, before, 0.0), axis=0, keepdims=True)
    r2 = jnp.sum(jnp.where(pick2, before, 0.0), axis=0, keepdims=True)
    run_scr[...] = run_scr[...] + jnp.sum(chosen, axis=1, keepdims=True)
    count_ref[...] = run_scr[...]
    zero = jnp.zeros_like(w1)
    route_ref[...] = jnp.concatenate([b1, b2, w1 / tot, w2 / tot, r1, r2, zero, zero], axis=0)


def moe_router(x, g, mod, shift_idx, scale_idx, w_router, router_bias, t_ctx, l_lat):
    t, d = x.shape
    grp = functools.partial(_group_of_tile, tm=TM, t_ctx=t_ctx, l_lat=l_lat)
    wr = w_router.astype(F32).T
    wh = wr.astype(BF16)
    wl = (wr - wh.astype(F32)).astype(BF16)
    rb = jnp.broadcast_to(router_bias.astype(F32)[:, None], (N_EXPERTS, LANES))
    idx = np.arange(TM)
    tri = jnp.asarray((idx[:, None] < idx[None, :]).astype(np.float32)).astype(BF16)
    return pl.pallas_call(
        functools.partial(_router_kernel, shift_idx=shift_idx, scale_idx=scale_idx),
        grid=(t // TM,),
        in_specs=[
            pl.BlockSpec((TM, d), lambda i: (i, 0)),
            _const_spec((1, d)),
            pl.BlockSpec((1,) + mod.shape[1:], lambda i: (grp(i), 0, 0)),
            _const_spec((N_EXPERTS, d)),
            _const_spec((N_EXPERTS, d)),
            _const_spec((N_EXPERTS, LANES)),
            _const_spec((TM, TM)),
        ],
        out_specs=[pl.BlockSpec((TM, d), lambda i: (i, 0)),
                   pl.BlockSpec((ROUTE_ROWS, TM), lambda i: (0, i)),
                   _const_spec((N_EXPERTS, LANES))],
        out_shape=[
            jax.ShapeDtypeStruct((t, d), BF16),
            jax.ShapeDtypeStruct((ROUTE_ROWS, t), F32),
            jax.ShapeDtypeStruct((N_EXPERTS, LANES), F32),
        ],
        scratch_shapes=[pltpu.VMEM((N_EXPERTS, LANES), F32)],
        compiler_params=_cparams(("arbitrary",)),
        name="moe_router",
    )(x, g.reshape(1, d), mod, wh, wl, rb, tri)


def _expert_kernel(te_ref, nt_ref, first_ref, nxt_ref, par_ref,
                   xs_ref, w1_hbm, w3_hbm, w2_hbm, ys_ref,
                   w1_buf, w3_buf, w2_buf, w1_bf, w3_bf, w2_bf, sem, *, layer):
    i = pl.program_id(0)
    active = i < nt_ref[0]

    def weight_copies(e, slot):
        return (pltpu.make_async_copy(w1_hbm.at[layer, e], w1_buf.at[slot], sem.at[slot, 0]),
                pltpu.make_async_copy(w3_hbm.at[layer, e], w3_buf.at[slot], sem.at[slot, 1]),
                pltpu.make_async_copy(w2_hbm.at[layer, e], w2_buf.at[slot], sem.at[slot, 2]))

    @pl.when(active & (i == 0))
    def _():
        for cp in weight_copies(te_ref[0], 0):
            cp.start()

    @pl.when(active & (first_ref[i] == 1))
    def _():
        slot = par_ref[i]
        for cp in weight_copies(te_ref[i], slot):
            cp.wait()

        @pl.when(nxt_ref[i] >= 0)
        def _():
            for cp in weight_copies(nxt_ref[i], 1 - slot):
                cp.start(priority=1)

        w1_bf[...] = w1_buf[slot].astype(BF16)
        w3_bf[...] = w3_buf[slot].astype(BF16)
        w2_bf[...] = w2_buf[slot].astype(BF16)

    @pl.when(active)
    def _():
        xs = xs_ref[...]
        a = _dot(xs, w1_bf[...])
        b = _dot(xs, w3_bf[...])
        hid = (_silu(a) * b).astype(BF16)
        ys_ref[...] = _dot(hid, w2_bf[...]).astype(ys_ref.dtype)

    @pl.when(jnp.logical_not(active))
    def _():
        ys_ref[...] = jnp.zeros_like(ys_ref)


def moe_experts(xs, plan, w1, w3, w2, layer):
    p, d = xs.shape
    de = w1.shape[-1]
    ntile = p // MOE_TILE
    tile_map = lambda i, *_: (i, 0)
    grid_spec = pltpu.PrefetchScalarGridSpec(
        num_scalar_prefetch=5,
        grid=(ntile,),
        in_specs=[
            pl.BlockSpec((MOE_TILE, d), tile_map),
            pl.BlockSpec(memory_space=pl.ANY),
            pl.BlockSpec(memory_space=pl.ANY),
            pl.BlockSpec(memory_space=pl.ANY),
        ],
        out_specs=pl.BlockSpec((MOE_TILE, d), tile_map),
        scratch_shapes=[
            pltpu.VMEM((2, d, de), F32), pltpu.VMEM((2, d, de), F32), pltpu.VMEM((2, de, d), F32),
            pltpu.VMEM((d, de), BF16), pltpu.VMEM((d, de), BF16), pltpu.VMEM((de, d), BF16),
            pltpu.SemaphoreType.DMA((2, 3)),
        ],
    )
    return pl.pallas_call(
        functools.partial(_expert_kernel, layer=layer),
        grid_spec=grid_spec,
        out_shape=jax.ShapeDtypeStruct((p, d), BF16),
        compiler_params=_cparams(("arbitrary",)),
        name="moe_experts",
    )(plan["tile_expert"], plan["n_tiles"], plan["first"], plan["next_expert"], plan["parity"],
      xs, w1, w3, w2)


def moe_dispatch_plan(route, counts):
    t = route.shape[1]
    p = 2 * t + N_EXPERTS * MOE_TILE
    p = -(-p // MOE_TILE) * MOE_TILE
    ntile = p // MOE_TILE
    experts = jnp.arange(N_EXPERTS, dtype=jnp.int32)
    expert = route[0:2].astype(jnp.int32)
    rank = route[4:6].astype(jnp.int32)
    padded = ((counts + MOE_TILE - 1) // MOE_TILE) * MOE_TILE
    ends = jnp.cumsum(padded)
    starts = ends - padded
    start_of = jnp.sum(jnp.where(expert[..., None] == experts, starts, 0), axis=-1)
    slot = start_of + rank
    token = jnp.tile(jnp.arange(t, dtype=jnp.int32), 2)
    slot_token = jnp.zeros((p,), jnp.int32).at[slot.reshape(-1)].set(
        token, unique_indices=True, mode="promise_in_bounds")
    tile_id = jnp.arange(ntile, dtype=jnp.int32)
    tile_expert = jnp.sum((tile_id[:, None] * MOE_TILE >= ends[None, :]).astype(jnp.int32), axis=1)
    n_tiles = (ends[-1] // MOE_TILE).astype(jnp.int32)
    used = tile_id < n_tiles
    tile_expert = jnp.where(used, tile_expert, tile_expert[jnp.maximum(n_tiles - 1, 0)])
    tile_expert = jnp.minimum(tile_expert, N_EXPERTS - 1).astype(jnp.int32)
    prev = jnp.concatenate([jnp.full((1,), -1, jnp.int32), tile_expert[:-1]])
    first = (used & (tile_expert != prev)).astype(jnp.int32)
    parity = ((jnp.cumsum(first) - 1) % 2).astype(jnp.int32)
    cand = jnp.where(counts > 0, experts, N_EXPERTS)
    later = lax.cummin(cand[::-1])[::-1]
    nxt_e = jnp.concatenate([later[1:], jnp.full((1,), N_EXPERTS, jnp.int32)])
    nxt_e = jnp.where(nxt_e >= N_EXPERTS, -1, nxt_e).astype(jnp.int32)
    next_expert = nxt_e[tile_expert]
    return dict(slot_token=slot_token, slot=slot, tile_expert=tile_expert,
                n_tiles=n_tiles.reshape(1), first=first, next_expert=next_expert, parity=parity)


def _combine_kernel(x_ref, y0_ref, y1_ref, wt_ref, mod_ref, *refs, gate_idx, final_norm):
    moe = wt_ref[:, 0:1] * y0_ref[...].astype(F32) + wt_ref[:, 1:2] * y1_ref[...].astype(F32)
    x = x_ref[...] + mod_ref[0, gate_idx:gate_idx + 1, :] * moe
    if final_norm:
        gf_ref, out_ref = refs
        ms = jnp.mean(x * x, axis=-1, keepdims=True)
        out_ref[...] = x * lax.rsqrt(ms + EPS) * gf_ref[...]
    else:
        (out_ref,) = refs
        out_ref[...] = x


def moe_combine(x, y0, y1, wt, mod, gate_idx, t_ctx, l_lat, g_final=None):
    t, d = x.shape
    grp = functools.partial(_group_of_tile, tm=TM, t_ctx=t_ctx, l_lat=l_lat)
    row_spec = lambda w: pl.BlockSpec((TM, w), lambda i: (i, 0))
    in_specs = [row_spec(d), row_spec(d), row_spec(d), row_spec(wt.shape[1]),
                pl.BlockSpec((1,) + mod.shape[1:], lambda i: (grp(i), 0, 0))]
    args = [x, y0, y1, wt, mod]
    if g_final is not None:
        in_specs.append(_const_spec((1, d)))
        args.append(g_final.reshape(1, d).astype(F32))
    return pl.pallas_call(
        functools.partial(_combine_kernel, gate_idx=gate_idx, final_norm=g_final is not None),
        grid=(t // TM,),
        in_specs=in_specs,
        out_specs=row_spec(d),
        out_shape=jax.ShapeDtypeStruct((t, d), F32),
        compiler_params=_cparams(("arbitrary",)),
        name="moe_combine",
    )(*args)


def moe_layer(x, g, mod, w_router, router_bias, w1, w3, w2, layer, t_ctx, l_lat, g_final=None):
    h, route, counts = moe_router(x, g, mod, 3, 4, w_router, router_bias, t_ctx, l_lat)
    plan = moe_dispatch_plan(route, counts[:, 0].astype(jnp.int32))
    take = lambda rows, index: rows.at[index].get(mode="promise_in_bounds")
    xs = take(h, plan["slot_token"])
    ys = moe_experts(xs, plan, w1, w3, w2, layer)
    y0 = take(ys, plan["slot"][0])
    y1 = take(ys, plan["slot"][1])
    return moe_combine(x, y0, y1, route[2:4].T, mod, 5, t_ctx, l_lat, g_final)


def _hy_conv_kernel(u_ref, up_ref, un_ref, cw_ref, cb_ref, x0_ref, vv_ref, *, t_ctx, l_ctx, l_lat):
    i = pl.program_id(0)
    has_prev, has_next = _seq_flags(i, TM, t_ctx, l_ctx, l_lat)
    c = _conv3(u_ref[...], up_ref[...], un_ref[...], cw_ref, has_prev, has_next) + cb_ref[...]
    d = c.shape[1] // 3
    x0_ref[...] = c[:, :d].astype(x0_ref.dtype)
    vv_ref[...] = (c[:, 2 * d:] * c[:, d:2 * d]).astype(vv_ref.dtype)


def hy_conv_gate(u, conv_w, conv_b, t_ctx, l_ctx, l_lat):
    t, c = u.shape
    d = c // 3
    cw = jnp.zeros((SUBLANES, c), F32).at[:conv_w.shape[0]].set(conv_w)
    row_spec = lambda w: pl.BlockSpec((TM, w), lambda i: (i, 0))
    return pl.pallas_call(
        functools.partial(_hy_conv_kernel, t_ctx=t_ctx, l_ctx=l_ctx, l_lat=l_lat),
        grid=(t // TM,),
        in_specs=_halo_specs(c, TM, t) + [_const_spec((SUBLANES, c)), _const_spec((1, c))],
        out_specs=[row_spec(d), row_spec(d)],
        out_shape=[jax.ShapeDtypeStruct((t, d), BF16), jax.ShapeDtypeStruct((t, d), BF16)],
        compiler_params=_cparams(("arbitrary",)),
        name="hy_conv_gate",
    )(u, u, u, cw, conv_b.reshape(1, c).astype(F32))


def _hy_features(n):
    t = np.linspace(0.0, 1.0, n)[:, None]
    bands = (HY_EMB - 1) // 2
    band_w = np.linspace(1e-4, bands - 1, bands)[None, :]
    ang = (2.0 * math.pi / n) * np.arange(n)[:, None] * band_w
    z = np.concatenate([t, np.cos(ang), -np.sin(ang)], axis=-1)
    zp = np.zeros((n, LANES))
    zp[:, :HY_EMB] = z
    return jnp.asarray(zp.astype(np.float32))


def _hy_hid_kernel(z_ref, f_ref, w1_ref, b1_ref, w2_ref, b2_ref, w3_ref, b3_ref, hid_ref):
    h = jnp.sin(f_ref[0:1, :] * (_dot(z_ref[...], w1_ref[...], HI) + b1_ref[...]))
    h = jnp.sin(f_ref[1:2, :] * (_dot(h, w2_ref[...], HI) + b2_ref[...]))
    h = jnp.sin(f_ref[2:3, :] * (_dot(h, w3_ref[...], HI) + b3_ref[...]))
    hid_ref[...] = h


def _hy_filt_kernel(hid_ref, w4_ref, z_ref, dl_ref, filt_ref):
    f = _dot(hid_ref[...], w4_ref[...], HI)
    f = f * jnp.exp(-z_ref[:, 0:1] * jnp.abs(dl_ref[...]))
    nrm = jnp.sum(jnp.abs(f), axis=0, keepdims=True)
    filt_ref[...] = f / nrm


def hy_filter(n, d, freq, w1, b1, w2, b2, w3, b3, w4):
    z = _hy_features(n)
    o = HY_ORDER
    w1p = jnp.zeros((LANES, o), F32).at[:HY_EMB].set(w1.astype(F32))
    fp = jnp.zeros((SUBLANES, o), F32).at[:3].set(freq.astype(F32))
    tr = min(n, TM)
    hid = pl.pallas_call(
        _hy_hid_kernel,
        grid=(n // tr,),
        in_specs=[pl.BlockSpec((tr, LANES), lambda i: (i, 0)), _const_spec((SUBLANES, o)),
                  _const_spec((LANES, o)), _const_spec((1, o)),
                  _const_spec((o, o)), _const_spec((1, o)),
                  _const_spec((o, o)), _const_spec((1, o))],
        out_specs=pl.BlockSpec((tr, o), lambda i: (i, 0)),
        out_shape=jax.ShapeDtypeStruct((n, o), F32),
        compiler_params=_cparams(("arbitrary",)),
        name="hy_filter_mlp",
    )(z, fp, w1p, b1.reshape(1, o).astype(F32), w2.astype(F32), b2.reshape(1, o).astype(F32),
      w3.astype(F32), b3.reshape(1, o).astype(F32))
    max_decay = math.log(HY_DECAY_TARGET) / HY_DECAY_SHORT_PCT
    min_decay = math.log(HY_DECAY_TARGET) / HY_DECAY_LONG_PCT
    deltas = np.tile(np.linspace(min_decay, max_decay, d), 2).astype(np.float32)[None, :]
    tc = 256
    return pl.pallas_call(
        _hy_filt_kernel,
        grid=(2 * d // tc,),
        in_specs=[_const_spec((n, o)), pl.BlockSpec((o, tc), lambda j: (0, j)),
                  _const_spec((n, LANES)), pl.BlockSpec((1, tc), lambda j: (0, j))],
        out_specs=pl.BlockSpec((n, tc), lambda j: (0, j)),
        out_shape=jax.ShapeDtypeStruct((n, 2 * d), F32),
        compiler_params=_cparams(("arbitrary",)),
        name="hy_filter_window",
    )(hid, w4.astype(F32), z, jnp.asarray(deltas))


def _dense_dft_tables(l):
    n = 2 * l
    k = np.arange(n)[:, None]
    j = np.arange(l)[None, :]
    th = 2.0 * math.pi * ((k * j) % n) / n
    fwd = np.concatenate([np.cos(th), -np.sin(th)], axis=0)
    inv = np.concatenate([np.cos(th).T, -np.sin(th).T], axis=1) / n
    return (jnp.asarray(fwd.astype(np.float32)).astype(BF16),
            jnp.asarray(inv.astype(np.float32)).astype(BF16))


def _dense_tf_kernel(hf_ref, hb_ref, fwd_ref, tf_ref):
    n = fwd_ref.shape[0] // 2
    hf = hf_ref[...]
    hb = hb_ref[...]
    tf_ref[0:n, :] = _dot(fwd_ref[0:n, :], (hf + hb).astype(BF16))
    tf_ref[n:, :] = _dot(fwd_ref[n:, :], (hf - hb).astype(BF16))


def _dense_conv_kernel(vv_ref, x0_ref, skip_ref, tf_ref, fwd_ref, inv_ref, out_ref):
    n = fwd_ref.shape[0] // 2
    vv = vv_ref[...]
    x = _dot(fwd_ref[...], vv.astype(BF16))
    xr, xi = x[:n], x[n:]
    tr, ti = tf_ref[0:n, :], tf_ref[n:, :]
    y = jnp.concatenate([xr * tr - xi * ti, xr * ti + xi * tr], axis=0).astype(BF16)
    conv = _dot(inv_ref[...], y)
    out = (conv + vv.astype(F32) * skip_ref[...]) * x0_ref[...].astype(F32)
    out_ref[...] = out.astype(out_ref.dtype)


def hy_longconv_dense(vv, x0, skip, filt, n_seq, l):
    d = vv.shape[1]
    n = 2 * l
    fwd, inv = _dense_dft_tables(l)
    dc = 512
    tf = pl.pallas_call(
        _dense_tf_kernel,
        grid=(d // dc,),
        in_specs=[pl.BlockSpec((l, dc), lambda j: (0, j)),
                  pl.BlockSpec((l, dc), lambda j: (0, d // dc + j)),
                  _const_spec(fwd.shape)],
        out_specs=pl.BlockSpec((2 * n, dc), lambda j: (0, j)),
        out_shape=jax.ShapeDtypeStruct((2 * n, d), F32),
        compiler_params=_cparams(("arbitrary",)),
        name="hy_tf_dense",
    )(filt, filt, fwd)
    return pl.pallas_call(
        _dense_conv_kernel,
        grid=(d // dc, n_seq),
        in_specs=[pl.BlockSpec((l, dc), lambda j, s: (s, j)),
                  pl.BlockSpec((l, dc), lambda j, s: (s, j)),
                  pl.BlockSpec((1, dc), lambda j, s: (0, j)),
                  pl.BlockSpec((2 * n, dc), lambda j, s: (0, j)),
                  _const_spec(fwd.shape), _const_spec(inv.shape)],
        out_specs=pl.BlockSpec((l, dc), lambda j, s: (s, j)),
        out_shape=jax.ShapeDtypeStruct((n_seq * l, d), BF16),
        compiler_params=_cparams(("arbitrary", "arbitrary")),
        name="hy_longconv_dense",
    )(vv, x0, skip.reshape(1, d).astype(F32), tf, fwd, inv)


def _two_stage_tables(l):
    n = 2 * l
    n2 = FFT_N2
    n1_full = n // n2
    n_slab = n1_full // 2 + 1
    n1 = -(-n_slab // FFT_UNROLL_SLAB) * FFT_UNROLL_SLAB
    p = l // n2
    a = np.arange(p)[None, None, :]
    c = np.arange(n1)[None, :, None]
    b = np.arange(n2)[:, None, None]
    live = (c < n_slab).astype(np.float64)
    weight = np.where((c == 0) | (c == n1_full // 2), 1.0, 2.0) * live
    ph = 2.0 * math.pi * (((a * c) % n1_full) / n1_full + ((b * c) % n) / n)
    f1 = np.concatenate([np.cos(ph) * live, -np.sin(ph) * live], axis=1)
    pht = np.transpose(ph, (0, 2, 1))
    wt = np.transpose(weight, (0, 2, 1)) / n
    g1 = np.concatenate([np.cos(pht) * wt, -np.sin(pht) * wt], axis=2)
    e = np.arange(n2)[:, None]
    bb = np.arange(n2)[None, :]
    th = 2.0 * math.pi * ((e * bb) % n2) / n2
    fr, fi = np.cos(th), -np.sin(th)
    f2 = np.block([[fr, -fi], [fi, fr]])
    f2c = np.block([[fr, fi], [-fi, fr]])
    cast = lambda m: jnp.asarray(m.astype(np.float32)).astype(BF16)
    return cast(f1), cast(f2), cast(f2c), cast(g1)


def _unrolled_loop(n, unroll, fn):
    def body(i, carry):
        fn([i * unroll + u for u in range(unroll)])
        return carry

    lax.fori_loop(0, n // unroll, body, 0)


FFT_UNROLL_SMALL = 16
FFT_UNROLL_SLAB = 4


def _stage1(src_ref, f1_ref, w_scr, combine=None):
    n2, two_n1, p = f1_ref.shape

    def step(bs):
        if combine is None:
            xs = [src_ref[pl.ds(b, p, stride=n2), :] for b in bs]
        else:
            xs = [combine(src_ref[0][pl.ds(b, p, stride=n2), :], src_ref[1][pl.ds(b, p, stride=n2), :])
                  for b in bs]
        res = [_dot(f1_ref[b], x.astype(BF16)) for b, x in zip(bs, xs)]
        for b, r in zip(bs, res):
            w_scr[pl.ds(pl.multiple_of(b * two_n1, two_n1), two_n1), :] = r

    _unrolled_loop(n2, FFT_UNROLL_SMALL, step)


def _load_slab(w_scr, c, n1, n2):
    re = w_scr[pl.ds(c, n2, stride=2 * n1), :]
    im = w_scr[pl.ds(n1 + c, n2, stride=2 * n1), :]
    return jnp.concatenate([re, im], axis=0).astype(BF16)


def _two_stage_tf_kernel(hf_ref, hb_ref, f1_ref, f2_ref, tfr_ref, tfi_ref, ws_scr, wd_scr):
    n2, two_n1, _ = f1_ref.shape
    n1 = two_n1 // 2
    _stage1((hf_ref, hb_ref), f1_ref, ws_scr, combine=lambda u, v: u + v)
    _stage1((hf_ref, hb_ref), f1_ref, wd_scr, combine=lambda u, v: u - v)

    def step(cs):
        xs = [_dot(f2_ref[0:n2, :], _load_slab(ws_scr, c, n1, n2)) for c in cs]
        xd = [_dot(f2_ref[n2:, :], _load_slab(wd_scr, c, n1, n2)) for c in cs]
        for c, a, b in zip(cs, xs, xd):
            r0 = pl.multiple_of(c * n2, n2)
            tfr_ref[pl.ds(r0, n2), :] = a
            tfi_ref[pl.ds(r0, n2), :] = b

    _unrolled_loop(n1, FFT_UNROLL_SLAB, step)


def _two_stage_conv_kernel(vv_ref, x0_ref, skip_ref, tfr_ref, tfi_ref,
                           f1_ref, f2_ref, f2c_ref, g1_ref, out_ref, w_scr, t_scr):
    n2, two_n1, p = f1_ref.shape
    n1 = two_n1 // 2
    t_scr[...] = vv_ref[...].astype(F32)
    _stage1(t_scr, f1_ref, w_scr)

    def step2(cs):
        xs = [_dot(f2_ref[...], _load_slab(w_scr, c, n1, n2)) for c in cs]
        ys = []
        for c, x in zip(cs, xs):
            r0 = pl.multiple_of(c * n2, n2)
            xr, xi = x[:n2], x[n2:]
            tr = tfr_ref[pl.ds(r0, n2), :]
            ti = tfi_ref[pl.ds(r0, n2), :]
            ys.append(jnp.concatenate([xr * tr - xi * ti, xr * ti + xi * tr], axis=0).astype(BF16))
        zs = [_dot(f2c_ref[...], y) for y in ys]
        for c, z in zip(cs, zs):
            w_scr[pl.ds(c, n2, stride=two_n1), :] = z[:n2]
            w_scr[pl.ds(n1 + c, n2, stride=two_n1), :] = z[n2:]

    _unrolled_loop(n1, FFT_UNROLL_SLAB, step2)

    def step3(bs):
        zb = [w_scr[pl.ds(pl.multiple_of(b * two_n1, two_n1), two_n1), :].astype(BF16) for b in bs]
        res = [_dot(g1_ref[b], z) for b, z in zip(bs, zb)]
        for b, r in zip(bs, res):
            t_scr[pl.ds(b, p, stride=n2), :] = r

    _unrolled_loop(n2, FFT_UNROLL_SMALL, step3)
    out = (t_scr[...] + vv_ref[...].astype(F32) * skip_ref[...]) * x0_ref[...].astype(F32)
    out_ref[...] = out.astype(out_ref.dtype)


def hy_longconv_two_stage(vv, x0, skip, filt, row0, n_seq, l):
    d = vv.shape[1]
    f1, f2, f2c, g1 = _two_stage_tables(l)
    n2, two_n1, p = f1.shape
    n = (two_n1 // 2) * n2
    dc = LANES
    tfr, tfi = pl.pallas_call(
        _two_stage_tf_kernel,
        grid=(d // dc,),
        in_specs=[pl.BlockSpec((l, dc), lambda j: (0, j)),
                  pl.BlockSpec((l, dc), lambda j: (0, d // dc + j)),
                  _const_spec(f1.shape), _const_spec(f2.shape)],
        out_specs=[pl.BlockSpec((n, dc), lambda j: (0, j)), pl.BlockSpec((n, dc), lambda j: (0, j))],
        out_shape=[jax.ShapeDtypeStruct((n, d), F32), jax.ShapeDtypeStruct((n, d), F32)],
        scratch_shapes=[pltpu.VMEM((n2 * two_n1, dc), F32), pltpu.VMEM((n2 * two_n1, dc), F32)],
        compiler_params=_cparams(("arbitrary",)),
        name="hy_tf_two_stage",
    )(filt, filt, f1, f2)
    base = row0 // l
    return pl.pallas_call(
        _two_stage_conv_kernel,
        grid=(d // dc, n_seq),
        in_specs=[pl.BlockSpec((l, dc), lambda j, s: (base + s, j)),
                  pl.BlockSpec((l, dc), lambda j, s: (base + s, j)),
                  pl.BlockSpec((1, dc), lambda j, s: (0, j)),
                  pl.BlockSpec((n, dc), lambda j, s: (0, j)),
                  pl.BlockSpec((n, dc), lambda j, s: (0, j)),
                  _const_spec(f1.shape), _const_spec(f2.shape), _const_spec(f2c.shape),
                  _const_spec(g1.shape)],
        out_specs=pl.BlockSpec((l, dc), lambda j, s: (s, j)),
        out_shape=jax.ShapeDtypeStruct((n_seq * l, d), BF16),
        scratch_shapes=[pltpu.VMEM((n2 * two_n1, dc), F32), pltpu.VMEM((l, dc), F32)],
        compiler_params=_cparams(("arbitrary", "arbitrary")),
        name="hy_longconv_two_stage",
    )(vv, x0, skip.reshape(1, d).astype(F32), tfr, tfi, f1, f2, f2c, g1)


def _hy_out_kernel(y_ref, w_ref, b_ref, x_ref, mod_ref, out_ref, *, gate_idx):
    y = _dot(y_ref[...].astype(BF16), w_ref[...]) + b_ref[...]
    out_ref[...] = x_ref[...] + mod_ref[0, gate_idx:gate_idx + 1, :] * y


def hy_out(yg, w_out, b_out, x, mod, gate_idx, t_ctx, l_lat):
    t, d = x.shape
    grp = functools.partial(_group_of_tile, tm=TM, t_ctx=t_ctx, l_lat=l_lat)
    row_spec = lambda w: pl.BlockSpec((TM, w), lambda i: (i, 0))
    return pl.pallas_call(
        functools.partial(_hy_out_kernel, gate_idx=gate_idx),
        grid=(t // TM,),
        in_specs=[row_spec(d), _const_spec(w_out.shape), _const_spec((1, d)), row_spec(d),
                  pl.BlockSpec((1,) + mod.shape[1:], lambda i: (grp(i), 0, 0))],
        out_specs=row_spec(d),
        out_shape=jax.ShapeDtypeStruct((t, d), F32),
        compiler_params=_cparams(("arbitrary",)),
        name="hy_out",
    )(yg, w_out, b_out.reshape(1, d).astype(F32), x, mod)


def _grid_pos_embed(rows, d):
    r = np.repeat(np.arange(rows, dtype=np.float64), GRID_W)
    col = np.tile(np.arange(GRID_W, dtype=np.float64), rows)
    quarter = d // 4
    omega = 1.0 / (10000.0 ** (np.arange(quarter, dtype=np.float64) / quarter))
    ang_r = r[:, None] * omega
    ang_c = col[:, None] * omega
    pe = np.concatenate([np.sin(ang_r), np.cos(ang_r), np.sin(ang_c), np.cos(ang_c)], axis=-1)
    return jnp.asarray(pe.astype(np.float32))


def delta_layer(x, g, mod, state_lat, w_in, conv_w, a_log, dt_bias, g_head, w_out, dims):
    t_ctx, l_ctx, n_ctx, l_lat, n_lat = dims
    hk = DN_HEADS * DN_DK
    nqkv = 3 * hk
    w_qkv = w_in[:, :nqkv].astype(BF16)
    w_z = w_in[:, nqkv:nqkv + hk].astype(BF16)
    w_ab = jnp.zeros((w_in.shape[0], LANES), BF16).at[:, :4 * DN_HEADS].set(w_in[:, nqkv + hk:].astype(BF16))
    qkv, z, ab = norm_mod_matmul(x, g, mod, 0, 1, [w_qkv, w_z, w_ab], None, [BF16, BF16, F32],
                                 t_ctx, l_lat)
    q, k, v, gc, beta, gct = dn_prep(qkv, ab, conv_w.astype(F32), a_log, dt_bias, t_ctx, l_ctx, l_lat)
    outs = []
    ctx_states = []
    seq_lens = (l_ctx,) * n_ctx + (l_lat,) * n_lat
    for direction in range(2):
        o, s_all = dn_chunk_scan(q, k, v, gc, beta, gct, state_lat.astype(F32), direction=direction,
                                 seq_lens=seq_lens, n_zero_init=n_ctx)
        outs.append(o)
        ctx_states.append(s_all[:n_ctx])
    x = dn_out(outs[0], outs[1], z, g_head, w_out.astype(BF16), x, mod, 2, t_ctx, l_lat)
    return x, jnp.stack(ctx_states, axis=1)


def hyena_layer(x, g, mod, w_in, b_in, conv_w, conv_b, freq, fw1, fb1, fw2, fb2, fw3, fb3, fw4,
                skip, w_out, b_out, dims):
    t_ctx, l_ctx, n_ctx, l_lat, n_lat = dims
    d = x.shape[1]
    (u,) = norm_mod_matmul(x, g, mod, 0, 1, [w_in.astype(BF16)], [b_in.astype(F32)], [BF16],
                           t_ctx, l_lat)
    x0, vv = hy_conv_gate(u, conv_w.astype(F32), conv_b, t_ctx, l_ctx, l_lat)
    filt_c = hy_filter(l_ctx, d, freq, fw1, fb1, fw2, fb2, fw3, fb3, fw4)
    filt_l = hy_filter(l_lat, d, freq, fw1, fb1, fw2, fb2, fw3, fb3, fw4)
    y_c = hy_longconv_dense(vv, x0, skip, filt_c, n_ctx, l_ctx)
    y_l = hy_longconv_two_stage(vv, x0, skip, filt_l, t_ctx, n_lat, l_lat)
    yg = jnp.concatenate([y_c, y_l], axis=0)
    return hy_out(yg, w_out.astype(BF16), b_out, x, mod, 2, t_ctx, l_lat)


def kernel(x_prompt, x_sample, state_delta, c, c_ctx, w_ada, b_ada, g_norm, dn_w_in, dn_conv, dn_a_log, dn_dt_bias, dn_g_head, dn_w_out, hy_w_in, hy_b_in, hy_conv, hy_conv_b, hy_freq, hy_f_w1, hy_f_b1, hy_f_w2, hy_f_b2, hy_f_w3, hy_f_b3, hy_f_w4, hy_skip, hy_w_out, hy_b_out, w_router, router_bias, moe_w1, moe_w3, moe_w2, g_final):
    n_ctx, l_ctx, d = x_prompt.shape
    n_lat, l_lat, _ = x_sample.shape
    depth = w_ada.shape[0]
    t_ctx = n_ctx * l_ctx
    dims = (t_ctx, l_ctx, n_ctx, l_lat, n_lat)
    assert l_ctx % TM == 0 and l_lat % TM == 0 and t_ctx % l_lat == 0
    assert n_lat + 1 <= SUBLANES

    pos = _grid_pos_embed(l_lat // GRID_W, d)
    x = jnp.concatenate([x_prompt.reshape(t_ctx, d).astype(F32),
                         (x_sample.astype(F32) + pos[None]).reshape(n_lat * l_lat, d)], axis=0)
    cond = jnp.zeros((SUBLANES, d), F32).at[0].set(c_ctx.astype(F32)).at[1:1 + n_lat].set(c.astype(F32))
    mod_all = ada_modulation(cond, w_ada.astype(F32), b_ada.astype(F32))
    mod_all = mod_all.reshape(depth, SUBLANES, 6, d)

    ctx_states = []
    for i in range(depth):
        mod = mod_all[i]
        j = i // 2
        if i % 2 == 0:
            x, s_ctx = delta_layer(x, g_norm[i, 0], mod, state_delta[:, j], dn_w_in[j], dn_conv[j],
                                   dn_a_log[j], dn_dt_bias[j], dn_g_head[j], dn_w_out[j], dims)
            ctx_states.append(s_ctx.astype(x_prompt.dtype))
        else:
            x = hyena_layer(x, g_norm[i, 0], mod, hy_w_in[j], hy_b_in[j], hy_conv[j], hy_conv_b[j],
                            hy_freq[j], hy_f_w1[j], hy_f_b1[j], hy_f_w2[j], hy_f_b2[j], hy_f_w3[j],
                            hy_f_b3[j], hy_f_w4[j], hy_skip[j], hy_w_out[j], hy_b_out[j], dims)
        x = moe_layer(x, g_norm[i, 1], mod, w_router, router_bias, moe_w1, moe_w3, moe_w2, i,
                      t_ctx, l_lat, g_final if i == depth - 1 else None)
    y_prompt = x[:t_ctx].reshape(n_ctx, l_ctx, d).astype(x_prompt.dtype)
    y_sample = x[t_ctx:].reshape(n_lat, l_lat, d).astype(x_sample.dtype)
    new_state = jnp.stack(ctx_states, axis=1)
    return (y_prompt, y_sample, new_state)
```

```python
import functools
import math

import numpy as np
import jax
import jax.numpy as jnp
from jax import lax
from jax.experimental import pallas as pl
from jax.experimental.pallas import tpu as pltpu

F32 = jnp.float32
BF16 = jnp.bfloat16
HI = lax.Precision.HIGHEST

EPS = 1e-6
GRID_W = 64
DN_HEADS = 8
DN_DK = 128
DN_DV = 128
DN_CHUNK = 64
HY_EMB = 33
HY_ORDER = 64
HY_DECAY_SHORT_PCT = 0.3
HY_DECAY_LONG_PCT = 1.5
HY_DECAY_TARGET = 1e-2
N_EXPERTS = 32
N_GROUPS = 4
EXPERTS_PER_GROUP = N_EXPERTS // N_GROUPS

LANES = 128
SUBLANES = 8
TM = 256
HALO_ROWS = 16
MOE_TILE = 256
FFT_N2 = 128
DN_PREP_CHUNKS = 4
VMEM_LIMIT = 48 * 1024 * 1024


def _cparams(sem):
    return pltpu.CompilerParams(dimension_semantics=sem, vmem_limit_bytes=VMEM_LIMIT)


def _sigmoid(x):
    return 1.0 / (1.0 + jnp.exp(-x))


def _silu(x):
    return x * _sigmoid(x)


def _dot(a, b, precision=None):
    return jnp.dot(a, b, preferred_element_type=F32, precision=precision)


def _const_spec(shape):
    nd = len(shape)
    return pl.BlockSpec(shape, lambda *_: (0,) * nd)


def _ada_kernel(cond_ref, w_ref, b_ref, out_ref):
    s = _silu(cond_ref[...])
    out_ref[0] = _dot(s, w_ref[0], HI) + b_ref[0]


def ada_modulation(cond, w_ada, b_ada):
    depth, d, n = w_ada.shape
    tn = 1536
    return pl.pallas_call(
        _ada_kernel,
        grid=(depth, n // tn),
        in_specs=[
            pl.BlockSpec((SUBLANES, d), lambda i, j: (0, 0)),
            pl.BlockSpec((1, d, tn), lambda i, j: (i, 0, j)),
            pl.BlockSpec((1, 1, tn), lambda i, j: (i, 0, j)),
        ],
        out_specs=pl.BlockSpec((1, SUBLANES, tn), lambda i, j: (i, 0, j)),
        out_shape=jax.ShapeDtypeStruct((depth, SUBLANES, n), F32),
        compiler_params=_cparams(("arbitrary", "arbitrary")),
        name="ada_modulation",
    )(cond, w_ada, b_ada.reshape(depth, 1, n))


def _norm_mod(x, g, mod_ref, shift_idx, scale_idx):
    ms = jnp.mean(x * x, axis=-1, keepdims=True)
    y = x * lax.rsqrt(ms + EPS) * g
    scale = mod_ref[0, scale_idx:scale_idx + 1, :]
    shift = mod_ref[0, shift_idx:shift_idx + 1, :]
    return y * (1.0 + scale) + shift


def _group_of_tile(i, tm, t_ctx, l_lat):
    r = i * tm
    return jnp.where(r < t_ctx, 0, 1 + (r - t_ctx) // l_lat)


def _nmm_kernel(x_ref, g_ref, mod_ref, *refs, n_out, has_bias, shift_idx, scale_idx):
    h = _norm_mod(x_ref[...], g_ref[...], mod_ref, shift_idx, scale_idx).astype(BF16)
    w_refs = refs[:n_out]
    b_refs = refs[n_out:n_out + (n_out if has_bias else 0)]
    o_refs = refs[n_out + len(b_refs):]
    for k in range(n_out):
        acc = _dot(h, w_refs[k][...])
        if has_bias:
            acc = acc + b_refs[k][...]
        o_refs[k][...] = acc.astype(o_refs[k].dtype)


def norm_mod_matmul(x, g, mod, shift_idx, scale_idx, weights, biases, out_dtypes, t_ctx, l_lat):
    t, d = x.shape
    n_out = len(weights)
    has_bias = biases is not None
    grp = functools.partial(_group_of_tile, tm=TM, t_ctx=t_ctx, l_lat=l_lat)
    in_specs = [
        pl.BlockSpec((TM, d), lambda i: (i, 0)),
        _const_spec((1, d)),
        pl.BlockSpec((1,) + mod.shape[1:], lambda i: (grp(i), 0, 0)),
    ]
    args = [x, g.reshape(1, d), mod]
    for w in weights:
        in_specs.append(_const_spec(w.shape))
        args.append(w)
    if has_bias:
        for b in biases:
            in_specs.append(_const_spec((1, b.shape[-1])))
            args.append(b.reshape(1, -1))
    out_specs = [pl.BlockSpec((TM, w.shape[1]), lambda i: (i, 0)) for w in weights]
    out_shape = [jax.ShapeDtypeStruct((t, w.shape[1]), dt) for w, dt in zip(weights, out_dtypes)]
    return pl.pallas_call(
        functools.partial(_nmm_kernel, n_out=n_out, has_bias=has_bias,
                          shift_idx=shift_idx, scale_idx=scale_idx),
        grid=(t // TM,),
        in_specs=in_specs,
        out_specs=out_specs,
        out_shape=out_shape,
        compiler_params=_cparams(("arbitrary",)),
        name="norm_mod_matmul",
    )(*args)


def _seq_flags(i, tm, t_ctx, l_ctx, l_lat):
    r = i * tm
    in_ctx = r < t_ctx
    pos = jnp.where(in_ctx, r % l_ctx, (r - t_ctx) % l_lat)
    length = jnp.where(in_ctx, l_ctx, l_lat)
    return pos != 0, (pos + tm) != length


def _conv3(x, prev_rows, next_rows, w_ref, has_prev, has_next):
    tm = x.shape[0]
    x = x.astype(F32)
    halo_p = jnp.where(has_prev, prev_rows[HALO_ROWS - 1:HALO_ROWS, :].astype(F32), 0.0)
    halo_n = jnp.where(has_next, next_rows[0:1, :].astype(F32), 0.0)
    w0, w1, w2 = w_ref[0:1, :], w_ref[1:2, :], w_ref[2:3, :]
    c = w0 * pltpu.roll(x, 1, 0) + w1 * x + w2 * pltpu.roll(x, tm - 1, 0)
    row = lax.broadcasted_iota(jnp.int32, (SUBLANES, 1), 0)
    top = c[:SUBLANES] + jnp.where(row == 0, w0 * (halo_p - x[tm - 1:tm]), 0.0)
    bot = c[tm - SUBLANES:] + jnp.where(row == SUBLANES - 1, w2 * (halo_n - x[0:1]), 0.0)
    return jnp.concatenate([top, c[SUBLANES:tm - SUBLANES], bot], axis=0)


def _halo_specs(c, tm, t):
    nblk = t // HALO_ROWS
    per = tm // HALO_ROWS
    return [
        pl.BlockSpec((tm, c), lambda i: (i, 0)),
        pl.BlockSpec((HALO_ROWS, c), lambda i: (jnp.maximum(i * per - 1, 0), 0)),
        pl.BlockSpec((HALO_ROWS, c), lambda i: (jnp.minimum((i + 1) * per, nblk - 1), 0)),
    ]


def _softplus(x):
    return jnp.maximum(x, 0.0) + jnp.log(1.0 + jnp.exp(-jnp.abs(x)))


def _dn_prep_kernel(x_ref, xp_ref, xn_ref, cw_ref, ab_ref, nega_ref, dtb_ref, tri_ref,
                    q_ref, k_ref, v_ref, gc_ref, beta_ref, gct_ref,
                    *, t_ctx, l_ctx, l_lat):
    i = pl.program_id(0)
    has_prev, has_next = _seq_flags(i, TM, t_ctx, l_ctx, l_lat)
    c = _conv3(x_ref[...], xp_ref[...], xn_ref[...], cw_ref, has_prev, has_next)
    s = _silu(c)
    hk = DN_HEADS * DN_DK
    for h in range(DN_HEADS):
        lo, hi = h * DN_DK, (h + 1) * DN_DK
        qh = s[:, lo:hi]
        kh = s[:, hk + lo:hk + hi]
        qn = lax.rsqrt(jnp.sum(qh * qh, axis=-1, keepdims=True) + EPS)
        kn = lax.rsqrt(jnp.sum(kh * kh, axis=-1, keepdims=True) + EPS)
        q_ref[:, lo:hi] = (qh * (qn * (DN_DK ** -0.5))).astype(BF16)
        k_ref[:, lo:hi] = (kh * kn).astype(BF16)
    v_ref[...] = s[:, 2 * hk:].astype(BF16)
    ab = ab_ref[...]
    lg = nega_ref[...] * _softplus(ab + dtb_ref[...])
    beta_ref[...] = _sigmoid(ab)
    nch = TM // DN_CHUNK
    for d in range(2):
        gc = _dot(tri_ref[d], lg, HI)
        gc_ref[d] = gc
        gct = gc.T
        for ch in range(nch):
            gct_ref[d, ch] = gct[0:2 * DN_HEADS, ch * DN_CHUNK:(ch + 1) * DN_CHUNK]


def _chunk_tri():
    idx = np.arange(TM)
    same = (idx[:, None] // DN_CHUNK) == (idx[None, :] // DN_CHUNK)
    fwd = same & (idx[None, :] <= idx[:, None])
    bwd = same & (idx[None, :] >= idx[:, None])
    return jnp.asarray(np.stack([fwd, bwd]).astype(np.float32))


def dn_prep(qkv, ab, conv_w, a_log, dt_bias, t_ctx, l_ctx, l_lat):
    t, c = qkv.shape
    hd = DN_HEADS * DN_DK
    nega = jnp.zeros((1, LANES), F32).at[0, :2 * DN_HEADS].set(-jnp.exp(a_log.astype(F32)).reshape(-1))
    dtb = jnp.zeros((1, LANES), F32).at[0, :2 * DN_HEADS].set(dt_bias.astype(F32).reshape(-1))
    cw = jnp.zeros((SUBLANES, c), F32).at[:conv_w.shape[0]].set(conv_w)
    nch = TM // DN_CHUNK
    row_spec = lambda w: pl.BlockSpec((TM, w), lambda i: (i, 0))
    return pl.pallas_call(
        functools.partial(_dn_prep_kernel, t_ctx=t_ctx, l_ctx=l_ctx, l_lat=l_lat),
        grid=(t // TM,),
        in_specs=_halo_specs(c, TM, t) + [
            _const_spec((SUBLANES, c)),
            row_spec(LANES),
            _const_spec((1, LANES)),
            _const_spec((1, LANES)),
            _const_spec((2, TM, TM)),
        ],
        out_specs=[
            row_spec(hd), row_spec(hd), row_spec(hd),
            pl.BlockSpec((2, TM, LANES), lambda i: (0, i, 0)),
            row_spec(LANES),
            pl.BlockSpec((2, nch, 2 * DN_HEADS, DN_CHUNK), lambda i: (0, i, 0, 0)),
        ],
        out_shape=[
            jax.ShapeDtypeStruct((t, hd), BF16),
            jax.ShapeDtypeStruct((t, hd), BF16),
            jax.ShapeDtypeStruct((t, hd), BF16),
            jax.ShapeDtypeStruct((2, t, LANES), F32),
            jax.ShapeDtypeStruct((t, LANES), F32),
            jax.ShapeDtypeStruct((2, t // DN_CHUNK, 2 * DN_HEADS, DN_CHUNK), F32),
        ],
        compiler_params=_cparams(("arbitrary",)),
        name="dn_prep",
    )(qkv, qkv, qkv, cw, ab, nega, dtb, _chunk_tri())


def _bmm(a, b):
    return _dot(a.astype(BF16), b.astype(BF16))


def _inv_unit_tri_batch(lms, in_blk, eye):
    dgs = [jnp.where(in_blk, lm, 0.0) for lm in lms]
    offs = [lm - dg for lm, dg in zip(lms, dgs)]
    n1 = [-dg for dg in dgs]
    n2 = [_bmm(a, a) for a in n1]
    p = [eye + a for a in n1]
    n4 = [_bmm(a, a) for a in n2]
    p = [x + _bmm(x, a) for x, a in zip(p, n2)]
    n8 = [_bmm(a, a) for a in n4]
    p = [x + _bmm(x, a) for x, a in zip(p, n4)]
    p = [x + _bmm(x, a) for x, a in zip(p, n8)]
    m = [_bmm(x, o) for x, o in zip(p, offs)]
    m2 = [_bmm(a, a) for a in m]
    t1 = [eye - a for a in m]
    t1 = [x + _bmm(x, a) for x, a in zip(t1, m2)]
    return [_bmm(x, y) for x, y in zip(t1, p)]


def _dn_chunk_kernel(blk_ref, seq_ref, first_ref, last_ref,
                     q_ref, k_ref, v_ref, gc_ref, beta_ref, gct_ref, s0_ref,
                     o_ref, sfin_ref, s_scr, attn_scr, u_scr, qw_scr, kd_scr, gl_scr,
                     *, direction, n_zero_init):
    step = pl.program_id(0)
    c_sz = DN_CHUNK
    nch = TM // c_sz
    heads = range(DN_HEADS)

    @pl.when(first_ref[step] == 1)
    def _():
        s_scr[...] = jnp.where(seq_ref[step] >= n_zero_init, s0_ref[0, 0], 0.0)

    row = lax.broadcasted_iota(jnp.int32, (c_sz, c_sz), 0)
    col = lax.broadcasted_iota(jnp.int32, (c_sz, c_sz), 1)
    if direction == 0:
        incl, strict = row >= col, row > col
        last = c_sz - 1
    else:
        incl, strict = row <= col, row < col
        last = 0
    in_blk = (row // 16) == (col // 16)
    eye = (row == col).astype(F32)

    def lanes(h):
        return slice(h * DN_DK, (h + 1) * DN_DK)

    def prep_body(it, carry):
        units = [(it * DN_PREP_CHUNKS + c, h) for c in range(DN_PREP_CHUNKS) for h in heads]
        idx = range(len(units))
        rows = [pl.ds(pl.multiple_of(ch * c_sz, c_sz), c_sz) for ch, _ in units]
        gls = [direction * DN_HEADS + h for _, h in units]
        qs = [q_ref[rows[i], lanes(units[i][1])].astype(F32) for i in idx]
        ks = [k_ref[rows[i], lanes(units[i][1])].astype(F32) for i in idx]
        gcc = [gc_ref[0, rows[i], gls[i]:gls[i] + 1] for i in idx]
        beta = [beta_ref[rows[i], 2 * DN_HEADS + gls[i]:2 * DN_HEADS + gls[i] + 1] for i in idx]
        gcr = [gct_ref[0, units[i][0]][gls[i]:gls[i] + 1, :] for i in idx]
        dec = [jnp.exp(jnp.where(incl, gcc[i] - gcr[i], -jnp.inf)) for i in idx]
        kb = [ks[i] * beta[i] for i in idx]
        prod = [lax.dot_general(jnp.concatenate([qs[i], kb[i]], axis=0).astype(BF16), ks[i].astype(BF16),
                                (((1,), (1,)), ((), ())), preferred_element_type=F32)
                for i in idx]
        for i, (ch, h) in enumerate(units):
            attn_scr[ch, h] = (prod[i][:c_sz] * dec[i]).astype(BF16)
        lms = [jnp.where(strict, prod[i][c_sz:] * dec[i], 0.0) for i in idx]
        tinv = _inv_unit_tri_batch(lms, in_blk, eye)
        eg = [jnp.exp(gcc[i]) for i in idx]
        uw = [_bmm(tinv[i], jnp.concatenate(
                  [v_ref[rows[i], lanes(units[i][1])].astype(F32) * beta[i], kb[i] * eg[i]], axis=1))
              for i in idx]
        for i, (ch, h) in enumerate(units):
            u_scr[rows[i], lanes(h)] = uw[i][:, :DN_DV]
            qw_scr[ch, h] = jnp.concatenate([qs[i] * eg[i], uw[i][:, DN_DV:]], axis=0).astype(BF16)
            g_last = gcc[i][last:last + 1, :]
            kd_scr[ch, h] = (ks[i] * jnp.exp(g_last - gcc[i])).T.astype(BF16)
            gl_scr[ch, h] = jnp.broadcast_to(jnp.exp(g_last), (1, DN_DV))
        return carry

    lax.fori_loop(0, nch // DN_PREP_CHUNKS, prep_body, 0)

    def scan_body(ci, carry):
        ch = ci if direction == 0 else nch - 1 - ci
        r0 = pl.multiple_of(ch * c_sz, c_sz)
        rows = pl.ds(r0, c_sz)
        s = [s_scr[h] for h in heads]
        qs_ws = [_dot(qw_scr[ch, h], s[h].astype(BF16)) for h in heads]
        vb = [(u_scr[rows, lanes(h)] - qs_ws[h][c_sz:]).astype(BF16) for h in heads]
        for h in heads:
            o_ref[rows, lanes(h)] = (qs_ws[h][:c_sz] + _dot(attn_scr[ch, h], vb[h])).astype(o_ref.dtype)
            s_scr[h] = s[h] * gl_scr[ch, h] + _dot(kd_scr[ch, h], vb[h])
        return carry

    lax.fori_loop(0, nch, scan_body, 0)

    @pl.when(last_ref[step] == 1)
    def _():
        sfin_ref[0, 0] = s_scr[...]


def _scan_tables(direction, seq_lens):
    blk, seq, first, last = [], [], [], []
    base = 0
    for s, length in enumerate(seq_lens):
        nblk = length // TM
        order = range(nblk) if direction == 0 else range(nblk - 1, -1, -1)
        for pos, jj in enumerate(order):
            blk.append(base + jj)
            seq.append(s)
            first.append(int(pos == 0))
            last.append(int(pos == nblk - 1))
        base += nblk
    return [jnp.asarray(np.asarray(a, np.int32)) for a in (blk, seq, first, last)]


def dn_chunk_scan(q, k, v, gc, beta, gct, s0, *, direction, seq_lens, n_zero_init):
    t, hd = q.shape
    nch = TM // DN_CHUNK
    n_seq = len(seq_lens)
    tables = _scan_tables(direction, seq_lens)
    n_steps = int(tables[0].shape[0])
    row_spec = lambda w: pl.BlockSpec((TM, w), lambda i, blk, seq, fst, lst: (blk[i], 0))
    state_blk = (1, 1, DN_HEADS, DN_DK, DN_DV)
    grid_spec = pltpu.PrefetchScalarGridSpec(
        num_scalar_prefetch=4,
        grid=(n_steps,),
        in_specs=[
            row_spec(hd), row_spec(hd), row_spec(hd),
            pl.BlockSpec((1, TM, LANES), lambda i, blk, seq, fst, lst: (direction, blk[i], 0)),
            row_spec(LANES),
            pl.BlockSpec((1, nch, 2 * DN_HEADS, DN_CHUNK),
                         lambda i, blk, seq, fst, lst: (direction, blk[i], 0, 0)),
            pl.BlockSpec(state_blk, lambda i, blk, seq, fst, lst:
                         (jnp.maximum(seq[i] - n_zero_init, 0), direction, 0, 0, 0)),
        ],
        out_specs=[
            row_spec(hd),
            pl.BlockSpec(state_blk, lambda i, blk, seq, fst, lst: (seq[i], 0, 0, 0, 0)),
        ],
        scratch_shapes=[
            pltpu.VMEM((DN_HEADS, DN_DK, DN_DV), F32),
            pltpu.VMEM((nch, DN_HEADS, DN_CHUNK, DN_CHUNK), BF16),
            pltpu.VMEM((TM, hd), F32),
            pltpu.VMEM((nch, DN_HEADS, 2 * DN_CHUNK, DN_DK), BF16),
            pltpu.VMEM((nch, DN_HEADS, DN_DK, DN_CHUNK), BF16),
            pltpu.VMEM((nch, DN_HEADS, 1, DN_DV), F32),
        ],
    )
    o, sfin = pl.pallas_call(
        functools.partial(_dn_chunk_kernel, direction=direction, n_zero_init=n_zero_init),
        grid_spec=grid_spec,
        out_shape=[
            jax.ShapeDtypeStruct((t, hd), BF16),
            jax.ShapeDtypeStruct((n_seq,) + state_blk[1:], F32),
        ],
        compiler_params=_cparams(("arbitrary",)),
        name="dn_chunk_scan_d%d" % direction,
    )(*tables, q, k, v, gc, beta, gct, s0)
    return o, sfin[:, 0]


def _dn_out_kernel(of_ref, ob_ref, z_ref, gh_ref, w_ref, x_ref, mod_ref, out_ref, *, gate_idx):
    o = of_ref[...].astype(F32) + ob_ref[...].astype(F32)
    z = z_ref[...].astype(F32)
    parts = []
    for h in range(DN_HEADS):
        lanes = slice(h * DN_DV, (h + 1) * DN_DV)
        oh = o[:, lanes]
        ms = jnp.mean(oh * oh, axis=-1, keepdims=True)
        parts.append(oh * lax.rsqrt(ms + EPS) * gh_ref[...])
    on = jnp.concatenate(parts, axis=1) * _silu(z)
    y = _dot(on.astype(BF16), w_ref[...])
    out_ref[...] = x_ref[...] + mod_ref[0, gate_idx:gate_idx + 1, :] * y


def dn_out(o_f, o_b, z, g_head, w_out, x, mod, gate_idx, t_ctx, l_lat):
    t, d = x.shape
    hd = o_f.shape[1]
    grp = functools.partial(_group_of_tile, tm=TM, t_ctx=t_ctx, l_lat=l_lat)
    row_spec = lambda w: pl.BlockSpec((TM, w), lambda i: (i, 0))
    return pl.pallas_call(
        functools.partial(_dn_out_kernel, gate_idx=gate_idx),
        grid=(t // TM,),
        in_specs=[
            row_spec(hd), row_spec(hd), row_spec(hd),
            _const_spec((1, DN_DV)),
            _const_spec(w_out.shape),
            row_spec(d),
            pl.BlockSpec((1,) + mod.shape[1:], lambda i: (grp(i), 0, 0)),
        ],
        out_specs=row_spec(d),
        out_shape=jax.ShapeDtypeStruct((t, d), F32),
        compiler_params=_cparams(("arbitrary",)),
        name="dn_out",
    )(o_f, o_b, z, g_head.reshape(1, DN_DV).astype(F32), w_out, x, mod)


ROUTE_ROWS = 8


def _first_argmax_rows(vals, row):
    m = jnp.max(vals, axis=0, keepdims=True)
    idx = jnp.min(jnp.where(vals == m, row, float(SUBLANES)), axis=0, keepdims=True)
    return m, idx


def _router_kernel(x_ref, g_ref, mod_ref, wh_ref, wl_ref, rb_ref, tri_ref,
                   h_ref, route_ref, count_ref, run_scr, *, shift_idx, scale_idx):
    i = pl.program_id(0)

    @pl.when(i == 0)
    def _():
        run_scr[...] = jnp.zeros_like(run_scr)

    h = _norm_mod(x_ref[...], g_ref[...], mod_ref, shift_idx, scale_idx)
    hb = h.astype(BF16)
    h_ref[...] = hb
    hl = (h - hb.astype(F32)).astype(BF16)
    nt = (((1,), (1,)), ((), ()))
    wh = wh_ref[...]
    logits = (lax.dot_general(wh, hb, nt, preferred_element_type=F32)
              + (lax.dot_general(wh, hl, nt, preferred_element_type=F32)
                 + lax.dot_general(wl_ref[...], hb, nt, preferred_element_type=F32)))
    scores = _sigmoid(logits)
    sel = scores + rb_ref[:, 0:1]
    tm = scores.shape[1]
    row8 = lax.broadcasted_iota(jnp.int32, (SUBLANES, tm), 0).astype(F32)
    best_score = None
    for g in range(N_GROUPS):
        s_g = sel[g * EXPERTS_PER_GROUP:(g + 1) * EXPERTS_PER_GROUP, :]
        m1, i1 = _first_argmax_rows(s_g, row8)
        m2, i2 = _first_argmax_rows(jnp.where(row8 == i1, -jnp.inf, s_g), row8)
        gs = m1 + m2
        e1, e2 = i1 + float(g * EXPERTS_PER_GROUP), i2 + float(g * EXPERTS_PER_GROUP)
        if best_score is None:
            best_score, b1, b2 = gs, e1, e2
        else:
            better = gs > best_score
            best_score = jnp.where(better, gs, best_score)
            b1 = jnp.where(better, e1, b1)
            b2 = jnp.where(better, e2, b2)
    row_e = lax.broadcasted_iota(jnp.int32, (N_EXPERTS, tm), 0).astype(F32)
    pick1, pick2 = row_e == b1, row_e == b2
    w1 = jnp.sum(jnp.where(pick1, scores, 0.0), axis=0, keepdims=True)
    w2 = jnp.sum(jnp.where(pick2, scores, 0.0), axis=0, keepdims=True)
    tot = w1 + w2
    chosen = jnp.where(pick1 | pick2, 1.0, 0.0)
    before = run_scr[:, 0:1] + _dot(chosen.astype(BF16), tri_ref[...])
    r1 = jnp.sum(jnp.where(pick1, before, 0.0), axis=0, keepdims=True)
    r2 = jnp.sum(jnp.where(pick2, before, 0.0), axis=0, keepdims=True)
    run_scr[...] = run_scr[...] + jnp.sum(chosen, axis=1, keepdims=True)
    count_ref[...] = run_scr[...]
    zero = jnp.zeros_like(w1)
    route_ref[...] = jnp.concatenate([b1, b2, w1 / tot, w2 / tot, r1, r2, zero, zero], axis=0)


def moe_router(x, g, mod, shift_idx, scale_idx, w_router, router_bias, t_ctx, l_lat):
    t, d = x.shape
    grp = functools.partial(_group_of_tile, tm=TM, t_ctx=t_ctx, l_lat=l_lat)
    wr = w_router.astype(F32).T
    wh = wr.astype(BF16)
    wl = (wr - wh.astype(F32)).astype(BF16)
    rb = jnp.broadcast_to(router_bias.astype(F32)[:, None], (N_EXPERTS, LANES))
    idx = np.arange(TM)
    tri = jnp.asarray((idx[:, None] < idx[None, :]).astype(np.float32)).astype(BF16)
    return pl.pallas_call(
        functools.partial(_router_kernel, shift_idx=shift_idx, scale_idx=scale_idx),
        grid=(t // TM,),
        in_specs=[
            pl.BlockSpec((TM, d), lambda i: (i, 0)),
            _const_spec((1, d)),
            pl.BlockSpec((1,) + mod.shape[1:], lambda i: (grp(i), 0, 0)),
            _const_spec((N_EXPERTS, d)),
            _const_spec((N_EXPERTS, d)),
            _const_spec((N_EXPERTS, LANES)),
            _const_spec((TM, TM)),
        ],
        out_specs=[pl.BlockSpec((TM, d), lambda i: (i, 0)),
                   pl.BlockSpec((ROUTE_ROWS, TM), lambda i: (0, i)),
                   _const_spec((N_EXPERTS, LANES))],
        out_shape=[
            jax.ShapeDtypeStruct((t, d), BF16),
            jax.ShapeDtypeStruct((ROUTE_ROWS, t), F32),
            jax.ShapeDtypeStruct((N_EXPERTS, LANES), F32),
        ],
        scratch_shapes=[pltpu.VMEM((N_EXPERTS, LANES), F32)],
        compiler_params=_cparams(("arbitrary",)),
        name="moe_router",
    )(x, g.reshape(1, d), mod, wh, wl, rb, tri)


def _expert_kernel(te_ref, nt_ref, first_ref, nxt_ref, par_ref,
                   xs_ref, w1_hbm, w3_hbm, w2_hbm, ys_ref,
                   w1_buf, w3_buf, w2_buf, w1_bf, w3_bf, w2_bf, sem, *, layer):
    i = pl.program_id(0)
    active = i < nt_ref[0]

    def weight_copies(e, slot):
        return (pltpu.make_async_copy(w1_hbm.at[layer, e], w1_buf.at[slot], sem.at[slot, 0]),
                pltpu.make_async_copy(w3_hbm.at[layer, e], w3_buf.at[slot], sem.at[slot, 1]),
                pltpu.make_async_copy(w2_hbm.at[layer, e], w2_buf.at[slot], sem.at[slot, 2]))

    @pl.when(active & (i == 0))
    def _():
        for cp in weight_copies(te_ref[0], 0):
            cp.start()

    @pl.when(active & (first_ref[i] == 1))
    def _():
        slot = par_ref[i]
        for cp in weight_copies(te_ref[i], slot):
            cp.wait()

        @pl.when(nxt_ref[i] >= 0)
        def _():
            for cp in weight_copies(nxt_ref[i], 1 - slot):
                cp.start(priority=1)

        w1_bf[...] = w1_buf[slot].astype(BF16)
        w3_bf[...] = w3_buf[slot].astype(BF16)
        w2_bf[...] = w2_buf[slot].astype(BF16)

    @pl.when(active)
    def _():
        xs = xs_ref[...]
        a = _dot(xs, w1_bf[...])
        b = _dot(xs, w3_bf[...])
        hid = (_silu(a) * b).astype(BF16)
        ys_ref[...] = _dot(hid, w2_bf[...]).astype(ys_ref.dtype)

    @pl.when(jnp.logical_not(active))
    def _():
        ys_ref[...] = jnp.zeros_like(ys_ref)


def moe_experts(xs, plan, w1, w3, w2, layer):
    p, d = xs.shape
    de = w1.shape[-1]
    ntile = p // MOE_TILE
    tile_map = lambda i, *_: (i, 0)
    grid_spec = pltpu.PrefetchScalarGridSpec(
        num_scalar_prefetch=5,
        grid=(ntile,),
        in_specs=[
            pl.BlockSpec((MOE_TILE, d), tile_map),
            pl.BlockSpec(memory_space=pl.ANY),
            pl.BlockSpec(memory_space=pl.ANY),
            pl.BlockSpec(memory_space=pl.ANY),
        ],
        out_specs=pl.BlockSpec((MOE_TILE, d), tile_map),
        scratch_shapes=[
            pltpu.VMEM((2, d, de), F32), pltpu.VMEM((2, d, de), F32), pltpu.VMEM((2, de, d), F32),
            pltpu.VMEM((d, de), BF16), pltpu.VMEM((d, de), BF16), pltpu.VMEM((de, d), BF16),
            pltpu.SemaphoreType.DMA((2, 3)),
        ],
    )
    return pl.pallas_call(
        functools.partial(_expert_kernel, layer=layer),
        grid_spec=grid_spec,
        out_shape=jax.ShapeDtypeStruct((p, d), BF16),
        compiler_params=_cparams(("arbitrary",)),
        name="moe_experts",
    )(plan["tile_expert"], plan["n_tiles"], plan["first"], plan["next_expert"], plan["parity"],
      xs, w1, w3, w2)


def moe_dispatch_plan(route, counts):
    t = route.shape[1]
    p = 2 * t + N_EXPERTS * MOE_TILE
    p = -(-p // MOE_TILE) * MOE_TILE
    ntile = p // MOE_TILE
    experts = jnp.arange(N_EXPERTS, dtype=jnp.int32)
    expert = route[0:2].astype(jnp.int32)
    rank = route[4:6].astype(jnp.int32)
    padded = ((counts + MOE_TILE - 1) // MOE_TILE) * MOE_TILE
    ends = jnp.cumsum(padded)
    starts = ends - padded
    start_of = jnp.sum(jnp.where(expert[..., None] == experts, starts, 0), axis=-1)
    slot = start_of + rank
    token = jnp.tile(jnp.arange(t, dtype=jnp.int32), 2)
    slot_token = jnp.zeros((p,), jnp.int32).at[slot.reshape(-1)].set(
        token, unique_indices=True, mode="promise_in_bounds")
    tile_id = jnp.arange(ntile, dtype=jnp.int32)
    tile_expert = jnp.sum((tile_id[:, None] * MOE_TILE >= ends[None, :]).astype(jnp.int32), axis=1)
    n_tiles = (ends[-1] // MOE_TILE).astype(jnp.int32)
    used = tile_id < n_tiles
    tile_expert = jnp.where(used, tile_expert, tile_expert[jnp.maximum(n_tiles - 1, 0)])
    tile_expert = jnp.minimum(tile_expert, N_EXPERTS - 1).astype(jnp.int32)
    prev = jnp.concatenate([jnp.full((1,), -1, jnp.int32), tile_expert[:-1]])
    first = (used & (tile_expert != prev)).astype(jnp.int32)
    parity = ((jnp.cumsum(first) - 1) % 2).astype(jnp.int32)
    cand = jnp.where(counts > 0, experts, N_EXPERTS)
    later = lax.cummin(cand[::-1])[::-1]
    nxt_e = jnp.concatenate([later[1:], jnp.full((1,), N_EXPERTS, jnp.int32)])
    nxt_e = jnp.where(nxt_e >= N_EXPERTS, -1, nxt_e).astype(jnp.int32)
    next_expert = nxt_e[tile_expert]
    return dict(slot_token=slot_token, slot=slot, tile_expert=tile_expert,
                n_tiles=n_tiles.reshape(1), first=first, next_expert=next_expert, parity=parity)


def _combine_kernel(x_ref, y0_ref, y1_ref, wt_ref, mod_ref, *refs, gate_idx, final_norm, n_ctx_tiles):
    moe = wt_ref[:, 0:1] * y0_ref[...].astype(F32) + wt_ref[:, 1:2] * y1_ref[...].astype(F32)
    x = x_ref[...] + mod_ref[0, gate_idx:gate_idx + 1, :] * moe
    if final_norm:
        gf_ref, ctx_ref, lat_ref = refs
        ms = jnp.mean(x * x, axis=-1, keepdims=True)
        y = x * lax.rsqrt(ms + EPS) * gf_ref[...]
        lat_ref[...] = y

        @pl.when(pl.program_id(0) < n_ctx_tiles)
        def _():
            ctx_ref[...] = y
    else:
        (out_ref,) = refs
        out_ref[...] = x


def moe_combine(x, y0, y1, wt, mod, gate_idx, t_ctx, l_lat, g_final=None):
    t, d = x.shape
    grp = functools.partial(_group_of_tile, tm=TM, t_ctx=t_ctx, l_lat=l_lat)
    row_spec = lambda w: pl.BlockSpec((TM, w), lambda i: (i, 0))
    in_specs = [row_spec(d), row_spec(d), row_spec(d), row_spec(wt.shape[1]),
                pl.BlockSpec((1,) + mod.shape[1:], lambda i: (grp(i), 0, 0))]
    args = [x, y0, y1, wt, mod]
    n_ctx_tiles = t_ctx // TM
    if g_final is not None:
        in_specs.append(_const_spec((1, d)))
        args.append(g_final.reshape(1, d).astype(F32))
        out_specs = [pl.BlockSpec((TM, d), lambda i: (jnp.minimum(i, n_ctx_tiles - 1), 0)),
                     pl.BlockSpec((TM, d), lambda i: (jnp.maximum(i - n_ctx_tiles, 0), 0))]
        out_shape = [jax.ShapeDtypeStruct((t_ctx, d), F32), jax.ShapeDtypeStruct((t - t_ctx, d), F32)]
    else:
        out_specs = row_spec(d)
        out_shape = jax.ShapeDtypeStruct((t, d), F32)
    return pl.pallas_call(
        functools.partial(_combine_kernel, gate_idx=gate_idx, final_norm=g_final is not None,
                          n_ctx_tiles=n_ctx_tiles),
        grid=(t // TM,),
        in_specs=in_specs,
        out_specs=out_specs,
        out_shape=out_shape,
        compiler_params=_cparams(("arbitrary",)),
        name="moe_combine",
    )(*args)


def moe_layer(x, g, mod, w_router, router_bias, w1, w3, w2, layer, t_ctx, l_lat, g_final=None):
    h, route, counts = moe_router(x, g, mod, 3, 4, w_router, router_bias, t_ctx, l_lat)
    plan = moe_dispatch_plan(route, counts[:, 0].astype(jnp.int32))
    take = lambda rows, index: rows.at[index].get(mode="promise_in_bounds")
    xs = take(h, plan["slot_token"])
    ys = moe_experts(xs, plan, w1, w3, w2, layer)
    y0 = take(ys, plan["slot"][0])
    y1 = take(ys, plan["slot"][1])
    return moe_combine(x, y0, y1, route[2:4].T, mod, 5, t_ctx, l_lat, g_final)


def _hy_conv_kernel(u_ref, up_ref, un_ref, cw_ref, cb_ref, x0_ref, vv_ref, *, t_ctx, l_ctx, l_lat):
    i = pl.program_id(0)
    has_prev, has_next = _seq_flags(i, TM, t_ctx, l_ctx, l_lat)
    c = _conv3(u_ref[...], up_ref[...], un_ref[...], cw_ref, has_prev, has_next) + cb_ref[...]
    d = c.shape[1] // 3
    x0_ref[...] = c[:, :d].astype(x0_ref.dtype)
    vv_ref[...] = (c[:, 2 * d:] * c[:, d:2 * d]).astype(vv_ref.dtype)


def hy_conv_gate(u, conv_w, conv_b, t_ctx, l_ctx, l_lat):
    t, c = u.shape
    d = c // 3
    cw = jnp.zeros((SUBLANES, c), F32).at[:conv_w.shape[0]].set(conv_w)
    row_spec = lambda w: pl.BlockSpec((TM, w), lambda i: (i, 0))
    return pl.pallas_call(
        functools.partial(_hy_conv_kernel, t_ctx=t_ctx, l_ctx=l_ctx, l_lat=l_lat),
        grid=(t // TM,),
        in_specs=_halo_specs(c, TM, t) + [_const_spec((SUBLANES, c)), _const_spec((1, c))],
        out_specs=[row_spec(d), row_spec(d)],
        out_shape=[jax.ShapeDtypeStruct((t, d), BF16), jax.ShapeDtypeStruct((t, d), BF16)],
        compiler_params=_cparams(("arbitrary",)),
        name="hy_conv_gate",
    )(u, u, u, cw, conv_b.reshape(1, c).astype(F32))


def _hy_features(n):
    t = np.linspace(0.0, 1.0, n)[:, None]
    bands = (HY_EMB - 1) // 2
    band_w = np.linspace(1e-4, bands - 1, bands)[None, :]
    ang = (2.0 * math.pi / n) * np.arange(n)[:, None] * band_w
    z = np.concatenate([t, np.cos(ang), -np.sin(ang)], axis=-1)
    zp = np.zeros((n, LANES))
    zp[:, :HY_EMB] = z
    return jnp.asarray(zp.astype(np.float32))


def _hy_hid_kernel(z_ref, f_ref, w1_ref, b1_ref, w2_ref, b2_ref, w3_ref, b3_ref, hid_ref):
    h = jnp.sin(f_ref[0:1, :] * (_dot(z_ref[...], w1_ref[...], HI) + b1_ref[...]))
    h = jnp.sin(f_ref[1:2, :] * (_dot(h, w2_ref[...], HI) + b2_ref[...]))
    h = jnp.sin(f_ref[2:3, :] * (_dot(h, w3_ref[...], HI) + b3_ref[...]))
    hid_ref[...] = h


def _hy_filt_kernel(hid_ref, w4_ref, z_ref, dl_ref, filt_ref):
    f = _dot(hid_ref[...], w4_ref[...], HI)
    f = f * jnp.exp(-z_ref[:, 0:1] * jnp.abs(dl_ref[...]))
    nrm = jnp.sum(jnp.abs(f), axis=0, keepdims=True)
    filt_ref[...] = f / nrm


def hy_filter(n, d, freq, w1, b1, w2, b2, w3, b3, w4):
    z = _hy_features(n)
    o = HY_ORDER
    w1p = jnp.zeros((LANES, o), F32).at[:HY_EMB].set(w1.astype(F32))
    fp = jnp.zeros((SUBLANES, o), F32).at[:3].set(freq.astype(F32))
    tr = min(n, TM)
    hid = pl.pallas_call(
        _hy_hid_kernel,
        grid=(n // tr,),
        in_specs=[pl.BlockSpec((tr, LANES), lambda i: (i, 0)), _const_spec((SUBLANES, o)),
                  _const_spec((LANES, o)), _const_spec((1, o)),
                  _const_spec((o, o)), _const_spec((1, o)),
                  _const_spec((o, o)), _const_spec((1, o))],
        out_specs=pl.BlockSpec((tr, o), lambda i: (i, 0)),
        out_shape=jax.ShapeDtypeStruct((n, o), F32),
        compiler_params=_cparams(("arbitrary",)),
        name="hy_filter_mlp",
    )(z, fp, w1p, b1.reshape(1, o).astype(F32), w2.astype(F32), b2.reshape(1, o).astype(F32),
      w3.astype(F32), b3.reshape(1, o).astype(F32))
    max_decay = math.log(HY_DECAY_TARGET) / HY_DECAY_SHORT_PCT
    min_decay = math.log(HY_DECAY_TARGET) / HY_DECAY_LONG_PCT
    deltas = np.tile(np.linspace(min_decay, max_decay, d), 2).astype(np.float32)[None, :]
    tc = 256
    return pl.pallas_call(
        _hy_filt_kernel,
        grid=(2 * d // tc,),
        in_specs=[_const_spec((n, o)), pl.BlockSpec((o, tc), lambda j: (0, j)),
                  _const_spec((n, LANES)), pl.BlockSpec((1, tc), lambda j: (0, j))],
        out_specs=pl.BlockSpec((n, tc), lambda j: (0, j)),
        out_shape=jax.ShapeDtypeStruct((n, 2 * d), F32),
        compiler_params=_cparams(("arbitrary",)),
        name="hy_filter_window",
    )(hid, w4.astype(F32), z, jnp.asarray(deltas))


def _dense_dft_tables(l):
    n = 2 * l
    k = np.arange(n)[:, None]
    j = np.arange(l)[None, :]
    th = 2.0 * math.pi * ((k * j) % n) / n
    fwd = np.concatenate([np.cos(th), -np.sin(th)], axis=0)
    inv = np.concatenate([np.cos(th).T, -np.sin(th).T], axis=1) / n
    return (jnp.asarray(fwd.astype(np.float32)).astype(BF16),
            jnp.asarray(inv.astype(np.float32)).astype(BF16))


def _dense_tf_kernel(hf_ref, hb_ref, fwd_ref, tf_ref):
    n = fwd_ref.shape[0] // 2
    hf = hf_ref[...]
    hb = hb_ref[...]
    tf_ref[0:n, :] = _dot(fwd_ref[0:n, :], (hf + hb).astype(BF16))
    tf_ref[n:, :] = _dot(fwd_ref[n:, :], (hf - hb).astype(BF16))


def _dense_conv_kernel(vv_ref, x0_ref, skip_ref, tf_ref, fwd_ref, inv_ref, out_ref):
    n = fwd_ref.shape[0] // 2
    vv = vv_ref[...]
    x = _dot(fwd_ref[...], vv.astype(BF16))
    xr, xi = x[:n], x[n:]
    tr, ti = tf_ref[0:n, :], tf_ref[n:, :]
    y = jnp.concatenate([xr * tr - xi * ti, xr * ti + xi * tr], axis=0).astype(BF16)
    conv = _dot(inv_ref[...], y)
    out = (conv + vv.astype(F32) * skip_ref[...]) * x0_ref[...].astype(F32)
    out_ref[...] = out.astype(out_ref.dtype)


def hy_longconv_dense(vv, x0, skip, filt, n_seq, l):
    d = vv.shape[1]
    n = 2 * l
    fwd, inv = _dense_dft_tables(l)
    dc = 512
    tf = pl.pallas_call(
        _dense_tf_kernel,
        grid=(d // dc,),
        in_specs=[pl.BlockSpec((l, dc), lambda j: (0, j)),
                  pl.BlockSpec((l, dc), lambda j: (0, d // dc + j)),
                  _const_spec(fwd.shape)],
        out_specs=pl.BlockSpec((2 * n, dc), lambda j: (0, j)),
        out_shape=jax.ShapeDtypeStruct((2 * n, d), F32),
        compiler_params=_cparams(("arbitrary",)),
        name="hy_tf_dense",
    )(filt, filt, fwd)
    return pl.pallas_call(
        _dense_conv_kernel,
        grid=(d // dc, n_seq),
        in_specs=[pl.BlockSpec((l, dc), lambda j, s: (s, j)),
                  pl.BlockSpec((l, dc), lambda j, s: (s, j)),
                  pl.BlockSpec((1, dc), lambda j, s: (0, j)),
                  pl.BlockSpec((2 * n, dc), lambda j, s: (0, j)),
                  _const_spec(fwd.shape), _const_spec(inv.shape)],
        out_specs=pl.BlockSpec((l, dc), lambda j, s: (s, j)),
        out_shape=jax.ShapeDtypeStruct((n_seq * l, d), BF16),
        compiler_params=_cparams(("arbitrary", "arbitrary")),
        name="hy_longconv_dense",
    )(vv, x0, skip.reshape(1, d).astype(F32), tf, fwd, inv)


def _two_stage_tables(l):
    n = 2 * l
    n2 = FFT_N2
    n1_full = n // n2
    n_slab = n1_full // 2 + 1
    n1 = -(-n_slab // FFT_UNROLL_SLAB) * FFT_UNROLL_SLAB
    p = l // n2
    a = np.arange(p)[None, None, :]
    c = np.arange(n1)[None, :, None]
    b = np.arange(n2)[:, None, None]
    live = (c < n_slab).astype(np.float64)
    weight = np.where((c == 0) | (c == n1_full // 2), 1.0, 2.0) * live
    ph = 2.0 * math.pi * (((a * c) % n1_full) / n1_full + ((b * c) % n) / n)
    f1 = np.concatenate([np.cos(ph) * live, -np.sin(ph) * live], axis=1)
    pht = np.transpose(ph, (0, 2, 1))
    wt = np.transpose(weight, (0, 2, 1)) / n
    g1 = np.concatenate([np.cos(pht) * wt, -np.sin(pht) * wt], axis=2)
    e = np.arange(n2)[:, None]
    bb = np.arange(n2)[None, :]
    th = 2.0 * math.pi * ((e * bb) % n2) / n2
    fr, fi = np.cos(th), -np.sin(th)
    f2 = np.block([[fr, -fi], [fi, fr]])
    f2c = np.block([[fr, fi], [-fi, fr]])
    cast = lambda m: jnp.asarray(m.astype(np.float32)).astype(BF16)
    return cast(f1), cast(f2), cast(f2c), cast(g1)


def _unrolled_loop(n, unroll, fn):
    def body(i, carry):
        fn([i * unroll + u for u in range(unroll)])
        return carry

    lax.fori_loop(0, n // unroll, body, 0)


FFT_UNROLL_SMALL = 16
FFT_UNROLL_SLAB = 4


def _stage1(src_ref, f1_ref, w_scr, combine=None):
    n2, two_n1, p = f1_ref.shape

    def step(bs):
        if combine is None:
            xs = [src_ref[pl.ds(b, p, stride=n2), :] for b in bs]
        else:
            xs = [combine(src_ref[0][pl.ds(b, p, stride=n2), :], src_ref[1][pl.ds(b, p, stride=n2), :])
                  for b in bs]
        res = [_dot(f1_ref[b], x.astype(BF16)) for b, x in zip(bs, xs)]
        for b, r in zip(bs, res):
            w_scr[pl.ds(pl.multiple_of(b * two_n1, two_n1), two_n1), :] = r

    _unrolled_loop(n2, FFT_UNROLL_SMALL, step)


def _load_slab(w_scr, c, n1, n2):
    re = w_scr[pl.ds(c, n2, stride=2 * n1), :]
    im = w_scr[pl.ds(n1 + c, n2, stride=2 * n1), :]
    return jnp.concatenate([re, im], axis=0).astype(BF16)


def _two_stage_tf_kernel(hf_ref, hb_ref, f1_ref, f2_ref, tfr_ref, tfi_ref, ws_scr, wd_scr):
    n2, two_n1, _ = f1_ref.shape
    n1 = two_n1 // 2
    _stage1((hf_ref, hb_ref), f1_ref, ws_scr, combine=lambda u, v: u + v)
    _stage1((hf_ref, hb_ref), f1_ref, wd_scr, combine=lambda u, v: u - v)

    def step(cs):
        xs = [_dot(f2_ref[0:n2, :], _load_slab(ws_scr, c, n1, n2)) for c in cs]
        xd = [_dot(f2_ref[n2:, :], _load_slab(wd_scr, c, n1, n2)) for c in cs]
        for c, a, b in zip(cs, xs, xd):
            r0 = pl.multiple_of(c * n2, n2)
            tfr_ref[pl.ds(r0, n2), :] = a
            tfi_ref[pl.ds(r0, n2), :] = b

    _unrolled_loop(n1, FFT_UNROLL_SLAB, step)


def _two_stage_conv_kernel(vv_ref, x0_ref, skip_ref, tfr_ref, tfi_ref,
                           f1_ref, f2_ref, f2c_ref, g1_ref, out_ref, w_scr, t_scr):
    n2, two_n1, p = f1_ref.shape
    n1 = two_n1 // 2
    t_scr[...] = vv_ref[...].astype(F32)
    _stage1(t_scr, f1_ref, w_scr)

    def step2(cs):
        xs = [_dot(f2_ref[...], _load_slab(w_scr, c, n1, n2)) for c in cs]
        ys = []
        for c, x in zip(cs, xs):
            r0 = pl.multiple_of(c * n2, n2)
            xr, xi = x[:n2], x[n2:]
            tr = tfr_ref[pl.ds(r0, n2), :]
            ti = tfi_ref[pl.ds(r0, n2), :]
            ys.append(jnp.concatenate([xr * tr - xi * ti, xr * ti + xi * tr], axis=0).astype(BF16))
        zs = [_dot(f2c_ref[...], y) for y in ys]
        for c, z in zip(cs, zs):
            w_scr[pl.ds(c, n2, stride=two_n1), :] = z[:n2]
            w_scr[pl.ds(n1 + c, n2, stride=two_n1), :] = z[n2:]

    _unrolled_loop(n1, FFT_UNROLL_SLAB, step2)

    def step3(bs):
        zb = [w_scr[pl.ds(pl.multiple_of(b * two_n1, two_n1), two_n1), :].astype(BF16) for b in bs]
        res = [_dot(g1_ref[b], z) for b, z in zip(bs, zb)]
        for b, r in zip(bs, res):
            t_scr[pl.ds(b, p, stride=n2), :] = r

    _unrolled_loop(n2, FFT_UNROLL_SMALL, step3)
    out = (t_scr[...] + vv_ref[...].astype(F32) * skip_ref[...]) * x0_ref[...].astype(F32)
    out_ref[...] = out.astype(out_ref.dtype)


def hy_longconv_two_stage(vv, x0, skip, filt, row0, n_seq, l):
    d = vv.shape[1]
    f1, f2, f2c, g1 = _two_stage_tables(l)
    n2, two_n1, p = f1.shape
    n = (two_n1 // 2) * n2
    dc = LANES
    tfr, tfi = pl.pallas_call(
        _two_stage_tf_kernel,
        grid=(d // dc,),
        in_specs=[pl.BlockSpec((l, dc), lambda j: (0, j)),
                  pl.BlockSpec((l, dc), lambda j: (0, d // dc + j)),
                  _const_spec(f1.shape), _const_spec(f2.shape)],
        out_specs=[pl.BlockSpec((n, dc), lambda j: (0, j)), pl.BlockSpec((n, dc), lambda j: (0, j))],
        out_shape=[jax.ShapeDtypeStruct((n, d), F32), jax.ShapeDtypeStruct((n, d), F32)],
        scratch_shapes=[pltpu.VMEM((n2 * two_n1, dc), F32), pltpu.VMEM((n2 * two_n1, dc), F32)],
        compiler_params=_cparams(("arbitrary",)),
        name="hy_tf_two_stage",
    )(filt, filt, f1, f2)
    base = row0 // l
    return pl.pallas_call(
        _two_stage_conv_kernel,
        grid=(d // dc, n_seq),
        in_specs=[pl.BlockSpec((l, dc), lambda j, s: (base + s, j)),
                  pl.BlockSpec((l, dc), lambda j, s: (base + s, j)),
                  pl.BlockSpec((1, dc), lambda j, s: (0, j)),
                  pl.BlockSpec((n, dc), lambda j, s: (0, j)),
                  pl.BlockSpec((n, dc), lambda j, s: (0, j)),
                  _const_spec(f1.shape), _const_spec(f2.shape), _const_spec(f2c.shape),
                  _const_spec(g1.shape)],
        out_specs=pl.BlockSpec((l, dc), lambda j, s: (s, j)),
        out_shape=jax.ShapeDtypeStruct((n_seq * l, d), BF16),
        scratch_shapes=[pltpu.VMEM((n2 * two_n1, dc), F32), pltpu.VMEM((l, dc), F32)],
        compiler_params=_cparams(("arbitrary", "arbitrary")),
        name="hy_longconv_two_stage",
    )(vv, x0, skip.reshape(1, d).astype(F32), tfr, tfi, f1, f2, f2c, g1)


def _hy_out_kernel(yc_ref, yl_ref, w_ref, b_ref, x_ref, mod_ref, out_ref, *, gate_idx, n_ctx_tiles):
    yg = jnp.where(pl.program_id(0) < n_ctx_tiles, yc_ref[...], yl_ref[...])
    y = _dot(yg.astype(BF16), w_ref[...]) + b_ref[...]
    out_ref[...] = x_ref[...] + mod_ref[0, gate_idx:gate_idx + 1, :] * y


def hy_out(y_ctx, y_lat, w_out, b_out, x, mod, gate_idx, t_ctx, l_lat):
    t, d = x.shape
    grp = functools.partial(_group_of_tile, tm=TM, t_ctx=t_ctx, l_lat=l_lat)
    row_spec = lambda w: pl.BlockSpec((TM, w), lambda i: (i, 0))
    n_ctx_tiles = t_ctx // TM
    return pl.pallas_call(
        functools.partial(_hy_out_kernel, gate_idx=gate_idx, n_ctx_tiles=n_ctx_tiles),
        grid=(t // TM,),
        in_specs=[pl.BlockSpec((TM, d), lambda i: (jnp.minimum(i, n_ctx_tiles - 1), 0)),
                  pl.BlockSpec((TM, d), lambda i: (jnp.maximum(i - n_ctx_tiles, 0), 0)),
                  _const_spec(w_out.shape), _const_spec((1, d)), row_spec(d),
                  pl.BlockSpec((1,) + mod.shape[1:], lambda i: (grp(i), 0, 0))],
        out_specs=row_spec(d),
        out_shape=jax.ShapeDtypeStruct((t, d), F32),
        compiler_params=_cparams(("arbitrary",)),
        name="hy_out",
    )(y_ctx, y_lat, w_out, b_out.reshape(1, d).astype(F32), x, mod)


def _grid_pos_embed(rows, d):
    r = np.repeat(np.arange(rows, dtype=np.float64), GRID_W)
    col = np.tile(np.arange(GRID_W, dtype=np.float64), rows)
    quarter = d // 4
    omega = 1.0 / (10000.0 ** (np.arange(quarter, dtype=np.float64) / quarter))
    ang_r = r[:, None] * omega
    ang_c = col[:, None] * omega
    pe = np.concatenate([np.sin(ang_r), np.cos(ang_r), np.sin(ang_c), np.cos(ang_c)], axis=-1)
    return jnp.asarray(pe.astype(np.float32))


def delta_layer(x, g, mod, state_lat, w_in, conv_w, a_log, dt_bias, g_head, w_out, dims):
    t_ctx, l_ctx, n_ctx, l_lat, n_lat = dims
    hk = DN_HEADS * DN_DK
    nqkv = 3 * hk
    w_qkv = w_in[:, :nqkv].astype(BF16)
    w_z = w_in[:, nqkv:nqkv + hk].astype(BF16)
    w_ab = jnp.zeros((w_in.shape[0], LANES), BF16).at[:, :4 * DN_HEADS].set(w_in[:, nqkv + hk:].astype(BF16))
    qkv, z, ab = norm_mod_matmul(x, g, mod, 0, 1, [w_qkv, w_z, w_ab], None, [BF16, BF16, F32],
                                 t_ctx, l_lat)
    q, k, v, gc, beta, gct = dn_prep(qkv, ab, conv_w.astype(F32), a_log, dt_bias, t_ctx, l_ctx, l_lat)
    outs = []
    ctx_states = []
    seq_lens = (l_ctx,) * n_ctx + (l_lat,) * n_lat
    for direction in range(2):
        o, s_all = dn_chunk_scan(q, k, v, gc, beta, gct, state_lat.astype(F32), direction=direction,
                                 seq_lens=seq_lens, n_zero_init=n_ctx)
        outs.append(o)
        ctx_states.append(s_all[:n_ctx])
    x = dn_out(outs[0], outs[1], z, g_head, w_out.astype(BF16), x, mod, 2, t_ctx, l_lat)
    return x, jnp.stack(ctx_states, axis=1)


def hyena_layer(x, g, mod, w_in, b_in, conv_w, conv_b, freq, fw1, fb1, fw2, fb2, fw3, fb3, fw4,
                skip, w_out, b_out, dims):
    t_ctx, l_ctx, n_ctx, l_lat, n_lat = dims
    d = x.shape[1]
    (u,) = norm_mod_matmul(x, g, mod, 0, 1, [w_in.astype(BF16)], [b_in.astype(F32)], [BF16],
                           t_ctx, l_lat)
    x0, vv = hy_conv_gate(u, conv_w.astype(F32), conv_b, t_ctx, l_ctx, l_lat)
    filt_c = hy_filter(l_ctx, d, freq, fw1, fb1, fw2, fb2, fw3, fb3, fw4)
    filt_l = hy_filter(l_lat, d, freq, fw1, fb1, fw2, fb2, fw3, fb3, fw4)
    y_c = hy_longconv_dense(vv, x0, skip, filt_c, n_ctx, l_ctx)
    y_l = hy_longconv_two_stage(vv, x0, skip, filt_l, t_ctx, n_lat, l_lat)
    return hy_out(y_c, y_l, w_out.astype(BF16), b_out, x, mod, 2, t_ctx, l_lat)


def kernel(x_prompt, x_sample, state_delta, c, c_ctx, w_ada, b_ada, g_norm, dn_w_in, dn_conv, dn_a_log, dn_dt_bias, dn_g_head, dn_w_out, hy_w_in, hy_b_in, hy_conv, hy_conv_b, hy_freq, hy_f_w1, hy_f_b1, hy_f_w2, hy_f_b2, hy_f_w3, hy_f_b3, hy_f_w4, hy_skip, hy_w_out, hy_b_out, w_router, router_bias, moe_w1, moe_w3, moe_w2, g_final):
    n_ctx, l_ctx, d = x_prompt.shape
    n_lat, l_lat, _ = x_sample.shape
    depth = w_ada.shape[0]
    t_ctx = n_ctx * l_ctx
    dims = (t_ctx, l_ctx, n_ctx, l_lat, n_lat)
    assert l_ctx % TM == 0 and l_lat % TM == 0 and t_ctx % l_lat == 0
    assert n_lat + 1 <= SUBLANES

    pos = _grid_pos_embed(l_lat // GRID_W, d)
    x = jnp.concatenate([x_prompt.reshape(t_ctx, d).astype(F32),
                         (x_sample.astype(F32) + pos[None]).reshape(n_lat * l_lat, d)], axis=0)
    cond = jnp.zeros((SUBLANES, d), F32).at[0].set(c_ctx.astype(F32)).at[1:1 + n_lat].set(c.astype(F32))
    mod_all = ada_modulation(cond, w_ada.astype(F32), b_ada.astype(F32))
    mod_all = mod_all.reshape(depth, SUBLANES, 6, d)

    ctx_states = []
    for i in range(depth):
        mod = mod_all[i]
        j = i // 2
        if i % 2 == 0:
            x, s_ctx = delta_layer(x, g_norm[i, 0], mod, state_delta[:, j], dn_w_in[j], dn_conv[j],
                                   dn_a_log[j], dn_dt_bias[j], dn_g_head[j], dn_w_out[j], dims)
            ctx_states.append(s_ctx.astype(x_prompt.dtype))
        else:
            x = hyena_layer(x, g_norm[i, 0], mod, hy_w_in[j], hy_b_in[j], hy_conv[j], hy_conv_b[j],
                            hy_freq[j], hy_f_w1[j], hy_f_b1[j], hy_f_w2[j], hy_f_b2[j], hy_f_w3[j],
                            hy_f_b3[j], hy_f_w4[j], hy_skip[j], hy_w_out[j], hy_b_out[j], dims)
        x = moe_layer(x, g_norm[i, 1], mod, w_router, router_bias, moe_w1, moe_w3, moe_w2, i,
                      t_ctx, l_lat, g_final if i == depth - 1 else None)
    y_ctx, y_lat = x
    y_prompt = y_ctx.reshape(n_ctx, l_ctx, d).astype(x_prompt.dtype)
    y_sample = y_lat.reshape(n_lat, l_lat, d).astype(x_sample.dtype)
    new_state = jnp.stack(ctx_states, axis=1)
    return (y_prompt, y_sample, new_state)
```

```python
import functools
import math

import numpy as np
import jax
import jax.numpy as jnp
from jax import lax
from jax.experimental import pallas as pl
from jax.experimental.pallas import tpu as pltpu

F32 = jnp.float32
BF16 = jnp.bfloat16
HI = lax.Precision.HIGHEST

EPS = 1e-6
GRID_W = 64
DN_HEADS = 8
DN_DK = 128
DN_DV = 128
DN_CHUNK = 64
HY_EMB = 33
HY_ORDER = 64
HY_DECAY_SHORT_PCT = 0.3
HY_DECAY_LONG_PCT = 1.5
HY_DECAY_TARGET = 1e-2
N_EXPERTS = 32
N_GROUPS = 4
EXPERTS_PER_GROUP = N_EXPERTS // N_GROUPS

LANES = 128
SUBLANES = 8
TM = 256
HALO_ROWS = 16
MOE_TILE = 256
FFT_N2 = 128
DN_PREP_CHUNKS = 4
VMEM_LIMIT = 48 * 1024 * 1024


def _cparams(sem):
    return pltpu.CompilerParams(dimension_semantics=sem, vmem_limit_bytes=VMEM_LIMIT)


def _sigmoid(x):
    return 1.0 / (1.0 + jnp.exp(-x))


def _silu(x):
    return x * _sigmoid(x)


def _dot(a, b, precision=None):
    return jnp.dot(a, b, preferred_element_type=F32, precision=precision)


def _const_spec(shape):
    nd = len(shape)
    return pl.BlockSpec(shape, lambda *_: (0,) * nd)


def _ada_kernel(cond_ref, w_ref, b_ref, out_ref):
    s = _silu(cond_ref[...])
    out_ref[0] = _dot(s, w_ref[0], HI) + b_ref[0]


def ada_modulation(cond, w_ada, b_ada):
    depth, d, n = w_ada.shape
    tn = 1536
    return pl.pallas_call(
        _ada_kernel,
        grid=(depth, n // tn),
        in_specs=[
            pl.BlockSpec((SUBLANES, d), lambda i, j: (0, 0)),
            pl.BlockSpec((1, d, tn), lambda i, j: (i, 0, j)),
            pl.BlockSpec((1, 1, tn), lambda i, j: (i, 0, j)),
        ],
        out_specs=pl.BlockSpec((1, SUBLANES, tn), lambda i, j: (i, 0, j)),
        out_shape=jax.ShapeDtypeStruct((depth, SUBLANES, n), F32),
        compiler_params=_cparams(("arbitrary", "arbitrary")),
        name="ada_modulation",
    )(cond, w_ada, b_ada.reshape(depth, 1, n))


def _norm_mod(x, g, mod_ref, shift_idx, scale_idx):
    ms = jnp.mean(x * x, axis=-1, keepdims=True)
    y = x * lax.rsqrt(ms + EPS) * g
    scale = mod_ref[0, scale_idx:scale_idx + 1, :]
    shift = mod_ref[0, shift_idx:shift_idx + 1, :]
    return y * (1.0 + scale) + shift


def _group_of_tile(i, tm, t_ctx, l_lat):
    r = i * tm
    return jnp.where(r < t_ctx, 0, 1 + (r - t_ctx) // l_lat)


def _stream_operands(x, t_ctx, l_lat):
    if not isinstance(x, tuple):
        t, d = x.shape
        return t, d, [pl.BlockSpec((TM, d), lambda i: (i, 0))], [x]
    xc, xl, pos = x
    d = xc.shape[1]
    nct = t_ctx // TM
    per = l_lat // TM
    specs = [pl.BlockSpec((TM, d), lambda i: (jnp.minimum(i, nct - 1), 0)),
             pl.BlockSpec((TM, d), lambda i: (jnp.maximum(i - nct, 0), 0)),
             pl.BlockSpec((TM, d), lambda i: (jnp.maximum(i - nct, 0) % per, 0))]
    return xc.shape[0] + xl.shape[0], d, specs, [xc, xl, pos]


def _load_stream(x_refs, n_ctx_tiles):
    if len(x_refs) == 1:
        return x_refs[0][...]
    xc_ref, xl_ref, pos_ref = x_refs
    return jnp.where(pl.program_id(0) < n_ctx_tiles, xc_ref[...], xl_ref[...] + pos_ref[...])


def _nmm_kernel(*refs, n_x, n_ctx_tiles, n_out, has_bias, shift_idx, scale_idx):
    x = _load_stream(refs[:n_x], n_ctx_tiles)
    g_ref, mod_ref = refs[n_x:n_x + 2]
    refs = refs[n_x + 2:]
    h = _norm_mod(x, g_ref[...], mod_ref, shift_idx, scale_idx).astype(BF16)
    w_refs = refs[:n_out]
    b_refs = refs[n_out:n_out + (n_out if has_bias else 0)]
    o_refs = refs[n_out + len(b_refs):]
    for k in range(n_out):
        acc = _dot(h, w_refs[k][...])
        if has_bias:
            acc = acc + b_refs[k][...]
        o_refs[k][...] = acc.astype(o_refs[k].dtype)


def norm_mod_matmul(x, g, mod, shift_idx, scale_idx, weights, biases, out_dtypes, t_ctx, l_lat):
    t, d, x_specs, x_args = _stream_operands(x, t_ctx, l_lat)
    n_out = len(weights)
    has_bias = biases is not None
    grp = functools.partial(_group_of_tile, tm=TM, t_ctx=t_ctx, l_lat=l_lat)
    in_specs = x_specs + [
        _const_spec((1, d)),
        pl.BlockSpec((1,) + mod.shape[1:], lambda i: (grp(i), 0, 0)),
    ]
    args = x_args + [g.reshape(1, d), mod]
    for w in weights:
        in_specs.append(_const_spec(w.shape))
        args.append(w)
    if has_bias:
        for b in biases:
            in_specs.append(_const_spec((1, b.shape[-1])))
            args.append(b.reshape(1, -1))
    out_specs = [pl.BlockSpec((TM, w.shape[1]), lambda i: (i, 0)) for w in weights]
    out_shape = [jax.ShapeDtypeStruct((t, w.shape[1]), dt) for w, dt in zip(weights, out_dtypes)]
    return pl.pallas_call(
        functools.partial(_nmm_kernel, n_x=len(x_args), n_ctx_tiles=t_ctx // TM, n_out=n_out,
                          has_bias=has_bias, shift_idx=shift_idx, scale_idx=scale_idx),
        grid=(t // TM,),
        in_specs=in_specs,
        out_specs=out_specs,
        out_shape=out_shape,
        compiler_params=_cparams(("arbitrary",)),
        name="norm_mod_matmul",
    )(*args)


def _seq_flags(i, tm, t_ctx, l_ctx, l_lat):
    r = i * tm
    in_ctx = r < t_ctx
    pos = jnp.where(in_ctx, r % l_ctx, (r - t_ctx) % l_lat)
    length = jnp.where(in_ctx, l_ctx, l_lat)
    return pos != 0, (pos + tm) != length


def _conv3(x, prev_rows, next_rows, w_ref, has_prev, has_next):
    tm = x.shape[0]
    x = x.astype(F32)
    halo_p = jnp.where(has_prev, prev_rows[HALO_ROWS - 1:HALO_ROWS, :].astype(F32), 0.0)
    halo_n = jnp.where(has_next, next_rows[0:1, :].astype(F32), 0.0)
    w0, w1, w2 = w_ref[0:1, :], w_ref[1:2, :], w_ref[2:3, :]
    c = w0 * pltpu.roll(x, 1, 0) + w1 * x + w2 * pltpu.roll(x, tm - 1, 0)
    row = lax.broadcasted_iota(jnp.int32, (SUBLANES, 1), 0)
    top = c[:SUBLANES] + jnp.where(row == 0, w0 * (halo_p - x[tm - 1:tm]), 0.0)
    bot = c[tm - SUBLANES:] + jnp.where(row == SUBLANES - 1, w2 * (halo_n - x[0:1]), 0.0)
    return jnp.concatenate([top, c[SUBLANES:tm - SUBLANES], bot], axis=0)


def _halo_specs(c, tm, t):
    nblk = t // HALO_ROWS
    per = tm // HALO_ROWS
    return [
        pl.BlockSpec((tm, c), lambda i: (i, 0)),
        pl.BlockSpec((HALO_ROWS, c), lambda i: (jnp.maximum(i * per - 1, 0), 0)),
        pl.BlockSpec((HALO_ROWS, c), lambda i: (jnp.minimum((i + 1) * per, nblk - 1), 0)),
    ]


def _softplus(x):
    return jnp.maximum(x, 0.0) + jnp.log(1.0 + jnp.exp(-jnp.abs(x)))


def _dn_prep_kernel(x_ref, xp_ref, xn_ref, cw_ref, ab_ref, nega_ref, dtb_ref, tri_ref,
                    q_ref, k_ref, v_ref, gc_ref, beta_ref, gct_ref,
                    *, t_ctx, l_ctx, l_lat):
    i = pl.program_id(0)
    has_prev, has_next = _seq_flags(i, TM, t_ctx, l_ctx, l_lat)
    c = _conv3(x_ref[...], xp_ref[...], xn_ref[...], cw_ref, has_prev, has_next)
    s = _silu(c)
    hk = DN_HEADS * DN_DK
    for h in range(DN_HEADS):
        lo, hi = h * DN_DK, (h + 1) * DN_DK
        qh = s[:, lo:hi]
        kh = s[:, hk + lo:hk + hi]
        qn = lax.rsqrt(jnp.sum(qh * qh, axis=-1, keepdims=True) + EPS)
        kn = lax.rsqrt(jnp.sum(kh * kh, axis=-1, keepdims=True) + EPS)
        q_ref[:, lo:hi] = (qh * (qn * (DN_DK ** -0.5))).astype(BF16)
        k_ref[:, lo:hi] = (kh * kn).astype(BF16)
    v_ref[...] = s[:, 2 * hk:].astype(BF16)
    ab = ab_ref[...]
    lg = nega_ref[...] * _softplus(ab + dtb_ref[...])
    beta_ref[...] = _sigmoid(ab)
    nch = TM // DN_CHUNK
    for d in range(2):
        gc = _dot(tri_ref[d], lg, HI)
        gc_ref[d] = gc
        gct = gc.T
        for ch in range(nch):
            gct_ref[d, ch] = gct[0:2 * DN_HEADS, ch * DN_CHUNK:(ch + 1) * DN_CHUNK]


def _chunk_tri():
    idx = np.arange(TM)
    same = (idx[:, None] // DN_CHUNK) == (idx[None, :] // DN_CHUNK)
    fwd = same & (idx[None, :] <= idx[:, None])
    bwd = same & (idx[None, :] >= idx[:, None])
    return jnp.asarray(np.stack([fwd, bwd]).astype(np.float32))


def dn_prep(qkv, ab, conv_w, a_log, dt_bias, t_ctx, l_ctx, l_lat):
    t, c = qkv.shape
    hd = DN_HEADS * DN_DK
    nega = jnp.zeros((1, LANES), F32).at[0, :2 * DN_HEADS].set(-jnp.exp(a_log.astype(F32)).reshape(-1))
    dtb = jnp.zeros((1, LANES), F32).at[0, :2 * DN_HEADS].set(dt_bias.astype(F32).reshape(-1))
    cw = jnp.zeros((SUBLANES, c), F32).at[:conv_w.shape[0]].set(conv_w)
    nch = TM // DN_CHUNK
    row_spec = lambda w: pl.BlockSpec((TM, w), lambda i: (i, 0))
    return pl.pallas_call(
        functools.partial(_dn_prep_kernel, t_ctx=t_ctx, l_ctx=l_ctx, l_lat=l_lat),
        grid=(t // TM,),
        in_specs=_halo_specs(c, TM, t) + [
            _const_spec((SUBLANES, c)),
            row_spec(LANES),
            _const_spec((1, LANES)),
            _const_spec((1, LANES)),
            _const_spec((2, TM, TM)),
        ],
        out_specs=[
            row_spec(hd), row_spec(hd), row_spec(hd),
            pl.BlockSpec((2, TM, LANES), lambda i: (0, i, 0)),
            row_spec(LANES),
            pl.BlockSpec((2, nch, 2 * DN_HEADS, DN_CHUNK), lambda i: (0, i, 0, 0)),
        ],
        out_shape=[
            jax.ShapeDtypeStruct((t, hd), BF16),
            jax.ShapeDtypeStruct((t, hd), BF16),
            jax.ShapeDtypeStruct((t, hd), BF16),
            jax.ShapeDtypeStruct((2, t, LANES), F32),
            jax.ShapeDtypeStruct((t, LANES), F32),
            jax.ShapeDtypeStruct((2, t // DN_CHUNK, 2 * DN_HEADS, DN_CHUNK), F32),
        ],
        compiler_params=_cparams(("arbitrary",)),
        name="dn_prep",
    )(qkv, qkv, qkv, cw, ab, nega, dtb, _chunk_tri())


def _bmm(a, b):
    return _dot(a.astype(BF16), b.astype(BF16))


def _inv_unit_tri_batch(lms, in_blk, eye):
    dgs = [jnp.where(in_blk, lm, 0.0) for lm in lms]
    offs = [lm - dg for lm, dg in zip(lms, dgs)]
    n1 = [-dg for dg in dgs]
    n2 = [_bmm(a, a) for a in n1]
    p = [eye + a for a in n1]
    n4 = [_bmm(a, a) for a in n2]
    p = [x + _bmm(x, a) for x, a in zip(p, n2)]
    n8 = [_bmm(a, a) for a in n4]
    p = [x + _bmm(x, a) for x, a in zip(p, n4)]
    p = [x + _bmm(x, a) for x, a in zip(p, n8)]
    m = [_bmm(x, o) for x, o in zip(p, offs)]
    m2 = [_bmm(a, a) for a in m]
    t1 = [eye - a for a in m]
    t1 = [x + _bmm(x, a) for x, a in zip(t1, m2)]
    return [_bmm(x, y) for x, y in zip(t1, p)]


def _dn_chunk_kernel(blk_ref, seq_ref, first_ref, last_ref,
                     q_ref, k_ref, v_ref, gc_ref, beta_ref, gct_ref, s0_ref,
                     o_ref, sfin_ref, s_scr, attn_scr, u_scr, qw_scr, kd_scr, gl_scr,
                     *, direction, n_zero_init):
    step = pl.program_id(0)
    c_sz = DN_CHUNK
    nch = TM // c_sz
    heads = range(DN_HEADS)

    @pl.when(first_ref[step] == 1)
    def _():
        s_scr[...] = jnp.where(seq_ref[step] >= n_zero_init, s0_ref[0, 0], 0.0)

    row = lax.broadcasted_iota(jnp.int32, (c_sz, c_sz), 0)
    col = lax.broadcasted_iota(jnp.int32, (c_sz, c_sz), 1)
    if direction == 0:
        incl, strict = row >= col, row > col
        last = c_sz - 1
    else:
        incl, strict = row <= col, row < col
        last = 0
    in_blk = (row // 16) == (col // 16)
    eye = (row == col).astype(F32)

    def lanes(h):
        return slice(h * DN_DK, (h + 1) * DN_DK)

    def prep_body(it, carry):
        units = [(it * DN_PREP_CHUNKS + c, h) for c in range(DN_PREP_CHUNKS) for h in heads]
        idx = range(len(units))
        rows = [pl.ds(pl.multiple_of(ch * c_sz, c_sz), c_sz) for ch, _ in units]
        gls = [direction * DN_HEADS + h for _, h in units]
        qs = [q_ref[rows[i], lanes(units[i][1])].astype(F32) for i in idx]
        ks = [k_ref[rows[i], lanes(units[i][1])].astype(F32) for i in idx]
        gcc = [gc_ref[0, rows[i], gls[i]:gls[i] + 1] for i in idx]
        beta = [beta_ref[rows[i], 2 * DN_HEADS + gls[i]:2 * DN_HEADS + gls[i] + 1] for i in idx]
        gcr = [gct_ref[0, units[i][0]][gls[i]:gls[i] + 1, :] for i in idx]
        dec = [jnp.exp(jnp.where(incl, gcc[i] - gcr[i], -jnp.inf)) for i in idx]
        kb = [ks[i] * beta[i] for i in idx]
        prod = [lax.dot_general(jnp.concatenate([qs[i], kb[i]], axis=0).astype(BF16), ks[i].astype(BF16),
                                (((1,), (1,)), ((), ())), preferred_element_type=F32)
                for i in idx]
        for i, (ch, h) in enumerate(units):
            attn_scr[ch, h] = (prod[i][:c_sz] * dec[i]).astype(BF16)
        lms = [jnp.where(strict, prod[i][c_sz:] * dec[i], 0.0) for i in idx]
        tinv = _inv_unit_tri_batch(lms, in_blk, eye)
        eg = [jnp.exp(gcc[i]) for i in idx]
        uw = [_bmm(tinv[i], jnp.concatenate(
                  [v_ref[rows[i], lanes(units[i][1])].astype(F32) * beta[i], kb[i] * eg[i]], axis=1))
              for i in idx]
        for i, (ch, h) in enumerate(units):
            u_scr[rows[i], lanes(h)] = uw[i][:, :DN_DV]
            qw_scr[ch, h] = jnp.concatenate([qs[i] * eg[i], uw[i][:, DN_DV:]], axis=0).astype(BF16)
            g_last = gcc[i][last:last + 1, :]
            kd_scr[ch, h] = (ks[i] * jnp.exp(g_last - gcc[i])).T.astype(BF16)
            gl_scr[ch, h] = jnp.broadcast_to(jnp.exp(g_last), (1, DN_DV))
        return carry

    lax.fori_loop(0, nch // DN_PREP_CHUNKS, prep_body, 0)

    def scan_body(ci, carry):
        ch = ci if direction == 0 else nch - 1 - ci
        r0 = pl.multiple_of(ch * c_sz, c_sz)
        rows = pl.ds(r0, c_sz)
        s = [s_scr[h] for h in heads]
        qs_ws = [_dot(qw_scr[ch, h], s[h].astype(BF16)) for h in heads]
        vb = [(u_scr[rows, lanes(h)] - qs_ws[h][c_sz:]).astype(BF16) for h in heads]
        for h in heads:
            o_ref[rows, lanes(h)] = (qs_ws[h][:c_sz] + _dot(attn_scr[ch, h], vb[h])).astype(o_ref.dtype)
            s_scr[h] = s[h] * gl_scr[ch, h] + _dot(kd_scr[ch, h], vb[h])
        return carry

    lax.fori_loop(0, nch, scan_body, 0)

    @pl.when(last_ref[step] == 1)
    def _():
        sfin_ref[0, 0] = s_scr[...]


def _scan_tables(direction, seq_lens):
    blk, seq, first, last = [], [], [], []
    base = 0
    for s, length in enumerate(seq_lens):
        nblk = length // TM
        order = range(nblk) if direction == 0 else range(nblk - 1, -1, -1)
        for pos, jj in enumerate(order):
            blk.append(base + jj)
            seq.append(s)
            first.append(int(pos == 0))
            last.append(int(pos == nblk - 1))
        base += nblk
    return [jnp.asarray(np.asarray(a, np.int32)) for a in (blk, seq, first, last)]


def dn_chunk_scan(q, k, v, gc, beta, gct, s0, *, direction, seq_lens, n_zero_init):
    t, hd = q.shape
    nch = TM // DN_CHUNK
    n_seq = len(seq_lens)
    tables = _scan_tables(direction, seq_lens)
    n_steps = int(tables[0].shape[0])
    row_spec = lambda w: pl.BlockSpec((TM, w), lambda i, blk, seq, fst, lst: (blk[i], 0))
    state_blk = (1, 1, DN_HEADS, DN_DK, DN_DV)
    grid_spec = pltpu.PrefetchScalarGridSpec(
        num_scalar_prefetch=4,
        grid=(n_steps,),
        in_specs=[
            row_spec(hd), row_spec(hd), row_spec(hd),
            pl.BlockSpec((1, TM, LANES), lambda i, blk, seq, fst, lst: (direction, blk[i], 0)),
            row_spec(LANES),
            pl.BlockSpec((1, nch, 2 * DN_HEADS, DN_CHUNK),
                         lambda i, blk, seq, fst, lst: (direction, blk[i], 0, 0)),
            pl.BlockSpec(state_blk, lambda i, blk, seq, fst, lst:
                         (jnp.maximum(seq[i] - n_zero_init, 0), direction, 0, 0, 0)),
        ],
        out_specs=[
            row_spec(hd),
            pl.BlockSpec(state_blk, lambda i, blk, seq, fst, lst: (seq[i], 0, 0, 0, 0)),
        ],
        scratch_shapes=[
            pltpu.VMEM((DN_HEADS, DN_DK, DN_DV), F32),
            pltpu.VMEM((nch, DN_HEADS, DN_CHUNK, DN_CHUNK), BF16),
            pltpu.VMEM((TM, hd), F32),
            pltpu.VMEM((nch, DN_HEADS, 2 * DN_CHUNK, DN_DK), BF16),
            pltpu.VMEM((nch, DN_HEADS, DN_DK, DN_CHUNK), BF16),
            pltpu.VMEM((nch, DN_HEADS, 1, DN_DV), F32),
        ],
    )
    o, sfin = pl.pallas_call(
        functools.partial(_dn_chunk_kernel, direction=direction, n_zero_init=n_zero_init),
        grid_spec=grid_spec,
        out_shape=[
            jax.ShapeDtypeStruct((t, hd), BF16),
            jax.ShapeDtypeStruct((n_seq,) + state_blk[1:], F32),
        ],
        compiler_params=_cparams(("arbitrary",)),
        name="dn_chunk_scan_d%d" % direction,
    )(*tables, q, k, v, gc, beta, gct, s0)
    return o, sfin[:, 0]


def _dn_out_kernel(of_ref, ob_ref, z_ref, gh_ref, w_ref, *refs, n_x, n_ctx_tiles, gate_idx):
    x = _load_stream(refs[:n_x], n_ctx_tiles)
    mod_ref, out_ref = refs[n_x:]
    o = of_ref[...].astype(F32) + ob_ref[...].astype(F32)
    z = z_ref[...].astype(F32)
    parts = []
    for h in range(DN_HEADS):
        lanes = slice(h * DN_DV, (h + 1) * DN_DV)
        oh = o[:, lanes]
        ms = jnp.mean(oh * oh, axis=-1, keepdims=True)
        parts.append(oh * lax.rsqrt(ms + EPS) * gh_ref[...])
    on = jnp.concatenate(parts, axis=1) * _silu(z)
    y = _dot(on.astype(BF16), w_ref[...])
    out_ref[...] = x + mod_ref[0, gate_idx:gate_idx + 1, :] * y


def dn_out(o_f, o_b, z, g_head, w_out, x, mod, gate_idx, t_ctx, l_lat):
    t, d, x_specs, x_args = _stream_operands(x, t_ctx, l_lat)
    hd = o_f.shape[1]
    grp = functools.partial(_group_of_tile, tm=TM, t_ctx=t_ctx, l_lat=l_lat)
    row_spec = lambda w: pl.BlockSpec((TM, w), lambda i: (i, 0))
    return pl.pallas_call(
        functools.partial(_dn_out_kernel, n_x=len(x_args), n_ctx_tiles=t_ctx // TM, gate_idx=gate_idx),
        grid=(t // TM,),
        in_specs=[
            row_spec(hd), row_spec(hd), row_spec(hd),
            _const_spec((1, DN_DV)),
            _const_spec(w_out.shape),
        ] + x_specs + [
            pl.BlockSpec((1,) + mod.shape[1:], lambda i: (grp(i), 0, 0)),
        ],
        out_specs=row_spec(d),
        out_shape=jax.ShapeDtypeStruct((t, d), F32),
        compiler_params=_cparams(("arbitrary",)),
        name="dn_out",
    )(o_f, o_b, z, g_head.reshape(1, DN_DV).astype(F32), w_out, *x_args, mod)


ROUTE_ROWS = 8


def _first_argmax_rows(vals, row):
    m = jnp.max(vals, axis=0, keepdims=True)
    idx = jnp.min(jnp.where(vals == m, row, float(SUBLANES)), axis=0, keepdims=True)
    return m, idx


def _router_kernel(x_ref, g_ref, mod_ref, wh_ref, wl_ref, rb_ref, tri_ref,
                   h_ref, route_ref, count_ref, run_scr, *, shift_idx, scale_idx):
    i = pl.program_id(0)

    @pl.when(i == 0)
    def _():
        run_scr[...] = jnp.zeros_like(run_scr)

    h = _norm_mod(x_ref[...], g_ref[...], mod_ref, shift_idx, scale_idx)
    hb = h.astype(BF16)
    h_ref[...] = hb
    hl = (h - hb.astype(F32)).astype(BF16)
    nt = (((1,), (1,)), ((), ()))
    wh = wh_ref[...]
    logits = (lax.dot_general(wh, hb, nt, preferred_element_type=F32)
              + (lax.dot_general(wh, hl, nt, preferred_element_type=F32)
                 + lax.dot_general(wl_ref[...], hb, nt, preferred_element_type=F32)))
    scores = _sigmoid(logits)
    sel = scores + rb_ref[:, 0:1]
    tm = scores.shape[1]
    row8 = lax.broadcasted_iota(jnp.int32, (SUBLANES, tm), 0).astype(F32)
    best_score = None
    for g in range(N_GROUPS):
        s_g = sel[g * EXPERTS_PER_GROUP:(g + 1) * EXPERTS_PER_GROUP, :]
        m1, i1 = _first_argmax_rows(s_g, row8)
        m2, i2 = _first_argmax_rows(jnp.where(row8 == i1, -jnp.inf, s_g), row8)
        gs = m1 + m2
        e1, e2 = i1 + float(g * EXPERTS_PER_GROUP), i2 + float(g * EXPERTS_PER_GROUP)
        if best_score is None:
            best_score, b1, b2 = gs, e1, e2
        else:
            better = gs > best_score
            best_score = jnp.where(better, gs, best_score)
            b1 = jnp.where(better, e1, b1)
            b2 = jnp.where(better, e2, b2)
    row_e = lax.broadcasted_iota(jnp.int32, (N_EXPERTS, tm), 0).astype(F32)
    pick1, pick2 = row_e == b1, row_e == b2
    w1 = jnp.sum(jnp.where(pick1, scores, 0.0), axis=0, keepdims=True)
    w2 = jnp.sum(jnp.where(pick2, scores, 0.0), axis=0, keepdims=True)
    tot = w1 + w2
    chosen = jnp.where(pick1 | pick2, 1.0, 0.0)
    before = run_scr[:, 0:1] + _dot(chosen.astype(BF16), tri_ref[...])
    r1 = jnp.sum(jnp.where(pick1, before, 0.0), axis=0, keepdims=True)
    r2 = jnp.sum(jnp.where(pick2, before, 0.0), axis=0, keepdims=True)
    run_scr[...] = run_scr[...] + jnp.sum(chosen, axis=1, keepdims=True)
    count_ref[...] = run_scr[...]
    zero = jnp.zeros_like(w1)
    route_ref[...] = jnp.concatenate([b1, b2, w1 / tot, w2 / tot, r1, r2, zero, zero], axis=0)


def moe_router(x, g, mod, shift_idx, scale_idx, w_router, router_bias, t_ctx, l_lat):
    t, d = x.shape
    grp = functools.partial(_group_of_tile, tm=TM, t_ctx=t_ctx, l_lat=l_lat)
    wr = w_router.astype(F32).T
    wh = wr.astype(BF16)
    wl = (wr - wh.astype(F32)).astype(BF16)
    rb = jnp.broadcast_to(router_bias.astype(F32)[:, None], (N_EXPERTS, LANES))
    idx = np.arange(TM)
    tri = jnp.asarray((idx[:, None] < idx[None, :]).astype(np.float32)).astype(BF16)
    return pl.pallas_call(
        functools.partial(_router_kernel, shift_idx=shift_idx, scale_idx=scale_idx),
        grid=(t // TM,),
        in_specs=[
            pl.BlockSpec((TM, d), lambda i: (i, 0)),
            _const_spec((1, d)),
            pl.BlockSpec((1,) + mod.shape[1:], lambda i: (grp(i), 0, 0)),
            _const_spec((N_EXPERTS, d)),
            _const_spec((N_EXPERTS, d)),
            _const_spec((N_EXPERTS, LANES)),
            _const_spec((TM, TM)),
        ],
        out_specs=[pl.BlockSpec((TM, d), lambda i: (i, 0)),
                   pl.BlockSpec((ROUTE_ROWS, TM), lambda i: (0, i)),
                   _const_spec((N_EXPERTS, LANES))],
        out_shape=[
            jax.ShapeDtypeStruct((t, d), BF16),
            jax.ShapeDtypeStruct((ROUTE_ROWS, t), F32),
            jax.ShapeDtypeStruct((N_EXPERTS, LANES), F32),
        ],
        scratch_shapes=[pltpu.VMEM((N_EXPERTS, LANES), F32)],
        compiler_params=_cparams(("arbitrary",)),
        name="moe_router",
    )(x, g.reshape(1, d), mod, wh, wl, rb, tri)


def _expert_kernel(te_ref, nt_ref, first_ref, nxt_ref, par_ref,
                   xs_ref, w1_hbm, w3_hbm, w2_hbm, ys_ref,
                   w1_buf, w3_buf, w2_buf, w1_bf, w3_bf, w2_bf, sem, *, layer):
    i = pl.program_id(0)
    active = i < nt_ref[0]

    def weight_copies(e, slot):
        return (pltpu.make_async_copy(w1_hbm.at[layer, e], w1_buf.at[slot], sem.at[slot, 0]),
                pltpu.make_async_copy(w3_hbm.at[layer, e], w3_buf.at[slot], sem.at[slot, 1]),
                pltpu.make_async_copy(w2_hbm.at[layer, e], w2_buf.at[slot], sem.at[slot, 2]))

    @pl.when(active & (i == 0))
    def _():
        for cp in weight_copies(te_ref[0], 0):
            cp.start()

    @pl.when(active & (first_ref[i] == 1))
    def _():
        slot = par_ref[i]
        for cp in weight_copies(te_ref[i], slot):
            cp.wait()

        @pl.when(nxt_ref[i] >= 0)
        def _():
            for cp in weight_copies(nxt_ref[i], 1 - slot):
                cp.start(priority=1)

        w1_bf[...] = w1_buf[slot].astype(BF16)
        w3_bf[...] = w3_buf[slot].astype(BF16)
        w2_bf[...] = w2_buf[slot].astype(BF16)

    @pl.when(active)
    def _():
        xs = xs_ref[...]
        a = _dot(xs, w1_bf[...])
        b = _dot(xs, w3_bf[...])
        hid = (_silu(a) * b).astype(BF16)
        ys_ref[...] = _dot(hid, w2_bf[...]).astype(ys_ref.dtype)

    @pl.when(jnp.logical_not(active))
    def _():
        ys_ref[...] = jnp.zeros_like(ys_ref)


def moe_experts(xs, plan, w1, w3, w2, layer):
    p, d = xs.shape
    de = w1.shape[-1]
    ntile = p // MOE_TILE
    tile_map = lambda i, *_: (i, 0)
    grid_spec = pltpu.PrefetchScalarGridSpec(
        num_scalar_prefetch=5,
        grid=(ntile,),
        in_specs=[
            pl.BlockSpec((MOE_TILE, d), tile_map),
            pl.BlockSpec(memory_space=pl.ANY),
            pl.BlockSpec(memory_space=pl.ANY),
            pl.BlockSpec(memory_space=pl.ANY),
        ],
        out_specs=pl.BlockSpec((MOE_TILE, d), tile_map),
        scratch_shapes=[
            pltpu.VMEM((2, d, de), F32), pltpu.VMEM((2, d, de), F32), pltpu.VMEM((2, de, d), F32),
            pltpu.VMEM((d, de), BF16), pltpu.VMEM((d, de), BF16), pltpu.VMEM((de, d), BF16),
            pltpu.SemaphoreType.DMA((2, 3)),
        ],
    )
    return pl.pallas_call(
        functools.partial(_expert_kernel, layer=layer),
        grid_spec=grid_spec,
        out_shape=jax.ShapeDtypeStruct((p, d), BF16),
        compiler_params=_cparams(("arbitrary",)),
        name="moe_experts",
    )(plan["tile_expert"], plan["n_tiles"], plan["first"], plan["next_expert"], plan["parity"],
      xs, w1, w3, w2)


def moe_dispatch_plan(route, counts):
    t = route.shape[1]
    p = 2 * t + N_EXPERTS * MOE_TILE
    p = -(-p // MOE_TILE) * MOE_TILE
    ntile = p // MOE_TILE
    experts = jnp.arange(N_EXPERTS, dtype=jnp.int32)
    expert = route[0:2].astype(jnp.int32)
    rank = route[4:6].astype(jnp.int32)
    padded = ((counts + MOE_TILE - 1) // MOE_TILE) * MOE_TILE
    ends = jnp.cumsum(padded)
    starts = ends - padded
    start_of = jnp.sum(jnp.where(expert[..., None] == experts, starts, 0), axis=-1)
    slot = start_of + rank
    token = jnp.tile(jnp.arange(t, dtype=jnp.int32), 2)
    slot_token = jnp.zeros((p,), jnp.int32).at[slot.reshape(-1)].set(
        token, unique_indices=True, mode="promise_in_bounds")
    tile_id = jnp.arange(ntile, dtype=jnp.int32)
    tile_expert = jnp.sum((tile_id[:, None] * MOE_TILE >= ends[None, :]).astype(jnp.int32), axis=1)
    n_tiles = (ends[-1] // MOE_TILE).astype(jnp.int32)
    used = tile_id < n_tiles
    tile_expert = jnp.where(used, tile_expert, tile_expert[jnp.maximum(n_tiles - 1, 0)])
    tile_expert = jnp.minimum(tile_expert, N_EXPERTS - 1).astype(jnp.int32)
    prev = jnp.concatenate([jnp.full((1,), -1, jnp.int32), tile_expert[:-1]])
    first = (used & (tile_expert != prev)).astype(jnp.int32)
    parity = ((jnp.cumsum(first) - 1) % 2).astype(jnp.int32)
    cand = jnp.where(counts > 0, experts, N_EXPERTS)
    later = lax.cummin(cand[::-1])[::-1]
    nxt_e = jnp.concatenate([later[1:], jnp.full((1,), N_EXPERTS, jnp.int32)])
    nxt_e = jnp.where(nxt_e >= N_EXPERTS, -1, nxt_e).astype(jnp.int32)
    next_expert = nxt_e[tile_expert]
    return dict(slot_token=slot_token, slot=slot, tile_expert=tile_expert,
                n_tiles=n_tiles.reshape(1), first=first, next_expert=next_expert, parity=parity)


def _combine_kernel(x_ref, y0_ref, y1_ref, wt_ref, mod_ref, *refs, gate_idx, final_norm, n_ctx_tiles):
    moe = wt_ref[:, 0:1] * y0_ref[...].astype(F32) + wt_ref[:, 1:2] * y1_ref[...].astype(F32)
    x = x_ref[...] + mod_ref[0, gate_idx:gate_idx + 1, :] * moe
    if final_norm:
        gf_ref, ctx_ref, lat_ref = refs
        ms = jnp.mean(x * x, axis=-1, keepdims=True)
        y = x * lax.rsqrt(ms + EPS) * gf_ref[...]
        lat_ref[...] = y

        @pl.when(pl.program_id(0) < n_ctx_tiles)
        def _():
            ctx_ref[...] = y
    else:
        (out_ref,) = refs
        out_ref[...] = x


def moe_combine(x, y0, y1, wt, mod, gate_idx, t_ctx, l_lat, g_final=None):
    t, d = x.shape
    grp = functools.partial(_group_of_tile, tm=TM, t_ctx=t_ctx, l_lat=l_lat)
    row_spec = lambda w: pl.BlockSpec((TM, w), lambda i: (i, 0))
    in_specs = [row_spec(d), row_spec(d), row_spec(d), row_spec(wt.shape[1]),
                pl.BlockSpec((1,) + mod.shape[1:], lambda i: (grp(i), 0, 0))]
    args = [x, y0, y1, wt, mod]
    n_ctx_tiles = t_ctx // TM
    if g_final is not None:
        in_specs.append(_const_spec((1, d)))
        args.append(g_final.reshape(1, d).astype(F32))
        out_specs = [pl.BlockSpec((TM, d), lambda i: (jnp.minimum(i, n_ctx_tiles - 1), 0)),
                     pl.BlockSpec((TM, d), lambda i: (jnp.maximum(i - n_ctx_tiles, 0), 0))]
        out_shape = [jax.ShapeDtypeStruct((t_ctx, d), F32), jax.ShapeDtypeStruct((t - t_ctx, d), F32)]
    else:
        out_specs = row_spec(d)
        out_shape = jax.ShapeDtypeStruct((t, d), F32)
    return pl.pallas_call(
        functools.partial(_combine_kernel, gate_idx=gate_idx, final_norm=g_final is not None,
                          n_ctx_tiles=n_ctx_tiles),
        grid=(t // TM,),
        in_specs=in_specs,
        out_specs=out_specs,
        out_shape=out_shape,
        compiler_params=_cparams(("arbitrary",)),
        name="moe_combine",
    )(*args)


def moe_layer(x, g, mod, w_router, router_bias, w1, w3, w2, layer, t_ctx, l_lat, g_final=None):
    h, route, counts = moe_router(x, g, mod, 3, 4, w_router, router_bias, t_ctx, l_lat)
    plan = moe_dispatch_plan(route, counts[:, 0].astype(jnp.int32))
    take = lambda rows, index: rows.at[index].get(mode="promise_in_bounds")
    xs = take(h, plan["slot_token"])
    ys = moe_experts(xs, plan, w1, w3, w2, layer)
    y0 = take(ys, plan["slot"][0])
    y1 = take(ys, plan["slot"][1])
    return moe_combine(x, y0, y1, route[2:4].T, mod, 5, t_ctx, l_lat, g_final)


def _hy_conv_kernel(u_ref, up_ref, un_ref, cw_ref, cb_ref, x0_ref, vv_ref, *, t_ctx, l_ctx, l_lat):
    i = pl.program_id(0)
    has_prev, has_next = _seq_flags(i, TM, t_ctx, l_ctx, l_lat)
    c = _conv3(u_ref[...], up_ref[...], un_ref[...], cw_ref, has_prev, has_next) + cb_ref[...]
    d = c.shape[1] // 3
    x0_ref[...] = c[:, :d].astype(x0_ref.dtype)
    vv_ref[...] = (c[:, 2 * d:] * c[:, d:2 * d]).astype(vv_ref.dtype)


def hy_conv_gate(u, conv_w, conv_b, t_ctx, l_ctx, l_lat):
    t, c = u.shape
    d = c // 3
    cw = jnp.zeros((SUBLANES, c), F32).at[:conv_w.shape[0]].set(conv_w)
    row_spec = lambda w: pl.BlockSpec((TM, w), lambda i: (i, 0))
    return pl.pallas_call(
        functools.partial(_hy_conv_kernel, t_ctx=t_ctx, l_ctx=l_ctx, l_lat=l_lat),
        grid=(t // TM,),
        in_specs=_halo_specs(c, TM, t) + [_const_spec((SUBLANES, c)), _const_spec((1, c))],
        out_specs=[row_spec(d), row_spec(d)],
        out_shape=[jax.ShapeDtypeStruct((t, d), BF16), jax.ShapeDtypeStruct((t, d), BF16)],
        compiler_params=_cparams(("arbitrary",)),
        name="hy_conv_gate",
    )(u, u, u, cw, conv_b.reshape(1, c).astype(F32))


def _hy_features(n):
    t = np.linspace(0.0, 1.0, n)[:, None]
    bands = (HY_EMB - 1) // 2
    band_w = np.linspace(1e-4, bands - 1, bands)[None, :]
    ang = (2.0 * math.pi / n) * np.arange(n)[:, None] * band_w
    z = np.concatenate([t, np.cos(ang), -np.sin(ang)], axis=-1)
    zp = np.zeros((n, LANES))
    zp[:, :HY_EMB] = z
    return jnp.asarray(zp.astype(np.float32))


def _hy_hid_kernel(z_ref, f_ref, w1_ref, b1_ref, w2_ref, b2_ref, w3_ref, b3_ref, hid_ref):
    h = jnp.sin(f_ref[0:1, :] * (_dot(z_ref[...], w1_ref[...], HI) + b1_ref[...]))
    h = jnp.sin(f_ref[1:2, :] * (_dot(h, w2_ref[...], HI) + b2_ref[...]))
    h = jnp.sin(f_ref[2:3, :] * (_dot(h, w3_ref[...], HI) + b3_ref[...]))
    hid_ref[...] = h


def _hy_filt_kernel(hid_ref, w4_ref, z_ref, dl_ref, filt_ref):
    f = _dot(hid_ref[...], w4_ref[...], HI)
    f = f * jnp.exp(-z_ref[:, 0:1] * jnp.abs(dl_ref[...]))
    nrm = jnp.sum(jnp.abs(f), axis=0, keepdims=True)
    filt_ref[...] = f / nrm


def hy_filter(n, d, freq, w1, b1, w2, b2, w3, b3, w4):
    z = _hy_features(n)
    o = HY_ORDER
    w1p = jnp.zeros((LANES, o), F32).at[:HY_EMB].set(w1.astype(F32))
    fp = jnp.zeros((SUBLANES, o), F32).at[:3].set(freq.astype(F32))
    tr = min(n, TM)
    hid = pl.pallas_call(
        _hy_hid_kernel,
        grid=(n // tr,),
        in_specs=[pl.BlockSpec((tr, LANES), lambda i: (i, 0)), _const_spec((SUBLANES, o)),
                  _const_spec((LANES, o)), _const_spec((1, o)),
                  _const_spec((o, o)), _const_spec((1, o)),
                  _const_spec((o, o)), _const_spec((1, o))],
        out_specs=pl.BlockSpec((tr, o), lambda i: (i, 0)),
        out_shape=jax.ShapeDtypeStruct((n, o), F32),
        compiler_params=_cparams(("arbitrary",)),
        name="hy_filter_mlp",
    )(z, fp, w1p, b1.reshape(1, o).astype(F32), w2.astype(F32), b2.reshape(1, o).astype(F32),
      w3.astype(F32), b3.reshape(1, o).astype(F32))
    max_decay = math.log(HY_DECAY_TARGET) / HY_DECAY_SHORT_PCT
    min_decay = math.log(HY_DECAY_TARGET) / HY_DECAY_LONG_PCT
    deltas = np.tile(np.linspace(min_decay, max_decay, d), 2).astype(np.float32)[None, :]
    tc = 256
    return pl.pallas_call(
        _hy_filt_kernel,
        grid=(2 * d // tc,),
        in_specs=[_const_spec((n, o)), pl.BlockSpec((o, tc), lambda j: (0, j)),
                  _const_spec((n, LANES)), pl.BlockSpec((1, tc), lambda j: (0, j))],
        out_specs=pl.BlockSpec((n, tc), lambda j: (0, j)),
        out_shape=jax.ShapeDtypeStruct((n, 2 * d), F32),
        compiler_params=_cparams(("arbitrary",)),
        name="hy_filter_window",
    )(hid, w4.astype(F32), z, jnp.asarray(deltas))


def _dense_dft_tables(l):
    n = 2 * l
    k = np.arange(n)[:, None]
    j = np.arange(l)[None, :]
    th = 2.0 * math.pi * ((k * j) % n) / n
    fwd = np.concatenate([np.cos(th), -np.sin(th)], axis=0)
    inv = np.concatenate([np.cos(th).T, -np.sin(th).T], axis=1) / n
    return (jnp.asarray(fwd.astype(np.float32)).astype(BF16),
            jnp.asarray(inv.astype(np.float32)).astype(BF16))


def _dense_tf_kernel(hf_ref, hb_ref, fwd_ref, tf_ref):
    n = fwd_ref.shape[0] // 2
    hf = hf_ref[...]
    hb = hb_ref[...]
    tf_ref[0:n, :] = _dot(fwd_ref[0:n, :], (hf + hb).astype(BF16))
    tf_ref[n:, :] = _dot(fwd_ref[n:, :], (hf - hb).astype(BF16))


def _dense_conv_kernel(vv_ref, x0_ref, skip_ref, tf_ref, fwd_ref, inv_ref, out_ref):
    n = fwd_ref.shape[0] // 2
    vv = vv_ref[...]
    x = _dot(fwd_ref[...], vv.astype(BF16))
    xr, xi = x[:n], x[n:]
    tr, ti = tf_ref[0:n, :], tf_ref[n:, :]
    y = jnp.concatenate([xr * tr - xi * ti, xr * ti + xi * tr], axis=0).astype(BF16)
    conv = _dot(inv_ref[...], y)
    out = (conv + vv.astype(F32) * skip_ref[...]) * x0_ref[...].astype(F32)
    out_ref[...] = out.astype(out_ref.dtype)


def hy_longconv_dense(vv, x0, skip, filt, n_seq, l):
    d = vv.shape[1]
    n = 2 * l
    fwd, inv = _dense_dft_tables(l)
    dc = 512
    tf = pl.pallas_call(
        _dense_tf_kernel,
        grid=(d // dc,),
        in_specs=[pl.BlockSpec((l, dc), lambda j: (0, j)),
                  pl.BlockSpec((l, dc), lambda j: (0, d // dc + j)),
                  _const_spec(fwd.shape)],
        out_specs=pl.BlockSpec((2 * n, dc), lambda j: (0, j)),
        out_shape=jax.ShapeDtypeStruct((2 * n, d), F32),
        compiler_params=_cparams(("arbitrary",)),
        name="hy_tf_dense",
    )(filt, filt, fwd)
    return pl.pallas_call(
        _dense_conv_kernel,
        grid=(d // dc, n_seq),
        in_specs=[pl.BlockSpec((l, dc), lambda j, s: (s, j)),
                  pl.BlockSpec((l, dc), lambda j, s: (s, j)),
                  pl.BlockSpec((1, dc), lambda j, s: (0, j)),
                  pl.BlockSpec((2 * n, dc), lambda j, s: (0, j)),
                  _const_spec(fwd.shape), _const_spec(inv.shape)],
        out_specs=pl.BlockSpec((l, dc), lambda j, s: (s, j)),
        out_shape=jax.ShapeDtypeStruct((n_seq * l, d), BF16),
        compiler_params=_cparams(("arbitrary", "arbitrary")),
        name="hy_longconv_dense",
    )(vv, x0, skip.reshape(1, d).astype(F32), tf, fwd, inv)


def _two_stage_tables(l):
    n = 2 * l
    n2 = FFT_N2
    n1_full = n // n2
    n_slab = n1_full // 2 + 1
    n1 = -(-n_slab // FFT_UNROLL_SLAB) * FFT_UNROLL_SLAB
    p = l // n2
    a = np.arange(p)[None, None, :]
    c = np.arange(n1)[None, :, None]
    b = np.arange(n2)[:, None, None]
    live = (c < n_slab).astype(np.float64)
    weight = np.where((c == 0) | (c == n1_full // 2), 1.0, 2.0) * live
    ph = 2.0 * math.pi * (((a * c) % n1_full) / n1_full + ((b * c) % n) / n)
    f1 = np.concatenate([np.cos(ph) * live, -np.sin(ph) * live], axis=1)
    pht = np.transpose(ph, (0, 2, 1))
    wt = np.transpose(weight, (0, 2, 1)) / n
    g1 = np.concatenate([np.cos(pht) * wt, -np.sin(pht) * wt], axis=2)
    e = np.arange(n2)[:, None]
    bb = np.arange(n2)[None, :]
    th = 2.0 * math.pi * ((e * bb) % n2) / n2
    fr, fi = np.cos(th), -np.sin(th)
    f2 = np.block([[fr, -fi], [fi, fr]])
    f2c = np.block([[fr, fi], [-fi, fr]])
    cast = lambda m: jnp.asarray(m.astype(np.float32)).astype(BF16)
    return cast(f1), cast(f2), cast(f2c), cast(g1)


def _unrolled_loop(n, unroll, fn):
    def body(i, carry):
        fn([i * unroll + u for u in range(unroll)])
        return carry

    lax.fori_loop(0, n // unroll, body, 0)


FFT_UNROLL_SMALL = 16
FFT_UNROLL_SLAB = 4


def _stage1(src_ref, f1_ref, w_scr, combine=None):
    n2, two_n1, p = f1_ref.shape

    def step(bs):
        if combine is None:
            xs = [src_ref[pl.ds(b, p, stride=n2), :] for b in bs]
        else:
            xs = [combine(src_ref[0][pl.ds(b, p, stride=n2), :], src_ref[1][pl.ds(b, p, stride=n2), :])
                  for b in bs]
        res = [_dot(f1_ref[b], x.astype(BF16)) for b, x in zip(bs, xs)]
        for b, r in zip(bs, res):
            w_scr[pl.ds(pl.multiple_of(b * two_n1, two_n1), two_n1), :] = r

    _unrolled_loop(n2, FFT_UNROLL_SMALL, step)


def _load_slab(w_scr, c, n1, n2):
    re = w_scr[pl.ds(c, n2, stride=2 * n1), :]
    im = w_scr[pl.ds(n1 + c, n2, stride=2 * n1), :]
    return jnp.concatenate([re, im], axis=0).astype(BF16)


def _two_stage_tf_kernel(hf_ref, hb_ref, f1_ref, f2_ref, tfr_ref, tfi_ref, ws_scr, wd_scr):
    n2, two_n1, _ = f1_ref.shape
    n1 = two_n1 // 2
    _stage1((hf_ref, hb_ref), f1_ref, ws_scr, combine=lambda u, v: u + v)
    _stage1((hf_ref, hb_ref), f1_ref, wd_scr, combine=lambda u, v: u - v)

    def step(cs):
        xs = [_dot(f2_ref[0:n2, :], _load_slab(ws_scr, c, n1, n2)) for c in cs]
        xd = [_dot(f2_ref[n2:, :], _load_slab(wd_scr, c, n1, n2)) for c in cs]
        for c, a, b in zip(cs, xs, xd):
            r0 = pl.multiple_of(c * n2, n2)
            tfr_ref[pl.ds(r0, n2), :] = a
            tfi_ref[pl.ds(r0, n2), :] = b

    _unrolled_loop(n1, FFT_UNROLL_SLAB, step)


def _two_stage_conv_kernel(vv_ref, x0_ref, skip_ref, tfr_ref, tfi_ref,
                           f1_ref, f2_ref, f2c_ref, g1_ref, out_ref, w_scr, t_scr):
    n2, two_n1, p = f1_ref.shape
    n1 = two_n1 // 2
    t_scr[...] = vv_ref[...].astype(F32)
    _stage1(t_scr, f1_ref, w_scr)

    def step2(cs):
        xs = [_dot(f2_ref[...], _load_slab(w_scr, c, n1, n2)) for c in cs]
        ys = []
        for c, x in zip(cs, xs):
            r0 = pl.multiple_of(c * n2, n2)
            xr, xi = x[:n2], x[n2:]
            tr = tfr_ref[pl.ds(r0, n2), :]
            ti = tfi_ref[pl.ds(r0, n2), :]
            ys.append(jnp.concatenate([xr * tr - xi * ti, xr * ti + xi * tr], axis=0).astype(BF16))
        zs = [_dot(f2c_ref[...], y) for y in ys]
        for c, z in zip(cs, zs):
            w_scr[pl.ds(c, n2, stride=two_n1), :] = z[:n2]
            w_scr[pl.ds(n1 + c, n2, stride=two_n1), :] = z[n2:]

    _unrolled_loop(n1, FFT_UNROLL_SLAB, step2)

    def step3(bs):
        zb = [w_scr[pl.ds(pl.multiple_of(b * two_n1, two_n1), two_n1), :].astype(BF16) for b in bs]
        res = [_dot(g1_ref[b], z) for b, z in zip(bs, zb)]
        for b, r in zip(bs, res):
            t_scr[pl.ds(b, p, stride=n2), :] = r

    _unrolled_loop(n2, FFT_UNROLL_SMALL, step3)
    out = (t_scr[...] + vv_ref[...].astype(F32) * skip_ref[...]) * x0_ref[...].astype(F32)
    out_ref[...] = out.astype(out_ref.dtype)


def hy_longconv_two_stage(vv, x0, skip, filt, row0, n_seq, l):
    d = vv.shape[1]
    f1, f2, f2c, g1 = _two_stage_tables(l)
    n2, two_n1, p = f1.shape
    n = (two_n1 // 2) * n2
    dc = LANES
    tfr, tfi = pl.pallas_call(
        _two_stage_tf_kernel,
        grid=(d // dc,),
        in_specs=[pl.BlockSpec((l, dc), lambda j: (0, j)),
                  pl.BlockSpec((l, dc), lambda j: (0, d // dc + j)),
                  _const_spec(f1.shape), _const_spec(f2.shape)],
        out_specs=[pl.BlockSpec((n, dc), lambda j: (0, j)), pl.BlockSpec((n, dc), lambda j: (0, j))],
        out_shape=[jax.ShapeDtypeStruct((n, d), F32), jax.ShapeDtypeStruct((n, d), F32)],
        scratch_shapes=[pltpu.VMEM((n2 * two_n1, dc), F32), pltpu.VMEM((n2 * two_n1, dc), F32)],
        compiler_params=_cparams(("arbitrary",)),
        name="hy_tf_two_stage",
    )(filt, filt, f1, f2)
    base = row0 // l
    return pl.pallas_call(
        _two_stage_conv_kernel,
        grid=(d // dc, n_seq),
        in_specs=[pl.BlockSpec((l, dc), lambda j, s: (base + s, j)),
                  pl.BlockSpec((l, dc), lambda j, s: (base + s, j)),
                  pl.BlockSpec((1, dc), lambda j, s: (0, j)),
                  pl.BlockSpec((n, dc), lambda j, s: (0, j)),
                  pl.BlockSpec((n, dc), lambda j, s: (0, j)),
                  _const_spec(f1.shape), _const_spec(f2.shape), _const_spec(f2c.shape),
                  _const_spec(g1.shape)],
        out_specs=pl.BlockSpec((l, dc), lambda j, s: (s, j)),
        out_shape=jax.ShapeDtypeStruct((n_seq * l, d), BF16),
        scratch_shapes=[pltpu.VMEM((n2 * two_n1, dc), F32), pltpu.VMEM((l, dc), F32)],
        compiler_params=_cparams(("arbitrary", "arbitrary")),
        name="hy_longconv_two_stage",
    )(vv, x0, skip.reshape(1, d).astype(F32), tfr, tfi, f1, f2, f2c, g1)


def _hy_out_kernel(yc_ref, yl_ref, w_ref, b_ref, x_ref, mod_ref, out_ref, *, gate_idx, n_ctx_tiles):
    yg = jnp.where(pl.program_id(0) < n_ctx_tiles, yc_ref[...], yl_ref[...])
    y = _dot(yg.astype(BF16), w_ref[...]) + b_ref[...]
    out_ref[...] = x_ref[...] + mod_ref[0, gate_idx:gate_idx + 1, :] * y


def hy_out(y_ctx, y_lat, w_out, b_out, x, mod, gate_idx, t_ctx, l_lat):
    t, d = x.shape
    grp = functools.partial(_group_of_tile, tm=TM, t_ctx=t_ctx, l_lat=l_lat)
    row_spec = lambda w: pl.BlockSpec((TM, w), lambda i: (i, 0))
    n_ctx_tiles = t_ctx // TM
    return pl.pallas_call(
        functools.partial(_hy_out_kernel, gate_idx=gate_idx, n_ctx_tiles=n_ctx_tiles),
        grid=(t // TM,),
        in_specs=[pl.BlockSpec((TM, d), lambda i: (jnp.minimum(i, n_ctx_tiles - 1), 0)),
                  pl.BlockSpec((TM, d), lambda i: (jnp.maximum(i - n_ctx_tiles, 0), 0)),
                  _const_spec(w_out.shape), _const_spec((1, d)), row_spec(d),
                  pl.BlockSpec((1,) + mod.shape[1:], lambda i: (grp(i), 0, 0))],
        out_specs=row_spec(d),
        out_shape=jax.ShapeDtypeStruct((t, d), F32),
        compiler_params=_cparams(("arbitrary",)),
        name="hy_out",
    )(y_ctx, y_lat, w_out, b_out.reshape(1, d).astype(F32), x, mod)


def _grid_pos_embed(rows, d):
    r = np.repeat(np.arange(rows, dtype=np.float64), GRID_W)
    col = np.tile(np.arange(GRID_W, dtype=np.float64), rows)
    quarter = d // 4
    omega = 1.0 / (10000.0 ** (np.arange(quarter, dtype=np.float64) / quarter))
    ang_r = r[:, None] * omega
    ang_c = col[:, None] * omega
    pe = np.concatenate([np.sin(ang_r), np.cos(ang_r), np.sin(ang_c), np.cos(ang_c)], axis=-1)
    return jnp.asarray(pe.astype(np.float32))


def delta_layer(x, g, mod, state_lat, w_in, conv_w, a_log, dt_bias, g_head, w_out, dims):
    t_ctx, l_ctx, n_ctx, l_lat, n_lat = dims
    hk = DN_HEADS * DN_DK
    nqkv = 3 * hk
    w_qkv = w_in[:, :nqkv].astype(BF16)
    w_z = w_in[:, nqkv:nqkv + hk].astype(BF16)
    w_ab = jnp.zeros((w_in.shape[0], LANES), BF16).at[:, :4 * DN_HEADS].set(w_in[:, nqkv + hk:].astype(BF16))
    qkv, z, ab = norm_mod_matmul(x, g, mod, 0, 1, [w_qkv, w_z, w_ab], None, [BF16, BF16, F32],
                                 t_ctx, l_lat)
    q, k, v, gc, beta, gct = dn_prep(qkv, ab, conv_w.astype(F32), a_log, dt_bias, t_ctx, l_ctx, l_lat)
    outs = []
    ctx_states = []
    seq_lens = (l_ctx,) * n_ctx + (l_lat,) * n_lat
    for direction in range(2):
        o, s_all = dn_chunk_scan(q, k, v, gc, beta, gct, state_lat.astype(F32), direction=direction,
                                 seq_lens=seq_lens, n_zero_init=n_ctx)
        outs.append(o)
        ctx_states.append(s_all[:n_ctx])
    x = dn_out(outs[0], outs[1], z, g_head, w_out.astype(BF16), x, mod, 2, t_ctx, l_lat)
    return x, jnp.stack(ctx_states, axis=1)


def hyena_layer(x, g, mod, w_in, b_in, conv_w, conv_b, freq, fw1, fb1, fw2, fb2, fw3, fb3, fw4,
                skip, w_out, b_out, dims):
    t_ctx, l_ctx, n_ctx, l_lat, n_lat = dims
    d = x.shape[1]
    (u,) = norm_mod_matmul(x, g, mod, 0, 1, [w_in.astype(BF16)], [b_in.astype(F32)], [BF16],
                           t_ctx, l_lat)
    x0, vv = hy_conv_gate(u, conv_w.astype(F32), conv_b, t_ctx, l_ctx, l_lat)
    filt_c = hy_filter(l_ctx, d, freq, fw1, fb1, fw2, fb2, fw3, fb3, fw4)
    filt_l = hy_filter(l_lat, d, freq, fw1, fb1, fw2, fb2, fw3, fb3, fw4)
    y_c = hy_longconv_dense(vv, x0, skip, filt_c, n_ctx, l_ctx)
    y_l = hy_longconv_two_stage(vv, x0, skip, filt_l, t_ctx, n_lat, l_lat)
    return hy_out(y_c, y_l, w_out.astype(BF16), b_out, x, mod, 2, t_ctx, l_lat)


def kernel(x_prompt, x_sample, state_delta, c, c_ctx, w_ada, b_ada, g_norm, dn_w_in, dn_conv, dn_a_log, dn_dt_bias, dn_g_head, dn_w_out, hy_w_in, hy_b_in, hy_conv, hy_conv_b, hy_freq, hy_f_w1, hy_f_b1, hy_f_w2, hy_f_b2, hy_f_w3, hy_f_b3, hy_f_w4, hy_skip, hy_w_out, hy_b_out, w_router, router_bias, moe_w1, moe_w3, moe_w2, g_final):
    n_ctx, l_ctx, d = x_prompt.shape
    n_lat, l_lat, _ = x_sample.shape
    depth = w_ada.shape[0]
    t_ctx = n_ctx * l_ctx
    dims = (t_ctx, l_ctx, n_ctx, l_lat, n_lat)
    assert l_ctx % TM == 0 and l_lat % TM == 0 and t_ctx % l_lat == 0
    assert n_lat + 1 <= SUBLANES

    pos = _grid_pos_embed(l_lat // GRID_W, d)
    x = (x_prompt.reshape(t_ctx, d).astype(F32), x_sample.reshape(n_lat * l_lat, d).astype(F32), pos)
    cond = jnp.zeros((SUBLANES, d), F32).at[0].set(c_ctx.astype(F32)).at[1:1 + n_lat].set(c.astype(F32))
    mod_all = ada_modulation(cond, w_ada.astype(F32), b_ada.astype(F32))
    mod_all = mod_all.reshape(depth, SUBLANES, 6, d)

    ctx_states = []
    for i in range(depth):
        mod = mod_all[i]
        j = i // 2
        if i % 2 == 0:
            x, s_ctx = delta_layer(x, g_norm[i, 0], mod, state_delta[:, j], dn_w_in[j], dn_conv[j],
                                   dn_a_log[j], dn_dt_bias[j], dn_g_head[j], dn_w_out[j], dims)
            ctx_states.append(s_ctx.astype(x_prompt.dtype))
        else:
            x = hyena_layer(x, g_norm[i, 0], mod, hy_w_in[j], hy_b_in[j], hy_conv[j], hy_conv_b[j],
                            hy_freq[j], hy_f_w1[j], hy_f_b1[j], hy_f_w2[j], hy_f_b2[j], hy_f_w3[j],
                            hy_f_b3[j], hy_f_w4[j], hy_skip[j], hy_w_out[j], hy_b_out[j], dims)
        x = moe_layer(x, g_norm[i, 1], mod, w_router, router_bias, moe_w1, moe_w3, moe_w2, i,
                      t_ctx, l_lat, g_final if i == depth - 1 else None)
    y_ctx, y_lat = x
    y_prompt = y_ctx.reshape(n_ctx, l_ctx, d).astype(x_prompt.dtype)
    y_sample = y_lat.reshape(n_lat, l_lat, d).astype(x_sample.dtype)
    new_state = jnp.stack(ctx_states, axis=1)
    return (y_prompt, y_sample, new_state)
```

```python
import functools
import math

import numpy as np
import jax
import jax.numpy as jnp
from jax import lax
from jax.experimental import pallas as pl
from jax.experimental.pallas import tpu as pltpu

F32 = jnp.float32
BF16 = jnp.bfloat16
HI = lax.Precision.HIGHEST

EPS = 1e-6
GRID_W = 64
DN_HEADS = 8
DN_DK = 128
DN_DV = 128
DN_CHUNK = 64
HY_EMB = 33
HY_ORDER = 64
HY_DECAY_SHORT_PCT = 0.3
HY_DECAY_LONG_PCT = 1.5
HY_DECAY_TARGET = 1e-2
N_EXPERTS = 32
N_GROUPS = 4
EXPERTS_PER_GROUP = N_EXPERTS // N_GROUPS

LANES = 128
SUBLANES = 8
TM = 256
HALO_ROWS = 16
MOE_TILE = 256
FFT_N2 = 128
DN_PREP_CHUNKS = 4
VMEM_LIMIT = 48 * 1024 * 1024


def _cparams(sem):
    return pltpu.CompilerParams(dimension_semantics=sem, vmem_limit_bytes=VMEM_LIMIT)


def _sigmoid(x):
    return 1.0 / (1.0 + jnp.exp(-x))


def _silu(x):
    return x * _sigmoid(x)


def _dot(a, b, precision=None):
    return jnp.dot(a, b, preferred_element_type=F32, precision=precision)


def _split_bf16(x):
    hi = x.astype(BF16)
    return hi, (x - hi.astype(F32)).astype(BF16)


def _dot3(a, b):
    ah, al = _split_bf16(a)
    bh, bl = _split_bf16(b)
    return _dot(ah, bh) + (_dot(al, bh) + _dot(ah, bl))


def _const_spec(shape):
    nd = len(shape)
    return pl.BlockSpec(shape, lambda *_: (0,) * nd)


def _ada_kernel(cond_ref, w_ref, b_ref, out_ref):
    s = _silu(cond_ref[...])
    out_ref[0] = _dot3(s, w_ref[0]) + b_ref[0]


def ada_modulation(cond, w_ada, b_ada):
    depth, d, n = w_ada.shape
    tn = 1536
    return pl.pallas_call(
        _ada_kernel,
        grid=(depth, n // tn),
        in_specs=[
            pl.BlockSpec((SUBLANES, d), lambda i, j: (0, 0)),
            pl.BlockSpec((1, d, tn), lambda i, j: (i, 0, j)),
            pl.BlockSpec((1, 1, tn), lambda i, j: (i, 0, j)),
        ],
        out_specs=pl.BlockSpec((1, SUBLANES, tn), lambda i, j: (i, 0, j)),
        out_shape=jax.ShapeDtypeStruct((depth, SUBLANES, n), F32),
        compiler_params=_cparams(("arbitrary", "arbitrary")),
        name="ada_modulation",
    )(cond, w_ada, b_ada.reshape(depth, 1, n))


def _norm_mod(x, g, mod_ref, shift_idx, scale_idx):
    ms = jnp.mean(x * x, axis=-1, keepdims=True)
    y = x * lax.rsqrt(ms + EPS) * g
    scale = mod_ref[0, scale_idx:scale_idx + 1, :]
    shift = mod_ref[0, shift_idx:shift_idx + 1, :]
    return y * (1.0 + scale) + shift


def _group_of_tile(i, tm, t_ctx, l_lat):
    r = i * tm
    return jnp.where(r < t_ctx, 0, 1 + (r - t_ctx) // l_lat)


def _stream_operands(x, t_ctx, l_lat):
    if not isinstance(x, tuple):
        t, d = x.shape
        return t, d, [pl.BlockSpec((TM, d), lambda i: (i, 0))], [x]
    xc, xl, pos = x
    d = xc.shape[1]
    nct = t_ctx // TM
    per = l_lat // TM
    specs = [pl.BlockSpec((TM, d), lambda i: (jnp.minimum(i, nct - 1), 0)),
             pl.BlockSpec((TM, d), lambda i: (jnp.maximum(i - nct, 0), 0)),
             pl.BlockSpec((TM, d), lambda i: (jnp.maximum(i - nct, 0) % per, 0))]
    return xc.shape[0] + xl.shape[0], d, specs, [xc, xl, pos]


def _load_stream(x_refs, n_ctx_tiles):
    if len(x_refs) == 1:
        return x_refs[0][...]
    xc_ref, xl_ref, pos_ref = x_refs
    return jnp.where(pl.program_id(0) < n_ctx_tiles, xc_ref[...], xl_ref[...] + pos_ref[...])


def _nmm_kernel(*refs, n_x, n_ctx_tiles, n_out, has_bias, shift_idx, scale_idx):
    x = _load_stream(refs[:n_x], n_ctx_tiles)
    g_ref, mod_ref = refs[n_x:n_x + 2]
    refs = refs[n_x + 2:]
    h = _norm_mod(x, g_ref[...], mod_ref, shift_idx, scale_idx).astype(BF16)
    w_refs = refs[:n_out]
    b_refs = refs[n_out:n_out + (n_out if has_bias else 0)]
    o_refs = refs[n_out + len(b_refs):]
    for k in range(n_out):
        acc = _dot(h, w_refs[k][...])
        if has_bias:
            acc = acc + b_refs[k][...]
        o_refs[k][...] = acc.astype(o_refs[k].dtype)


def norm_mod_matmul(x, g, mod, shift_idx, scale_idx, weights, biases, out_dtypes, t_ctx, l_lat):
    t, d, x_specs, x_args = _stream_operands(x, t_ctx, l_lat)
    n_out = len(weights)
    has_bias = biases is not None
    grp = functools.partial(_group_of_tile, tm=TM, t_ctx=t_ctx, l_lat=l_lat)
    in_specs = x_specs + [
        _const_spec((1, d)),
        pl.BlockSpec((1,) + mod.shape[1:], lambda i: (grp(i), 0, 0)),
    ]
    args = x_args + [g.reshape(1, d), mod]
    for w in weights:
        in_specs.append(_const_spec(w.shape))
        args.append(w)
    if has_bias:
        for b in biases:
            in_specs.append(_const_spec((1, b.shape[-1])))
            args.append(b.reshape(1, -1))
    out_specs = [pl.BlockSpec((TM, w.shape[1]), lambda i: (i, 0)) for w in weights]
    out_shape = [jax.ShapeDtypeStruct((t, w.shape[1]), dt) for w, dt in zip(weights, out_dtypes)]
    return pl.pallas_call(
        functools.partial(_nmm_kernel, n_x=len(x_args), n_ctx_tiles=t_ctx // TM, n_out=n_out,
                          has_bias=has_bias, shift_idx=shift_idx, scale_idx=scale_idx),
        grid=(t // TM,),
        in_specs=in_specs,
        out_specs=out_specs,
        out_shape=out_shape,
        compiler_params=_cparams(("arbitrary",)),
        name="norm_mod_matmul",
    )(*args)


def _seq_flags(i, tm, t_ctx, l_ctx, l_lat):
    r = i * tm
    in_ctx = r < t_ctx
    pos = jnp.where(in_ctx, r % l_ctx, (r - t_ctx) % l_lat)
    length = jnp.where(in_ctx, l_ctx, l_lat)
    return pos != 0, (pos + tm) != length


def _conv3(x, prev_rows, next_rows, w_ref, has_prev, has_next):
    tm = x.shape[0]
    x = x.astype(F32)
    halo_p = jnp.where(has_prev, prev_rows[HALO_ROWS - 1:HALO_ROWS, :].astype(F32), 0.0)
    halo_n = jnp.where(has_next, next_rows[0:1, :].astype(F32), 0.0)
    w0, w1, w2 = w_ref[0:1, :], w_ref[1:2, :], w_ref[2:3, :]
    c = w0 * pltpu.roll(x, 1, 0) + w1 * x + w2 * pltpu.roll(x, tm - 1, 0)
    row = lax.broadcasted_iota(jnp.int32, (SUBLANES, 1), 0)
    top = c[:SUBLANES] + jnp.where(row == 0, w0 * (halo_p - x[tm - 1:tm]), 0.0)
    bot = c[tm - SUBLANES:] + jnp.where(row == SUBLANES - 1, w2 * (halo_n - x[0:1]), 0.0)
    return jnp.concatenate([top, c[SUBLANES:tm - SUBLANES], bot], axis=0)


def _halo_specs(c, tm, t):
    nblk = t // HALO_ROWS
    per = tm // HALO_ROWS
    return [
        pl.BlockSpec((tm, c), lambda i: (i, 0)),
        pl.BlockSpec((HALO_ROWS, c), lambda i: (jnp.maximum(i * per - 1, 0), 0)),
        pl.BlockSpec((HALO_ROWS, c), lambda i: (jnp.minimum((i + 1) * per, nblk - 1), 0)),
    ]


def _softplus(x):
    return jnp.maximum(x, 0.0) + jnp.log(1.0 + jnp.exp(-jnp.abs(x)))


def _dn_prep_kernel(x_ref, xp_ref, xn_ref, cw_ref, ab_ref, nega_ref, dtb_ref, tri_ref,
                    q_ref, k_ref, v_ref, gc_ref, beta_ref, gct_ref,
                    *, t_ctx, l_ctx, l_lat):
    i = pl.program_id(0)
    has_prev, has_next = _seq_flags(i, TM, t_ctx, l_ctx, l_lat)
    c = _conv3(x_ref[...], xp_ref[...], xn_ref[...], cw_ref, has_prev, has_next)
    s = _silu(c)
    hk = DN_HEADS * DN_DK
    for h in range(DN_HEADS):
        lo, hi = h * DN_DK, (h + 1) * DN_DK
        qh = s[:, lo:hi]
        kh = s[:, hk + lo:hk + hi]
        qn = lax.rsqrt(jnp.sum(qh * qh, axis=-1, keepdims=True) + EPS)
        kn = lax.rsqrt(jnp.sum(kh * kh, axis=-1, keepdims=True) + EPS)
        q_ref[:, lo:hi] = (qh * (qn * (DN_DK ** -0.5))).astype(BF16)
        k_ref[:, lo:hi] = (kh * kn).astype(BF16)
    v_ref[...] = s[:, 2 * hk:].astype(BF16)
    ab = ab_ref[...]
    lg = nega_ref[...] * _softplus(ab + dtb_ref[...])
    beta_ref[...] = _sigmoid(ab)
    nch = TM // DN_CHUNK
    for d in range(2):
        gc = _dot(tri_ref[d], lg, HI)
        gc_ref[d] = gc
        gct = gc.T
        for ch in range(nch):
            gct_ref[d, ch] = gct[0:2 * DN_HEADS, ch * DN_CHUNK:(ch + 1) * DN_CHUNK]


def _chunk_tri():
    idx = np.arange(TM)
    same = (idx[:, None] // DN_CHUNK) == (idx[None, :] // DN_CHUNK)
    fwd = same & (idx[None, :] <= idx[:, None])
    bwd = same & (idx[None, :] >= idx[:, None])
    return jnp.asarray(np.stack([fwd, bwd]).astype(np.float32))


def dn_prep(qkv, ab, conv_w, a_log, dt_bias, t_ctx, l_ctx, l_lat):
    t, c = qkv.shape
    hd = DN_HEADS * DN_DK
    nega = jnp.zeros((1, LANES), F32).at[0, :2 * DN_HEADS].set(-jnp.exp(a_log.astype(F32)).reshape(-1))
    dtb = jnp.zeros((1, LANES), F32).at[0, :2 * DN_HEADS].set(dt_bias.astype(F32).reshape(-1))
    cw = jnp.zeros((SUBLANES, c), F32).at[:conv_w.shape[0]].set(conv_w)
    nch = TM // DN_CHUNK
    row_spec = lambda w: pl.BlockSpec((TM, w), lambda i: (i, 0))
    return pl.pallas_call(
        functools.partial(_dn_prep_kernel, t_ctx=t_ctx, l_ctx=l_ctx, l_lat=l_lat),
        grid=(t // TM,),
        in_specs=_halo_specs(c, TM, t) + [
            _const_spec((SUBLANES, c)),
            row_spec(LANES),
            _const_spec((1, LANES)),
            _const_spec((1, LANES)),
            _const_spec((2, TM, TM)),
        ],
        out_specs=[
            row_spec(hd), row_spec(hd), row_spec(hd),
            pl.BlockSpec((2, TM, LANES), lambda i: (0, i, 0)),
            row_spec(LANES),
            pl.BlockSpec((2, nch, 2 * DN_HEADS, DN_CHUNK), lambda i: (0, i, 0, 0)),
        ],
        out_shape=[
            jax.ShapeDtypeStruct((t, hd), BF16),
            jax.ShapeDtypeStruct((t, hd), BF16),
            jax.ShapeDtypeStruct((t, hd), BF16),
            jax.ShapeDtypeStruct((2, t, LANES), F32),
            jax.ShapeDtypeStruct((t, LANES), F32),
            jax.ShapeDtypeStruct((2, t // DN_CHUNK, 2 * DN_HEADS, DN_CHUNK), F32),
        ],
        compiler_params=_cparams(("arbitrary",)),
        name="dn_prep",
    )(qkv, qkv, qkv, cw, ab, nega, dtb, _chunk_tri())


def _bmm(a, b):
    return _dot(a.astype(BF16), b.astype(BF16))


def _inv_unit_tri_batch(lms, in_blk, eye):
    dgs = [jnp.where(in_blk, lm, 0.0) for lm in lms]
    offs = [lm - dg for lm, dg in zip(lms, dgs)]
    n1 = [-dg for dg in dgs]
    n2 = [_bmm(a, a) for a in n1]
    p = [eye + a for a in n1]
    n4 = [_bmm(a, a) for a in n2]
    p = [x + _bmm(x, a) for x, a in zip(p, n2)]
    n8 = [_bmm(a, a) for a in n4]
    p = [x + _bmm(x, a) for x, a in zip(p, n4)]
    p = [x + _bmm(x, a) for x, a in zip(p, n8)]
    m = [_bmm(x, o) for x, o in zip(p, offs)]
    m2 = [_bmm(a, a) for a in m]
    t1 = [eye - a for a in m]
    t1 = [x + _bmm(x, a) for x, a in zip(t1, m2)]
    return [_bmm(x, y) for x, y in zip(t1, p)]


def _dn_chunk_kernel(blk_ref, seq_ref, first_ref, last_ref,
                     q_ref, k_ref, v_ref, gc_ref, beta_ref, gct_ref, s0_ref,
                     o_ref, sfin_ref, s_scr, attn_scr, u_scr, qw_scr, kd_scr, gl_scr,
                     *, direction, n_zero_init):
    step = pl.program_id(0)
    c_sz = DN_CHUNK
    nch = TM // c_sz
    heads = range(DN_HEADS)

    @pl.when(first_ref[step] == 1)
    def _():
        s_scr[...] = jnp.where(seq_ref[step] >= n_zero_init, s0_ref[0, 0], 0.0)

    row = lax.broadcasted_iota(jnp.int32, (c_sz, c_sz), 0)
    col = lax.broadcasted_iota(jnp.int32, (c_sz, c_sz), 1)
    if direction == 0:
        incl, strict = row >= col, row > col
        last = c_sz - 1
    else:
        incl, strict = row <= col, row < col
        last = 0
    in_blk = (row // 16) == (col // 16)
    eye = (row == col).astype(F32)

    def lanes(h):
        return slice(h * DN_DK, (h + 1) * DN_DK)

    def prep_body(it, carry):
        units = [(it * DN_PREP_CHUNKS + c, h) for c in range(DN_PREP_CHUNKS) for h in heads]
        idx = range(len(units))
        rows = [pl.ds(pl.multiple_of(ch * c_sz, c_sz), c_sz) for ch, _ in units]
        gls = [direction * DN_HEADS + h for _, h in units]
        qs = [q_ref[rows[i], lanes(units[i][1])].astype(F32) for i in idx]
        ks = [k_ref[rows[i], lanes(units[i][1])].astype(F32) for i in idx]
        gcc = [gc_ref[0, rows[i], gls[i]:gls[i] + 1] for i in idx]
        beta = [beta_ref[rows[i], 2 * DN_HEADS + gls[i]:2 * DN_HEADS + gls[i] + 1] for i in idx]
        gcr = [gct_ref[0, units[i][0]][gls[i]:gls[i] + 1, :] for i in idx]
        dec = [jnp.exp(jnp.where(incl, gcc[i] - gcr[i], -jnp.inf)) for i in idx]
        kb = [ks[i] * beta[i] for i in idx]
        prod = [lax.dot_general(jnp.concatenate([qs[i], kb[i]], axis=0).astype(BF16), ks[i].astype(BF16),
                                (((1,), (1,)), ((), ())), preferred_element_type=F32)
                for i in idx]
        for i, (ch, h) in enumerate(units):
            attn_scr[ch, h] = (prod[i][:c_sz] * dec[i]).astype(BF16)
        lms = [jnp.where(strict, prod[i][c_sz:] * dec[i], 0.0) for i in idx]
        tinv = _inv_unit_tri_batch(lms, in_blk, eye)
        eg = [jnp.exp(gcc[i]) for i in idx]
        uw = [_bmm(tinv[i], jnp.concatenate(
                  [v_ref[rows[i], lanes(units[i][1])].astype(F32) * beta[i], kb[i] * eg[i]], axis=1))
              for i in idx]
        for i, (ch, h) in enumerate(units):
            u_scr[rows[i], lanes(h)] = uw[i][:, :DN_DV]
            qw_scr[ch, h] = jnp.concatenate([qs[i] * eg[i], uw[i][:, DN_DV:]], axis=0).astype(BF16)
            g_last = gcc[i][last:last + 1, :]
            kd_scr[ch, h] = (ks[i] * jnp.exp(g_last - gcc[i])).T.astype(BF16)
            gl_scr[ch, h] = jnp.broadcast_to(jnp.exp(g_last), (1, DN_DV))
        return carry

    lax.fori_loop(0, nch // DN_PREP_CHUNKS, prep_body, 0)

    def scan_body(ci, carry):
        ch = ci if direction == 0 else nch - 1 - ci
        r0 = pl.multiple_of(ch * c_sz, c_sz)
        rows = pl.ds(r0, c_sz)
        s = [s_scr[h] for h in heads]
        qs_ws = [_dot(qw_scr[ch, h], s[h].astype(BF16)) for h in heads]
        vb = [(u_scr[rows, lanes(h)] - qs_ws[h][c_sz:]).astype(BF16) for h in heads]
        for h in heads:
            o_ref[rows, lanes(h)] = (qs_ws[h][:c_sz] + _dot(attn_scr[ch, h], vb[h])).astype(o_ref.dtype)
            s_scr[h] = s[h] * gl_scr[ch, h] + _dot(kd_scr[ch, h], vb[h])
        return carry

    lax.fori_loop(0, nch, scan_body, 0)

    @pl.when(last_ref[step] == 1)
    def _():
        sfin_ref[0, 0] = s_scr[...]


def _scan_tables(direction, seq_lens):
    blk, seq, first, last = [], [], [], []
    base = 0
    for s, length in enumerate(seq_lens):
        nblk = length // TM
        order = range(nblk) if direction == 0 else range(nblk - 1, -1, -1)
        for pos, jj in enumerate(order):
            blk.append(base + jj)
            seq.append(s)
            first.append(int(pos == 0))
            last.append(int(pos == nblk - 1))
        base += nblk
    return [jnp.asarray(np.asarray(a, np.int32)) for a in (blk, seq, first, last)]


def dn_chunk_scan(q, k, v, gc, beta, gct, s0, *, direction, seq_lens, n_zero_init):
    t, hd = q.shape
    nch = TM // DN_CHUNK
    n_seq = len(seq_lens)
    tables = _scan_tables(direction, seq_lens)
    n_steps = int(tables[0].shape[0])
    row_spec = lambda w: pl.BlockSpec((TM, w), lambda i, blk, seq, fst, lst: (blk[i], 0))
    state_blk = (1, 1, DN_HEADS, DN_DK, DN_DV)
    grid_spec = pltpu.PrefetchScalarGridSpec(
        num_scalar_prefetch=4,
        grid=(n_steps,),
        in_specs=[
            row_spec(hd), row_spec(hd), row_spec(hd),
            pl.BlockSpec((1, TM, LANES), lambda i, blk, seq, fst, lst: (direction, blk[i], 0)),
            row_spec(LANES),
            pl.BlockSpec((1, nch, 2 * DN_HEADS, DN_CHUNK),
                         lambda i, blk, seq, fst, lst: (direction, blk[i], 0, 0)),
            pl.BlockSpec(state_blk, lambda i, blk, seq, fst, lst:
                         (jnp.maximum(seq[i] - n_zero_init, 0), direction, 0, 0, 0)),
        ],
        out_specs=[
            row_spec(hd),
            pl.BlockSpec(state_blk, lambda i, blk, seq, fst, lst: (seq[i], 0, 0, 0, 0)),
        ],
        scratch_shapes=[
            pltpu.VMEM((DN_HEADS, DN_DK, DN_DV), F32),
            pltpu.VMEM((nch, DN_HEADS, DN_CHUNK, DN_CHUNK), BF16),
            pltpu.VMEM((TM, hd), F32),
            pltpu.VMEM((nch, DN_HEADS, 2 * DN_CHUNK, DN_DK), BF16),
            pltpu.VMEM((nch, DN_HEADS, DN_DK, DN_CHUNK), BF16),
            pltpu.VMEM((nch, DN_HEADS, 1, DN_DV), F32),
        ],
    )
    o, sfin = pl.pallas_call(
        functools.partial(_dn_chunk_kernel, direction=direction, n_zero_init=n_zero_init),
        grid_spec=grid_spec,
        out_shape=[
            jax.ShapeDtypeStruct((t, hd), BF16),
            jax.ShapeDtypeStruct((n_seq,) + state_blk[1:], F32),
        ],
        compiler_params=_cparams(("arbitrary",)),
        name="dn_chunk_scan_d%d" % direction,
    )(*tables, q, k, v, gc, beta, gct, s0)
    return o, sfin[:, 0]


def _dn_out_kernel(of_ref, ob_ref, z_ref, gh_ref, w_ref, *refs, n_x, n_ctx_tiles, gate_idx):
    x = _load_stream(refs[:n_x], n_ctx_tiles)
    mod_ref, out_ref = refs[n_x:]
    o = of_ref[...].astype(F32) + ob_ref[...].astype(F32)
    z = z_ref[...].astype(F32)
    parts = []
    for h in range(DN_HEADS):
        lanes = slice(h * DN_DV, (h + 1) * DN_DV)
        oh = o[:, lanes]
        ms = jnp.mean(oh * oh, axis=-1, keepdims=True)
        parts.append(oh * lax.rsqrt(ms + EPS) * gh_ref[...])
    on = jnp.concatenate(parts, axis=1) * _silu(z)
    y = _dot(on.astype(BF16), w_ref[...])
    out_ref[...] = x + mod_ref[0, gate_idx:gate_idx + 1, :] * y


def dn_out(o_f, o_b, z, g_head, w_out, x, mod, gate_idx, t_ctx, l_lat):
    t, d, x_specs, x_args = _stream_operands(x, t_ctx, l_lat)
    hd = o_f.shape[1]
    grp = functools.partial(_group_of_tile, tm=TM, t_ctx=t_ctx, l_lat=l_lat)
    row_spec = lambda w: pl.BlockSpec((TM, w), lambda i: (i, 0))
    return pl.pallas_call(
        functools.partial(_dn_out_kernel, n_x=len(x_args), n_ctx_tiles=t_ctx // TM, gate_idx=gate_idx),
        grid=(t // TM,),
        in_specs=[
            row_spec(hd), row_spec(hd), row_spec(hd),
            _const_spec((1, DN_DV)),
            _const_spec(w_out.shape),
        ] + x_specs + [
            pl.BlockSpec((1,) + mod.shape[1:], lambda i: (grp(i), 0, 0)),
        ],
        out_specs=row_spec(d),
        out_shape=jax.ShapeDtypeStruct((t, d), F32),
        compiler_params=_cparams(("arbitrary",)),
        name="dn_out",
    )(o_f, o_b, z, g_head.reshape(1, DN_DV).astype(F32), w_out, *x_args, mod)


ROUTE_ROWS = 8


def _first_argmax_rows(vals, row):
    m = jnp.max(vals, axis=0, keepdims=True)
    idx = jnp.min(jnp.where(vals == m, row, float(SUBLANES)), axis=0, keepdims=True)
    return m, idx


def _router_kernel(x_ref, g_ref, mod_ref, wh_ref, wl_ref, rb_ref, tri_ref,
                   h_ref, route_ref, count_ref, run_scr, *, shift_idx, scale_idx):
    i = pl.program_id(0)

    @pl.when(i == 0)
    def _():
        run_scr[...] = jnp.zeros_like(run_scr)

    h = _norm_mod(x_ref[...], g_ref[...], mod_ref, shift_idx, scale_idx)
    hb = h.astype(BF16)
    h_ref[...] = hb
    hl = (h - hb.astype(F32)).astype(BF16)
    nt = (((1,), (1,)), ((), ()))
    wh = wh_ref[...]
    logits = (lax.dot_general(wh, hb, nt, preferred_element_type=F32)
              + (lax.dot_general(wh, hl, nt, preferred_element_type=F32)
                 + lax.dot_general(wl_ref[...], hb, nt, preferred_element_type=F32)))
    scores = _sigmoid(logits)
    sel = scores + rb_ref[:, 0:1]
    tm = scores.shape[1]
    row8 = lax.broadcasted_iota(jnp.int32, (SUBLANES, tm), 0).astype(F32)
    best_score = None
    for g in range(N_GROUPS):
        s_g = sel[g * EXPERTS_PER_GROUP:(g + 1) * EXPERTS_PER_GROUP, :]
        m1, i1 = _first_argmax_rows(s_g, row8)
        m2, i2 = _first_argmax_rows(jnp.where(row8 == i1, -jnp.inf, s_g), row8)
        gs = m1 + m2
        e1, e2 = i1 + float(g * EXPERTS_PER_GROUP), i2 + float(g * EXPERTS_PER_GROUP)
        if best_score is None:
            best_score, b1, b2 = gs, e1, e2
        else:
            better = gs > best_score
            best_score = jnp.where(better, gs, best_score)
            b1 = jnp.where(better, e1, b1)
            b2 = jnp.where(better, e2, b2)
    row_e = lax.broadcasted_iota(jnp.int32, (N_EXPERTS, tm), 0).astype(F32)
    pick1, pick2 = row_e == b1, row_e == b2
    w1 = jnp.sum(jnp.where(pick1, scores, 0.0), axis=0, keepdims=True)
    w2 = jnp.sum(jnp.where(pick2, scores, 0.0), axis=0, keepdims=True)
    tot = w1 + w2
    chosen = jnp.where(pick1 | pick2, 1.0, 0.0)
    before = run_scr[:, 0:1] + _dot(chosen.astype(BF16), tri_ref[...])
    r1 = jnp.sum(jnp.where(pick1, before, 0.0), axis=0, keepdims=True)
    r2 = jnp.sum(jnp.where(pick2, before, 0.0), axis=0, keepdims=True)
    run_scr[...] = run_scr[...] + jnp.sum(chosen, axis=1, keepdims=True)
    count_ref[...] = run_scr[...]
    zero = jnp.zeros_like(w1)
    route_ref[...] = jnp.concatenate([b1, b2, w1 / tot, w2 / tot, r1, r2, zero, zero], axis=0)


def moe_router(x, g, mod, shift_idx, scale_idx, w_router, router_bias, t_ctx, l_lat):
    t, d = x.shape
    grp = functools.partial(_group_of_tile, tm=TM, t_ctx=t_ctx, l_lat=l_lat)
    wr = w_router.astype(F32).T
    wh = wr.astype(BF16)
    wl = (wr - wh.astype(F32)).astype(BF16)
    rb = jnp.broadcast_to(router_bias.astype(F32)[:, None], (N_EXPERTS, LANES))
    idx = np.arange(TM)
    tri = jnp.asarray((idx[:, None] < idx[None, :]).astype(np.float32)).astype(BF16)
    return pl.pallas_call(
        functools.partial(_router_kernel, shift_idx=shift_idx, scale_idx=scale_idx),
        grid=(t // TM,),
        in_specs=[
            pl.BlockSpec((TM, d), lambda i: (i, 0)),
            _const_spec((1, d)),
            pl.BlockSpec((1,) + mod.shape[1:], lambda i: (grp(i), 0, 0)),
            _const_spec((N_EXPERTS, d)),
            _const_spec((N_EXPERTS, d)),
            _const_spec((N_EXPERTS, LANES)),
            _const_spec((TM, TM)),
        ],
        out_specs=[pl.BlockSpec((TM, d), lambda i: (i, 0)),
                   pl.BlockSpec((ROUTE_ROWS, TM), lambda i: (0, i)),
                   _const_spec((N_EXPERTS, LANES))],
        out_shape=[
            jax.ShapeDtypeStruct((t, d), BF16),
            jax.ShapeDtypeStruct((ROUTE_ROWS, t), F32),
            jax.ShapeDtypeStruct((N_EXPERTS, LANES), F32),
        ],
        scratch_shapes=[pltpu.VMEM((N_EXPERTS, LANES), F32)],
        compiler_params=_cparams(("arbitrary",)),
        name="moe_router",
    )(x, g.reshape(1, d), mod, wh, wl, rb, tri)


def _expert_kernel(te_ref, nt_ref, first_ref, nxt_ref, par_ref,
                   xs_ref, w1_hbm, w3_hbm, w2_hbm, ys_ref,
                   w1_buf, w3_buf, w2_buf, w1_bf, w3_bf, w2_bf, sem, *, layer):
    i = pl.program_id(0)
    active = i < nt_ref[0]

    def weight_copies(e, slot):
        return (pltpu.make_async_copy(w1_hbm.at[layer, e], w1_buf.at[slot], sem.at[slot, 0]),
                pltpu.make_async_copy(w3_hbm.at[layer, e], w3_buf.at[slot], sem.at[slot, 1]),
                pltpu.make_async_copy(w2_hbm.at[layer, e], w2_buf.at[slot], sem.at[slot, 2]))

    @pl.when(active & (i == 0))
    def _():
        for cp in weight_copies(te_ref[0], 0):
            cp.start()

    @pl.when(active & (first_ref[i] == 1))
    def _():
        slot = par_ref[i]
        for cp in weight_copies(te_ref[i], slot):
            cp.wait()

        @pl.when(nxt_ref[i] >= 0)
        def _():
            for cp in weight_copies(nxt_ref[i], 1 - slot):
                cp.start(priority=1)

        w1_bf[...] = w1_buf[slot].astype(BF16)
        w3_bf[...] = w3_buf[slot].astype(BF16)
        w2_bf[...] = w2_buf[slot].astype(BF16)

    @pl.when(active)
    def _():
        xs = xs_ref[...]
        a = _dot(xs, w1_bf[...])
        b = _dot(xs, w3_bf[...])
        hid = (_silu(a) * b).astype(BF16)
        ys_ref[...] = _dot(hid, w2_bf[...]).astype(ys_ref.dtype)

    @pl.when(jnp.logical_not(active))
    def _():
        ys_ref[...] = jnp.zeros_like(ys_ref)


def moe_experts(xs, plan, w1, w3, w2, layer):
    p, d = xs.shape
    de = w1.shape[-1]
    ntile = p // MOE_TILE
    tile_map = lambda i, *_: (i, 0)
    grid_spec = pltpu.PrefetchScalarGridSpec(
        num_scalar_prefetch=5,
        grid=(ntile,),
        in_specs=[
            pl.BlockSpec((MOE_TILE, d), tile_map),
            pl.BlockSpec(memory_space=pl.ANY),
            pl.BlockSpec(memory_space=pl.ANY),
            pl.BlockSpec(memory_space=pl.ANY),
        ],
        out_specs=pl.BlockSpec((MOE_TILE, d), tile_map),
        scratch_shapes=[
            pltpu.VMEM((2, d, de), F32), pltpu.VMEM((2, d, de), F32), pltpu.VMEM((2, de, d), F32),
            pltpu.VMEM((d, de), BF16), pltpu.VMEM((d, de), BF16), pltpu.VMEM((de, d), BF16),
            pltpu.SemaphoreType.DMA((2, 3)),
        ],
    )
    return pl.pallas_call(
        functools.partial(_expert_kernel, layer=layer),
        grid_spec=grid_spec,
        out_shape=jax.ShapeDtypeStruct((p, d), BF16),
        compiler_params=_cparams(("arbitrary",)),
        name="moe_experts",
    )(plan["tile_expert"], plan["n_tiles"], plan["first"], plan["next_expert"], plan["parity"],
      xs, w1, w3, w2)


def moe_dispatch_plan(route, counts):
    t = route.shape[1]
    p = 2 * t + N_EXPERTS * MOE_TILE
    p = -(-p // MOE_TILE) * MOE_TILE
    ntile = p // MOE_TILE
    experts = jnp.arange(N_EXPERTS, dtype=jnp.int32)
    expert = route[0:2].astype(jnp.int32)
    rank = route[4:6].astype(jnp.int32)
    padded = ((counts + MOE_TILE - 1) // MOE_TILE) * MOE_TILE
    ends = jnp.cumsum(padded)
    starts = ends - padded
    start_of = jnp.sum(jnp.where(expert[..., None] == experts, starts, 0), axis=-1)
    slot = start_of + rank
    token = jnp.tile(jnp.arange(t, dtype=jnp.int32), 2)
    slot_token = jnp.zeros((p,), jnp.int32).at[slot.reshape(-1)].set(
        token, unique_indices=True, mode="promise_in_bounds")
    tile_id = jnp.arange(ntile, dtype=jnp.int32)
    tile_expert = jnp.sum((tile_id[:, None] * MOE_TILE >= ends[None, :]).astype(jnp.int32), axis=1)
    n_tiles = (ends[-1] // MOE_TILE).astype(jnp.int32)
    used = tile_id < n_tiles
    tile_expert = jnp.where(used, tile_expert, tile_expert[jnp.maximum(n_tiles - 1, 0)])
    tile_expert = jnp.minimum(tile_expert, N_EXPERTS - 1).astype(jnp.int32)
    prev = jnp.concatenate([jnp.full((1,), -1, jnp.int32), tile_expert[:-1]])
    first = (used & (tile_expert != prev)).astype(jnp.int32)
    parity = ((jnp.cumsum(first) - 1) % 2).astype(jnp.int32)
    cand = jnp.where(counts > 0, experts, N_EXPERTS)
    later = lax.cummin(cand[::-1])[::-1]
    nxt_e = jnp.concatenate([later[1:], jnp.full((1,), N_EXPERTS, jnp.int32)])
    nxt_e = jnp.where(nxt_e >= N_EXPERTS, -1, nxt_e).astype(jnp.int32)
    next_expert = nxt_e[tile_expert]
    return dict(slot_token=slot_token, slot=slot, tile_expert=tile_expert,
                n_tiles=n_tiles.reshape(1), first=first, next_expert=next_expert, parity=parity)


def _combine_kernel(x_ref, y0_ref, y1_ref, wt_ref, mod_ref, *refs, gate_idx, final_norm, n_ctx_tiles):
    moe = wt_ref[:, 0:1] * y0_ref[...].astype(F32) + wt_ref[:, 1:2] * y1_ref[...].astype(F32)
    x = x_ref[...] + mod_ref[0, gate_idx:gate_idx + 1, :] * moe
    if final_norm:
        gf_ref, ctx_ref, lat_ref = refs
        ms = jnp.mean(x * x, axis=-1, keepdims=True)
        y = x * lax.rsqrt(ms + EPS) * gf_ref[...]
        lat_ref[...] = y

        @pl.when(pl.program_id(0) < n_ctx_tiles)
        def _():
            ctx_ref[...] = y
    else:
        (out_ref,) = refs
        out_ref[...] = x


def moe_combine(x, y0, y1, wt, mod, gate_idx, t_ctx, l_lat, g_final=None):
    t, d = x.shape
    grp = functools.partial(_group_of_tile, tm=TM, t_ctx=t_ctx, l_lat=l_lat)
    row_spec = lambda w: pl.BlockSpec((TM, w), lambda i: (i, 0))
    in_specs = [row_spec(d), row_spec(d), row_spec(d), row_spec(wt.shape[1]),
                pl.BlockSpec((1,) + mod.shape[1:], lambda i: (grp(i), 0, 0))]
    args = [x, y0, y1, wt, mod]
    n_ctx_tiles = t_ctx // TM
    if g_final is not None:
        in_specs.append(_const_spec((1, d)))
        args.append(g_final.reshape(1, d).astype(F32))
        out_specs = [pl.BlockSpec((TM, d), lambda i: (jnp.minimum(i, n_ctx_tiles - 1), 0)),
                     pl.BlockSpec((TM, d), lambda i: (jnp.maximum(i - n_ctx_tiles, 0), 0))]
        out_shape = [jax.ShapeDtypeStruct((t_ctx, d), F32), jax.ShapeDtypeStruct((t - t_ctx, d), F32)]
    else:
        out_specs = row_spec(d)
        out_shape = jax.ShapeDtypeStruct((t, d), F32)
    return pl.pallas_call(
        functools.partial(_combine_kernel, gate_idx=gate_idx, final_norm=g_final is not None,
                          n_ctx_tiles=n_ctx_tiles),
        grid=(t // TM,),
        in_specs=in_specs,
        out_specs=out_specs,
        out_shape=out_shape,
        compiler_params=_cparams(("arbitrary",)),
        name="moe_combine",
    )(*args)


def moe_layer(x, g, mod, w_router, router_bias, w1, w3, w2, layer, t_ctx, l_lat, g_final=None):
    h, route, counts = moe_router(x, g, mod, 3, 4, w_router, router_bias, t_ctx, l_lat)
    plan = moe_dispatch_plan(route, counts[:, 0].astype(jnp.int32))
    take = lambda rows, index: rows.at[index].get(mode="promise_in_bounds")
    xs = take(h, plan["slot_token"])
    ys = moe_experts(xs, plan, w1, w3, w2, layer)
    y0 = take(ys, plan["slot"][0])
    y1 = take(ys, plan["slot"][1])
    return moe_combine(x, y0, y1, route[2:4].T, mod, 5, t_ctx, l_lat, g_final)


def _hy_conv_kernel(u_ref, up_ref, un_ref, cw_ref, cb_ref, x0_ref, vv_ref, *, t_ctx, l_ctx, l_lat):
    i = pl.program_id(0)
    has_prev, has_next = _seq_flags(i, TM, t_ctx, l_ctx, l_lat)
    c = _conv3(u_ref[...], up_ref[...], un_ref[...], cw_ref, has_prev, has_next) + cb_ref[...]
    d = c.shape[1] // 3
    x0_ref[...] = c[:, :d].astype(x0_ref.dtype)
    vv_ref[...] = (c[:, 2 * d:] * c[:, d:2 * d]).astype(vv_ref.dtype)


def hy_conv_gate(u, conv_w, conv_b, t_ctx, l_ctx, l_lat):
    t, c = u.shape
    d = c // 3
    cw = jnp.zeros((SUBLANES, c), F32).at[:conv_w.shape[0]].set(conv_w)
    row_spec = lambda w: pl.BlockSpec((TM, w), lambda i: (i, 0))
    return pl.pallas_call(
        functools.partial(_hy_conv_kernel, t_ctx=t_ctx, l_ctx=l_ctx, l_lat=l_lat),
        grid=(t // TM,),
        in_specs=_halo_specs(c, TM, t) + [_const_spec((SUBLANES, c)), _const_spec((1, c))],
        out_specs=[row_spec(d), row_spec(d)],
        out_shape=[jax.ShapeDtypeStruct((t, d), BF16), jax.ShapeDtypeStruct((t, d), BF16)],
        compiler_params=_cparams(("arbitrary",)),
        name="hy_conv_gate",
    )(u, u, u, cw, conv_b.reshape(1, c).astype(F32))


def _hy_features(n):
    t = np.linspace(0.0, 1.0, n)[:, None]
    bands = (HY_EMB - 1) // 2
    band_w = np.linspace(1e-4, bands - 1, bands)[None, :]
    ang = (2.0 * math.pi / n) * np.arange(n)[:, None] * band_w
    z = np.concatenate([t, np.cos(ang), -np.sin(ang)], axis=-1)
    zp = np.zeros((n, LANES))
    zp[:, :HY_EMB] = z
    return jnp.asarray(zp.astype(np.float32))


def _hy_hid_kernel(z_ref, f_ref, w1_ref, b1_ref, w2_ref, b2_ref, w3_ref, b3_ref, hid_ref):
    h = jnp.sin(f_ref[0:1, :] * (_dot(z_ref[...], w1_ref[...], HI) + b1_ref[...]))
    h = jnp.sin(f_ref[1:2, :] * (_dot(h, w2_ref[...], HI) + b2_ref[...]))
    h = jnp.sin(f_ref[2:3, :] * (_dot(h, w3_ref[...], HI) + b3_ref[...]))
    hid_ref[...] = h


def _hy_filt_kernel(hid_ref, w4_ref, z_ref, dl_ref, filt_ref):
    f = _dot3(hid_ref[...], w4_ref[...])
    f = f * jnp.exp(-z_ref[:, 0:1] * jnp.abs(dl_ref[...]))
    nrm = jnp.sum(jnp.abs(f), axis=0, keepdims=True)
    filt_ref[...] = f / nrm


def hy_filter(n, d, freq, w1, b1, w2, b2, w3, b3, w4):
    z = _hy_features(n)
    o = HY_ORDER
    w1p = jnp.zeros((LANES, o), F32).at[:HY_EMB].set(w1.astype(F32))
    fp = jnp.zeros((SUBLANES, o), F32).at[:3].set(freq.astype(F32))
    tr = min(n, TM)
    hid = pl.pallas_call(
        _hy_hid_kernel,
        grid=(n // tr,),
        in_specs=[pl.BlockSpec((tr, LANES), lambda i: (i, 0)), _const_spec((SUBLANES, o)),
                  _const_spec((LANES, o)), _const_spec((1, o)),
                  _const_spec((o, o)), _const_spec((1, o)),
                  _const_spec((o, o)), _const_spec((1, o))],
        out_specs=pl.BlockSpec((tr, o), lambda i: (i, 0)),
        out_shape=jax.ShapeDtypeStruct((n, o), F32),
        compiler_params=_cparams(("arbitrary",)),
        name="hy_filter_mlp",
    )(z, fp, w1p, b1.reshape(1, o).astype(F32), w2.astype(F32), b2.reshape(1, o).astype(F32),
      w3.astype(F32), b3.reshape(1, o).astype(F32))
    max_decay = math.log(HY_DECAY_TARGET) / HY_DECAY_SHORT_PCT
    min_decay = math.log(HY_DECAY_TARGET) / HY_DECAY_LONG_PCT
    deltas = np.tile(np.linspace(min_decay, max_decay, d), 2).astype(np.float32)[None, :]
    tc = 256
    return pl.pallas_call(
        _hy_filt_kernel,
        grid=(2 * d // tc,),
        in_specs=[_const_spec((n, o)), pl.BlockSpec((o, tc), lambda j: (0, j)),
                  _const_spec((n, LANES)), pl.BlockSpec((1, tc), lambda j: (0, j))],
        out_specs=pl.BlockSpec((n, tc), lambda j: (0, j)),
        out_shape=jax.ShapeDtypeStruct((n, 2 * d), F32),
        compiler_params=_cparams(("arbitrary",)),
        name="hy_filter_window",
    )(hid, w4.astype(F32), z, jnp.asarray(deltas))


def _dense_dft_tables(l):
    n = 2 * l
    k = np.arange(n)[:, None]
    j = np.arange(l)[None, :]
    th = 2.0 * math.pi * ((k * j) % n) / n
    fwd = np.concatenate([np.cos(th), -np.sin(th)], axis=0)
    inv = np.concatenate([np.cos(th).T, -np.sin(th).T], axis=1) / n
    return (jnp.asarray(fwd.astype(np.float32)).astype(BF16),
            jnp.asarray(inv.astype(np.float32)).astype(BF16))


def _dense_tf_kernel(hf_ref, hb_ref, fwd_ref, tf_ref):
    n = fwd_ref.shape[0] // 2
    hf = hf_ref[...]
    hb = hb_ref[...]
    tf_ref[0:n, :] = _dot(fwd_ref[0:n, :], (hf + hb).astype(BF16))
    tf_ref[n:, :] = _dot(fwd_ref[n:, :], (hf - hb).astype(BF16))


def _dense_conv_kernel(vv_ref, x0_ref, skip_ref, tf_ref, fwd_ref, inv_ref, out_ref):
    n = fwd_ref.shape[0] // 2
    vv = vv_ref[...]
    x = _dot(fwd_ref[...], vv.astype(BF16))
    xr, xi = x[:n], x[n:]
    tr, ti = tf_ref[0:n, :], tf_ref[n:, :]
    y = jnp.concatenate([xr * tr - xi * ti, xr * ti + xi * tr], axis=0).astype(BF16)
    conv = _dot(inv_ref[...], y)
    out = (conv + vv.astype(F32) * skip_ref[...]) * x0_ref[...].astype(F32)
    out_ref[...] = out.astype(out_ref.dtype)


def hy_longconv_dense(vv, x0, skip, filt, n_seq, l):
    d = vv.shape[1]
    n = 2 * l
    fwd, inv = _dense_dft_tables(l)
    dc = 512
    tf = pl.pallas_call(
        _dense_tf_kernel,
        grid=(d // dc,),
        in_specs=[pl.BlockSpec((l, dc), lambda j: (0, j)),
                  pl.BlockSpec((l, dc), lambda j: (0, d // dc + j)),
                  _const_spec(fwd.shape)],
        out_specs=pl.BlockSpec((2 * n, dc), lambda j: (0, j)),
        out_shape=jax.ShapeDtypeStruct((2 * n, d), F32),
        compiler_params=_cparams(("arbitrary",)),
        name="hy_tf_dense",
    )(filt, filt, fwd)
    return pl.pallas_call(
        _dense_conv_kernel,
        grid=(d // dc, n_seq),
        in_specs=[pl.BlockSpec((l, dc), lambda j, s: (s, j)),
                  pl.BlockSpec((l, dc), lambda j, s: (s, j)),
                  pl.BlockSpec((1, dc), lambda j, s: (0, j)),
                  pl.BlockSpec((2 * n, dc), lambda j, s: (0, j)),
                  _const_spec(fwd.shape), _const_spec(inv.shape)],
        out_specs=pl.BlockSpec((l, dc), lambda j, s: (s, j)),
        out_shape=jax.ShapeDtypeStruct((n_seq * l, d), BF16),
        compiler_params=_cparams(("arbitrary", "arbitrary")),
        name="hy_longconv_dense",
    )(vv, x0, skip.reshape(1, d).astype(F32), tf, fwd, inv)


def _two_stage_tables(l):
    n = 2 * l
    n2 = FFT_N2
    n1_full = n // n2
    n_slab = n1_full // 2 + 1
    n1 = -(-n_slab // FFT_UNROLL_SLAB) * FFT_UNROLL_SLAB
    p = l // n2
    a = np.arange(p)[None, None, :]
    c = np.arange(n1)[None, :, None]
    b = np.arange(n2)[:, None, None]
    live = (c < n_slab).astype(np.float64)
    weight = np.where((c == 0) | (c == n1_full // 2), 1.0, 2.0) * live
    ph = 2.0 * math.pi * (((a * c) % n1_full) / n1_full + ((b * c) % n) / n)
    f1 = np.concatenate([np.cos(ph) * live, -np.sin(ph) * live], axis=1)
    pht = np.transpose(ph, (0, 2, 1))
    wt = np.transpose(weight, (0, 2, 1)) / n
    g1 = np.concatenate([np.cos(pht) * wt, -np.sin(pht) * wt], axis=2)
    e = np.arange(n2)[:, None]
    bb = np.arange(n2)[None, :]
    th = 2.0 * math.pi * ((e * bb) % n2) / n2
    fr, fi = np.cos(th), -np.sin(th)
    f2 = np.block([[fr, -fi], [fi, fr]])
    f2c = np.block([[fr, fi], [-fi, fr]])
    cast = lambda m: jnp.asarray(m.astype(np.float32)).astype(BF16)
    return cast(f1), cast(f2), cast(f2c), cast(g1)


def _unrolled_loop(n, unroll, fn):
    def body(i, carry):
        fn([i * unroll + u for u in range(unroll)])
        return carry

    lax.fori_loop(0, n // unroll, body, 0)


FFT_UNROLL_SMALL = 16
FFT_UNROLL_SLAB = 4


def _stage1(src_ref, f1_ref, w_scr, combine=None):
    n2, two_n1, p = f1_ref.shape

    def step(bs):
        if combine is None:
            xs = [src_ref[pl.ds(b, p, stride=n2), :] for b in bs]
        else:
            xs = [combine(src_ref[0][pl.ds(b, p, stride=n2), :], src_ref[1][pl.ds(b, p, stride=n2), :])
                  for b in bs]
        res = [_dot(f1_ref[b], x.astype(BF16)) for b, x in zip(bs, xs)]
        for b, r in zip(bs, res):
            w_scr[pl.ds(pl.multiple_of(b * two_n1, two_n1), two_n1), :] = r

    _unrolled_loop(n2, FFT_UNROLL_SMALL, step)


def _load_slab(w_scr, c, n1, n2):
    re = w_scr[pl.ds(c, n2, stride=2 * n1), :]
    im = w_scr[pl.ds(n1 + c, n2, stride=2 * n1), :]
    return jnp.concatenate([re, im], axis=0).astype(BF16)


def _two_stage_tf_kernel(hf_ref, hb_ref, f1_ref, f2_ref, tfr_ref, tfi_ref, ws_scr, wd_scr):
    n2, two_n1, _ = f1_ref.shape
    n1 = two_n1 // 2
    _stage1((hf_ref, hb_ref), f1_ref, ws_scr, combine=lambda u, v: u + v)
    _stage1((hf_ref, hb_ref), f1_ref, wd_scr, combine=lambda u, v: u - v)

    def step(cs):
        xs = [_dot(f2_ref[0:n2, :], _load_slab(ws_scr, c, n1, n2)) for c in cs]
        xd = [_dot(f2_ref[n2:, :], _load_slab(wd_scr, c, n1, n2)) for c in cs]
        for c, a, b in zip(cs, xs, xd):
            r0 = pl.multiple_of(c * n2, n2)
            tfr_ref[pl.ds(r0, n2), :] = a
            tfi_ref[pl.ds(r0, n2), :] = b

    _unrolled_loop(n1, FFT_UNROLL_SLAB, step)


def _two_stage_conv_kernel(vv_ref, x0_ref, skip_ref, tfr_ref, tfi_ref,
                           f1_ref, f2_ref, f2c_ref, g1_ref, out_ref, w_scr, t_scr):
    n2, two_n1, p = f1_ref.shape
    n1 = two_n1 // 2
    t_scr[...] = vv_ref[...].astype(F32)
    _stage1(t_scr, f1_ref, w_scr)

    def step2(cs):
        xs = [_dot(f2_ref[...], _load_slab(w_scr, c, n1, n2)) for c in cs]
        ys = []
        for c, x in zip(cs, xs):
            r0 = pl.multiple_of(c * n2, n2)
            xr, xi = x[:n2], x[n2:]
            tr = tfr_ref[pl.ds(r0, n2), :]
            ti = tfi_ref[pl.ds(r0, n2), :]
            ys.append(jnp.concatenate([xr * tr - xi * ti, xr * ti + xi * tr], axis=0).astype(BF16))
        zs = [_dot(f2c_ref[...], y) for y in ys]
        for c, z in zip(cs, zs):
            w_scr[pl.ds(c, n2, stride=two_n1), :] = z[:n2]
            w_scr[pl.ds(n1 + c, n2, stride=two_n1), :] = z[n2:]

    _unrolled_loop(n1, FFT_UNROLL_SLAB, step2)

    def step3(bs):
        zb = [w_scr[pl.ds(pl.multiple_of(b * two_n1, two_n1), two_n1), :].astype(BF16) for b in bs]
        res = [_dot(g1_ref[b], z) for b, z in zip(bs, zb)]
        for b, r in zip(bs, res):
            t_scr[pl.ds(b, p, stride=n2), :] = r

    _unrolled_loop(n2, FFT_UNROLL_SMALL, step3)
    out = (t_scr[...] + vv_ref[...].astype(F32) * skip_ref[...]) * x0_ref[...].astype(F32)
    out_ref[...] = out.astype(out_ref.dtype)


def hy_longconv_two_stage(vv, x0, skip, filt, row0, n_seq, l):
    d = vv.shape[1]
    f1, f2, f2c, g1 = _two_stage_tables(l)
    n2, two_n1, p = f1.shape
    n = (two_n1 // 2) * n2
    dc = LANES
    tfr, tfi = pl.pallas_call(
        _two_stage_tf_kernel,
        grid=(d // dc,),
        in_specs=[pl.BlockSpec((l, dc), lambda j: (0, j)),
                  pl.BlockSpec((l, dc), lambda j: (0, d // dc + j)),
                  _const_spec(f1.shape), _const_spec(f2.shape)],
        out_specs=[pl.BlockSpec((n, dc), lambda j: (0, j)), pl.BlockSpec((n, dc), lambda j: (0, j))],
        out_shape=[jax.ShapeDtypeStruct((n, d), F32), jax.ShapeDtypeStruct((n, d), F32)],
        scratch_shapes=[pltpu.VMEM((n2 * two_n1, dc), F32), pltpu.VMEM((n2 * two_n1, dc), F32)],
        compiler_params=_cparams(("arbitrary",)),
        name="hy_tf_two_stage",
    )(filt, filt, f1, f2)
    base = row0 // l
    return pl.pallas_call(
        _two_stage_conv_kernel,
        grid=(d // dc, n_seq),
        in_specs=[pl.BlockSpec((l, dc), lambda j, s: (base + s, j)),
                  pl.BlockSpec((l, dc), lambda j, s: (base + s, j)),
                  pl.BlockSpec((1, dc), lambda j, s: (0, j)),
                  pl.BlockSpec((n, dc), lambda j, s: (0, j)),
                  pl.BlockSpec((n, dc), lambda j, s: (0, j)),
                  _const_spec(f1.shape), _const_spec(f2.shape), _const_spec(f2c.shape),
                  _const_spec(g1.shape)],
        out_specs=pl.BlockSpec((l, dc), lambda j, s: (s, j)),
        out_shape=jax.ShapeDtypeStruct((n_seq * l, d), BF16),
        scratch_shapes=[pltpu.VMEM((n2 * two_n1, dc), F32), pltpu.VMEM((l, dc), F32)],
        compiler_params=_cparams(("arbitrary", "arbitrary")),
        name="hy_longconv_two_stage",
    )(vv, x0, skip.reshape(1, d).astype(F32), tfr, tfi, f1, f2, f2c, g1)


def _hy_out_kernel(yc_ref, yl_ref, w_ref, b_ref, x_ref, mod_ref, out_ref, *, gate_idx, n_ctx_tiles):
    yg = jnp.where(pl.program_id(0) < n_ctx_tiles, yc_ref[...], yl_ref[...])
    y = _dot(yg.astype(BF16), w_ref[...]) + b_ref[...]
    out_ref[...] = x_ref[...] + mod_ref[0, gate_idx:gate_idx + 1, :] * y


def hy_out(y_ctx, y_lat, w_out, b_out, x, mod, gate_idx, t_ctx, l_lat):
    t, d = x.shape
    grp = functools.partial(_group_of_tile, tm=TM, t_ctx=t_ctx, l_lat=l_lat)
    row_spec = lambda w: pl.BlockSpec((TM, w), lambda i: (i, 0))
    n_ctx_tiles = t_ctx // TM
    return pl.pallas_call(
        functools.partial(_hy_out_kernel, gate_idx=gate_idx, n_ctx_tiles=n_ctx_tiles),
        grid=(t // TM,),
        in_specs=[pl.BlockSpec((TM, d), lambda i: (jnp.minimum(i, n_ctx_tiles - 1), 0)),
                  pl.BlockSpec((TM, d), lambda i: (jnp.maximum(i - n_ctx_tiles, 0), 0)),
                  _const_spec(w_out.shape), _const_spec((1, d)), row_spec(d),
                  pl.BlockSpec((1,) + mod.shape[1:], lambda i: (grp(i), 0, 0))],
        out_specs=row_spec(d),
        out_shape=jax.ShapeDtypeStruct((t, d), F32),
        compiler_params=_cparams(("arbitrary",)),
        name="hy_out",
    )(y_ctx, y_lat, w_out, b_out.reshape(1, d).astype(F32), x, mod)


def _grid_pos_embed(rows, d):
    r = np.repeat(np.arange(rows, dtype=np.float64), GRID_W)
    col = np.tile(np.arange(GRID_W, dtype=np.float64), rows)
    quarter = d // 4
    omega = 1.0 / (10000.0 ** (np.arange(quarter, dtype=np.float64) / quarter))
    ang_r = r[:, None] * omega
    ang_c = col[:, None] * omega
    pe = np.concatenate([np.sin(ang_r), np.cos(ang_r), np.sin(ang_c), np.cos(ang_c)], axis=-1)
    return jnp.asarray(pe.astype(np.float32))


def delta_layer(x, g, mod, state_lat, w_in, conv_w, a_log, dt_bias, g_head, w_out, dims):
    t_ctx, l_ctx, n_ctx, l_lat, n_lat = dims
    hk = DN_HEADS * DN_DK
    nqkv = 3 * hk
    w_qkv = w_in[:, :nqkv].astype(BF16)
    w_z = w_in[:, nqkv:nqkv + hk].astype(BF16)
    w_ab = jnp.zeros((w_in.shape[0], LANES), BF16).at[:, :4 * DN_HEADS].set(w_in[:, nqkv + hk:].astype(BF16))
    qkv, z, ab = norm_mod_matmul(x, g, mod, 0, 1, [w_qkv, w_z, w_ab], None, [BF16, BF16, F32],
                                 t_ctx, l_lat)
    q, k, v, gc, beta, gct = dn_prep(qkv, ab, conv_w.astype(F32), a_log, dt_bias, t_ctx, l_ctx, l_lat)
    outs = []
    ctx_states = []
    seq_lens = (l_ctx,) * n_ctx + (l_lat,) * n_lat
    for direction in range(2):
        o, s_all = dn_chunk_scan(q, k, v, gc, beta, gct, state_lat.astype(F32), direction=direction,
                                 seq_lens=seq_lens, n_zero_init=n_ctx)
        outs.append(o)
        ctx_states.append(s_all[:n_ctx])
    x = dn_out(outs[0], outs[1], z, g_head, w_out.astype(BF16), x, mod, 2, t_ctx, l_lat)
    return x, jnp.stack(ctx_states, axis=1)


def hyena_layer(x, g, mod, w_in, b_in, conv_w, conv_b, freq, fw1, fb1, fw2, fb2, fw3, fb3, fw4,
                skip, w_out, b_out, dims):
    t_ctx, l_ctx, n_ctx, l_lat, n_lat = dims
    d = x.shape[1]
    (u,) = norm_mod_matmul(x, g, mod, 0, 1, [w_in.astype(BF16)], [b_in.astype(F32)], [BF16],
                           t_ctx, l_lat)
    x0, vv = hy_conv_gate(u, conv_w.astype(F32), conv_b, t_ctx, l_ctx, l_lat)
    filt_c = hy_filter(l_ctx, d, freq, fw1, fb1, fw2, fb2, fw3, fb3, fw4)
    filt_l = hy_filter(l_lat, d, freq, fw1, fb1, fw2, fb2, fw3, fb3, fw4)
    y_c = hy_longconv_dense(vv, x0, skip, filt_c, n_ctx, l_ctx)
    y_l = hy_longconv_two_stage(vv, x0, skip, filt_l, t_ctx, n_lat, l_lat)
    return hy_out(y_c, y_l, w_out.astype(BF16), b_out, x, mod, 2, t_ctx, l_lat)


def kernel(x_prompt, x_sample, state_delta, c, c_ctx, w_ada, b_ada, g_norm, dn_w_in, dn_conv, dn_a_log, dn_dt_bias, dn_g_head, dn_w_out, hy_w_in, hy_b_in, hy_conv, hy_conv_b, hy_freq, hy_f_w1, hy_f_b1, hy_f_w2, hy_f_b2, hy_f_w3, hy_f_b3, hy_f_w4, hy_skip, hy_w_out, hy_b_out, w_router, router_bias, moe_w1, moe_w3, moe_w2, g_final):
    n_ctx, l_ctx, d = x_prompt.shape
    n_lat, l_lat, _ = x_sample.shape
    depth = w_ada.shape[0]
    t_ctx = n_ctx * l_ctx
    dims = (t_ctx, l_ctx, n_ctx, l_lat, n_lat)
    assert l_ctx % TM == 0 and l_lat % TM == 0 and t_ctx % l_lat == 0
    assert n_lat + 1 <= SUBLANES

    pos = _grid_pos_embed(l_lat // GRID_W, d)
    x = (x_prompt.reshape(t_ctx, d).astype(F32), x_sample.reshape(n_lat * l_lat, d).astype(F32), pos)
    cond = jnp.zeros((SUBLANES, d), F32).at[0].set(c_ctx.astype(F32)).at[1:1 + n_lat].set(c.astype(F32))
    mod_all = ada_modulation(cond, w_ada.astype(F32), b_ada.astype(F32))
    mod_all = mod_all.reshape(depth, SUBLANES, 6, d)

    ctx_states = []
    for i in range(depth):
        mod = mod_all[i]
        j = i // 2
        if i % 2 == 0:
            x, s_ctx = delta_layer(x, g_norm[i, 0], mod, state_delta[:, j], dn_w_in[j], dn_conv[j],
                                   dn_a_log[j], dn_dt_bias[j], dn_g_head[j], dn_w_out[j], dims)
            ctx_states.append(s_ctx.astype(x_prompt.dtype))
        else:
            x = hyena_layer(x, g_norm[i, 0], mod, hy_w_in[j], hy_b_in[j], hy_conv[j], hy_conv_b[j],
                            hy_freq[j], hy_f_w1[j], hy_f_b1[j], hy_f_w2[j], hy_f_b2[j], hy_f_w3[j],
                            hy_f_b3[j], hy_f_w4[j], hy_skip[j], hy_w_out[j], hy_b_out[j], dims)
        x = moe_layer(x, g_norm[i, 1], mod, w_router, router_bias, moe_w1, moe_w3, moe_w2, i,
                      t_ctx, l_lat, g_final if i == depth - 1 else None)
    y_ctx, y_lat = x
    y_prompt = y_ctx.reshape(n_ctx, l_ctx, d).astype(x_prompt.dtype)
    y_sample = y_lat.reshape(n_lat, l_lat, d).astype(x_sample.dtype)
    new_state = jnp.stack(ctx_states, axis=1)
    return (y_prompt, y_sample, new_state)
```
